```python
import jax, jax.numpy as jnp
from jax import lax
import numpy as np

D_MODEL = 1024
BATCH = 4
SEQ = 4096
DEPTH = 1
DEC_BATCH = 32
DEC_SEQ = 1
PAST_LEN = 16384
PAGE_SIZE = 128

ATT_HEADS = 8
ATT_HEAD_DIM = 64
ATT_WIDTH = ATT_HEADS * ATT_HEAD_DIM
MOBA_BLOCK = 256
MOBA_TOPK = 3
ATT_QUERY_CHUNK = 64
M_HEADS = 4
M_WIDTH = D_MODEL
M_V_DIM = M_WIDTH // M_HEADS
M_QK_DIM = M_V_DIM // 2
M_QK_WIDTH = M_HEADS * M_QK_DIM
M_CONV = 4
M_CHUNK = 64
NORM_EPS = 1e-6
NEG_SCORE = -1e30
COLUMN_WIDTHS = (ATT_WIDTH, ATT_WIDTH, ATT_WIDTH, ATT_WIDTH, 2 * M_QK_WIDTH, M_WIDTH, M_WIDTH, M_WIDTH,
                 2 * M_HEADS, D_MODEL, D_MODEL)

kernel_name = "moba_mlstm_gated_parallel_decoder_step"

F32 = jnp.float32


def rms_norm(x, g):
    xf = x.astype(F32)
    y = xf * lax.rsqrt(jnp.mean(xf * xf, axis=-1, keepdims=True) + NORM_EPS)
    return (y * g.astype(F32)).astype(x.dtype)


def split_columns(z):
    parts, start = [], 0
    for w in COLUMN_WIDTHS:
        parts.append(z[..., start:start + w])
        start += w
    return parts


def moba_rows(q, k_means, t, ksel):
    j = t // MOBA_BLOCK
    nb = k_means.shape[2]
    s = jnp.einsum('bhqd,bhnd->bhqn', q.astype(F32), k_means)
    past = jnp.arange(nb, dtype=jnp.int32)[None, :] < j[:, None]
    s = jnp.where(past, s, NEG_SCORE)
    _, idx = lax.top_k(s, ksel)
    own = jnp.broadcast_to(j[:, None], idx.shape[:-1] + (1,)).astype(idx.dtype)
    blocks = jnp.concatenate([idx, own], axis=-1)
    block_ok = jnp.concatenate([idx < j[:, None], jnp.ones(own.shape, bool)], axis=-1)
    pos = blocks[..., None] * MOBA_BLOCK + jnp.arange(MOBA_BLOCK, dtype=jnp.int32)
    valid = block_ok[..., None] & (pos <= t[:, None, None])
    shp = pos.shape[:3] + (-1,)
    return pos.reshape(shp), valid.reshape(shp)


def block_attend(q, k_rows, v_rows, valid):
    s = jnp.einsum('bhqd,bhqrd->bhqr', q, k_rows).astype(F32) * (ATT_HEAD_DIM ** -0.5)
    p = jax.nn.softmax(jnp.where(valid, s, NEG_SCORE), axis=-1)
    return jnp.einsum('bhqr,bhqrd->bhqd', p.astype(v_rows.dtype), v_rows)


def moba_prompt(q, k, v):
    B, S = q.shape[:2]
    qh, kh, vh = (t.transpose(0, 2, 1, 3) for t in (q, k, v))
    nb = -(-S // MOBA_BLOCK)
    pad = nb * MOBA_BLOCK - S
    k_means = jnp.pad(kh.astype(F32), ((0, 0), (0, 0), (0, pad), (0, 0)))
    k_means = k_means.reshape(B, ATT_HEADS, nb, MOBA_BLOCK, ATT_HEAD_DIM).sum(3) / MOBA_BLOCK
    ksel = min(MOBA_TOPK, max(nb - 1, 1))
    qc = ATT_QUERY_CHUNK if S % ATT_QUERY_CHUNK == 0 else S
    nqc = S // qc
    q_chunks = jnp.moveaxis(qh.reshape(B, ATT_HEADS, nqc, qc, ATT_HEAD_DIM), 2, 0)
    t_chunks = jnp.arange(S, dtype=jnp.int32).reshape(nqc, qc)
    gather = jax.vmap(jax.vmap(lambda src, idx: src[idx]))

    def body(inp):
        q_blk, t = inp
        pos, valid = moba_rows(q_blk, k_means, t, ksel)
        pos = jnp.clip(pos, 0, S - 1)
        return block_attend(q_blk, gather(kh, pos), gather(vh, pos), valid)

    out = lax.map(body, (q_chunks, t_chunks))
    out = jnp.moveaxis(out, 0, 2).reshape(B, ATT_HEADS, S, ATT_HEAD_DIM)
    return out.transpose(0, 2, 1, 3).reshape(B, S, ATT_WIDTH)


def rows_from_cache(pool, layer, new, page_table, pos):
    DB, Qn, H = new.shape[:3]
    P = page_table.shape[1] * PAGE_SIZE
    b_idx = jnp.arange(DB, dtype=jnp.int32)[:, None, None, None]
    h_idx = jnp.arange(H, dtype=jnp.int32)[None, :, None, None]
    pp = jnp.clip(pos, 0, P - 1)
    page = page_table[b_idx, pp // PAGE_SIZE]
    past = pool[layer, page, pp % PAGE_SIZE, h_idx]
    fresh = new[b_idx, jnp.clip(pos - P, 0, Qn - 1), h_idx]
    return jnp.where((pos < P)[..., None], past, fresh.astype(past.dtype))


def moba_sample(q, k, v, cache_k, cache_v, page_table, layer):
    DB, Qn = q.shape[:2]
    n_pages = page_table.shape[1]
    P = n_pages * PAGE_SIZE
    nb = -(-(P + Qn) // MOBA_BLOCK)
    ksel = min(MOBA_TOPK, max(nb - 1, 1))
    page_sums = cache_k[layer, page_table].astype(F32).sum(2)
    page_block = jnp.arange(n_pages, dtype=jnp.int32) * PAGE_SIZE // MOBA_BLOCK
    t = P + jnp.arange(Qn, dtype=jnp.int32)
    sums = (jax.ops.segment_sum(jnp.moveaxis(page_sums, 1, 0), page_block, num_segments=nb)
            + jax.ops.segment_sum(jnp.moveaxis(k.astype(F32), 1, 0), t // MOBA_BLOCK, num_segments=nb))
    k_means = jnp.transpose(sums / MOBA_BLOCK, (1, 2, 0, 3))
    qh = q.transpose(0, 2, 1, 3)
    pos, valid = moba_rows(qh, k_means, t, ksel)
    k_rows = rows_from_cache(cache_k, layer, k, page_table, pos)
    v_rows = rows_from_cache(cache_v, layer, v, page_table, pos)
    out = block_attend(qh, k_rows.astype(qh.dtype), v_rows, valid)
    return out.transpose(0, 2, 1, 3).reshape(DB, Qn, ATT_WIDTH).astype(q.dtype)


def mlstm_cell(q, k, v, i_pre, f_pre, C0, n0, m0):
    B, H, L = q.shape[:3]
    lc = M_CHUNK if L % M_CHUNK == 0 else L
    nc = L // lc

    def chunks(a):
        return jnp.moveaxis(a.reshape((B, H, nc, lc) + a.shape[3:]), 2, 0)

    causal = jnp.tril(jnp.ones((lc, lc), bool))
    logf = jax.nn.log_sigmoid(f_pre)

    def step(carry, inp):
        C, n, m = carry
        qc, kc, vc, ic, lfc = inp
        b = jnp.cumsum(lfc, axis=-1)
        d = jnp.where(causal, b[..., :, None] - b[..., None, :] + ic[..., None, :], -jnp.inf)
        inter = b + m[..., None]
        m_t = jnp.maximum(inter, d.max(-1))
        w = jnp.exp(d - m_t[..., None])
        a = jnp.exp(inter - m_t)
        s = jnp.einsum('bhtd,bhsd->bhts', qc, kc) * w
        num = a[..., None] * jnp.einsum('bhvd,bhtd->bhtv', C, qc) + jnp.einsum('bhts,bhsv->bhtv', s, vc)
        den = a * jnp.einsum('bhd,bhtd->bht', n, qc) + s.sum(-1)
        h = num / jnp.maximum(jnp.abs(den), jnp.exp(-m_t))[..., None]
        g = b[..., -1:] - b + ic
        m_new = jnp.maximum(b[..., -1] + m, g.max(-1))
        ws = jnp.exp(g - m_new[..., None])
        a_end = jnp.exp(b[..., -1] + m - m_new)
        C_new = a_end[..., None, None] * C + jnp.einsum('bhs,bhsv,bhsd->bhvd', ws, vc, kc)
        n_new = a_end[..., None] * n + jnp.einsum('bhs,bhsd->bhd', ws, kc)
        return (C_new, n_new, m_new), h

    (C, n, m), h = lax.scan(step, (C0, n0, m0),
                            (chunks(q), chunks(k), chunks(v), chunks(i_pre), chunks(logf)))
    h = jnp.moveaxis(h, 0, 2).reshape(B, H, L, M_V_DIM)
    return h, C, n, m


def mlstm_branch(qk_pre, v, o_pre, if_pre, conv_buf, C0, n0, m0, w_conv, b_conv, m_norm_g):
    B, L, ch = qk_pre.shape
    xp = jnp.concatenate([conv_buf.astype(qk_pre.dtype), qk_pre], axis=1)
    qk = lax.conv_general_dilated(xp, w_conv.astype(xp.dtype)[:, None, :], (1,), 'VALID',
                                  dimension_numbers=('NWC', 'WIO', 'NWC'), feature_group_count=ch)
    qk = jax.nn.silu(qk + b_conv.astype(qk.dtype))
    q, k = jnp.split(qk, 2, axis=-1)

    def to_heads(a, d):
        return a.reshape(B, L, M_HEADS, d).transpose(0, 2, 1, 3).astype(F32)

    qh = to_heads(q, M_QK_DIM)
    kh = to_heads(k, M_QK_DIM) * (M_QK_DIM ** -0.5)
    vh = to_heads(v, M_V_DIM)
    gates = if_pre.transpose(0, 2, 1)
    h, C, n, m = mlstm_cell(qh, kh, vh, gates[:, :M_HEADS], gates[:, M_HEADS:],
                            C0.astype(F32), n0.astype(F32), m0.astype(F32))
    h = h.transpose(0, 2, 1, 3)
    h = h * lax.rsqrt(jnp.mean(h * h, axis=-1, keepdims=True) + NORM_EPS)
    h = h * m_norm_g.astype(F32).reshape(M_HEADS, M_V_DIM)
    h = h.reshape(B, L, M_WIDTH) * jax.nn.sigmoid(o_pre.astype(F32))
    return h.astype(v.dtype), xp[:, -(M_CONV - 1):], C, n, m


def layer(x, c, attend, conv_buf, C0, n0, m0, w_ada, b_ada, norm_g, w_in, b_if, w_conv, b_conv,
          m_norm_g, w_att, w_mlstm, w_out):
    B, L, _ = x.shape
    mod = jax.nn.silu(c) @ w_ada + b_ada
    shift, scale, gate = jnp.split(mod, 3, axis=-1)
    h = rms_norm(x, norm_g) * (1 + scale[:, None]) + shift[:, None]
    aq, ak, av, az, mqk, mv, mo, mz, mif, ga, gm = split_columns(h @ w_in)

    def att_heads(a):
        return a.reshape(B, L, ATT_HEADS, ATT_HEAD_DIM)

    k_rows, v_rows = att_heads(ak), att_heads(av)
    ya = attend(att_heads(aq), k_rows, v_rows)
    ym, conv_new, C, n, m = mlstm_branch(mqk, mv, mo, mif.astype(F32) + b_if.astype(F32), conv_buf,
                                         C0, n0, m0, w_conv, b_conv, m_norm_g)
    ya = (ya * jax.nn.silu(az)) @ w_att
    ym = (ym * jax.nn.silu(mz)) @ w_mlstm
    u = jax.nn.sigmoid(ga) * ya + jax.nn.sigmoid(gm) * ym
    y = x + gate[:, None] * (u @ w_out)
    return y, k_rows, v_rows, C, n, m, conv_new


def setup_inputs(seed: int = 0) -> dict:
    key = jax.random.key(seed)
    ks = jax.random.split(key, 24)
    n_pages = PAST_LEN // PAGE_SIZE
    n_phys = (DEC_BATCH * n_pages * 5) // 4
    d_in = sum(COLUMN_WIDTHS)

    def nrm(k, shape, s=1.0):
        return s * jax.random.normal(k, shape, F32)

    page_table = jax.random.permutation(ks[0], n_phys)[:DEC_BATCH * n_pages]
    page_table = page_table.reshape(DEC_BATCH, n_pages).astype(jnp.int32)
    b_if = jnp.concatenate([nrm(ks[1], (DEPTH, M_HEADS), 0.1),
                            jnp.linspace(3.0, 6.0, M_HEADS, dtype=F32)[None] + nrm(ks[2], (DEPTH, M_HEADS), 0.1)],
                           axis=-1)
    return {
        "x_prompt": nrm(ks[3], (BATCH, SEQ, D_MODEL)),
        "x_sample": nrm(ks[4], (DEC_BATCH, DEC_SEQ, D_MODEL)),
        "cache_k": nrm(ks[5], (DEPTH, n_phys, PAGE_SIZE, ATT_HEADS, ATT_HEAD_DIM)),
        "cache_v": nrm(ks[6], (DEPTH, n_phys, PAGE_SIZE, ATT_HEADS, ATT_HEAD_DIM)),
        "state_C": nrm(ks[7], (DEPTH, DEC_BATCH, M_HEADS, M_V_DIM, M_QK_DIM)),
        "state_n": nrm(ks[8], (DEPTH, DEC_BATCH, M_HEADS, M_QK_DIM)),
        "state_m": nrm(ks[9], (DEPTH, DEC_BATCH, M_HEADS), 0.5),
        "state_conv": nrm(ks[10], (DEPTH, DEC_BATCH, M_CONV - 1, 2 * M_QK_WIDTH)),
        "page_table": page_table,
        "c_prompt": nrm(ks[11], (BATCH, D_MODEL)),
        "c_sample": nrm(ks[12], (DEC_BATCH, D_MODEL)),
        "w_ada": nrm(ks[13], (DEPTH, D_MODEL, 3 * D_MODEL), 0.5 * D_MODEL ** -0.5),
        "b_ada": nrm(ks[14], (DEPTH, 3 * D_MODEL), 0.02),
        "norm_g": 1.0 + nrm(ks[15], (DEPTH, D_MODEL), 0.05),
        "w_in": nrm(ks[16], (DEPTH, D_MODEL, d_in), D_MODEL ** -0.5),
        "b_if": b_if,
        "w_conv": nrm(ks[17], (DEPTH, M_CONV, 2 * M_QK_WIDTH), M_CONV ** -0.5),
        "b_conv": nrm(ks[18], (DEPTH, 2 * M_QK_WIDTH), 0.02),
        "m_norm_g": 1.0 + nrm(ks[19], (DEPTH, M_WIDTH), 0.05),
        "w_att": nrm(ks[20], (DEPTH, ATT_WIDTH, D_MODEL), ATT_WIDTH ** -0.5),
        "w_mlstm": nrm(ks[21], (DEPTH, M_WIDTH, D_MODEL), M_WIDTH ** -0.5),
        "w_out": nrm(ks[22], (DEPTH, D_MODEL, D_MODEL), D_MODEL ** -0.5),
        "final_g": 1.0 + nrm(ks[23], (D_MODEL,), 0.05),
    }


def reference(x_prompt, x_sample, cache_k, cache_v, state_C, state_n, state_m, state_conv, page_table,
              c_prompt, c_sample, w_ada, b_ada, norm_g, w_in, b_if, w_conv, b_conv, m_norm_g,
              w_att, w_mlstm, w_out, final_g):
    B = x_prompt.shape[0]
    xp, xs = x_prompt, x_sample
    outs_p, outs_s = [], []
    for l in range(DEPTH):
        w = (w_ada[l], b_ada[l], norm_g[l], w_in[l], b_if[l], w_conv[l], b_conv[l], m_norm_g[l],
             w_att[l], w_mlstm[l], w_out[l])
        conv0 = jnp.zeros((B, M_CONV - 1, 2 * M_QK_WIDTH), xp.dtype)
        C0 = jnp.zeros((B, M_HEADS, M_V_DIM, M_QK_DIM), F32)
        n0 = jnp.zeros((B, M_HEADS, M_QK_DIM), F32)
        m0 = jnp.zeros((B, M_HEADS), F32)
        xp, *st_p = layer(xp, c_prompt, moba_prompt, conv0, C0, n0, m0, *w)
        attend_s = lambda q, k, v, l=l: moba_sample(q, k, v, cache_k, cache_v, page_table, l)
        xs, *st_s = layer(xs, c_sample, attend_s, state_conv[l], state_C[l], state_n[l], state_m[l], *w)
        outs_p.append(st_p)
        outs_s.append(st_s)
    k_p, v_p, C_p, n_p, m_p, conv_p = [jnp.stack([o[i] for o in outs_p]) for i in range(6)]
    k_s, v_s, C_s, n_s, m_s, conv_s = [jnp.stack([o[i] for o in outs_s]) for i in range(6)]
    y_prompt = rms_norm(xp, final_g)
    y_sample = rms_norm(xs, final_g)
    return (y_prompt, y_sample, k_p, v_p, C_p, n_p, m_p, conv_p, k_s, v_s, C_s, n_s, m_s, conv_s)
```

```python
import functools

import jax
import jax.numpy as jnp
from jax import lax
from jax.experimental import pallas as pl
from jax.experimental.pallas import tpu as pltpu

F32 = jnp.float32
BF16 = jnp.bfloat16

D_MODEL = 1024
ATT_HEADS = 8
ATT_HEAD_DIM = 64
ATT_WIDTH = ATT_HEADS * ATT_HEAD_DIM
MOBA_BLOCK = 256
MOBA_TOPK = 3
M_HEADS = 4
M_V_DIM = 256
M_QK_DIM = 128
M_QK_WIDTH = M_HEADS * M_QK_DIM
M_WIDTH = M_HEADS * M_V_DIM
M_CONV = 4
PAGE_SIZE = 128
NORM_EPS = 1e-6
NEG_SCORE = -1e30

V7X_LANES = 128
V7X_SUBLANES = 8
V7X_VMEM_LIMIT_BYTES = 56 * 1024 * 1024

_C_AQ, _C_AK, _C_AV, _C_AZ = 0, 512, 1024, 1536
_C_MQK, _C_MV, _C_MO, _C_MZ = 2048, 3072, 4096, 5120
_C_MIF, _C_GA, _C_GM, _C_END = 6144, 6152, 7176, 8200
_P_GA, _P_GM, _P_END = 6144, 7168, 8192


def _silu(x):
    return x * jax.nn.sigmoid(x)


def _split3(x):
    hi = x.astype(BF16)
    r1 = x - hi.astype(F32)
    mid = r1.astype(BF16)
    lo = (r1 - mid.astype(F32)).astype(BF16)
    return hi, mid, lo


def _pack_w_in(w_in):
    w_main = jnp.concatenate([w_in[:, :_C_MIF], w_in[:, _C_GA:_C_END]], axis=1).astype(BF16)
    w_ifT = w_in[:, _C_MIF:_C_GA].T.astype(BF16)
    return w_main, w_ifT


def _params(*sem):
    return pltpu.CompilerParams(dimension_semantics=sem, vmem_limit_bytes=V7X_VMEM_LIMIT_BYTES)


def _ada_kernel(c_ref, w_ref, b_ref, o_ref):
    a = _silu(c_ref[...])
    a_hi = a.astype(BF16)
    a_lo = (a - a_hi.astype(F32)).astype(BF16)
    w = w_ref[...]
    w_hi = w.astype(BF16)
    w_lo = (w - w_hi.astype(F32)).astype(BF16)
    acc = jnp.dot(a_hi, w_hi, preferred_element_type=F32)
    acc += jnp.dot(a_hi, w_lo, preferred_element_type=F32)
    acc += jnp.dot(a_lo, w_hi, preferred_element_type=F32)
    o_ref[...] = acc + b_ref[...]


def _ada(c, w_ada, b_ada):
    n, d = c.shape
    d3 = w_ada.shape[1]
    bn = 512
    return pl.pallas_call(
        _ada_kernel,
        grid=(d3 // bn,),
        in_specs=[pl.BlockSpec((n, d), lambda i: (0, 0)),
                  pl.BlockSpec((d, bn), lambda i: (0, i)),
                  pl.BlockSpec((1, bn), lambda i: (0, i))],
        out_specs=pl.BlockSpec((n, bn), lambda i: (0, i)),
        out_shape=jax.ShapeDtypeStruct((n, d3), F32),
        compiler_params=_params("arbitrary"),
        name="ada_mod",
    )(c, w_ada, b_ada.reshape(1, d3))


def _inproj_kernel(x_ref, mod_ref, ng_ref, w_ref, wif_ref, bif_ref,
                   qT_ref, kh_ref, vT_ref, kf_ref, vf_ref, kmean_ref,
                   az_ref, mqk_ref, mv_ref, mo_ref, mz_ref, ga_ref, gm_ref, gT_ref):
    j = pl.program_id(1)
    x = x_ref[0]
    xn = x * lax.rsqrt(jnp.mean(x * x, axis=-1, keepdims=True) + NORM_EPS) * ng_ref[...]
    shift = mod_ref[0, 0:1, :]
    scale = mod_ref[0, 1:2, :]
    h = (xn * (1.0 + scale) + shift).astype(BF16)

    def proj(a, b):
        return jnp.dot(h, w_ref[:, a:b], preferred_element_type=F32)

    q = proj(_C_AQ, _C_AK) * (ATT_HEAD_DIM ** -0.5)
    qT_ref[0, 0] = q.T.astype(BF16)
    k = proj(_C_AK, _C_AV)
    kf_ref[0] = k
    ksum = jnp.sum(k, axis=0, keepdims=True) / MOBA_BLOCK
    for hh in range(ATT_HEADS):
        sl = slice(hh * ATT_HEAD_DIM, (hh + 1) * ATT_HEAD_DIM)
        kh_ref[0, 0, hh] = k[:, sl].astype(BF16)
        kmean_ref[0, hh, pl.ds(j, 1), :] = ksum[:, sl]
    v = proj(_C_AV, _C_AZ)
    vf_ref[0] = v
    vT_ref[0, 0] = v.T.astype(BF16)
    az_ref[0] = proj(_C_AZ, _C_MQK)
    mqk_ref[0] = proj(_C_MQK, _C_MV)
    mv_ref[0] = proj(_C_MV, _C_MO).astype(BF16)
    mo_ref[0] = proj(_C_MO, _C_MZ)
    mz_ref[0] = proj(_C_MZ, _C_MIF)
    ga_ref[0] = proj(_P_GA, _P_GM)
    gm_ref[0] = proj(_P_GM, _P_END)
    gT = lax.dot_general(wif_ref[...], h, (((1,), (1,)), ((), ())), preferred_element_type=F32)
    gT_ref[0] = gT + bif_ref[...]


def _inproj(x, mod3, norm_g, w_main, w_ifT, b_if):
    B, S, D = x.shape
    tm = MOBA_BLOCK
    nb = S // tm
    H, dh = ATT_HEADS, ATT_HEAD_DIM
    const = lambda b, j: (0, 0)
    tok = lambda w: pl.BlockSpec((1, tm, w), lambda b, j: (b, j, 0))
    blk = pl.BlockSpec((1, 1, ATT_WIDTH, tm), lambda b, j: (b, j, 0, 0))
    out_shape = [
        jax.ShapeDtypeStruct((B, nb, ATT_WIDTH, tm), BF16),
        jax.ShapeDtypeStruct((B, nb, H, tm, dh), BF16),
        jax.ShapeDtypeStruct((B, nb, ATT_WIDTH, tm), BF16),
        jax.ShapeDtypeStruct((B, S, ATT_WIDTH), F32),
        jax.ShapeDtypeStruct((B, S, ATT_WIDTH), F32),
        jax.ShapeDtypeStruct((B, H, nb, dh), F32),
        jax.ShapeDtypeStruct((B, S, ATT_WIDTH), F32),
        jax.ShapeDtypeStruct((B, S, 2 * M_QK_WIDTH), F32),
        jax.ShapeDtypeStruct((B, S, M_WIDTH), BF16),
        jax.ShapeDtypeStruct((B, S, M_WIDTH), F32),
        jax.ShapeDtypeStruct((B, S, M_WIDTH), F32),
        jax.ShapeDtypeStruct((B, S, D), F32),
        jax.ShapeDtypeStruct((B, S, D), F32),
        jax.ShapeDtypeStruct((B, 2 * M_HEADS, S), F32),
    ]
    out_specs = [
        blk,
        pl.BlockSpec((1, 1, H, tm, dh), lambda b, j: (b, j, 0, 0, 0)),
        blk,
        tok(ATT_WIDTH), tok(ATT_WIDTH),
        pl.BlockSpec((1, H, nb, dh), lambda b, j: (b, 0, 0, 0)),
        tok(ATT_WIDTH), tok(2 * M_QK_WIDTH), tok(M_WIDTH), tok(M_WIDTH), tok(M_WIDTH), tok(D), tok(D),
        pl.BlockSpec((1, 2 * M_HEADS, tm), lambda b, j: (b, 0, j)),
    ]
    in_specs = [
        pl.BlockSpec((1, tm, D), lambda b, j: (b, j, 0)),
        pl.BlockSpec((1, 3, D), lambda b, j: (b, 0, 0)),
        pl.BlockSpec((1, D), const),
        pl.BlockSpec(w_main.shape, const, pipeline_mode=pl.Buffered(1)),
        pl.BlockSpec(w_ifT.shape, const),
        pl.BlockSpec((2 * M_HEADS, 1), const),
    ]
    return pl.pallas_call(
        _inproj_kernel,
        grid=(B, nb),
        in_specs=in_specs,
        out_specs=out_specs,
        out_shape=out_shape,
        compiler_params=_params("arbitrary", "arbitrary"),
        name="prompt_inproj",
    )(x, mod3, norm_g.reshape(1, D), w_main, w_ifT, b_if.reshape(2 * M_HEADS, 1))


def _moba_kernel(qT_ref, kh_ref, vT_ref, kmean_ref, o_ref, sel_ref):
    j = pl.program_id(2)
    nb = kmean_ref.shape[2]
    tq = qT_ref.shape[3]
    qT = qT_ref[0, 0]

    km_hi, km_mid, km_lo = _split3(kmean_ref[0, 0])
    sg = (jnp.dot(km_hi, qT, preferred_element_type=F32)
          + jnp.dot(km_mid, qT, preferred_element_type=F32)
          + jnp.dot(km_lo, qT, preferred_element_type=F32))
    blk = lax.broadcasted_iota(jnp.int32, (nb, tq), 0)
    past = blk < j
    sg = jnp.where(past, sg, NEG_SCORE)
    cnt = jnp.zeros((nb, tq), jnp.int32)
    for m in range(nb):
        row = sg[m:m + 1, :]
        beats = (row > sg) | ((row == sg) & (blk > m))
        cnt += beats.astype(jnp.int32)
    sel_ref[...] = jnp.where((cnt < MOBA_TOPK) & past, 1.0, 0.0)

    s = jnp.dot(kh_ref[0, j, 0], qT, preferred_element_type=F32)
    kpos = lax.broadcasted_iota(jnp.int32, s.shape, 0)
    qpos = lax.broadcasted_iota(jnp.int32, s.shape, 1)
    s = jnp.where(kpos <= qpos, s, NEG_SCORE)
    m0 = jnp.max(s, axis=0, keepdims=True)
    p = jnp.exp(s - m0)
    l0 = jnp.sum(p, axis=0, keepdims=True)
    acc0 = jnp.dot(vT_ref[0, j], p.astype(BF16), preferred_element_type=F32)

    def body(n, carry):
        m, l, acc = carry
        s = jnp.dot(kh_ref[0, n, 0], qT, preferred_element_type=F32)
        s = jnp.where(sel_ref[pl.ds(n, 1), :] > 0.5, s, NEG_SCORE)
        m_new = jnp.maximum(m, jnp.max(s, axis=0, keepdims=True))
        alpha = jnp.exp(m - m_new)
        p = jnp.exp(s - m_new)
        l = alpha * l + jnp.sum(p, axis=0, keepdims=True)
        acc = alpha * acc + jnp.dot(vT_ref[0, n], p.astype(BF16), preferred_element_type=F32)
        return m_new, l, acc

    _, l, acc = lax.fori_loop(0, j, body, (m0, l0, acc0))
    o_ref[0, 0] = acc / l


def _moba_prompt(qT, kh, vT, kmean):
    B, nb, _, tq = qT.shape
    H, dh = ATT_HEADS, ATT_HEAD_DIM
    return pl.pallas_call(
        _moba_kernel,
        grid=(B, H, nb),
        in_specs=[
            pl.BlockSpec((1, 1, dh, tq), lambda b, h, j: (b, j, h, 0)),
            pl.BlockSpec((1, nb, 1, tq, dh), lambda b, h, j: (b, 0, h, 0, 0)),
            pl.BlockSpec((1, nb, dh, tq), lambda b, h, j: (b, 0, h, 0)),
            pl.BlockSpec((1, 1, nb, dh), lambda b, h, j: (b, h, 0, 0)),
        ],
        out_specs=pl.BlockSpec((1, 1, dh, tq), lambda b, h, j: (b, j, h, 0)),
        out_shape=jax.ShapeDtypeStruct((B, nb, ATT_WIDTH, tq), F32),
        scratch_shapes=[pltpu.VMEM((nb, tq), F32)],
        compiler_params=_params("arbitrary", "arbitrary", "arbitrary"),
        name="prompt_moba",
    )(qT, kh, vT, kmean)


def _lane_scan(x, op):
    n = x.shape[-1]
    lane = lax.broadcasted_iota(jnp.int32, x.shape, x.ndim - 1)
    sh = 1
    while sh < n:
        r = pltpu.roll(x, sh, axis=x.ndim - 1)
        x = jnp.where(lane >= sh, op(x, r), x)
        sh *= 2
    return x


def _causal_conv(x, tail_ref, w_ref, b_ref):
    L = x.shape[0]
    prev = tail_ref[...]
    row = lax.broadcasted_iota(jnp.int32, (V7X_SUBLANES, x.shape[1]), 0)
    acc = x * w_ref[M_CONV - 1:M_CONV, :] + b_ref[...]
    for d in range(1, M_CONV):
        r = pltpu.roll(x, d, axis=0)
        head = jnp.where(row < d, pltpu.roll(prev, d, axis=0), r[:V7X_SUBLANES])
        shifted = jnp.concatenate([head, r[V7X_SUBLANES:]], axis=0)
        acc += shifted * w_ref[M_CONV - 1 - d:M_CONV - d, :]
    tail_ref[...] = x[L - V7X_SUBLANES:, :]
    return acc


def _mlstm_kernel(q_ref, k_ref, v_ref, g_ref, o_ref, wq_ref, wk_ref, bq_ref, bk_ref, ng_ref,
                  h_ref, C_out, n_out, m_out,
                  C_s, n_s, m_s, tq_s, tk_s):
    hd = pl.program_id(1)
    c = pl.program_id(2)
    L = q_ref.shape[1]

    @pl.when(c == 0)
    def _():
        C_s[...] = jnp.zeros_like(C_s)
        n_s[...] = jnp.zeros_like(n_s)
        m_s[...] = jnp.zeros_like(m_s)
        tq_s[...] = jnp.zeros_like(tq_s)
        tk_s[...] = jnp.zeros_like(tk_s)

    qc = _silu(_causal_conv(q_ref[0], tq_s, wq_ref, bq_ref))
    kc = _silu(_causal_conv(k_ref[0], tk_s, wk_ref, bk_ref)) * (M_QK_DIM ** -0.5)
    qb = qc.astype(BF16)
    kb = kc.astype(BF16)
    vb = v_ref[0]

    i_row = g_ref[0, pl.ds(hd, 1), :]
    f_row = g_ref[0, pl.ds(M_HEADS + hd, 1), :]
    logf = jnp.minimum(f_row, 0.0) - jnp.log1p(jnp.exp(-jnp.abs(f_row)))
    b_row = _lane_scan(logf, jnp.add)
    c_row = i_row - b_row
    m_prev = m_s[0:1, 0:1]
    M_row = jnp.maximum(_lane_scan(c_row, jnp.maximum), m_prev)
    M_last = jnp.max(M_row, axis=1, keepdims=True)

    r8 = lax.broadcasted_iota(jnp.int32, (V7X_SUBLANES, L), 0)
    rows = jnp.where(r8 == 0, c_row, jnp.where(r8 == 1, M_row, jnp.where(r8 == 2, b_row, 0.0)))
    rows = jnp.concatenate([rows, jnp.zeros((V7X_LANES - V7X_SUBLANES, L), F32)], axis=0)
    cols = rows.T
    c_col = cols[:, 0:1]
    M_col = cols[:, 1:2]
    b_col = cols[:, 2:3]

    t_idx = lax.broadcasted_iota(jnp.int32, (L, L), 0)
    s_idx = lax.broadcasted_iota(jnp.int32, (L, L), 1)
    w = jnp.where(s_idx <= t_idx, jnp.exp(c_row - M_col), 0.0)
    a_col = jnp.exp(m_prev - M_col)

    s = lax.dot_general(qb, kb, (((1,), (1,)), ((), ())), preferred_element_type=F32) * w
    C_b = C_s[...].astype(BF16)
    inter = lax.dot_general(qb, C_b, (((1,), (1,)), ((), ())), preferred_element_type=F32)
    num = a_col * inter + jnp.dot(s.astype(BF16), vb, preferred_element_type=F32)
    n_b = n_s[...].astype(BF16).astype(F32)
    qn = jnp.sum(qb.astype(F32) * n_b, axis=1, keepdims=True)
    den = a_col * qn + jnp.sum(s, axis=1, keepdims=True)
    hh = num / jnp.maximum(jnp.abs(den), jnp.exp(-(b_col + M_col)))
    hh = hh * lax.rsqrt(jnp.mean(hh * hh, axis=-1, keepdims=True) + NORM_EPS)
    hh = hh * ng_ref[...] * jax.nn.sigmoid(o_ref[0])
    h_ref[0] = hh

    ws_col = jnp.exp(c_col - M_last)
    ws_row = jnp.exp(c_row - M_last)
    a_end = jnp.exp(m_prev - M_last)
    vwT = (vb.astype(F32) * ws_col).T.astype(BF16)
    C_s[...] = a_end * C_s[...] + jnp.dot(vwT, kb, preferred_element_type=F32)
    ws8 = jnp.broadcast_to(ws_row, (V7X_SUBLANES, L)).astype(BF16)
    n_s[...] = a_end * n_s[...] + jnp.dot(ws8, kb, preferred_element_type=F32)[0:1]
    m_s[...] = jnp.broadcast_to(b_row[:, L - 1:L] + M_last, m_s.shape)

    @pl.when(c == pl.num_programs(2) - 1)
    def _():
        C_out[0, 0] = C_s[...]
        n_out[0, 0] = n_s[...]
        m_out[0, 0] = m_s[...]


def _mlstm_prompt(mqk, mv, gT, mo, w_conv, b_conv, m_norm_g):
    B, S, _ = mqk.shape
    L = 256
    nc = S // L
    MH, dk, dv = M_HEADS, M_QK_DIM, M_V_DIM
    b_conv = b_conv.reshape(1, -1)
    in_specs = [
        pl.BlockSpec((1, L, dk), lambda b, h, c: (b, c, h)),
        pl.BlockSpec((1, L, dk), lambda b, h, c: (b, c, MH + h)),
        pl.BlockSpec((1, L, dv), lambda b, h, c: (b, c, h)),
        pl.BlockSpec((1, 2 * MH, L), lambda b, h, c: (b, 0, c)),
        pl.BlockSpec((1, L, dv), lambda b, h, c: (b, c, h)),
        pl.BlockSpec((M_CONV, dk), lambda b, h, c: (0, h)),
        pl.BlockSpec((M_CONV, dk), lambda b, h, c: (0, MH + h)),
        pl.BlockSpec((1, dk), lambda b, h, c: (0, h)),
        pl.BlockSpec((1, dk), lambda b, h, c: (0, MH + h)),
        pl.BlockSpec((1, dv), lambda b, h, c: (0, h)),
    ]
    out_shape = [
        jax.ShapeDtypeStruct((B, S, M_WIDTH), F32),
        jax.ShapeDtypeStruct((B, MH, dv, dk), F32),
        jax.ShapeDtypeStruct((B, MH, 1, dk), F32),
        jax.ShapeDtypeStruct((B, MH, 1, V7X_LANES), F32),
    ]
    out_specs = [
        pl.BlockSpec((1, L, dv), lambda b, h, c: (b, c, h)),
        pl.BlockSpec((1, 1, dv, dk), lambda b, h, c: (b, h, 0, 0)),
        pl.BlockSpec((1, 1, 1, dk), lambda b, h, c: (b, h, 0, 0)),
        pl.BlockSpec((1, 1, 1, V7X_LANES), lambda b, h, c: (b, h, 0, 0)),
    ]
    scratch = [
        pltpu.VMEM((dv, dk), F32), pltpu.VMEM((1, dk), F32), pltpu.VMEM((1, V7X_LANES), F32),
        pltpu.VMEM((V7X_SUBLANES, dk), F32), pltpu.VMEM((V7X_SUBLANES, dk), F32),
    ]
    return pl.pallas_call(
        _mlstm_kernel,
        grid=(B, MH, nc),
        in_specs=in_specs,
        out_specs=out_specs,
        out_shape=out_shape,
        scratch_shapes=scratch,
        compiler_params=_params("arbitrary", "arbitrary", "arbitrary"),
        name="prompt_mlstm",
    )(mqk, mqk, mv, gT, mo, w_conv, w_conv, b_conv, b_conv, m_norm_g.reshape(1, -1))


def _merge_kernel(yaT_ref, az_ref, hm_ref, mz_ref, ga_ref, gm_ref, x_ref, mod_ref,
                  watt_ref, wml_ref, wout_ref, fg_ref, y_ref):
    ya = yaT_ref[0, 0].T * _silu(az_ref[0])
    ya = jnp.dot(ya.astype(BF16), watt_ref[...], preferred_element_type=F32)
    ym = hm_ref[0] * _silu(mz_ref[0])
    ym = jnp.dot(ym.astype(BF16), wml_ref[...], preferred_element_type=F32)
    u = jax.nn.sigmoid(ga_ref[0]) * ya + jax.nn.sigmoid(gm_ref[0]) * ym
    upd = jnp.dot(u.astype(BF16), wout_ref[...], preferred_element_type=F32)
    y = x_ref[0] + mod_ref[0, 2:3, :] * upd
    y = y * lax.rsqrt(jnp.mean(y * y, axis=-1, keepdims=True) + NORM_EPS)
    y_ref[0] = y * fg_ref[...]


def _merge_prompt(yaT, az, hm, mz, ga, gm, x, mod3, w_att, w_mlstm, w_out, final_g):
    B, S, D = x.shape
    tm = yaT.shape[3]
    nb = S // tm
    const = lambda b, j: (0, 0)
    tok = lambda w: pl.BlockSpec((1, tm, w), lambda b, j: (b, j, 0))
    return pl.pallas_call(
        _merge_kernel,
        grid=(B, nb),
        in_specs=[
            pl.BlockSpec((1, 1, ATT_WIDTH, tm), lambda b, j: (b, j, 0, 0)),
            tok(ATT_WIDTH), tok(M_WIDTH), tok(M_WIDTH), tok(D), tok(D), tok(D),
            pl.BlockSpec((1, 3, D), lambda b, j: (b, 0, 0)),
            pl.BlockSpec(w_att.shape, const),
            pl.BlockSpec(w_mlstm.shape, const),
            pl.BlockSpec(w_out.shape, const),
            pl.BlockSpec((1, D), const),
        ],
        out_specs=tok(D),
        out_shape=jax.ShapeDtypeStruct((B, S, D), F32),
        compiler_params=_params("arbitrary", "arbitrary"),
        name="prompt_merge",
    )(yaT, az, hm, mz, ga, gm, x, mod3, w_att, w_mlstm, w_out, final_g.reshape(1, D))


def _prompt_layer(x, mod3, norm_g, w_main, w_ifT, b_if, w_conv, b_conv, m_norm_g,
                  w_att, w_mlstm, w_out, final_g):
    B, S, _ = x.shape
    (qT, kh, vT, kf, vf, kmean, az, mqk, mv, mo, mz, ga, gm, gT) = _inproj(
        x, mod3, norm_g, w_main, w_ifT, b_if)
    yaT = _moba_prompt(qT, kh, vT, kmean)
    hm, C, n, m = _mlstm_prompt(mqk, mv, gT, mo, w_conv, b_conv, m_norm_g)
    y = _merge_prompt(yaT, az, hm, mz, ga, gm, x, mod3, w_att, w_mlstm, w_out, final_g)
    k_rows = kf.reshape(B, S, ATT_HEADS, ATT_HEAD_DIM)
    v_rows = vf.reshape(B, S, ATT_HEADS, ATT_HEAD_DIM)
    conv_new = mqk[:, S - (M_CONV - 1):, :]
    return y, k_rows, v_rows, C, n[:, :, 0, :], m[:, :, 0, 0], conv_new


def _inproj_sample_kernel(x_ref, mod_ref, ng_ref, w_ref, wif_ref, bif_ref, z_ref, g_ref):
    x = x_ref[...]
    d = x.shape[1]
    xn = x * lax.rsqrt(jnp.mean(x * x, axis=-1, keepdims=True) + NORM_EPS) * ng_ref[...]
    h = (xn * (1.0 + mod_ref[:, d:2 * d]) + mod_ref[:, 0:d]).astype(BF16)
    z_ref[...] = jnp.dot(h, w_ref[...], preferred_element_type=F32)
    g = lax.dot_general(h, wif_ref[...], (((1,), (1,)), ((), ())), preferred_element_type=F32)
    g_ref[...] = g + bif_ref[...]


def _inproj_sample(x, mod, norm_g, w_main, w_ifT, b_if):
    n, d = x.shape
    bn = 1024
    const = lambda i: (0, 0)
    return pl.pallas_call(
        _inproj_sample_kernel,
        grid=(w_main.shape[1] // bn,),
        in_specs=[pl.BlockSpec((n, d), const),
                  pl.BlockSpec(mod.shape, const),
                  pl.BlockSpec((1, d), const),
                  pl.BlockSpec((d, bn), lambda i: (0, i)),
                  pl.BlockSpec(w_ifT.shape, const),
                  pl.BlockSpec((1, 2 * M_HEADS), const)],
        out_specs=[pl.BlockSpec((n, bn), lambda i: (0, i)),
                   pl.BlockSpec((n, 2 * M_HEADS), const)],
        out_shape=[jax.ShapeDtypeStruct((n, w_main.shape[1]), F32),
                   jax.ShapeDtypeStruct((n, 2 * M_HEADS), F32)],
        compiler_params=_params("arbitrary"),
        name="sample_inproj",
    )(x, mod, norm_g.reshape(1, d), w_main, w_ifT, b_if.reshape(1, 2 * M_HEADS))


_PAGES_PER_CHUNK = 8
_PAGES_PER_BLOCK = MOBA_BLOCK // PAGE_SIZE


def _pagesum_kernel(pt_ref, ck_ref, out_ref, buf, sem):
    b = pl.program_id(0)
    nseq = pl.num_programs(0)
    n_pages = pt_ref.shape[1]
    cpb = n_pages // _PAGES_PER_CHUNK
    bpc = _PAGES_PER_CHUNK // _PAGES_PER_BLOCK

    def chunk_copy(bb, ci, i, slot):
        page = pt_ref[bb, ci * _PAGES_PER_CHUNK + i]
        return pltpu.make_async_copy(ck_ref.at[0, page], buf.at[slot, i], sem.at[slot])

    def start(bb, ci, slot):
        for i in range(_PAGES_PER_CHUNK):
            chunk_copy(bb, ci, i, slot).start()

    @pl.when(b == 0)
    def _():
        start(0, 0, 0)

    def body(ci, carry):
        slot = (b * cpb + ci) % 2
        nxt = ci + 1

        @pl.when(nxt < cpb)
        def _():
            start(b, nxt, 1 - slot)

        @pl.when((nxt == cpb) & (b + 1 < nseq))
        def _():
            start(b + 1, 0, 1 - slot)

        for i in range(_PAGES_PER_CHUNK):
            chunk_copy(b, ci, i, slot).wait()
        data = buf[slot]
        data = data.reshape(bpc, MOBA_BLOCK, data.shape[2], data.shape[3])
        out_ref[0, pl.ds(ci * bpc, bpc)] = jnp.sum(data, axis=1)
        return carry

    lax.fori_loop(0, cpb, body, 0)


def _page_block_sums(cache_k, page_table):
    nseq, n_pages = page_table.shape
    _, _, ps, H, dh = cache_k.shape
    nblk = n_pages // _PAGES_PER_BLOCK
    grid_spec = pltpu.PrefetchScalarGridSpec(
        num_scalar_prefetch=1,
        grid=(nseq,),
        in_specs=[pl.BlockSpec(memory_space=pl.ANY)],
        out_specs=pl.BlockSpec((1, nblk, H, dh), lambda b, pt: (b, 0, 0, 0)),
        scratch_shapes=[pltpu.VMEM((2, _PAGES_PER_CHUNK, ps, H, dh), F32),
                        pltpu.SemaphoreType.DMA((2,))],
    )
    return pl.pallas_call(
        _pagesum_kernel,
        grid_spec=grid_spec,
        out_shape=jax.ShapeDtypeStruct((nseq, nblk, H, dh), F32),
        compiler_params=_params("arbitrary"),
        name="sample_page_sums",
    )(page_table, cache_k)


def _gate_sample_kernel(sums_ref, q_ref, idx_ref):
    nblk = sums_ref.shape[1]
    lane = lax.broadcasted_iota(jnp.int32, (nblk, V7X_LANES), 1)
    sc = jnp.full((nblk, V7X_LANES), NEG_SCORE, F32)
    for hh in range(ATT_HEADS):
        km = sums_ref[0, :, hh, :] / MOBA_BLOCK
        col = jnp.sum(km * q_ref[0, hh:hh + 1, :], axis=1, keepdims=True)
        sc = jnp.where(lane == hh, col, sc)
    blk = lax.broadcasted_iota(jnp.int32, (nblk, V7X_LANES), 0)
    cnt = jnp.zeros((nblk, V7X_LANES), jnp.int32)
    for m in range(nblk):
        row = sc[m:m + 1, :]
        beats = (row > sc) | ((row == sc) & (blk > m))
        cnt += beats.astype(jnp.int32)
    out = jnp.zeros((V7X_SUBLANES, V7X_LANES), jnp.int32)
    r8 = lax.broadcasted_iota(jnp.int32, (V7X_SUBLANES, V7X_LANES), 0)
    for r in range(MOBA_TOPK):
        idx = jnp.sum(jnp.where(cnt == r, blk, 0), axis=0, keepdims=True)
        out = jnp.where(r8 == r, idx, out)
    idx_ref[0] = out


def _gate_sample(sums, q):
    nseq, nblk, H, dh = sums.shape
    return pl.pallas_call(
        _gate_sample_kernel,
        grid=(nseq,),
        in_specs=[pl.BlockSpec((1, nblk, H, dh), lambda b: (b, 0, 0, 0)),
                  pl.BlockSpec((1, H, dh), lambda b: (b, 0, 0))],
        out_specs=pl.BlockSpec((1, V7X_SUBLANES, V7X_LANES), lambda b: (b, 0, 0)),
        out_shape=jax.ShapeDtypeStruct((nseq, V7X_SUBLANES, V7X_LANES), jnp.int32),
        compiler_params=_params("arbitrary"),
        name="sample_gate",
    )(sums, q)


def _attend_sample_kernel(pt_ref, sel_ref, ck_ref, cv_ref, q_ref, kn_ref, vn_ref, o_ref,
                          kbuf, vbuf, sem):
    b = pl.program_id(0)
    nseq = pl.num_programs(0)
    H, dh = ATT_HEADS, ATT_HEAD_DIM
    npg = MOBA_TOPK * _PAGES_PER_BLOCK

    def copies(bb, slot):
        out = []
        for hh in range(H):
            for r in range(MOBA_TOPK):
                blk = sel_ref[(bb * MOBA_TOPK + r) * H + hh]
                for i in range(_PAGES_PER_BLOCK):
                    page = pt_ref[bb, blk * _PAGES_PER_BLOCK + i]
                    rows = pl.ds((r * _PAGES_PER_BLOCK + i) * PAGE_SIZE, PAGE_SIZE)
                    out.append(pltpu.make_async_copy(
                        ck_ref.at[0, page, :, hh, :], kbuf.at[slot, hh, rows, :], sem.at[0, slot]))
                    out.append(pltpu.make_async_copy(
                        cv_ref.at[0, page, :, hh, :], vbuf.at[slot, hh, rows, :], sem.at[1, slot]))
        return out

    @pl.when(b == 0)
    def _():
        for cp in copies(0, 0):
            cp.start()

    slot = b % 2

    @pl.when(b + 1 < nseq)
    def _():
        for cp in copies(b + 1, 1 - slot):
            cp.start()

    for cp in copies(b, slot):
        cp.wait()

    for hh in range(H):
        q8 = jnp.broadcast_to(q_ref[0, hh:hh + 1, :], (V7X_SUBLANES, dh)).astype(BF16)
        kk = kbuf[slot, hh].astype(BF16)
        vv = vbuf[slot, hh].astype(BF16)
        kn = kn_ref[0, hh:hh + 1, :].astype(BF16).astype(F32)
        vn = vn_ref[0, hh:hh + 1, :].astype(BF16).astype(F32)
        scale = ATT_HEAD_DIM ** -0.5
        s = lax.dot_general(q8, kk, (((1,), (1,)), ((), ())), preferred_element_type=F32)[0:1] * scale
        s_new = jnp.sum(q8[0:1].astype(F32) * kn, axis=1, keepdims=True) * scale
        m = jnp.maximum(jnp.max(s, axis=1, keepdims=True), s_new)
        p = jnp.exp(s - m)
        p_new = jnp.exp(s_new - m)
        l = jnp.sum(p, axis=1, keepdims=True) + p_new
        p8 = jnp.broadcast_to(p, (V7X_SUBLANES, p.shape[1])).astype(BF16)
        acc = jnp.dot(p8, vv, preferred_element_type=F32)[0:1]
        acc = acc + p_new.astype(BF16).astype(F32) * vn
        o_ref[0, :, hh * dh:(hh + 1) * dh] = acc / l


def _attend_sample(cache_k, cache_v, page_table, sel_flat, q, k_new, v_new):
    nseq, n_pages = page_table.shape
    H, dh = ATT_HEADS, ATT_HEAD_DIM
    rows = MOBA_TOPK * MOBA_BLOCK
    grid_spec = pltpu.PrefetchScalarGridSpec(
        num_scalar_prefetch=2,
        grid=(nseq,),
        in_specs=[pl.BlockSpec(memory_space=pl.ANY),
                  pl.BlockSpec(memory_space=pl.ANY),
                  pl.BlockSpec((1, H, dh), lambda b, pt, sel: (b, 0, 0)),
                  pl.BlockSpec((1, H, dh), lambda b, pt, sel: (b, 0, 0)),
                  pl.BlockSpec((1, H, dh), lambda b, pt, sel: (b, 0, 0))],
        out_specs=pl.BlockSpec((1, 1, ATT_WIDTH), lambda b, pt, sel: (b, 0, 0)),
        scratch_shapes=[pltpu.VMEM((2, H, rows, dh), F32),
                        pltpu.VMEM((2, H, rows, dh), F32),
                        pltpu.SemaphoreType.DMA((2, 2))],
    )
    return pl.pallas_call(
        _attend_sample_kernel,
        grid_spec=grid_spec,
        out_shape=jax.ShapeDtypeStruct((nseq, 1, ATT_WIDTH), F32),
        compiler_params=_params("arbitrary"),
        name="sample_attend",
    )(page_table, sel_flat, cache_k, cache_v, q, k_new, v_new)


def _mlstm_sample_kernel(qk_ref, v_ref, o_ref, g_ref, conv_ref, C_ref, n_ref, wc_ref, bc_ref, ng_ref,
                         h_ref, C_out, n_out, m_out, conv_out):
    x = qk_ref[0]
    conv = conv_ref[0]
    acc = x * wc_ref[M_CONV - 1:M_CONV, :] + bc_ref[...]
    for jj in range(M_CONV - 1):
        acc += conv[jj:jj + 1, :] * wc_ref[jj:jj + 1, :]
    qk = _silu(acc)
    conv_out[0] = jnp.concatenate([conv[1:], x], axis=0)
    g = g_ref[0]
    m_new = []
    for hh in range(M_HEADS):
        q = qk[:, hh * M_QK_DIM:(hh + 1) * M_QK_DIM]
        k = qk[:, M_QK_WIDTH + hh * M_QK_DIM:M_QK_WIDTH + (hh + 1) * M_QK_DIM] * (M_QK_DIM ** -0.5)
        qb = q.astype(BF16)
        kb = k.astype(BF16)
        v = v_ref[0][:, hh * M_V_DIM:(hh + 1) * M_V_DIM]
        i_pre = g[:, hh:hh + 1]
        f_pre = g[:, M_HEADS + hh:M_HEADS + hh + 1]
        m_prev = g[:, 2 * M_HEADS + hh:2 * M_HEADS + hh + 1]
        bb = jnp.minimum(f_pre, 0.0) - jnp.log1p(jnp.exp(-jnp.abs(f_pre)))
        cc = i_pre - bb
        M = jnp.maximum(m_prev, cc)
        w = jnp.exp(cc - M)
        a = jnp.exp(m_prev - M)
        C = C_ref[0, hh]
        n = n_ref[0, hh:hh + 1, :]
        qf = qb.astype(F32)
        s = jnp.sum(qf * kb.astype(F32), axis=1, keepdims=True) * w
        q8 = jnp.broadcast_to(qb, (V7X_SUBLANES, M_QK_DIM))
        inter = lax.dot_general(q8, C.astype(BF16), (((1,), (1,)), ((), ())),
                                preferred_element_type=F32)[0:1]
        vb = v.astype(BF16).astype(F32)
        num = a * inter + s.astype(BF16).astype(F32) * vb
        den = a * jnp.sum(qf * n.astype(BF16).astype(F32), axis=1, keepdims=True) + s
        hv = num / jnp.maximum(jnp.abs(den), jnp.exp(-(bb + M)))
        hv = hv * lax.rsqrt(jnp.mean(hv * hv, axis=-1, keepdims=True) + NORM_EPS)
        sl = slice(hh * M_V_DIM, (hh + 1) * M_V_DIM)
        h_ref[0, :, sl] = hv * ng_ref[:, sl] * jax.nn.sigmoid(o_ref[0][:, sl])
        wv = jnp.broadcast_to(w * v, (V7X_SUBLANES, M_V_DIM))
        wv = jnp.concatenate([wv, jnp.zeros((V7X_LANES - V7X_SUBLANES, M_V_DIM), F32)], axis=0)
        wv_col = wv.T[:, 0:1]
        C_out[0, hh] = a * C + wv_col * k
        n_out[0, hh:hh + 1, :] = a * n + w * k
        m_new.append(bb + M)
    lane = lax.broadcasted_iota(jnp.int32, (1, V7X_LANES), 1)
    mrow = jnp.zeros((1, V7X_LANES), F32)
    for hh in range(M_HEADS):
        mrow = jnp.where(lane == hh, m_new[hh], mrow)
    m_out[0] = mrow


def _mlstm_sample(z3, g16, state_conv, state_C, state_n, w_conv, b_conv, m_norm_g):
    nseq = z3.shape[0]
    MH, dk, dv = M_HEADS, M_QK_DIM, M_V_DIM
    W = 2 * M_QK_WIDTH
    const = lambda b: (0, 0)
    return pl.pallas_call(
        _mlstm_sample_kernel,
        grid=(nseq,),
        in_specs=[
            pl.BlockSpec((1, 1, W), lambda b: (b, 0, _C_MQK // W)),
            pl.BlockSpec((1, 1, M_WIDTH), lambda b: (b, 0, _C_MV // M_WIDTH)),
            pl.BlockSpec((1, 1, M_WIDTH), lambda b: (b, 0, _C_MO // M_WIDTH)),
            pl.BlockSpec((1, 1, 4 * MH), lambda b: (b, 0, 0)),
            pl.BlockSpec((1, M_CONV - 1, W), lambda b: (b, 0, 0)),
            pl.BlockSpec((1, MH, dv, dk), lambda b: (b, 0, 0, 0)),
            pl.BlockSpec((1, MH, dk), lambda b: (b, 0, 0)),
            pl.BlockSpec((M_CONV, W), const),
            pl.BlockSpec((1, W), const),
            pl.BlockSpec((1, M_WIDTH), const),
        ],
        out_specs=[
            pl.BlockSpec((1, 1, M_WIDTH), lambda b: (b, 0, 0)),
            pl.BlockSpec((1, MH, dv, dk), lambda b: (b, 0, 0, 0)),
            pl.BlockSpec((1, MH, dk), lambda b: (b, 0, 0)),
            pl.BlockSpec((1, 1, V7X_LANES), lambda b: (b, 0, 0)),
            pl.BlockSpec((1, M_CONV - 1, W), lambda b: (b, 0, 0)),
        ],
        out_shape=[
            jax.ShapeDtypeStruct((nseq, 1, M_WIDTH), F32),
            jax.ShapeDtypeStruct((nseq, MH, dv, dk), F32),
            jax.ShapeDtypeStruct((nseq, MH, dk), F32),
            jax.ShapeDtypeStruct((nseq, 1, V7X_LANES), F32),
            jax.ShapeDtypeStruct((nseq, M_CONV - 1, W), F32),
        ],
        compiler_params=_params("arbitrary"),
        name="sample_mlstm",
    )(z3, z3, z3, g16, state_conv, state_C, state_n, w_conv, b_conv.reshape(1, W),
      m_norm_g.reshape(1, M_WIDTH))


def _merge_sample_kernel(ya_ref, hm_ref, z_ref, x_ref, mod_ref, watt_ref, wml_ref, wout_ref, fg_ref,
                         y_ref):
    d = x_ref.shape[1]
    ya = ya_ref[...] * _silu(z_ref[:, _C_AZ:_C_MQK])
    ya = jnp.dot(ya.astype(BF16), watt_ref[...], preferred_element_type=F32)
    ym = hm_ref[...] * _silu(z_ref[:, _C_MZ:_C_MIF])
    ym = jnp.dot(ym.astype(BF16), wml_ref[...], preferred_element_type=F32)
    u = jax.nn.sigmoid(z_ref[:, _P_GA:_P_GM]) * ya + jax.nn.sigmoid(z_ref[:, _P_GM:_P_END]) * ym
    upd = jnp.dot(u.astype(BF16), wout_ref[...], preferred_element_type=F32)
    y = x_ref[...] + mod_ref[:, 2 * d:3 * d] * upd
    y = y * lax.rsqrt(jnp.mean(y * y, axis=-1, keepdims=True) + NORM_EPS)
    y_ref[...] = y * fg_ref[...]


def _merge_sample(ya, hm, z, x, mod, w_att, w_mlstm, w_out, final_g):
    n, d = x.shape
    full = lambda a: pl.BlockSpec(a.shape, lambda i: (0,) * a.ndim)
    fg = final_g.reshape(1, d)
    args = (ya, hm, z, x, mod, w_att, w_mlstm, w_out, fg)
    return pl.pallas_call(
        _merge_sample_kernel,
        grid=(1,),
        in_specs=[full(a) for a in args],
        out_specs=pl.BlockSpec((n, d), lambda i: (0, 0)),
        out_shape=jax.ShapeDtypeStruct((n, d), F32),
        compiler_params=_params("arbitrary"),
        name="sample_merge",
    )(*args)


def _sample_layer(x, mod, cache_k, cache_v, page_table, state_C, state_n, state_m, state_conv,
                  norm_g, w_main, w_ifT, b_if, w_conv, b_conv, m_norm_g, w_att, w_mlstm, w_out, final_g):
    nseq = x.shape[0]
    H, dh = ATT_HEADS, ATT_HEAD_DIM
    assert page_table.shape[1] // _PAGES_PER_BLOCK >= MOBA_TOPK
    z, g = _inproj_sample(x, mod, norm_g, w_main, w_ifT, b_if)
    q = z[:, _C_AQ:_C_AK].reshape(nseq, H, dh)
    k_new = z[:, _C_AK:_C_AV].reshape(nseq, H, dh)
    v_new = z[:, _C_AV:_C_AZ].reshape(nseq, H, dh)
    sums = _page_block_sums(cache_k, page_table)
    sel = _gate_sample(sums, q)
    sel_flat = sel[:, :MOBA_TOPK, :H].reshape(-1)
    ya = _attend_sample(cache_k, cache_v, page_table, sel_flat, q, k_new, v_new)
    g16 = jnp.concatenate([g, state_m, jnp.zeros_like(state_m)], axis=1).reshape(nseq, 1, 4 * M_HEADS)
    hm, C, n, m, conv_new = _mlstm_sample(z.reshape(nseq, 1, -1), g16, state_conv, state_C, state_n,
                                          w_conv, b_conv, m_norm_g)
    y = _merge_sample(ya.reshape(nseq, ATT_WIDTH), hm.reshape(nseq, M_WIDTH), z, x, mod,
                      w_att, w_mlstm, w_out, final_g)
    return (y.reshape(nseq, 1, -1), k_new.reshape(nseq, 1, H, dh), v_new.reshape(nseq, 1, H, dh),
            C, n, m[:, 0, :M_HEADS], conv_new)


def kernel(x_prompt, x_sample, cache_k, cache_v, state_C, state_n, state_m, state_conv, page_table,
           c_prompt, c_sample, w_ada, b_ada, norm_g, w_in, b_if, w_conv, b_conv, m_norm_g,
           w_att, w_mlstm, w_out, final_g):
    depth = w_in.shape[0]
    assert depth == 1, "single-layer step"
    B = x_prompt.shape[0]
    nseq = x_sample.shape[0]
    pad = (-(B + nseq)) % V7X_SUBLANES
    c_all = jnp.concatenate([c_prompt, c_sample, jnp.zeros((pad, c_prompt.shape[1]), F32)], axis=0)
    mod = _ada(c_all, w_ada[0], b_ada[0])
    mod_p = mod[:B].reshape(B, 3, D_MODEL)
    mod_s = mod[B:B + nseq]
    w_main, w_ifT = _pack_w_in(w_in[0])
    wa, wm, wo = w_att[0].astype(BF16), w_mlstm[0].astype(BF16), w_out[0].astype(BF16)
    outs_p = _prompt_layer(x_prompt, mod_p, norm_g[0], w_main, w_ifT, b_if[0], w_conv[0], b_conv[0],
                           m_norm_g[0], wa, wm, wo, final_g)
    outs_s = _sample_layer(x_sample[:, 0, :], mod_s, cache_k, cache_v, page_table,
                           state_C[0], state_n[0], state_m[0], state_conv[0],
                           norm_g[0], w_main, w_ifT, b_if[0], w_conv[0], b_conv[0], m_norm_g[0],
                           wa, wm, wo, final_g)
    y_p, *st_p = outs_p
    y_s, *st_s = outs_s
    return (y_p, y_s) + tuple(a[None] for a in st_p) + tuple(a[None] for a in st_s)
```

```python
import functools

import jax
import jax.numpy as jnp
from jax import lax
from jax.experimental import pallas as pl
from jax.experimental.pallas import tpu as pltpu

F32 = jnp.float32
BF16 = jnp.bfloat16

D_MODEL = 1024
ATT_HEADS = 8
ATT_HEAD_DIM = 64
ATT_WIDTH = ATT_HEADS * ATT_HEAD_DIM
MOBA_BLOCK = 256
MOBA_TOPK = 3
M_HEADS = 4
M_V_DIM = 256
M_QK_DIM = 128
M_QK_WIDTH = M_HEADS * M_QK_DIM
M_WIDTH = M_HEADS * M_V_DIM
M_CONV = 4
PAGE_SIZE = 128
NORM_EPS = 1e-6
NEG_SCORE = -1e30
_LOG2E = 1.4426950408889634

V7X_LANES = 128
V7X_SUBLANES = 8
V7X_VMEM_LIMIT_BYTES = 56 * 1024 * 1024

_C_AQ, _C_AK, _C_AV, _C_AZ = 0, 512, 1024, 1536
_C_MQK, _C_MV, _C_MO, _C_MZ = 2048, 3072, 4096, 5120
_C_MIF, _C_GA, _C_GM, _C_END = 6144, 6152, 7176, 8200
_P_GA, _P_GM, _P_END = 6144, 7168, 8192


def _silu(x):
    return x * jax.nn.sigmoid(x)


def _split3(x):
    hi = x.astype(BF16)
    r1 = x - hi.astype(F32)
    mid = r1.astype(BF16)
    lo = (r1 - mid.astype(F32)).astype(BF16)
    return hi, mid, lo


def _pack_w_in(w_in):
    w_main = jnp.concatenate([w_in[:, :_C_MIF], w_in[:, _C_GA:_C_END]], axis=1).astype(BF16)
    w_ifT = w_in[:, _C_MIF:_C_GA].T.astype(BF16)
    return w_main, w_ifT


def _params(*sem):
    return pltpu.CompilerParams(dimension_semantics=sem, vmem_limit_bytes=V7X_VMEM_LIMIT_BYTES)


def _ada_kernel(c_ref, w_ref, b_ref, o_ref):
    a = _silu(c_ref[...])
    a_hi = a.astype(BF16)
    a_lo = (a - a_hi.astype(F32)).astype(BF16)
    w = w_ref[...]
    w_hi = w.astype(BF16)
    w_lo = (w - w_hi.astype(F32)).astype(BF16)
    acc = jnp.dot(a_hi, w_hi, preferred_element_type=F32)
    acc += jnp.dot(a_hi, w_lo, preferred_element_type=F32)
    acc += jnp.dot(a_lo, w_hi, preferred_element_type=F32)
    o_ref[...] = acc + b_ref[...]


def _ada(c, w_ada, b_ada):
    n, d = c.shape
    d3 = w_ada.shape[1]
    bn = 512
    return pl.pallas_call(
        _ada_kernel,
        grid=(d3 // bn,),
        in_specs=[pl.BlockSpec((n, d), lambda i: (0, 0)),
                  pl.BlockSpec((d, bn), lambda i: (0, i)),
                  pl.BlockSpec((1, bn), lambda i: (0, i))],
        out_specs=pl.BlockSpec((n, bn), lambda i: (0, i)),
        out_shape=jax.ShapeDtypeStruct((n, d3), F32),
        compiler_params=_params("arbitrary"),
        name="ada_mod",
    )(c, w_ada, b_ada.reshape(1, d3))


def _inproj_kernel(x_ref, mod_ref, ng_ref, w_ref, wif_ref, bif_ref, wifn_ref, bifn_ref,
                   qT_ref, kh_ref, vT_ref, kf_ref, vf_ref, kmean_ref,
                   az_ref, mqk_ref, mv_ref, mo_ref, mz_ref, ga_ref, gm_ref, gT_ref, mvT_ref, gc_ref):
    j = pl.program_id(1)
    x = x_ref[0]
    xn = x * lax.rsqrt(jnp.mean(x * x, axis=-1, keepdims=True) + NORM_EPS) * ng_ref[...]
    shift = mod_ref[0, 0:1, :]
    scale = mod_ref[0, 1:2, :]
    h = (xn * (1.0 + scale) + shift).astype(BF16)

    def proj(a, b):
        return jnp.dot(h, w_ref[:, a:b], preferred_element_type=F32)

    q = proj(_C_AQ, _C_AK) * (ATT_HEAD_DIM ** -0.5 * _LOG2E)
    qT_ref[0, 0] = q.T.astype(BF16)
    k = proj(_C_AK, _C_AV)
    kf_ref[0] = k
    ksum = jnp.sum(k, axis=0, keepdims=True) / MOBA_BLOCK
    for hh in range(ATT_HEADS):
        sl = slice(hh * ATT_HEAD_DIM, (hh + 1) * ATT_HEAD_DIM)
        kh_ref[0, 0, hh] = k[:, sl].astype(BF16)
        kmean_ref[0, hh, pl.ds(j, 1), :] = ksum[:, sl]
    v = proj(_C_AV, _C_AZ)
    vf_ref[0] = v
    vT_ref[0, 0] = v.T.astype(BF16)
    az_ref[0] = proj(_C_AZ, _C_MQK)
    mqk_ref[0] = proj(_C_MQK, _C_MV)
    mv = proj(_C_MV, _C_MO)
    mv_ref[0] = mv.astype(BF16)
    mvT_ref[0, 0] = mv.T.astype(BF16)
    mo_ref[0] = proj(_C_MO, _C_MZ)
    mz_ref[0] = proj(_C_MZ, _C_MIF)
    ga_ref[0] = proj(_P_GA, _P_GM)
    gm_ref[0] = proj(_P_GM, _P_END)
    gT = lax.dot_general(wif_ref[...], h, (((1,), (1,)), ((), ())), preferred_element_type=F32)
    gT_ref[0] = gT + bif_ref[...]
    gc_ref[0] = jnp.dot(h, wifn_ref[...], preferred_element_type=F32) + bifn_ref[...]


def _inproj(x, mod3, norm_g, w_main, w_ifT, b_if):
    B, S, D = x.shape
    tm = MOBA_BLOCK
    nb = S // tm
    H, dh = ATT_HEADS, ATT_HEAD_DIM
    const = lambda b, j: (0, 0)
    tok = lambda w: pl.BlockSpec((1, tm, w), lambda b, j: (b, j, 0))
    blk = pl.BlockSpec((1, 1, ATT_WIDTH, tm), lambda b, j: (b, j, 0, 0))
    out_shape = [
        jax.ShapeDtypeStruct((B, nb, ATT_WIDTH, tm), BF16),
        jax.ShapeDtypeStruct((B, nb, H, tm, dh), BF16),
        jax.ShapeDtypeStruct((B, nb, ATT_WIDTH, tm), BF16),
        jax.ShapeDtypeStruct((B, S, ATT_WIDTH), F32),
        jax.ShapeDtypeStruct((B, S, ATT_WIDTH), F32),
        jax.ShapeDtypeStruct((B, H, nb, dh), F32),
        jax.ShapeDtypeStruct((B, S, ATT_WIDTH), F32),
        jax.ShapeDtypeStruct((B, S, 2 * M_QK_WIDTH), F32),
        jax.ShapeDtypeStruct((B, S, M_WIDTH), BF16),
        jax.ShapeDtypeStruct((B, S, M_WIDTH), F32),
        jax.ShapeDtypeStruct((B, S, M_WIDTH), F32),
        jax.ShapeDtypeStruct((B, S, D), F32),
        jax.ShapeDtypeStruct((B, S, D), F32),
        jax.ShapeDtypeStruct((B, 2 * M_HEADS, S), F32),
        jax.ShapeDtypeStruct((B, nb, M_WIDTH, tm), BF16),
        jax.ShapeDtypeStruct((B, S, 2 * M_HEADS), F32),
    ]
    out_specs = [
        blk,
        pl.BlockSpec((1, 1, H, tm, dh), lambda b, j: (b, j, 0, 0, 0)),
        blk,
        tok(ATT_WIDTH), tok(ATT_WIDTH),
        pl.BlockSpec((1, H, nb, dh), lambda b, j: (b, 0, 0, 0)),
        tok(ATT_WIDTH), tok(2 * M_QK_WIDTH), tok(M_WIDTH), tok(M_WIDTH), tok(M_WIDTH), tok(D), tok(D),
        pl.BlockSpec((1, 2 * M_HEADS, tm), lambda b, j: (b, 0, j)),
        pl.BlockSpec((1, 1, M_WIDTH, tm), lambda b, j: (b, j, 0, 0)),
        tok(2 * M_HEADS),
    ]
    in_specs = [
        pl.BlockSpec((1, tm, D), lambda b, j: (b, j, 0)),
        pl.BlockSpec((1, 3, D), lambda b, j: (b, 0, 0)),
        pl.BlockSpec((1, D), const),
        pl.BlockSpec(w_main.shape, const, pipeline_mode=pl.Buffered(1)),
        pl.BlockSpec(w_ifT.shape, const),
        pl.BlockSpec((2 * M_HEADS, 1), const),
        pl.BlockSpec((D, 2 * M_HEADS), const),
        pl.BlockSpec((1, 2 * M_HEADS), const),
    ]
    return pl.pallas_call(
        _inproj_kernel,
        grid=(B, nb),
        in_specs=in_specs,
        out_specs=out_specs,
        out_shape=out_shape,
        compiler_params=_params("arbitrary", "arbitrary"),
        name="prompt_inproj",
    )(x, mod3, norm_g.reshape(1, D), w_main, w_ifT, b_if.reshape(2 * M_HEADS, 1),
      w_ifT.T, b_if.reshape(1, 2 * M_HEADS))


_MOBA_HEADS_PER_STEP = 8


def _moba_kernel(qT_ref, kh_ref, vT_ref, kmean_ref, o_ref, bias_ref):
    j = pl.program_id(2)
    hp = kmean_ref.shape[1]
    nb = kmean_ref.shape[2]
    tq = qT_ref.shape[3]
    dh = ATT_HEAD_DIM
    qTs = [qT_ref[0, 0, hh * dh:(hh + 1) * dh, :] for hh in range(hp)]

    blk = lax.broadcasted_iota(jnp.int32, (nb, tq), 0)
    past = blk < j
    for hh in range(hp):
        sg = sum(jnp.dot(part, qTs[hh], preferred_element_type=F32)
                 for part in _split3(kmean_ref[0, hh]))
        sg = jnp.where(past, sg, NEG_SCORE)
        cnt = jnp.zeros((nb, tq), jnp.int32)
        for m in range(nb):
            row = sg[m:m + 1, :]
            beats = (row > sg) | ((row == sg) & (blk > m))
            cnt += beats.astype(jnp.int32)
        bias_ref[hh] = jnp.where((cnt < MOBA_TOPK) & past, 0.0, NEG_SCORE)

    kpos = lax.broadcasted_iota(jnp.int32, (tq, tq), 0)
    qpos = lax.broadcasted_iota(jnp.int32, (tq, tq), 1)
    causal = kpos <= qpos

    def scores(n):
        return [jnp.dot(kh_ref[0, n, hh], qTs[hh], preferred_element_type=F32) for hh in range(hp)]

    def values(n, ps):
        return [jnp.dot(vT_ref[0, n, hh * dh:(hh + 1) * dh, :], ps[hh], preferred_element_type=F32)
                for hh in range(hp)]

    ss = [jnp.where(causal, s, NEG_SCORE) for s in scores(j)]
    ms = [jnp.max(s, axis=0, keepdims=True) for s in ss]
    ps = [jnp.exp2(s - m) for s, m in zip(ss, ms)]
    ls = [jnp.sum(p, axis=0, keepdims=True) for p in ps]
    accs = values(j, [p.astype(BF16) for p in ps])
    carry0 = tuple(ms) + tuple(ls) + tuple(accs)

    def body(n, carry):
        ms, ls, accs = carry[:hp], carry[hp:2 * hp], carry[2 * hp:]
        ss = [s + bias_ref[hh, pl.ds(n, 1), :] for hh, s in enumerate(scores(n))]
        m_new = [jnp.maximum(m, jnp.max(s, axis=0, keepdims=True)) for m, s in zip(ms, ss)]
        alphas = [jnp.exp2(m - mn) for m, mn in zip(ms, m_new)]
        ps = [jnp.exp2(s - mn) for s, mn in zip(ss, m_new)]
        ls = [a * l + jnp.sum(p, axis=0, keepdims=True) for a, l, p in zip(alphas, ls, ps)]
        pv = values(n, [p.astype(BF16) for p in ps])
        accs = [a * acc + x for a, acc, x in zip(alphas, accs, pv)]
        return tuple(m_new) + tuple(ls) + tuple(accs)

    fin = lax.fori_loop(0, j, body, carry0)
    fin = [x for hh in range(hp) for x in (fin[hh], fin[hp + hh], fin[2 * hp + hh])]
    for hh in range(hp):
        o_ref[0, 0, hh * dh:(hh + 1) * dh, :] = fin[3 * hh + 2] / fin[3 * hh + 1]


def _moba_prompt(qT, kh, vT, kmean):
    B, nb, _, tq = qT.shape
    H, dh, hp = ATT_HEADS, ATT_HEAD_DIM, _MOBA_HEADS_PER_STEP
    return pl.pallas_call(
        _moba_kernel,
        grid=(B, H // hp, nb),
        in_specs=[
            pl.BlockSpec((1, 1, hp * dh, tq), lambda b, g, j: (b, j, g, 0)),
            pl.BlockSpec((1, nb, hp, tq, dh), lambda b, g, j: (b, 0, g, 0, 0)),
            pl.BlockSpec((1, nb, hp * dh, tq), lambda b, g, j: (b, 0, g, 0)),
            pl.BlockSpec((1, hp, nb, dh), lambda b, g, j: (b, g, 0, 0)),
        ],
        out_specs=pl.BlockSpec((1, 1, hp * dh, tq), lambda b, g, j: (b, j, g, 0)),
        out_shape=jax.ShapeDtypeStruct((B, nb, ATT_WIDTH, tq), F32),
        scratch_shapes=[pltpu.VMEM((hp, nb, tq), F32)],
        compiler_params=_params("arbitrary", "arbitrary", "arbitrary"),
        name="prompt_moba",
    )(qT, kh, vT, kmean)


def _log_sigmoid(x):
    return jnp.minimum(x, 0.0) - jnp.log1p(jnp.exp(-jnp.abs(x)))


def _causal_conv(x, tail_ref, w_ref, b_ref):
    L = x.shape[0]
    prev = tail_ref[...]
    row = lax.broadcasted_iota(jnp.int32, (V7X_SUBLANES, x.shape[1]), 0)
    acc = x * w_ref[M_CONV - 1:M_CONV, :] + b_ref[...]
    for d in range(1, M_CONV):
        r = pltpu.roll(x, d, axis=0)
        head = jnp.where(row < d, pltpu.roll(prev, d, axis=0), r[:V7X_SUBLANES])
        shifted = jnp.concatenate([head, r[V7X_SUBLANES:]], axis=0)
        acc += shifted * w_ref[M_CONV - 1 - d:M_CONV - d, :]
    tail_ref[...] = x[L - V7X_SUBLANES:, :]
    return acc


def _mlstm_kernel(qk_ref, v_ref, vT_ref, g_ref, gc_ref, o_ref, wc_ref, bc_ref, ng_ref,
                  h_ref, C_out, n_out, m_out,
                  C_s, n_s, m_s, tail_s):
    c = pl.program_id(1)
    L = qk_ref.shape[1]
    MH, dk, dv = M_HEADS, M_QK_DIM, M_V_DIM
    nt = (((1,), (1,)), ((), ()))

    @pl.when(c == 0)
    def _():
        C_s[...] = jnp.zeros_like(C_s)
        n_s[...] = jnp.zeros_like(n_s)
        m_s[...] = jnp.zeros_like(m_s)
        tail_s[...] = jnp.zeros_like(tail_s)

    qk = _silu(_causal_conv(qk_ref[0], tail_s, wc_ref, bc_ref))

    t_idx = lax.broadcasted_iota(jnp.int32, (L, L), 0)
    s_idx = lax.broadcasted_iota(jnp.int32, (L, L), 1)
    causal = s_idx <= t_idx
    tri = jnp.where(causal, 1.0, 0.0).astype(BF16)
    g_rows = g_ref[0]
    b_rows = sum(lax.dot_general(part, tri, nt, preferred_element_type=F32)
                 for part in _split3(_log_sigmoid(g_rows)))
    b_cols = sum(jnp.dot(tri, part, preferred_element_type=F32)
                 for part in _split3(_log_sigmoid(gc_ref[0])))

    for hd in range(MH):
        qb = qk[:, hd * dk:(hd + 1) * dk].astype(BF16)
        kb = (qk[:, M_QK_WIDTH + hd * dk:M_QK_WIDTH + (hd + 1) * dk] * (dk ** -0.5)).astype(BF16)
        vsl = slice(hd * dv, (hd + 1) * dv)
        b_row = b_rows[MH + hd:MH + hd + 1]
        c_row = g_rows[hd:hd + 1] - b_row
        b_col = b_cols[:, MH + hd:MH + hd + 1]
        m_prev = m_s[hd:hd + 1, 0:1]
        M_col = jnp.maximum(jnp.max(jnp.where(causal, c_row, NEG_SCORE), axis=1, keepdims=True),
                            m_prev)
        M_last = jnp.maximum(jnp.max(c_row, axis=1, keepdims=True), m_prev)
        w = jnp.where(causal, jnp.exp(c_row - M_col), 0.0)
        a_col = jnp.exp(m_prev - M_col)

        s = lax.dot_general(qb, kb, nt, preferred_element_type=F32) * w
        inter = lax.dot_general(qb, C_s[hd].astype(BF16), nt, preferred_element_type=F32)
        num = a_col * inter + jnp.dot(s.astype(BF16), v_ref[0, :, vsl], preferred_element_type=F32)
        n_b = n_s[hd:hd + 1, :].astype(BF16).astype(F32)
        qn = jnp.sum(qb.astype(F32) * n_b, axis=1, keepdims=True)
        den = a_col * qn + jnp.sum(s, axis=1, keepdims=True)
        hh = num / jnp.maximum(jnp.abs(den), jnp.exp(-(b_col + M_col)))
        hh = hh * lax.rsqrt(jnp.mean(hh * hh, axis=-1, keepdims=True) + NORM_EPS)
        h_ref[0, :, vsl] = hh * ng_ref[:, vsl] * jax.nn.sigmoid(o_ref[0, :, vsl])

        ws_row = jnp.exp(c_row - M_last)
        a_end = jnp.exp(m_prev - M_last)
        vwT = (vT_ref[0, 0, vsl, :].astype(F32) * ws_row).astype(BF16)
        C_s[hd] = a_end * C_s[hd] + jnp.dot(vwT, kb, preferred_element_type=F32)
        ws8 = jnp.broadcast_to(ws_row, (V7X_SUBLANES, L)).astype(BF16)
        n_s[hd:hd + 1, :] = (a_end * n_s[hd:hd + 1, :]
                             + jnp.dot(ws8, kb, preferred_element_type=F32)[0:1])
        m_s[hd:hd + 1, :] = jnp.broadcast_to(b_row[:, L - 1:L] + M_last, (1, m_s.shape[1]))

    @pl.when(c == pl.num_programs(1) - 1)
    def _():
        C_out[0] = C_s[...]
        n_out[0] = n_s[...]
        m_out[0] = m_s[...]


def _mlstm_prompt(mqk, mv, mvT, gT, gc, mo, w_conv, b_conv, m_norm_g):
    B, S, W = mqk.shape
    L = mvT.shape[3]
    nc = S // L
    MH, dk, dv = M_HEADS, M_QK_DIM, M_V_DIM
    const = lambda b, c: (0, 0)
    tok = lambda w: pl.BlockSpec((1, L, w), lambda b, c: (b, c, 0))
    in_specs = [
        tok(W), tok(M_WIDTH),
        pl.BlockSpec((1, 1, M_WIDTH, L), lambda b, c: (b, c, 0, 0)),
        pl.BlockSpec((1, 2 * MH, L), lambda b, c: (b, 0, c)),
        tok(2 * MH), tok(M_WIDTH),
        pl.BlockSpec((M_CONV, W), const),
        pl.BlockSpec((1, W), const),
        pl.BlockSpec((1, M_WIDTH), const),
    ]
    out_shape = [
        jax.ShapeDtypeStruct((B, S, M_WIDTH), F32),
        jax.ShapeDtypeStruct((B, MH, dv, dk), F32),
        jax.ShapeDtypeStruct((B, V7X_SUBLANES, dk), F32),
        jax.ShapeDtypeStruct((B, V7X_SUBLANES, V7X_LANES), F32),
    ]
    out_specs = [
        tok(M_WIDTH),
        pl.BlockSpec((1, MH, dv, dk), lambda b, c: (b, 0, 0, 0)),
        pl.BlockSpec((1, V7X_SUBLANES, dk), lambda b, c: (b, 0, 0)),
        pl.BlockSpec((1, V7X_SUBLANES, V7X_LANES), lambda b, c: (b, 0, 0)),
    ]
    scratch = [
        pltpu.VMEM((MH, dv, dk), F32), pltpu.VMEM((V7X_SUBLANES, dk), F32),
        pltpu.VMEM((V7X_SUBLANES, V7X_LANES), F32), pltpu.VMEM((V7X_SUBLANES, W), F32),
    ]
    return pl.pallas_call(
        _mlstm_kernel,
        grid=(B, nc),
        in_specs=in_specs,
        out_specs=out_specs,
        out_shape=out_shape,
        scratch_shapes=scratch,
        compiler_params=_params("arbitrary", "arbitrary"),
        name="prompt_mlstm",
    )(mqk, mv, mvT, gT, gc, mo, w_conv, b_conv.reshape(1, W), m_norm_g.reshape(1, M_WIDTH))


def _merge_kernel(yaT_ref, az_ref, hm_ref, mz_ref, ga_ref, gm_ref, x_ref, mod_ref,
                  watt_ref, wml_ref, wout_ref, fg_ref, y_ref):
    ya = yaT_ref[0, 0].T * _silu(az_ref[0])
    ya = jnp.dot(ya.astype(BF16), watt_ref[...], preferred_element_type=F32)
    ym = hm_ref[0] * _silu(mz_ref[0])
    ym = jnp.dot(ym.astype(BF16), wml_ref[...], preferred_element_type=F32)
    u = jax.nn.sigmoid(ga_ref[0]) * ya + jax.nn.sigmoid(gm_ref[0]) * ym
    upd = jnp.dot(u.astype(BF16), wout_ref[...], preferred_element_type=F32)
    y = x_ref[0] + mod_ref[0, 2:3, :] * upd
    y = y * lax.rsqrt(jnp.mean(y * y, axis=-1, keepdims=True) + NORM_EPS)
    y_ref[0] = y * fg_ref[...]


def _merge_prompt(yaT, az, hm, mz, ga, gm, x, mod3, w_att, w_mlstm, w_out, final_g):
    B, S, D = x.shape
    tm = yaT.shape[3]
    nb = S // tm
    const = lambda b, j: (0, 0)
    tok = lambda w: pl.BlockSpec((1, tm, w), lambda b, j: (b, j, 0))
    return pl.pallas_call(
        _merge_kernel,
        grid=(B, nb),
        in_specs=[
            pl.BlockSpec((1, 1, ATT_WIDTH, tm), lambda b, j: (b, j, 0, 0)),
            tok(ATT_WIDTH), tok(M_WIDTH), tok(M_WIDTH), tok(D), tok(D), tok(D),
            pl.BlockSpec((1, 3, D), lambda b, j: (b, 0, 0)),
            pl.BlockSpec(w_att.shape, const),
            pl.BlockSpec(w_mlstm.shape, const),
            pl.BlockSpec(w_out.shape, const),
            pl.BlockSpec((1, D), const),
        ],
        out_specs=tok(D),
        out_shape=jax.ShapeDtypeStruct((B, S, D), F32),
        compiler_params=_params("arbitrary", "arbitrary"),
        name="prompt_merge",
    )(yaT, az, hm, mz, ga, gm, x, mod3, w_att, w_mlstm, w_out, final_g.reshape(1, D))


def _prompt_layer(x, mod3, norm_g, w_main, w_ifT, b_if, w_conv, b_conv, m_norm_g,
                  w_att, w_mlstm, w_out, final_g):
    B, S, _ = x.shape
    (qT, kh, vT, kf, vf, kmean, az, mqk, mv, mo, mz, ga, gm, gT, mvT, gc) = _inproj(
        x, mod3, norm_g, w_main, w_ifT, b_if)
    yaT = _moba_prompt(qT, kh, vT, kmean)
    hm, C, n, m = _mlstm_prompt(mqk, mv, mvT, gT, gc, mo, w_conv, b_conv, m_norm_g)
    y = _merge_prompt(yaT, az, hm, mz, ga, gm, x, mod3, w_att, w_mlstm, w_out, final_g)
    k_rows = kf.reshape(B, S, ATT_HEADS, ATT_HEAD_DIM)
    v_rows = vf.reshape(B, S, ATT_HEADS, ATT_HEAD_DIM)
    conv_new = mqk[:, S - (M_CONV - 1):, :]
    return y, k_rows, v_rows, C, n[:, :M_HEADS, :], m[:, :M_HEADS, 0], conv_new


def _inproj_sample_kernel(x_ref, mod_ref, ng_ref, w_ref, wif_ref, bif_ref, z_ref, g_ref):
    x = x_ref[...]
    d = x.shape[1]
    xn = x * lax.rsqrt(jnp.mean(x * x, axis=-1, keepdims=True) + NORM_EPS) * ng_ref[...]
    h = (xn * (1.0 + mod_ref[:, d:2 * d]) + mod_ref[:, 0:d]).astype(BF16)
    z_ref[...] = jnp.dot(h, w_ref[...], preferred_element_type=F32)
    g = lax.dot_general(h, wif_ref[...], (((1,), (1,)), ((), ())), preferred_element_type=F32)
    g_ref[...] = g + bif_ref[...]


def _inproj_sample(x, mod, norm_g, w_main, w_ifT, b_if):
    n, d = x.shape
    bn = 1024
    const = lambda i: (0, 0)
    return pl.pallas_call(
        _inproj_sample_kernel,
        grid=(w_main.shape[1] // bn,),
        in_specs=[pl.BlockSpec((n, d), const),
                  pl.BlockSpec(mod.shape, const),
                  pl.BlockSpec((1, d), const),
                  pl.BlockSpec((d, bn), lambda i: (0, i)),
                  pl.BlockSpec(w_ifT.shape, const),
                  pl.BlockSpec((1, 2 * M_HEADS), const)],
        out_specs=[pl.BlockSpec((n, bn), lambda i: (0, i)),
                   pl.BlockSpec((n, 2 * M_HEADS), const)],
        out_shape=[jax.ShapeDtypeStruct((n, w_main.shape[1]), F32),
                   jax.ShapeDtypeStruct((n, 2 * M_HEADS), F32)],
        compiler_params=_params("arbitrary"),
        name="sample_inproj",
    )(x, mod, norm_g.reshape(1, d), w_main, w_ifT, b_if.reshape(1, 2 * M_HEADS))


_PAGES_PER_CHUNK = 8
_PAGES_PER_BLOCK = MOBA_BLOCK // PAGE_SIZE


def _pagescore_kernel(pt_ref, ck_ref, q_ref, out_ref, buf, sem):
    b = pl.program_id(0)
    nseq = pl.num_programs(0)
    n_pages = pt_ref.shape[1]
    cpb = n_pages // _PAGES_PER_CHUNK
    bpc = _PAGES_PER_CHUNK // _PAGES_PER_BLOCK

    def chunk_copy(bb, ci, i, slot):
        page = pt_ref[bb, ci * _PAGES_PER_CHUNK + i]
        return pltpu.make_async_copy(ck_ref.at[0, page], buf.at[slot, i], sem.at[slot])

    def start(bb, ci, slot):
        for i in range(_PAGES_PER_CHUNK):
            chunk_copy(bb, ci, i, slot).start()

    @pl.when(b == 0)
    def _():
        start(0, 0, 0)

    qb = jnp.broadcast_to(q_ref[0], buf.shape[2:])

    def body(ci, carry):
        slot = (b * cpb + ci) % 2
        nxt = ci + 1

        @pl.when(nxt < cpb)
        def _():
            start(b, nxt, 1 - slot)

        @pl.when((nxt == cpb) & (b + 1 < nseq))
        def _():
            start(b + 1, 0, 1 - slot)

        for i in range(_PAGES_PER_CHUNK):
            chunk_copy(b, ci, i, slot).wait()
        for blk in range(bpc):
            p = buf[slot, blk * _PAGES_PER_BLOCK]
            for i in range(1, _PAGES_PER_BLOCK):
                p = p + buf[slot, blk * _PAGES_PER_BLOCK + i]
            out_ref[0, ci * bpc + blk] = jnp.sum(p * qb, axis=1)
        return carry

    lax.fori_loop(0, cpb, body, 0)


def _page_block_scores(cache_kT, page_table, q_col):
    nseq, n_pages = page_table.shape
    _, _, H, dh, ps = cache_kT.shape
    nblk = n_pages // _PAGES_PER_BLOCK
    grid_spec = pltpu.PrefetchScalarGridSpec(
        num_scalar_prefetch=1,
        grid=(nseq,),
        in_specs=[pl.BlockSpec(memory_space=pl.ANY),
                  pl.BlockSpec((1, H, dh, 1), lambda b, pt: (b, 0, 0, 0))],
        out_specs=pl.BlockSpec((1, nblk, H, ps), lambda b, pt: (b, 0, 0, 0)),
        scratch_shapes=[pltpu.VMEM((2, _PAGES_PER_CHUNK, H, dh, ps), F32),
                        pltpu.SemaphoreType.DMA((2,))],
    )
    return pl.pallas_call(
        _pagescore_kernel,
        grid_spec=grid_spec,
        out_shape=jax.ShapeDtypeStruct((nseq, nblk, H, ps), F32),
        compiler_params=_params("arbitrary"),
        name="sample_page_scores",
    )(page_table, cache_kT, q_col)


def _gate_sample_kernel(part_ref, idx_ref):
    nblk, H = part_ref.shape[1], part_ref.shape[2]
    n = nblk * H
    x = part_ref[0].reshape(n, part_ref.shape[3])
    ones = jnp.ones((V7X_SUBLANES, x.shape[1]), BF16)
    nt = (((1,), (1,)), ((), ()))
    sc = sum(lax.dot_general(ones, part, nt, preferred_element_type=F32) for part in _split3(x))
    sc = sc[0:1] / MOBA_BLOCK
    lane = lax.broadcasted_iota(jnp.int32, (1, n), 1)
    cnt = jnp.zeros((1, n), jnp.int32)
    for k in range(1, nblk):
        r = pltpu.roll(sc, k * H, axis=1)
        beats = (r > sc) | ((r == sc) & (lane >= k * H))
        cnt += beats.astype(jnp.int32)
    blk = lane // H
    j = lax.broadcasted_iota(jnp.int32, (n, V7X_LANES), 0)
    hcol = lax.broadcasted_iota(jnp.int32, (n, V7X_LANES), 1)
    gather = jnp.where(j % H == hcol, 1.0, 0.0).astype(BF16)
    out = jnp.zeros((V7X_SUBLANES, V7X_LANES), jnp.int32)
    r8 = lax.broadcasted_iota(jnp.int32, (V7X_SUBLANES, V7X_LANES), 0)
    for r in range(MOBA_TOPK):
        vals = jnp.where(cnt == r, blk, 0).astype(F32)
        vals = jnp.broadcast_to(vals, (V7X_SUBLANES, n)).astype(BF16)
        idx = jnp.dot(vals, gather, preferred_element_type=F32)[0:1]
        out = jnp.where(r8 == r, idx.astype(jnp.int32), out)
    idx_ref[0] = out


def _gate_sample(partials):
    nseq, nblk, H, ps = partials.shape
    return pl.pallas_call(
        _gate_sample_kernel,
        grid=(nseq,),
        in_specs=[pl.BlockSpec((1, nblk, H, ps), lambda b: (b, 0, 0, 0))],
        out_specs=pl.BlockSpec((1, V7X_SUBLANES, V7X_LANES), lambda b: (b, 0, 0)),
        out_shape=jax.ShapeDtypeStruct((nseq, V7X_SUBLANES, V7X_LANES), jnp.int32),
        compiler_params=_params("arbitrary"),
        name="sample_gate",
    )(partials)


def _attend_sample_kernel(pt_ref, sel_ref, ck_ref, cv_ref, q_ref, kn_ref, vn_ref, o_ref,
                          kbuf, vbuf, sem):
    b = pl.program_id(0)
    nseq = pl.num_programs(0)
    H, dh = ATT_HEADS, ATT_HEAD_DIM
    npg = MOBA_TOPK * _PAGES_PER_BLOCK

    def copies(bb, slot):
        out = []
        for hh in range(H):
            for r in range(MOBA_TOPK):
                blk = sel_ref[(bb * MOBA_TOPK + r) * H + hh]
                for i in range(_PAGES_PER_BLOCK):
                    page = pt_ref[bb, blk * _PAGES_PER_BLOCK + i]
                    pg = r * _PAGES_PER_BLOCK + i
                    out.append(pltpu.make_async_copy(
                        ck_ref.at[0, page, hh], kbuf.at[slot, hh, pg], sem.at[0, slot]))
                    out.append(pltpu.make_async_copy(
                        cv_ref.at[0, page, hh], vbuf.at[slot, hh, pg], sem.at[1, slot]))
        return out

    @pl.when(b == 0)
    def _():
        for cp in copies(0, 0):
            cp.start()

    slot = b % 2

    @pl.when(b + 1 < nseq)
    def _():
        for cp in copies(b + 1, 1 - slot):
            cp.start()

    for cp in copies(b, slot):
        cp.wait()

    for hh in range(H):
        q8 = jnp.broadcast_to(q_ref[0, hh:hh + 1, :], (V7X_SUBLANES, dh)).astype(BF16)
        kn = kn_ref[0, hh:hh + 1, :].astype(BF16).astype(F32)
        vn = vn_ref[0, hh:hh + 1, :].astype(BF16).astype(F32)
        scale = ATT_HEAD_DIM ** -0.5
        s = jnp.concatenate(
            [jnp.dot(q8, kbuf[slot, hh, pg].astype(BF16), preferred_element_type=F32)[0:1]
             for pg in range(npg)], axis=1) * scale
        s_new = jnp.sum(q8[0:1].astype(F32) * kn, axis=1, keepdims=True) * scale
        m = jnp.maximum(jnp.max(s, axis=1, keepdims=True), s_new)
        p = jnp.exp(s - m)
        p_new = jnp.exp(s_new - m)
        l = jnp.sum(p, axis=1, keepdims=True) + p_new
        p8 = jnp.broadcast_to(p, (V7X_SUBLANES, p.shape[1])).astype(BF16)
        acc = p_new.astype(BF16).astype(F32) * vn
        for pg in range(npg):
            acc = acc + lax.dot_general(
                p8[:, pg * PAGE_SIZE:(pg + 1) * PAGE_SIZE], vbuf[slot, hh, pg].astype(BF16),
                (((1,), (1,)), ((), ())), preferred_element_type=F32)[0:1]
        o_ref[0, :, hh * dh:(hh + 1) * dh] = acc / l


def _attend_sample(cache_k, cache_v, page_table, sel_flat, q, k_new, v_new):
    nseq, n_pages = page_table.shape
    H, dh = ATT_HEADS, ATT_HEAD_DIM
    npg = MOBA_TOPK * _PAGES_PER_BLOCK
    ps = cache_k.shape[-1]
    grid_spec = pltpu.PrefetchScalarGridSpec(
        num_scalar_prefetch=2,
        grid=(nseq,),
        in_specs=[pl.BlockSpec(memory_space=pl.ANY),
                  pl.BlockSpec(memory_space=pl.ANY),
                  pl.BlockSpec((1, H, dh), lambda b, pt, sel: (b, 0, 0)),
                  pl.BlockSpec((1, H, dh), lambda b, pt, sel: (b, 0, 0)),
                  pl.BlockSpec((1, H, dh), lambda b, pt, sel: (b, 0, 0))],
        out_specs=pl.BlockSpec((1, 1, ATT_WIDTH), lambda b, pt, sel: (b, 0, 0)),
        scratch_shapes=[pltpu.VMEM((2, H, npg, dh, ps), F32),
                        pltpu.VMEM((2, H, npg, dh, ps), F32),
                        pltpu.SemaphoreType.DMA((2, 2))],
    )
    return pl.pallas_call(
        _attend_sample_kernel,
        grid_spec=grid_spec,
        out_shape=jax.ShapeDtypeStruct((nseq, 1, ATT_WIDTH), F32),
        compiler_params=_params("arbitrary"),
        name="sample_attend",
    )(page_table, sel_flat, cache_k, cache_v, q, k_new, v_new)


def _mlstm_sample_kernel(qk_ref, v_ref, o_ref, g_ref, conv_ref, C_ref, n_ref, wc_ref, bc_ref, ng_ref,
                         h_ref, C_out, n_out, m_out, conv_out):
    x = qk_ref[0]
    conv = conv_ref[0]
    acc = x * wc_ref[M_CONV - 1:M_CONV, :] + bc_ref[...]
    for jj in range(M_CONV - 1):
        acc += conv[jj:jj + 1, :] * wc_ref[jj:jj + 1, :]
    qk = _silu(acc)
    conv_out[0] = jnp.concatenate([conv[1:], x], axis=0)
    g = g_ref[0]
    m_new = []
    for hh in range(M_HEADS):
        q = qk[:, hh * M_QK_DIM:(hh + 1) * M_QK_DIM]
        k = qk[:, M_QK_WIDTH + hh * M_QK_DIM:M_QK_WIDTH + (hh + 1) * M_QK_DIM] * (M_QK_DIM ** -0.5)
        qb = q.astype(BF16)
        kb = k.astype(BF16)
        v = v_ref[0][:, hh * M_V_DIM:(hh + 1) * M_V_DIM]
        i_pre = g[:, hh:hh + 1]
        f_pre = g[:, M_HEADS + hh:M_HEADS + hh + 1]
        m_prev = g[:, 2 * M_HEADS + hh:2 * M_HEADS + hh + 1]
        bb = jnp.minimum(f_pre, 0.0) - jnp.log1p(jnp.exp(-jnp.abs(f_pre)))
        cc = i_pre - bb
        M = jnp.maximum(m_prev, cc)
        w = jnp.exp(cc - M)
        a = jnp.exp(m_prev - M)
        C = C_ref[0, hh]
        n = n_ref[0, hh:hh + 1, :]
        qf = qb.astype(F32)
        s = jnp.sum(qf * kb.astype(F32), axis=1, keepdims=True) * w
        q8 = jnp.broadcast_to(qb, (V7X_SUBLANES, M_QK_DIM))
        inter = lax.dot_general(q8, C.astype(BF16), (((1,), (1,)), ((), ())),
                                preferred_element_type=F32)[0:1]
        vb = v.astype(BF16).astype(F32)
        num = a * inter + s.astype(BF16).astype(F32) * vb
        den = a * jnp.sum(qf * n.astype(BF16).astype(F32), axis=1, keepdims=True) + s
        hv = num / jnp.maximum(jnp.abs(den), jnp.exp(-(bb + M)))
        hv = hv * lax.rsqrt(jnp.mean(hv * hv, axis=-1, keepdims=True) + NORM_EPS)
        sl = slice(hh * M_V_DIM, (hh + 1) * M_V_DIM)
        h_ref[0, :, sl] = hv * ng_ref[:, sl] * jax.nn.sigmoid(o_ref[0][:, sl])
        wv = jnp.broadcast_to(w * v, (V7X_SUBLANES, M_V_DIM))
        wv = jnp.concatenate([wv, jnp.zeros((V7X_LANES - V7X_SUBLANES, M_V_DIM), F32)], axis=0)
        wv_col = wv.T[:, 0:1]
        C_out[0, hh] = a * C + wv_col * k
        n_out[0, hh:hh + 1, :] = a * n + w * k
        m_new.append(bb + M)
    lane = lax.broadcasted_iota(jnp.int32, (1, V7X_LANES), 1)
    mrow = jnp.zeros((1, V7X_LANES), F32)
    for hh in range(M_HEADS):
        mrow = jnp.where(lane == hh, m_new[hh], mrow)
    m_out[0] = mrow


def _mlstm_sample(z3, g16, state_conv, state_C, state_n, w_conv, b_conv, m_norm_g):
    nseq = z3.shape[0]
    MH, dk, dv = M_HEADS, M_QK_DIM, M_V_DIM
    W = 2 * M_QK_WIDTH
    const = lambda b: (0, 0)
    return pl.pallas_call(
        _mlstm_sample_kernel,
        grid=(nseq,),
        in_specs=[
            pl.BlockSpec((1, 1, W), lambda b: (b, 0, _C_MQK // W)),
            pl.BlockSpec((1, 1, M_WIDTH), lambda b: (b, 0, _C_MV // M_WIDTH)),
            pl.BlockSpec((1, 1, M_WIDTH), lambda b: (b, 0, _C_MO // M_WIDTH)),
            pl.BlockSpec((1, 1, 4 * MH), lambda b: (b, 0, 0)),
            pl.BlockSpec((1, M_CONV - 1, W), lambda b: (b, 0, 0)),
            pl.BlockSpec((1, MH, dv, dk), lambda b: (b, 0, 0, 0)),
            pl.BlockSpec((1, MH, dk), lambda b: (b, 0, 0)),
            pl.BlockSpec((M_CONV, W), const),
            pl.BlockSpec((1, W), const),
            pl.BlockSpec((1, M_WIDTH), const),
        ],
        out_specs=[
            pl.BlockSpec((1, 1, M_WIDTH), lambda b: (b, 0, 0)),
            pl.BlockSpec((1, MH, dv, dk), lambda b: (b, 0, 0, 0)),
            pl.BlockSpec((1, MH, dk), lambda b: (b, 0, 0)),
            pl.BlockSpec((1, 1, V7X_LANES), lambda b: (b, 0, 0)),
            pl.BlockSpec((1, M_CONV - 1, W), lambda b: (b, 0, 0)),
        ],
        out_shape=[
            jax.ShapeDtypeStruct((nseq, 1, M_WIDTH), F32),
            jax.ShapeDtypeStruct((nseq, MH, dv, dk), F32),
            jax.ShapeDtypeStruct((nseq, MH, dk), F32),
            jax.ShapeDtypeStruct((nseq, 1, V7X_LANES), F32),
            jax.ShapeDtypeStruct((nseq, M_CONV - 1, W), F32),
        ],
        compiler_params=_params("arbitrary"),
        name="sample_mlstm",
    )(z3, z3, z3, g16, state_conv, state_C, state_n, w_conv, b_conv.reshape(1, W),
      m_norm_g.reshape(1, M_WIDTH))


def _merge_sample_kernel(ya_ref, hm_ref, z_ref, x_ref, mod_ref, watt_ref, wml_ref, wout_ref, fg_ref,
                         y_ref):
    d = x_ref.shape[1]
    ya = ya_ref[...] * _silu(z_ref[:, _C_AZ:_C_MQK])
    ya = jnp.dot(ya.astype(BF16), watt_ref[...], preferred_element_type=F32)
    ym = hm_ref[...] * _silu(z_ref[:, _C_MZ:_C_MIF])
    ym = jnp.dot(ym.astype(BF16), wml_ref[...], preferred_element_type=F32)
    u = jax.nn.sigmoid(z_ref[:, _P_GA:_P_GM]) * ya + jax.nn.sigmoid(z_ref[:, _P_GM:_P_END]) * ym
    upd = jnp.dot(u.astype(BF16), wout_ref[...], preferred_element_type=F32)
    y = x_ref[...] + mod_ref[:, 2 * d:3 * d] * upd
    y = y * lax.rsqrt(jnp.mean(y * y, axis=-1, keepdims=True) + NORM_EPS)
    y_ref[...] = y * fg_ref[...]


def _merge_sample(ya, hm, z, x, mod, w_att, w_mlstm, w_out, final_g):
    n, d = x.shape
    full = lambda a: pl.BlockSpec(a.shape, lambda i: (0,) * a.ndim)
    fg = final_g.reshape(1, d)
    args = (ya, hm, z, x, mod, w_att, w_mlstm, w_out, fg)
    return pl.pallas_call(
        _merge_sample_kernel,
        grid=(1,),
        in_specs=[full(a) for a in args],
        out_specs=pl.BlockSpec((n, d), lambda i: (0, 0)),
        out_shape=jax.ShapeDtypeStruct((n, d), F32),
        compiler_params=_params("arbitrary"),
        name="sample_merge",
    )(*args)


def _sample_layer(x, mod, cache_k, cache_v, page_table, state_C, state_n, state_m, state_conv,
                  norm_g, w_main, w_ifT, b_if, w_conv, b_conv, m_norm_g, w_att, w_mlstm, w_out, final_g):
    nseq = x.shape[0]
    H, dh = ATT_HEADS, ATT_HEAD_DIM
    assert page_table.shape[1] // _PAGES_PER_BLOCK >= MOBA_TOPK
    z, g = _inproj_sample(x, mod, norm_g, w_main, w_ifT, b_if)
    q = z[:, _C_AQ:_C_AK].reshape(nseq, H, dh)
    k_new = z[:, _C_AK:_C_AV].reshape(nseq, H, dh)
    v_new = z[:, _C_AV:_C_AZ].reshape(nseq, H, dh)
    cache_kT = jnp.transpose(cache_k, (0, 1, 3, 4, 2))
    cache_vT = jnp.transpose(cache_v, (0, 1, 3, 4, 2))
    partials = _page_block_scores(cache_kT, page_table, q.reshape(nseq, H, dh, 1))
    sel = _gate_sample(partials)
    sel_flat = sel[:, :MOBA_TOPK, :H].reshape(-1)
    ya = _attend_sample(cache_kT, cache_vT, page_table, sel_flat, q, k_new, v_new)
    g16 = jnp.concatenate([g, state_m, jnp.zeros_like(state_m)], axis=1).reshape(nseq, 1, 4 * M_HEADS)
    hm, C, n, m, conv_new = _mlstm_sample(z.reshape(nseq, 1, -1), g16, state_conv, state_C, state_n,
                                          w_conv, b_conv, m_norm_g)
    y = _merge_sample(ya.reshape(nseq, ATT_WIDTH), hm.reshape(nseq, M_WIDTH), z, x, mod,
                      w_att, w_mlstm, w_out, final_g)
    return (y.reshape(nseq, 1, -1), k_new.reshape(nseq, 1, H, dh), v_new.reshape(nseq, 1, H, dh),
            C, n, m[:, 0, :M_HEADS], conv_new)


def kernel(x_prompt, x_sample, cache_k, cache_v, state_C, state_n, state_m, state_conv, page_table,
           c_prompt, c_sample, w_ada, b_ada, norm_g, w_in, b_if, w_conv, b_conv, m_norm_g,
           w_att, w_mlstm, w_out, final_g):
    depth = w_in.shape[0]
    assert depth == 1, "single-layer step"
    B = x_prompt.shape[0]
    nseq = x_sample.shape[0]
    pad = (-(B + nseq)) % V7X_SUBLANES
    c_all = jnp.concatenate([c_prompt, c_sample, jnp.zeros((pad, c_prompt.shape[1]), F32)], axis=0)
    mod = _ada(c_all, w_ada[0], b_ada[0])
    mod_p = mod[:B].reshape(B, 3, D_MODEL)
    mod_s = mod[B:B + nseq]
    w_main, w_ifT = _pack_w_in(w_in[0])
    wa, wm, wo = w_att[0].astype(BF16), w_mlstm[0].astype(BF16), w_out[0].astype(BF16)
    outs_p = _prompt_layer(x_prompt, mod_p, norm_g[0], w_main, w_ifT, b_if[0], w_conv[0], b_conv[0],
                           m_norm_g[0], wa, wm, wo, final_g)
    outs_s = _sample_layer(x_sample[:, 0, :], mod_s, cache_k, cache_v, page_table,
                           state_C[0], state_n[0], state_m[0], state_conv[0],
                           norm_g[0], w_main, w_ifT, b_if[0], w_conv[0], b_conv[0], m_norm_g[0],
                           wa, wm, wo, final_g)
    y_p, *st_p = outs_p
    y_s, *st_s = outs_s
    return (y_p, y_s) + tuple(a[None] for a in st_p) + tuple(a[None] for a in st_s)
```

```python
import functools

import jax
import jax.numpy as jnp
from jax import lax
from jax.experimental import pallas as pl
from jax.experimental.pallas import tpu as pltpu

F32 = jnp.float32
BF16 = jnp.bfloat16

D_MODEL = 1024
ATT_HEADS = 8
ATT_HEAD_DIM = 64
ATT_WIDTH = ATT_HEADS * ATT_HEAD_DIM
MOBA_BLOCK = 256
MOBA_TOPK = 3
M_HEADS = 4
M_V_DIM = 256
M_QK_DIM = 128
M_QK_WIDTH = M_HEADS * M_QK_DIM
M_WIDTH = M_HEADS * M_V_DIM
M_CONV = 4
PAGE_SIZE = 128
NORM_EPS = 1e-6
NEG_SCORE = -1e30
_LOG2E = 1.4426950408889634

V7X_LANES = 128
V7X_SUBLANES = 8
V7X_VMEM_LIMIT_BYTES = 56 * 1024 * 1024

_C_AQ, _C_AK, _C_AV, _C_AZ = 0, 512, 1024, 1536
_C_MQK, _C_MV, _C_MO, _C_MZ = 2048, 3072, 4096, 5120
_C_MIF, _C_GA, _C_GM, _C_END = 6144, 6152, 7176, 8200
_P_GA, _P_GM, _P_END = 6144, 7168, 8192


def _silu(x):
    return x * jax.nn.sigmoid(x)


def _split3(x):
    hi = x.astype(BF16)
    r1 = x - hi.astype(F32)
    mid = r1.astype(BF16)
    lo = (r1 - mid.astype(F32)).astype(BF16)
    return hi, mid, lo


def _pack_w_in(w_in):
    w_main = jnp.concatenate([w_in[:, :_C_MIF], w_in[:, _C_GA:_C_END]], axis=1).astype(BF16)
    w_ifT = w_in[:, _C_MIF:_C_GA].T.astype(BF16)
    return w_main, w_ifT


def _params(*sem):
    return pltpu.CompilerParams(dimension_semantics=sem, vmem_limit_bytes=V7X_VMEM_LIMIT_BYTES)


def _ada_kernel(c_ref, w_ref, b_ref, o_ref):
    a = _silu(c_ref[...])
    a_hi = a.astype(BF16)
    a_lo = (a - a_hi.astype(F32)).astype(BF16)
    w = w_ref[...]
    w_hi = w.astype(BF16)
    w_lo = (w - w_hi.astype(F32)).astype(BF16)
    acc = jnp.dot(a_hi, w_hi, preferred_element_type=F32)
    acc += jnp.dot(a_hi, w_lo, preferred_element_type=F32)
    acc += jnp.dot(a_lo, w_hi, preferred_element_type=F32)
    o_ref[...] = acc + b_ref[...]


def _ada(c, w_ada, b_ada):
    n, d = c.shape
    d3 = w_ada.shape[1]
    bn = 512
    return pl.pallas_call(
        _ada_kernel,
        grid=(d3 // bn,),
        in_specs=[pl.BlockSpec((n, d), lambda i: (0, 0)),
                  pl.BlockSpec((d, bn), lambda i: (0, i)),
                  pl.BlockSpec((1, bn), lambda i: (0, i))],
        out_specs=pl.BlockSpec((n, bn), lambda i: (0, i)),
        out_shape=jax.ShapeDtypeStruct((n, d3), F32),
        compiler_params=_params("arbitrary"),
        name="ada_mod",
    )(c, w_ada, b_ada.reshape(1, d3))


def _inproj_kernel(x_ref, mod_ref, ng_ref, w_ref, wif_ref, bif_ref, wifn_ref, bifn_ref,
                   qT_ref, kh_ref, vT_ref, kf_ref, vf_ref, kmean_ref,
                   az_ref, mqk_ref, mv_ref, mo_ref, mz_ref, ga_ref, gm_ref, gT_ref, mvT_ref, gc_ref):
    j = pl.program_id(1)
    x = x_ref[0]
    xn = x * lax.rsqrt(jnp.mean(x * x, axis=-1, keepdims=True) + NORM_EPS) * ng_ref[...]
    shift = mod_ref[0, 0:1, :]
    scale = mod_ref[0, 1:2, :]
    h = (xn * (1.0 + scale) + shift).astype(BF16)

    def proj(a, b):
        return jnp.dot(h, w_ref[:, a:b], preferred_element_type=F32)

    q = proj(_C_AQ, _C_AK) * (ATT_HEAD_DIM ** -0.5 * _LOG2E)
    qT_ref[0, 0] = q.T.astype(BF16)
    k = proj(_C_AK, _C_AV)
    kf_ref[0] = k
    ksum = jnp.sum(k, axis=0, keepdims=True) / MOBA_BLOCK
    dh = ATT_HEAD_DIM
    tm = k.shape[0]
    one_col = jnp.where(lax.broadcasted_iota(jnp.int32, (tm, dh), 1) == 0, 1.0, 0.0)
    one_row = jnp.where(lax.broadcasted_iota(jnp.int32, (_MOBA_V_PAD, tm), 0) == 0, 1.0, 0.0)
    v = proj(_C_AV, _C_AZ)
    vf_ref[0] = v
    vT = v.T
    for hh in range(ATT_HEADS):
        sl = slice(hh * dh, (hh + 1) * dh)
        kh_ref[0, 0, hh] = jnp.concatenate([k[:, sl], one_col], axis=1).astype(BF16)
        kmean_ref[0, hh, pl.ds(j, 1), :] = ksum[:, sl]
        r0 = hh * _MOBA_V_ROWS
        vT_ref[0, 0, r0:r0 + dh, :] = vT[sl].astype(BF16)
        vT_ref[0, 0, r0 + dh:r0 + _MOBA_V_ROWS, :] = one_row.astype(BF16)
    az_ref[0] = proj(_C_AZ, _C_MQK).astype(BF16)
    mqk_ref[0] = proj(_C_MQK, _C_MV)
    mv = proj(_C_MV, _C_MO)
    mv_ref[0] = mv.astype(BF16)
    mvT_ref[0, 0] = mv.T.astype(BF16)
    mo_ref[0] = proj(_C_MO, _C_MZ).astype(BF16)
    mz_ref[0] = proj(_C_MZ, _C_MIF).astype(BF16)
    ga_ref[0] = proj(_P_GA, _P_GM).astype(BF16)
    gm_ref[0] = proj(_P_GM, _P_END).astype(BF16)
    gT = lax.dot_general(wif_ref[...], h, (((1,), (1,)), ((), ())), preferred_element_type=F32)
    gT_ref[0] = gT + bif_ref[...]
    gc_ref[0] = jnp.dot(h, wifn_ref[...], preferred_element_type=F32) + bifn_ref[...]


def _inproj(x, mod3, norm_g, w_main, w_ifT, b_if):
    B, S, D = x.shape
    tm = MOBA_BLOCK
    nb = S // tm
    H, dh = ATT_HEADS, ATT_HEAD_DIM
    const = lambda b, j: (0, 0)
    tok = lambda w: pl.BlockSpec((1, tm, w), lambda b, j: (b, j, 0))
    blk = pl.BlockSpec((1, 1, ATT_WIDTH, tm), lambda b, j: (b, j, 0, 0))
    out_shape = [
        jax.ShapeDtypeStruct((B, nb, ATT_WIDTH, tm), BF16),
        jax.ShapeDtypeStruct((B, nb, H, tm, 2 * dh), BF16),
        jax.ShapeDtypeStruct((B, nb, H * _MOBA_V_ROWS, tm), BF16),
        jax.ShapeDtypeStruct((B, S, ATT_WIDTH), F32),
        jax.ShapeDtypeStruct((B, S, ATT_WIDTH), F32),
        jax.ShapeDtypeStruct((B, H, nb, dh), F32),
        jax.ShapeDtypeStruct((B, S, ATT_WIDTH), BF16),
        jax.ShapeDtypeStruct((B, S, 2 * M_QK_WIDTH), F32),
        jax.ShapeDtypeStruct((B, S, M_WIDTH), BF16),
        jax.ShapeDtypeStruct((B, S, M_WIDTH), BF16),
        jax.ShapeDtypeStruct((B, S, M_WIDTH), BF16),
        jax.ShapeDtypeStruct((B, S, D), BF16),
        jax.ShapeDtypeStruct((B, S, D), BF16),
        jax.ShapeDtypeStruct((B, 2 * M_HEADS, S), F32),
        jax.ShapeDtypeStruct((B, nb, M_WIDTH, tm), BF16),
        jax.ShapeDtypeStruct((B, S, 2 * M_HEADS), F32),
    ]
    out_specs = [
        blk,
        pl.BlockSpec((1, 1, H, tm, 2 * dh), lambda b, j: (b, j, 0, 0, 0)),
        pl.BlockSpec((1, 1, H * _MOBA_V_ROWS, tm), lambda b, j: (b, j, 0, 0)),
        tok(ATT_WIDTH), tok(ATT_WIDTH),
        pl.BlockSpec((1, H, nb, dh), lambda b, j: (b, 0, 0, 0)),
        tok(ATT_WIDTH), tok(2 * M_QK_WIDTH), tok(M_WIDTH), tok(M_WIDTH), tok(M_WIDTH), tok(D), tok(D),
        pl.BlockSpec((1, 2 * M_HEADS, tm), lambda b, j: (b, 0, j)),
        pl.BlockSpec((1, 1, M_WIDTH, tm), lambda b, j: (b, j, 0, 0)),
        tok(2 * M_HEADS),
    ]
    in_specs = [
        pl.BlockSpec((1, tm, D), lambda b, j: (b, j, 0)),
        pl.BlockSpec((1, 3, D), lambda b, j: (b, 0, 0)),
        pl.BlockSpec((1, D), const),
        pl.BlockSpec(w_main.shape, const, pipeline_mode=pl.Buffered(1)),
        pl.BlockSpec(w_ifT.shape, const),
        pl.BlockSpec((2 * M_HEADS, 1), const),
        pl.BlockSpec((D, 2 * M_HEADS), const),
        pl.BlockSpec((1, 2 * M_HEADS), const),
    ]
    return pl.pallas_call(
        _inproj_kernel,
        grid=(B, nb),
        in_specs=in_specs,
        out_specs=out_specs,
        out_shape=out_shape,
        compiler_params=_params("arbitrary", "arbitrary"),
        name="prompt_inproj",
    )(x, mod3, norm_g.reshape(1, D), w_main, w_ifT, b_if.reshape(2 * M_HEADS, 1),
      w_ifT.T, b_if.reshape(1, 2 * M_HEADS))


_MOBA_HEADS_PER_STEP = 8
_MOBA_V_PAD = 16
_MOBA_V_ROWS = ATT_HEAD_DIM + _MOBA_V_PAD


def _moba_kernel(qT_ref, kh_ref, vT_ref, kmean_ref, o_ref, bias_ref, qs_ref, acc_ref):
    j = pl.program_id(2)
    hp = kmean_ref.shape[1]
    nb = kmean_ref.shape[2]
    tq = qT_ref.shape[3]
    dh = ATT_HEAD_DIM
    vr = _MOBA_V_ROWS
    qTs = [qT_ref[0, 0, hh * dh:(hh + 1) * dh, :] for hh in range(hp)]
    for hh in range(hp):
        qs_ref[hh, 0:dh, :] = qTs[hh]
        qs_ref[hh, dh:2 * dh, :] = jnp.zeros((dh, tq), BF16)

    blk = lax.broadcasted_iota(jnp.int32, (nb, tq), 0)
    past = blk < j
    for hh in range(hp):
        sg = sum(jnp.dot(part, qTs[hh], preferred_element_type=F32)
                 for part in _split3(kmean_ref[0, hh]))
        sg = jnp.where(past, sg, NEG_SCORE)
        cnt = jnp.zeros((nb, tq), jnp.int32)
        for m in range(nb):
            row = sg[m:m + 1, :]
            beats = (row > sg) | ((row == sg) & (blk > m))
            cnt += beats.astype(jnp.int32)
        bias_ref[hh] = jnp.where((cnt < MOBA_TOPK) & past, 0.0, NEG_SCORE)

    kpos = lax.broadcasted_iota(jnp.int32, (tq, tq), 0)
    qpos = lax.broadcasted_iota(jnp.int32, (tq, tq), 1)
    causal = kpos <= qpos

    def scores(n):
        return [jnp.dot(kh_ref[0, n, hh], qs_ref[hh], preferred_element_type=F32)
                for hh in range(hp)]

    def values(n, ps):
        return [jnp.dot(vT_ref[0, n, hh * vr:(hh + 1) * vr, :], ps[hh], preferred_element_type=F32)
                for hh in range(hp)]

    ss = [jnp.where(causal, s, NEG_SCORE) for s in scores(j)]
    ms = [jnp.max(s, axis=0, keepdims=True) for s in ss]
    pbs = [jnp.exp2((s - m).astype(BF16)) for s, m in zip(ss, ms)]
    for hh, x in enumerate(values(j, pbs)):
        acc_ref[hh] = x

    row0 = lax.broadcasted_iota(jnp.int32, (_MOBA_V_PAD, tq), 0) == 0

    def body(n, ms):
        for hh in range(hp):
            mask_row = jnp.where(row0, bias_ref[hh, pl.ds(n, 1), :], 0.0)
            qs_ref[hh, dh:dh + _MOBA_V_PAD, :] = mask_row.astype(BF16)
        ss = scores(n)
        m_new = [jnp.maximum(m, jnp.max(s, axis=0, keepdims=True)) for m, s in zip(ms, ss)]
        alphas = [jnp.exp2(m - mn) for m, mn in zip(ms, m_new)]
        pbs = [jnp.exp2((s - mn).astype(BF16)) for s, mn in zip(ss, m_new)]
        for hh, x in enumerate(values(n, pbs)):
            acc_ref[hh] = alphas[hh] * acc_ref[hh] + x
        return tuple(m_new)

    lax.fori_loop(0, j, body, tuple(ms))
    for hh in range(hp):
        acc = acc_ref[hh]
        o_ref[0, 0, hh * dh:(hh + 1) * dh, :] = (acc[0:dh] / acc[dh:dh + 1]).astype(o_ref.dtype)


def _moba_prompt(qT, kh, vT, kmean):
    B, nb, _, tq = qT.shape
    H, dh, hp = ATT_HEADS, ATT_HEAD_DIM, _MOBA_HEADS_PER_STEP
    return pl.pallas_call(
        _moba_kernel,
        grid=(B, H // hp, nb),
        in_specs=[
            pl.BlockSpec((1, 1, hp * dh, tq), lambda b, g, j: (b, j, g, 0)),
            pl.BlockSpec((1, nb, hp, tq, 2 * dh), lambda b, g, j: (b, 0, g, 0, 0)),
            pl.BlockSpec((1, nb, hp * _MOBA_V_ROWS, tq), lambda b, g, j: (b, 0, g, 0)),
            pl.BlockSpec((1, hp, nb, dh), lambda b, g, j: (b, g, 0, 0)),
        ],
        out_specs=pl.BlockSpec((1, 1, hp * dh, tq), lambda b, g, j: (b, j, g, 0)),
        out_shape=jax.ShapeDtypeStruct((B, nb, ATT_WIDTH, tq), BF16),
        scratch_shapes=[pltpu.VMEM((hp, nb, tq), F32), pltpu.VMEM((hp, 2 * dh, tq), BF16),
                        pltpu.VMEM((hp, _MOBA_V_ROWS, tq), F32)],
        compiler_params=_params("arbitrary", "arbitrary", "arbitrary"),
        name="prompt_moba",
    )(qT, kh, vT, kmean)


def _log_sigmoid(x):
    return jnp.minimum(x, 0.0) - jnp.log1p(jnp.exp(-jnp.abs(x)))


def _causal_conv(x, tail_ref, w_ref, b_ref):
    L = x.shape[0]
    prev = tail_ref[...]
    row = lax.broadcasted_iota(jnp.int32, (V7X_SUBLANES, x.shape[1]), 0)
    acc = x * w_ref[M_CONV - 1:M_CONV, :] + b_ref[...]
    for d in range(1, M_CONV):
        r = pltpu.roll(x, d, axis=0)
        head = jnp.where(row < d, pltpu.roll(prev, d, axis=0), r[:V7X_SUBLANES])
        shifted = jnp.concatenate([head, r[V7X_SUBLANES:]], axis=0)
        acc += shifted * w_ref[M_CONV - 1 - d:M_CONV - d, :]
    tail_ref[...] = x[L - V7X_SUBLANES:, :]
    return acc


def _mlstm_kernel(qk_ref, v_ref, vT_ref, g_ref, gc_ref, o_ref, wc_ref, bc_ref, ng_ref,
                  h_ref, C_out, n_out, m_out,
                  C_s, n_s, m_s, tail_s):
    c = pl.program_id(1)
    L = qk_ref.shape[1]
    MH, dk, dv = M_HEADS, M_QK_DIM, M_V_DIM
    nt = (((1,), (1,)), ((), ()))

    @pl.when(c == 0)
    def _():
        C_s[...] = jnp.zeros_like(C_s)
        n_s[...] = jnp.zeros_like(n_s)
        m_s[...] = jnp.zeros_like(m_s)
        tail_s[...] = jnp.zeros_like(tail_s)

    qk = _silu(_causal_conv(qk_ref[0], tail_s, wc_ref, bc_ref))

    t_idx = lax.broadcasted_iota(jnp.int32, (L, L), 0)
    s_idx = lax.broadcasted_iota(jnp.int32, (L, L), 1)
    causal = s_idx <= t_idx
    tri = jnp.where(causal, 1.0, 0.0).astype(BF16)
    g_rows = g_ref[0]
    b_rows = sum(lax.dot_general(part, tri, nt, preferred_element_type=F32)
                 for part in _split3(_log_sigmoid(g_rows)))
    b_cols = sum(jnp.dot(tri, part, preferred_element_type=F32)
                 for part in _split3(_log_sigmoid(gc_ref[0])))

    for hd in range(MH):
        qb = qk[:, hd * dk:(hd + 1) * dk].astype(BF16)
        kb = (qk[:, M_QK_WIDTH + hd * dk:M_QK_WIDTH + (hd + 1) * dk] * (dk ** -0.5)).astype(BF16)
        vsl = slice(hd * dv, (hd + 1) * dv)
        b_row = b_rows[MH + hd:MH + hd + 1]
        c_row = g_rows[hd:hd + 1] - b_row
        b_col = b_cols[:, MH + hd:MH + hd + 1]
        m_prev = m_s[hd:hd + 1, 0:1]
        M_col = jnp.maximum(jnp.max(jnp.where(causal, c_row, NEG_SCORE), axis=1, keepdims=True),
                            m_prev)
        M_last = jnp.maximum(jnp.max(c_row, axis=1, keepdims=True), m_prev)
        w = jnp.where(causal, jnp.exp(c_row - M_col), 0.0)
        a_col = jnp.exp(m_prev - M_col)

        s = lax.dot_general(qb, kb, nt, preferred_element_type=F32) * w
        inter = lax.dot_general(qb, C_s[hd].astype(BF16), nt, preferred_element_type=F32)
        num = a_col * inter + jnp.dot(s.astype(BF16), v_ref[0, :, vsl], preferred_element_type=F32)
        n_b = n_s[hd:hd + 1, :].astype(BF16).astype(F32)
        qn = jnp.sum(qb.astype(F32) * n_b, axis=1, keepdims=True)
        den = a_col * qn + jnp.sum(s, axis=1, keepdims=True)
        hh = num / jnp.maximum(jnp.abs(den), jnp.exp(-(b_col + M_col)))
        hh = hh * lax.rsqrt(jnp.mean(hh * hh, axis=-1, keepdims=True) + NORM_EPS)
        gate = jax.nn.sigmoid(o_ref[0, :, vsl].astype(F32))
        h_ref[0, :, vsl] = (hh * ng_ref[:, vsl] * gate).astype(h_ref.dtype)

        ws_row = jnp.exp(c_row - M_last)
        a_end = jnp.exp(m_prev - M_last)
        vwT = (vT_ref[0, 0, vsl, :].astype(F32) * ws_row).astype(BF16)
        C_s[hd] = a_end * C_s[hd] + jnp.dot(vwT, kb, preferred_element_type=F32)
        ws8 = jnp.broadcast_to(ws_row, (V7X_SUBLANES, L)).astype(BF16)
        n_s[hd:hd + 1, :] = (a_end * n_s[hd:hd + 1, :]
                             + jnp.dot(ws8, kb, preferred_element_type=F32)[0:1])
        m_s[hd:hd + 1, :] = jnp.broadcast_to(b_row[:, L - 1:L] + M_last, (1, m_s.shape[1]))

    @pl.when(c == pl.num_programs(1) - 1)
    def _():
        C_out[0] = C_s[...]
        n_out[0] = n_s[...]
        m_out[0] = m_s[...]


def _mlstm_prompt(mqk, mv, mvT, gT, gc, mo, w_conv, b_conv, m_norm_g):
    B, S, W = mqk.shape
    L = mvT.shape[3]
    nc = S // L
    MH, dk, dv = M_HEADS, M_QK_DIM, M_V_DIM
    const = lambda b, c: (0, 0)
    tok = lambda w: pl.BlockSpec((1, L, w), lambda b, c: (b, c, 0))
    in_specs = [
        tok(W), tok(M_WIDTH),
        pl.BlockSpec((1, 1, M_WIDTH, L), lambda b, c: (b, c, 0, 0)),
        pl.BlockSpec((1, 2 * MH, L), lambda b, c: (b, 0, c)),
        tok(2 * MH), tok(M_WIDTH),
        pl.BlockSpec((M_CONV, W), const),
        pl.BlockSpec((1, W), const),
        pl.BlockSpec((1, M_WIDTH), const),
    ]
    out_shape = [
        jax.ShapeDtypeStruct((B, S, M_WIDTH), BF16),
        jax.ShapeDtypeStruct((B, MH, dv, dk), F32),
        jax.ShapeDtypeStruct((B, V7X_SUBLANES, dk), F32),
        jax.ShapeDtypeStruct((B, V7X_SUBLANES, V7X_LANES), F32),
    ]
    out_specs = [
        tok(M_WIDTH),
        pl.BlockSpec((1, MH, dv, dk), lambda b, c: (b, 0, 0, 0)),
        pl.BlockSpec((1, V7X_SUBLANES, dk), lambda b, c: (b, 0, 0)),
        pl.BlockSpec((1, V7X_SUBLANES, V7X_LANES), lambda b, c: (b, 0, 0)),
    ]
    scratch = [
        pltpu.VMEM((MH, dv, dk), F32), pltpu.VMEM((V7X_SUBLANES, dk), F32),
        pltpu.VMEM((V7X_SUBLANES, V7X_LANES), F32), pltpu.VMEM((V7X_SUBLANES, W), F32),
    ]
    return pl.pallas_call(
        _mlstm_kernel,
        grid=(B, nc),
        in_specs=in_specs,
        out_specs=out_specs,
        out_shape=out_shape,
        scratch_shapes=scratch,
        compiler_params=_params("arbitrary", "arbitrary"),
        name="prompt_mlstm",
    )(mqk, mv, mvT, gT, gc, mo, w_conv, b_conv.reshape(1, W), m_norm_g.reshape(1, M_WIDTH))


def _merge_kernel(yaT_ref, az_ref, hm_ref, mz_ref, ga_ref, gm_ref, x_ref, mod_ref,
                  watt_ref, wml_ref, wout_ref, fg_ref, y_ref):
    ya = yaT_ref[0, 0].astype(F32).T * _silu(az_ref[0].astype(F32))
    ya = jnp.dot(ya.astype(BF16), watt_ref[...], preferred_element_type=F32)
    ym = hm_ref[0].astype(F32) * _silu(mz_ref[0].astype(F32))
    ym = jnp.dot(ym.astype(BF16), wml_ref[...], preferred_element_type=F32)
    u = (jax.nn.sigmoid(ga_ref[0].astype(F32)) * ya
         + jax.nn.sigmoid(gm_ref[0].astype(F32)) * ym)
    upd = jnp.dot(u.astype(BF16), wout_ref[...], preferred_element_type=F32)
    y = x_ref[0] + mod_ref[0, 2:3, :] * upd
    y = y * lax.rsqrt(jnp.mean(y * y, axis=-1, keepdims=True) + NORM_EPS)
    y_ref[0] = y * fg_ref[...]


def _merge_prompt(yaT, az, hm, mz, ga, gm, x, mod3, w_att, w_mlstm, w_out, final_g):
    B, S, D = x.shape
    tm = yaT.shape[3]
    nb = S // tm
    const = lambda b, j: (0, 0)
    tok = lambda w: pl.BlockSpec((1, tm, w), lambda b, j: (b, j, 0))
    return pl.pallas_call(
        _merge_kernel,
        grid=(B, nb),
        in_specs=[
            pl.BlockSpec((1, 1, ATT_WIDTH, tm), lambda b, j: (b, j, 0, 0)),
            tok(ATT_WIDTH), tok(M_WIDTH), tok(M_WIDTH), tok(D), tok(D), tok(D),
            pl.BlockSpec((1, 3, D), lambda b, j: (b, 0, 0)),
            pl.BlockSpec(w_att.shape, const),
            pl.BlockSpec(w_mlstm.shape, const),
            pl.BlockSpec(w_out.shape, const),
            pl.BlockSpec((1, D), const),
        ],
        out_specs=tok(D),
        out_shape=jax.ShapeDtypeStruct((B, S, D), F32),
        compiler_params=_params("arbitrary", "arbitrary"),
        name="prompt_merge",
    )(yaT, az, hm, mz, ga, gm, x, mod3, w_att, w_mlstm, w_out, final_g.reshape(1, D))


def _prompt_layer(x, mod3, norm_g, w_main, w_ifT, b_if, w_conv, b_conv, m_norm_g,
                  w_att, w_mlstm, w_out, final_g):
    B, S, _ = x.shape
    (qT, kh, vT, kf, vf, kmean, az, mqk, mv, mo, mz, ga, gm, gT, mvT, gc) = _inproj(
        x, mod3, norm_g, w_main, w_ifT, b_if)
    yaT = _moba_prompt(qT, kh, vT, kmean)
    hm, C, n, m = _mlstm_prompt(mqk, mv, mvT, gT, gc, mo, w_conv, b_conv, m_norm_g)
    y = _merge_prompt(yaT, az, hm, mz, ga, gm, x, mod3, w_att, w_mlstm, w_out, final_g)
    k_rows = kf.reshape(B, S, ATT_HEADS, ATT_HEAD_DIM)
    v_rows = vf.reshape(B, S, ATT_HEADS, ATT_HEAD_DIM)
    conv_new = mqk[:, S - (M_CONV - 1):, :]
    return y, k_rows, v_rows, C, n[:, :M_HEADS, :], m[:, :M_HEADS, 0], conv_new


def _inproj_sample_kernel(x_ref, mod_ref, ng_ref, w_ref, wif_ref, bif_ref, z_ref, g_ref):
    x = x_ref[...]
    d = x.shape[1]
    xn = x * lax.rsqrt(jnp.mean(x * x, axis=-1, keepdims=True) + NORM_EPS) * ng_ref[...]
    h = (xn * (1.0 + mod_ref[:, d:2 * d]) + mod_ref[:, 0:d]).astype(BF16)
    z_ref[...] = jnp.dot(h, w_ref[...], preferred_element_type=F32)
    g = lax.dot_general(h, wif_ref[...], (((1,), (1,)), ((), ())), preferred_element_type=F32)
    g_ref[...] = g + bif_ref[...]


def _inproj_sample(x, mod, norm_g, w_main, w_ifT, b_if):
    n, d = x.shape
    bn = 1024
    const = lambda i: (0, 0)
    return pl.pallas_call(
        _inproj_sample_kernel,
        grid=(w_main.shape[1] // bn,),
        in_specs=[pl.BlockSpec((n, d), const),
                  pl.BlockSpec(mod.shape, const),
                  pl.BlockSpec((1, d), const),
                  pl.BlockSpec((d, bn), lambda i: (0, i)),
                  pl.BlockSpec(w_ifT.shape, const),
                  pl.BlockSpec((1, 2 * M_HEADS), const)],
        out_specs=[pl.BlockSpec((n, bn), lambda i: (0, i)),
                   pl.BlockSpec((n, 2 * M_HEADS), const)],
        out_shape=[jax.ShapeDtypeStruct((n, w_main.shape[1]), F32),
                   jax.ShapeDtypeStruct((n, 2 * M_HEADS), F32)],
        compiler_params=_params("arbitrary"),
        name="sample_inproj",
    )(x, mod, norm_g.reshape(1, d), w_main, w_ifT, b_if.reshape(1, 2 * M_HEADS))


_PAGES_PER_CHUNK = 8
_PAGES_PER_BLOCK = MOBA_BLOCK // PAGE_SIZE


def _pagescore_kernel(pt_ref, ck_ref, q_ref, out_ref, buf, sem):
    b = pl.program_id(0)
    nseq = pl.num_programs(0)
    n_pages = pt_ref.shape[1]
    cpb = n_pages // _PAGES_PER_CHUNK
    bpc = _PAGES_PER_CHUNK // _PAGES_PER_BLOCK

    def chunk_copy(bb, ci, i, slot):
        page = pt_ref[bb, ci * _PAGES_PER_CHUNK + i]
        return pltpu.make_async_copy(ck_ref.at[0, page], buf.at[slot, i], sem.at[slot])

    def start(bb, ci, slot):
        for i in range(_PAGES_PER_CHUNK):
            chunk_copy(bb, ci, i, slot).start(priority=i % 2)

    @pl.when(b == 0)
    def _():
        start(0, 0, 0)

    qb = jnp.broadcast_to(q_ref[0], buf.shape[2:])

    def body(ci, carry):
        slot = (b * cpb + ci) % 2
        nxt = ci + 1

        @pl.when(nxt < cpb)
        def _():
            start(b, nxt, 1 - slot)

        @pl.when((nxt == cpb) & (b + 1 < nseq))
        def _():
            start(b + 1, 0, 1 - slot)

        for i in range(_PAGES_PER_CHUNK):
            chunk_copy(b, ci, i, slot).wait()
        for blk in range(bpc):
            p = buf[slot, blk * _PAGES_PER_BLOCK]
            for i in range(1, _PAGES_PER_BLOCK):
                p = p + buf[slot, blk * _PAGES_PER_BLOCK + i]
            out_ref[0, ci * bpc + blk] = jnp.sum(p * qb, axis=1)
        return carry

    lax.fori_loop(0, cpb, body, 0)


def _page_block_scores(cache_kT, page_table, q_col):
    nseq, n_pages = page_table.shape
    _, _, H, dh, ps = cache_kT.shape
    nblk = n_pages // _PAGES_PER_BLOCK
    grid_spec = pltpu.PrefetchScalarGridSpec(
        num_scalar_prefetch=1,
        grid=(nseq,),
        in_specs=[pl.BlockSpec(memory_space=pl.ANY),
                  pl.BlockSpec((1, H, dh, 1), lambda b, pt: (b, 0, 0, 0))],
        out_specs=pl.BlockSpec((1, nblk, H, ps), lambda b, pt: (b, 0, 0, 0)),
        scratch_shapes=[pltpu.VMEM((2, _PAGES_PER_CHUNK, H, dh, ps), F32),
                        pltpu.SemaphoreType.DMA((2,))],
    )
    return pl.pallas_call(
        _pagescore_kernel,
        grid_spec=grid_spec,
        out_shape=jax.ShapeDtypeStruct((nseq, nblk, H, ps), F32),
        compiler_params=_params("arbitrary"),
        name="sample_page_scores",
    )(page_table, cache_kT, q_col)


def _gate_sample_kernel(part_ref, idx_ref):
    nblk, H = part_ref.shape[1], part_ref.shape[2]
    n = nblk * H
    x = part_ref[0].reshape(n, part_ref.shape[3])
    ones = jnp.ones((V7X_SUBLANES, x.shape[1]), BF16)
    nt = (((1,), (1,)), ((), ()))
    sc = sum(lax.dot_general(ones, part, nt, preferred_element_type=F32) for part in _split3(x))
    sc = sc[0:1] / MOBA_BLOCK
    lane = lax.broadcasted_iota(jnp.int32, (1, n), 1)
    cnt = jnp.zeros((1, n), jnp.int32)
    for k in range(1, nblk):
        r = pltpu.roll(sc, k * H, axis=1)
        beats = (r > sc) | ((r == sc) & (lane >= k * H))
        cnt += beats.astype(jnp.int32)
    blk = lane // H
    j = lax.broadcasted_iota(jnp.int32, (n, V7X_LANES), 0)
    hcol = lax.broadcasted_iota(jnp.int32, (n, V7X_LANES), 1)
    gather = jnp.where(j % H == hcol, 1.0, 0.0).astype(BF16)
    out = jnp.zeros((V7X_SUBLANES, V7X_LANES), jnp.int32)
    r8 = lax.broadcasted_iota(jnp.int32, (V7X_SUBLANES, V7X_LANES), 0)
    for r in range(MOBA_TOPK):
        vals = jnp.where(cnt == r, blk, 0).astype(F32)
        vals = jnp.broadcast_to(vals, (V7X_SUBLANES, n)).astype(BF16)
        idx = jnp.dot(vals, gather, preferred_element_type=F32)[0:1]
        out = jnp.where(r8 == r, idx.astype(jnp.int32), out)
    idx_ref[0] = out


def _gate_sample(partials):
    nseq, nblk, H, ps = partials.shape
    return pl.pallas_call(
        _gate_sample_kernel,
        grid=(nseq,),
        in_specs=[pl.BlockSpec((1, nblk, H, ps), lambda b: (b, 0, 0, 0))],
        out_specs=pl.BlockSpec((1, V7X_SUBLANES, V7X_LANES), lambda b: (b, 0, 0)),
        out_shape=jax.ShapeDtypeStruct((nseq, V7X_SUBLANES, V7X_LANES), jnp.int32),
        compiler_params=_params("arbitrary"),
        name="sample_gate",
    )(partials)


def _attend_sample_kernel(pt_ref, sel_ref, ck_ref, cv_ref, q_ref, kn_ref, vn_ref, o_ref,
                          kbuf, vbuf, sem):
    b = pl.program_id(0)
    nseq = pl.num_programs(0)
    H, dh = ATT_HEADS, ATT_HEAD_DIM
    npg = MOBA_TOPK * _PAGES_PER_BLOCK

    def copies(bb, slot):
        out = []
        for hh in range(H):
            for r in range(MOBA_TOPK):
                blk = sel_ref[(bb * MOBA_TOPK + r) * H + hh]
                for i in range(_PAGES_PER_BLOCK):
                    page = pt_ref[bb, blk * _PAGES_PER_BLOCK + i]
                    pg = r * _PAGES_PER_BLOCK + i
                    out.append(pltpu.make_async_copy(
                        ck_ref.at[0, page, hh], kbuf.at[slot, hh, pg], sem.at[0, slot]))
                    out.append(pltpu.make_async_copy(
                        cv_ref.at[0, page, hh], vbuf.at[slot, hh, pg], sem.at[1, slot]))
        return out

    @pl.when(b == 0)
    def _():
        for cp in copies(0, 0):
            cp.start()

    slot = b % 2

    @pl.when(b + 1 < nseq)
    def _():
        for cp in copies(b + 1, 1 - slot):
            cp.start()

    for cp in copies(b, slot):
        cp.wait()

    for hh in range(H):
        q8 = jnp.broadcast_to(q_ref[0, hh:hh + 1, :], (V7X_SUBLANES, dh)).astype(BF16)
        kn = kn_ref[0, hh:hh + 1, :].astype(BF16).astype(F32)
        vn = vn_ref[0, hh:hh + 1, :].astype(BF16).astype(F32)
        scale = ATT_HEAD_DIM ** -0.5
        s = jnp.concatenate(
            [jnp.dot(q8, kbuf[slot, hh, pg].astype(BF16), preferred_element_type=F32)[0:1]
             for pg in range(npg)], axis=1) * scale
        s_new = jnp.sum(q8[0:1].astype(F32) * kn, axis=1, keepdims=True) * scale
        m = jnp.maximum(jnp.max(s, axis=1, keepdims=True), s_new)
        p = jnp.exp(s - m)
        p_new = jnp.exp(s_new - m)
        l = jnp.sum(p, axis=1, keepdims=True) + p_new
        p8 = jnp.broadcast_to(p, (V7X_SUBLANES, p.shape[1])).astype(BF16)
        acc = p_new.astype(BF16).astype(F32) * vn
        for pg in range(npg):
            acc = acc + lax.dot_general(
                p8[:, pg * PAGE_SIZE:(pg + 1) * PAGE_SIZE], vbuf[slot, hh, pg].astype(BF16),
                (((1,), (1,)), ((), ())), preferred_element_type=F32)[0:1]
        o_ref[0, :, hh * dh:(hh + 1) * dh] = acc / l


def _attend_sample(cache_k, cache_v, page_table, sel_flat, q, k_new, v_new):
    nseq, n_pages = page_table.shape
    H, dh = ATT_HEADS, ATT_HEAD_DIM
    npg = MOBA_TOPK * _PAGES_PER_BLOCK
    ps = cache_k.shape[-1]
    grid_spec = pltpu.PrefetchScalarGridSpec(
        num_scalar_prefetch=2,
        grid=(nseq,),
        in_specs=[pl.BlockSpec(memory_space=pl.ANY),
                  pl.BlockSpec(memory_space=pl.ANY),
                  pl.BlockSpec((1, H, dh), lambda b, pt, sel: (b, 0, 0)),
                  pl.BlockSpec((1, H, dh), lambda b, pt, sel: (b, 0, 0)),
                  pl.BlockSpec((1, H, dh), lambda b, pt, sel: (b, 0, 0))],
        out_specs=pl.BlockSpec((1, 1, ATT_WIDTH), lambda b, pt, sel: (b, 0, 0)),
        scratch_shapes=[pltpu.VMEM((2, H, npg, dh, ps), F32),
                        pltpu.VMEM((2, H, npg, dh, ps), F32),
                        pltpu.SemaphoreType.DMA((2, 2))],
    )
    return pl.pallas_call(
        _attend_sample_kernel,
        grid_spec=grid_spec,
        out_shape=jax.ShapeDtypeStruct((nseq, 1, ATT_WIDTH), F32),
        compiler_params=_params("arbitrary"),
        name="sample_attend",
    )(page_table, sel_flat, cache_k, cache_v, q, k_new, v_new)


def _mlstm_sample_kernel(qk_ref, v_ref, o_ref, g_ref, conv_ref, C_ref, n_ref, wc_ref, bc_ref, ng_ref,
                         h_ref, C_out, n_out, m_out, conv_out):
    x = qk_ref[0]
    conv = conv_ref[0]
    acc = x * wc_ref[M_CONV - 1:M_CONV, :] + bc_ref[...]
    for jj in range(M_CONV - 1):
        acc += conv[jj:jj + 1, :] * wc_ref[jj:jj + 1, :]
    qk = _silu(acc)
    conv_out[0] = jnp.concatenate([conv[1:], x], axis=0)
    g = g_ref[0]
    m_new = []
    for hh in range(M_HEADS):
        q = qk[:, hh * M_QK_DIM:(hh + 1) * M_QK_DIM]
        k = qk[:, M_QK_WIDTH + hh * M_QK_DIM:M_QK_WIDTH + (hh + 1) * M_QK_DIM] * (M_QK_DIM ** -0.5)
        qb = q.astype(BF16)
        kb = k.astype(BF16)
        v = v_ref[0][:, hh * M_V_DIM:(hh + 1) * M_V_DIM]
        i_pre = g[:, hh:hh + 1]
        f_pre = g[:, M_HEADS + hh:M_HEADS + hh + 1]
        m_prev = g[:, 2 * M_HEADS + hh:2 * M_HEADS + hh + 1]
        bb = jnp.minimum(f_pre, 0.0) - jnp.log1p(jnp.exp(-jnp.abs(f_pre)))
        cc = i_pre - bb
        M = jnp.maximum(m_prev, cc)
        w = jnp.exp(cc - M)
        a = jnp.exp(m_prev - M)
        C = C_ref[0, hh]
        n = n_ref[0, hh:hh + 1, :]
        qf = qb.astype(F32)
        s = jnp.sum(qf * kb.astype(F32), axis=1, keepdims=True) * w
        q8 = jnp.broadcast_to(qb, (V7X_SUBLANES, M_QK_DIM))
        inter = lax.dot_general(q8, C.astype(BF16), (((1,), (1,)), ((), ())),
                                preferred_element_type=F32)[0:1]
        vb = v.astype(BF16).astype(F32)
        num = a * inter + s.astype(BF16).astype(F32) * vb
        den = a * jnp.sum(qf * n.astype(BF16).astype(F32), axis=1, keepdims=True) + s
        hv = num / jnp.maximum(jnp.abs(den), jnp.exp(-(bb + M)))
        hv = hv * lax.rsqrt(jnp.mean(hv * hv, axis=-1, keepdims=True) + NORM_EPS)
        sl = slice(hh * M_V_DIM, (hh + 1) * M_V_DIM)
        h_ref[0, :, sl] = hv * ng_ref[:, sl] * jax.nn.sigmoid(o_ref[0][:, sl])
        wv = jnp.broadcast_to(w * v, (V7X_SUBLANES, M_V_DIM))
        wv = jnp.concatenate([wv, jnp.zeros((V7X_LANES - V7X_SUBLANES, M_V_DIM), F32)], axis=0)
        wv_col = wv.T[:, 0:1]
        C_out[0, hh] = a * C + wv_col * k
        n_out[0, hh:hh + 1, :] = a * n + w * k
        m_new.append(bb + M)
    lane = lax.broadcasted_iota(jnp.int32, (1, V7X_LANES), 1)
    mrow = jnp.zeros((1, V7X_LANES), F32)
    for hh in range(M_HEADS):
        mrow = jnp.where(lane == hh, m_new[hh], mrow)
    m_out[0] = mrow


def _mlstm_sample(z3, g16, state_conv, state_C, state_n, w_conv, b_conv, m_norm_g):
    nseq = z3.shape[0]
    MH, dk, dv = M_HEADS, M_QK_DIM, M_V_DIM
    W = 2 * M_QK_WIDTH
    const = lambda b: (0, 0)
    return pl.pallas_call(
        _mlstm_sample_kernel,
        grid=(nseq,),
        in_specs=[
            pl.BlockSpec((1, 1, W), lambda b: (b, 0, _C_MQK // W)),
            pl.BlockSpec((1, 1, M_WIDTH), lambda b: (b, 0, _C_MV // M_WIDTH)),
            pl.BlockSpec((1, 1, M_WIDTH), lambda b: (b, 0, _C_MO // M_WIDTH)),
            pl.BlockSpec((1, 1, 4 * MH), lambda b: (b, 0, 0)),
            pl.BlockSpec((1, M_CONV - 1, W), lambda b: (b, 0, 0)),
            pl.BlockSpec((1, MH, dv, dk), lambda b: (b, 0, 0, 0)),
            pl.BlockSpec((1, MH, dk), lambda b: (b, 0, 0)),
            pl.BlockSpec((M_CONV, W), const),
            pl.BlockSpec((1, W), const),
            pl.BlockSpec((1, M_WIDTH), const),
        ],
        out_specs=[
            pl.BlockSpec((1, 1, M_WIDTH), lambda b: (b, 0, 0)),
            pl.BlockSpec((1, MH, dv, dk), lambda b: (b, 0, 0, 0)),
            pl.BlockSpec((1, MH, dk), lambda b: (b, 0, 0)),
            pl.BlockSpec((1, 1, V7X_LANES), lambda b: (b, 0, 0)),
            pl.BlockSpec((1, M_CONV - 1, W), lambda b: (b, 0, 0)),
        ],
        out_shape=[
            jax.ShapeDtypeStruct((nseq, 1, M_WIDTH), F32),
            jax.ShapeDtypeStruct((nseq, MH, dv, dk), F32),
            jax.ShapeDtypeStruct((nseq, MH, dk), F32),
            jax.ShapeDtypeStruct((nseq, 1, V7X_LANES), F32),
            jax.ShapeDtypeStruct((nseq, M_CONV - 1, W), F32),
        ],
        compiler_params=_params("arbitrary"),
        name="sample_mlstm",
    )(z3, z3, z3, g16, state_conv, state_C, state_n, w_conv, b_conv.reshape(1, W),
      m_norm_g.reshape(1, M_WIDTH))


def _merge_sample_kernel(ya_ref, hm_ref, z_ref, x_ref, mod_ref, watt_ref, wml_ref, wout_ref, fg_ref,
                         y_ref):
    d = x_ref.shape[1]
    ya = ya_ref[...] * _silu(z_ref[:, _C_AZ:_C_MQK])
    ya = jnp.dot(ya.astype(BF16), watt_ref[...], preferred_element_type=F32)
    ym = hm_ref[...] * _silu(z_ref[:, _C_MZ:_C_MIF])
    ym = jnp.dot(ym.astype(BF16), wml_ref[...], preferred_element_type=F32)
    u = jax.nn.sigmoid(z_ref[:, _P_GA:_P_GM]) * ya + jax.nn.sigmoid(z_ref[:, _P_GM:_P_END]) * ym
    upd = jnp.dot(u.astype(BF16), wout_ref[...], preferred_element_type=F32)
    y = x_ref[...] + mod_ref[:, 2 * d:3 * d] * upd
    y = y * lax.rsqrt(jnp.mean(y * y, axis=-1, keepdims=True) + NORM_EPS)
    y_ref[...] = y * fg_ref[...]


def _merge_sample(ya, hm, z, x, mod, w_att, w_mlstm, w_out, final_g):
    n, d = x.shape
    full = lambda a: pl.BlockSpec(a.shape, lambda i: (0,) * a.ndim)
    fg = final_g.reshape(1, d)
    args = (ya, hm, z, x, mod, w_att, w_mlstm, w_out, fg)
    return pl.pallas_call(
        _merge_sample_kernel,
        grid=(1,),
        in_specs=[full(a) for a in args],
        out_specs=pl.BlockSpec((n, d), lambda i: (0, 0)),
        out_shape=jax.ShapeDtypeStruct((n, d), F32),
        compiler_params=_params("arbitrary"),
        name="sample_merge",
    )(*args)


def _sample_layer(x, mod, cache_k, cache_v, page_table, state_C, state_n, state_m, state_conv,
                  norm_g, w_main, w_ifT, b_if, w_conv, b_conv, m_norm_g, w_att, w_mlstm, w_out, final_g):
    nseq = x.shape[0]
    H, dh = ATT_HEADS, ATT_HEAD_DIM
    assert page_table.shape[1] // _PAGES_PER_BLOCK >= MOBA_TOPK
    z, g = _inproj_sample(x, mod, norm_g, w_main, w_ifT, b_if)
    q = z[:, _C_AQ:_C_AK].reshape(nseq, H, dh)
    k_new = z[:, _C_AK:_C_AV].reshape(nseq, H, dh)
    v_new = z[:, _C_AV:_C_AZ].reshape(nseq, H, dh)
    cache_kT = jnp.transpose(cache_k, (0, 1, 3, 4, 2))
    cache_vT = jnp.transpose(cache_v, (0, 1, 3, 4, 2))
    partials = _page_block_scores(cache_kT, page_table, q.reshape(nseq, H, dh, 1))
    sel = _gate_sample(partials)
    sel_flat = sel[:, :MOBA_TOPK, :H].reshape(-1)
    ya = _attend_sample(cache_kT, cache_vT, page_table, sel_flat, q, k_new, v_new)
    g16 = jnp.concatenate([g, state_m, jnp.zeros_like(state_m)], axis=1).reshape(nseq, 1, 4 * M_HEADS)
    hm, C, n, m, conv_new = _mlstm_sample(z.reshape(nseq, 1, -1), g16, state_conv, state_C, state_n,
                                          w_conv, b_conv, m_norm_g)
    y = _merge_sample(ya.reshape(nseq, ATT_WIDTH), hm.reshape(nseq, M_WIDTH), z, x, mod,
                      w_att, w_mlstm, w_out, final_g)
    return (y.reshape(nseq, 1, -1), k_new.reshape(nseq, 1, H, dh), v_new.reshape(nseq, 1, H, dh),
            C, n, m[:, 0, :M_HEADS], conv_new)


def kernel(x_prompt, x_sample, cache_k, cache_v, state_C, state_n, state_m, state_conv, page_table,
           c_prompt, c_sample, w_ada, b_ada, norm_g, w_in, b_if, w_conv, b_conv, m_norm_g,
           w_att, w_mlstm, w_out, final_g):
    depth = w_in.shape[0]
    assert depth == 1, "single-layer step"
    B = x_prompt.shape[0]
    nseq = x_sample.shape[0]
    pad = (-(B + nseq)) % V7X_SUBLANES
    c_all = jnp.concatenate([c_prompt, c_sample, jnp.zeros((pad, c_prompt.shape[1]), F32)], axis=0)
    mod = _ada(c_all, w_ada[0], b_ada[0])
    mod_p = mod[:B].reshape(B, 3, D_MODEL)
    mod_s = mod[B:B + nseq]
    w_main, w_ifT = _pack_w_in(w_in[0])
    wa, wm, wo = w_att[0].astype(BF16), w_mlstm[0].astype(BF16), w_out[0].astype(BF16)
    outs_p = _prompt_layer(x_prompt, mod_p, norm_g[0], w_main, w_ifT, b_if[0], w_conv[0], b_conv[0],
                           m_norm_g[0], wa, wm, wo, final_g)
    outs_s = _sample_layer(x_sample[:, 0, :], mod_s, cache_k, cache_v, page_table,
                           state_C[0], state_n[0], state_m[0], state_conv[0],
                           norm_g[0], w_main, w_ifT, b_if[0], w_conv[0], b_conv[0], m_norm_g[0],
                           wa, wm, wo, final_g)
    y_p, *st_p = outs_p
    y_s, *st_s = outs_s
    return (y_p, y_s) + tuple(a[None] for a in st_p) + tuple(a[None] for a in st_s)
```

```python
import functools

import jax
import jax.numpy as jnp
from jax import lax
from jax.experimental import pallas as pl
from jax.experimental.pallas import tpu as pltpu

F32 = jnp.float32
BF16 = jnp.bfloat16

D_MODEL = 1024
ATT_HEADS = 8
ATT_HEAD_DIM = 64
ATT_WIDTH = ATT_HEADS * ATT_HEAD_DIM
MOBA_BLOCK = 256
MOBA_TOPK = 3
M_HEADS = 4
M_V_DIM = 256
M_QK_DIM = 128
M_QK_WIDTH = M_HEADS * M_QK_DIM
M_WIDTH = M_HEADS * M_V_DIM
M_CONV = 4
PAGE_SIZE = 128
NORM_EPS = 1e-6
NEG_SCORE = -1e30
_LOG2E = 1.4426950408889634

V7X_LANES = 128
V7X_SUBLANES = 8
V7X_VMEM_LIMIT_BYTES = 56 * 1024 * 1024

_C_AQ, _C_AK, _C_AV, _C_AZ = 0, 512, 1024, 1536
_C_MQK, _C_MV, _C_MO, _C_MZ = 2048, 3072, 4096, 5120
_C_MIF, _C_GA, _C_GM, _C_END = 6144, 6152, 7176, 8200
_P_GA, _P_GM, _P_END = 6144, 7168, 8192


def _silu(x):
    return x * jax.nn.sigmoid(x)


def _split3(x):
    hi = x.astype(BF16)
    r1 = x - hi.astype(F32)
    mid = r1.astype(BF16)
    lo = (r1 - mid.astype(F32)).astype(BF16)
    return hi, mid, lo


def _pack_w_in(w_in):
    w_main = jnp.concatenate([w_in[:, :_C_MIF], w_in[:, _C_GA:_C_END]], axis=1).astype(BF16)
    w_ifT = w_in[:, _C_MIF:_C_GA].T.astype(BF16)
    return w_main, w_ifT


def _params(*sem):
    return pltpu.CompilerParams(dimension_semantics=sem, vmem_limit_bytes=V7X_VMEM_LIMIT_BYTES)


def _ada_kernel(c_ref, w_ref, b_ref, o_ref):
    a = _silu(c_ref[...])
    a_hi = a.astype(BF16)
    a_lo = (a - a_hi.astype(F32)).astype(BF16)
    w = w_ref[...]
    w_hi = w.astype(BF16)
    w_lo = (w - w_hi.astype(F32)).astype(BF16)
    acc = jnp.dot(a_hi, w_hi, preferred_element_type=F32)
    acc += jnp.dot(a_hi, w_lo, preferred_element_type=F32)
    acc += jnp.dot(a_lo, w_hi, preferred_element_type=F32)
    o_ref[...] = acc + b_ref[...]


def _ada(c, w_ada, b_ada):
    n, d = c.shape
    d3 = w_ada.shape[1]
    bn = 512
    return pl.pallas_call(
        _ada_kernel,
        grid=(d3 // bn,),
        in_specs=[pl.BlockSpec((n, d), lambda i: (0, 0)),
                  pl.BlockSpec((d, bn), lambda i: (0, i)),
                  pl.BlockSpec((1, bn), lambda i: (0, i))],
        out_specs=pl.BlockSpec((n, bn), lambda i: (0, i)),
        out_shape=jax.ShapeDtypeStruct((n, d3), F32),
        compiler_params=_params("arbitrary"),
        name="ada_mod",
    )(c, w_ada, b_ada.reshape(1, d3))


def _inproj_kernel(x_ref, mod_ref, ng_ref, w_ref, wif_ref, bif_ref, wifn_ref, bifn_ref,
                   qT_ref, kh_ref, vT_ref, kf_ref, vf_ref, kmean_ref,
                   az_ref, mqk_ref, mv_ref, mo_ref, mz_ref, ga_ref, gm_ref, gT_ref, mvT_ref, gc_ref):
    j = pl.program_id(1)
    x = x_ref[0]
    xn = x * lax.rsqrt(jnp.mean(x * x, axis=-1, keepdims=True) + NORM_EPS) * ng_ref[...]
    shift = mod_ref[0, 0:1, :]
    scale = mod_ref[0, 1:2, :]
    h = (xn * (1.0 + scale) + shift).astype(BF16)

    def proj(a, b):
        return jnp.dot(h, w_ref[:, a:b], preferred_element_type=F32)

    q = proj(_C_AQ, _C_AK) * (ATT_HEAD_DIM ** -0.5 * _LOG2E)
    qT_ref[0, 0] = q.T.astype(BF16)
    k = proj(_C_AK, _C_AV)
    kf_ref[0] = k
    ksum = jnp.sum(k, axis=0, keepdims=True) / MOBA_BLOCK
    dh = ATT_HEAD_DIM
    tm = k.shape[0]
    one_col = jnp.where(lax.broadcasted_iota(jnp.int32, (tm, dh), 1) == 0, 1.0, 0.0)
    one_row = jnp.where(lax.broadcasted_iota(jnp.int32, (_MOBA_V_PAD, tm), 0) == 0, 1.0, 0.0)
    v = proj(_C_AV, _C_AZ)
    vf_ref[0] = v
    vT = v.T
    for hh in range(ATT_HEADS):
        sl = slice(hh * dh, (hh + 1) * dh)
        kh_ref[0, 0, hh] = jnp.concatenate([k[:, sl], one_col], axis=1).astype(BF16)
        kmean_ref[0, hh, pl.ds(j, 1), :] = ksum[:, sl]
        r0 = hh * _MOBA_V_ROWS
        vT_ref[0, 0, r0:r0 + dh, :] = vT[sl].astype(BF16)
        vT_ref[0, 0, r0 + dh:r0 + _MOBA_V_ROWS, :] = one_row.astype(BF16)
    az_ref[0] = proj(_C_AZ, _C_MQK).astype(BF16)
    mqk_ref[0] = proj(_C_MQK, _C_MV)
    mv = proj(_C_MV, _C_MO)
    mv_ref[0] = mv.astype(BF16)
    mvT_ref[0, 0] = mv.T.astype(BF16)
    mo_ref[0] = proj(_C_MO, _C_MZ).astype(BF16)
    mz_ref[0] = proj(_C_MZ, _C_MIF).astype(BF16)
    ga_ref[0] = proj(_P_GA, _P_GM).astype(BF16)
    gm_ref[0] = proj(_P_GM, _P_END).astype(BF16)
    gT = lax.dot_general(wif_ref[...], h, (((1,), (1,)), ((), ())), preferred_element_type=F32)
    gT_ref[0] = gT + bif_ref[...]
    gc_ref[0] = jnp.dot(h, wifn_ref[...], preferred_element_type=F32) + bifn_ref[...]


def _inproj(x, mod3, norm_g, w_main, w_ifT, b_if):
    B, S, D = x.shape
    tm = MOBA_BLOCK
    nb = S // tm
    H, dh = ATT_HEADS, ATT_HEAD_DIM
    const = lambda b, j: (0, 0)
    tok = lambda w: pl.BlockSpec((1, tm, w), lambda b, j: (b, j, 0))
    blk = pl.BlockSpec((1, 1, ATT_WIDTH, tm), lambda b, j: (b, j, 0, 0))
    out_shape = [
        jax.ShapeDtypeStruct((B, nb, ATT_WIDTH, tm), BF16),
        jax.ShapeDtypeStruct((B, nb, H, tm, 2 * dh), BF16),
        jax.ShapeDtypeStruct((B, nb, H * _MOBA_V_ROWS, tm), BF16),
        jax.ShapeDtypeStruct((B, S, ATT_WIDTH), F32),
        jax.ShapeDtypeStruct((B, S, ATT_WIDTH), F32),
        jax.ShapeDtypeStruct((B, H, nb, dh), F32),
        jax.ShapeDtypeStruct((B, S, ATT_WIDTH), BF16),
        jax.ShapeDtypeStruct((B, S, 2 * M_QK_WIDTH), F32),
        jax.ShapeDtypeStruct((B, S, M_WIDTH), BF16),
        jax.ShapeDtypeStruct((B, S, M_WIDTH), BF16),
        jax.ShapeDtypeStruct((B, S, M_WIDTH), BF16),
        jax.ShapeDtypeStruct((B, S, D), BF16),
        jax.ShapeDtypeStruct((B, S, D), BF16),
        jax.ShapeDtypeStruct((B, 2 * M_HEADS, S), F32),
        jax.ShapeDtypeStruct((B, nb, M_WIDTH, tm), BF16),
        jax.ShapeDtypeStruct((B, S, 2 * M_HEADS), F32),
    ]
    out_specs = [
        blk,
        pl.BlockSpec((1, 1, H, tm, 2 * dh), lambda b, j: (b, j, 0, 0, 0)),
        pl.BlockSpec((1, 1, H * _MOBA_V_ROWS, tm), lambda b, j: (b, j, 0, 0)),
        tok(ATT_WIDTH), tok(ATT_WIDTH),
        pl.BlockSpec((1, H, nb, dh), lambda b, j: (b, 0, 0, 0)),
        tok(ATT_WIDTH), tok(2 * M_QK_WIDTH), tok(M_WIDTH), tok(M_WIDTH), tok(M_WIDTH), tok(D), tok(D),
        pl.BlockSpec((1, 2 * M_HEADS, tm), lambda b, j: (b, 0, j)),
        pl.BlockSpec((1, 1, M_WIDTH, tm), lambda b, j: (b, j, 0, 0)),
        tok(2 * M_HEADS),
    ]
    in_specs = [
        pl.BlockSpec((1, tm, D), lambda b, j: (b, j, 0)),
        pl.BlockSpec((1, 3, D), lambda b, j: (b, 0, 0)),
        pl.BlockSpec((1, D), const),
        pl.BlockSpec(w_main.shape, const, pipeline_mode=pl.Buffered(1)),
        pl.BlockSpec(w_ifT.shape, const),
        pl.BlockSpec((2 * M_HEADS, 1), const),
        pl.BlockSpec((D, 2 * M_HEADS), const),
        pl.BlockSpec((1, 2 * M_HEADS), const),
    ]
    return pl.pallas_call(
        _inproj_kernel,
        grid=(B, nb),
        in_specs=in_specs,
        out_specs=out_specs,
        out_shape=out_shape,
        compiler_params=_params("arbitrary", "arbitrary"),
        name="prompt_inproj",
    )(x, mod3, norm_g.reshape(1, D), w_main, w_ifT, b_if.reshape(2 * M_HEADS, 1),
      w_ifT.T, b_if.reshape(1, 2 * M_HEADS))


_MOBA_HEADS_PER_STEP = 8
_MOBA_V_PAD = 16
_MOBA_V_ROWS = ATT_HEAD_DIM + _MOBA_V_PAD


def _moba_kernel(pt_ref, qT_ref, kh_ref, vT_ref, kmean_ref, ck_ref, qb_ref, o_ref, sc_ref,
                 bias_ref, qs_ref, acc_ref, pbuf, qbuf, psem, qsem):
    b = pl.program_id(0)
    j = pl.program_id(2)
    nbq = pl.num_programs(2)
    nseq, n_pages = pt_ref.shape
    cpb = n_pages // _PAGES_PER_CHUNK
    bpc = _PAGES_PER_CHUNK // _PAGES_PER_BLOCK
    n_chunks = nseq * cpb
    ring = pbuf.shape[0]
    ahead = ring - 1

    def chunk_copy(g, i):
        page = pt_ref[g // cpb, (g % cpb) * _PAGES_PER_CHUNK + i]
        slot = g % ring
        return pltpu.make_async_copy(ck_ref.at[0, page], pbuf.at[slot, i], psem.at[slot])

    def q_copy(seq):
        return pltpu.make_async_copy(qb_ref.at[seq], qbuf.at[seq % 2], qsem.at[seq % 2])

    def start_chunk(g):
        @pl.when(g < n_chunks)
        def _():
            for i in range(_PAGES_PER_CHUNK):
                chunk_copy(g, i).start()

    @pl.when((b == 0) & (j == 0))
    def _():
        sc_ref[...] = jnp.zeros_like(sc_ref)
        q_copy(0).start()
        for g in range(ahead):
            start_chunk(g)

    lane_blk = lax.broadcasted_iota(jnp.int32, sc_ref.shape[1:], 1)

    def page_phase(g):
        live = g < n_chunks
        gc = jnp.minimum(g, n_chunks - 1)
        seq = gc // cpb
        ci = gc % cpb
        slot = gc % ring

        @pl.when(live & (ci == 0))
        def _():
            q_copy(seq).wait()

        @pl.when(live & (ci == 0) & (seq + 1 < nseq))
        def _():
            q_copy(seq + 1).start()

        start_chunk(g + ahead)

        @pl.when(live)
        def _():
            for i in range(_PAGES_PER_CHUNK):
                chunk_copy(gc, i).wait()

        qb = qbuf[seq % 2]
        row = sc_ref[seq]
        for blk in range(bpc):
            p = pbuf[slot, blk * _PAGES_PER_BLOCK]
            for i in range(1, _PAGES_PER_BLOCK):
                p = p + pbuf[slot, blk * _PAGES_PER_BLOCK + i]
            part = jnp.sum(p * qb, axis=1)
            val = jnp.sum(part, axis=1, keepdims=True)
            row = jnp.where(lane_blk == ci * bpc + blk, val, row)
        sc_ref[seq] = row

    phase0 = b * (nbq * (nbq + 1) // 2) + j * (j + 1) // 2
    hp = kmean_ref.shape[1]
    nb = kmean_ref.shape[2]
    tq = qT_ref.shape[3]
    dh = ATT_HEAD_DIM
    vr = _MOBA_V_ROWS
    qTs = [qT_ref[0, 0, hh * dh:(hh + 1) * dh, :] for hh in range(hp)]
    for hh in range(hp):
        qs_ref[hh, 0:dh, :] = qTs[hh]
        qs_ref[hh, dh:2 * dh, :] = jnp.zeros((dh, tq), BF16)

    blk = lax.broadcasted_iota(jnp.int32, (nb, tq), 0)
    past = blk < j
    for hh in range(hp):
        sg = sum(jnp.dot(part, qTs[hh], preferred_element_type=F32)
                 for part in _split3(kmean_ref[0, hh]))
        sg = jnp.where(past, sg, NEG_SCORE)
        cnt = jnp.zeros((nb, tq), jnp.int32)
        for m in range(nb):
            row = sg[m:m + 1, :]
            beats = (row > sg) | ((row == sg) & (blk > m))
            cnt += beats.astype(jnp.int32)
        bias_ref[hh] = jnp.where((cnt < MOBA_TOPK) & past, 0.0, NEG_SCORE)

    kpos = lax.broadcasted_iota(jnp.int32, (tq, tq), 0)
    qpos = lax.broadcasted_iota(jnp.int32, (tq, tq), 1)
    causal = kpos <= qpos

    def scores(n):
        return [jnp.dot(kh_ref[0, n, hh], qs_ref[hh], preferred_element_type=F32)
                for hh in range(hp)]

    def values(n, ps):
        return [jnp.dot(vT_ref[0, n, hh * vr:(hh + 1) * vr, :], ps[hh], preferred_element_type=F32)
                for hh in range(hp)]

    ss = [jnp.where(causal, s, NEG_SCORE) for s in scores(j)]
    ms = [jnp.max(s, axis=0, keepdims=True) for s in ss]
    pbs = [jnp.exp2((s - m).astype(BF16)) for s, m in zip(ss, ms)]
    for hh, x in enumerate(values(j, pbs)):
        acc_ref[hh] = x
    page_phase(phase0)

    row0 = lax.broadcasted_iota(jnp.int32, (_MOBA_V_PAD, tq), 0) == 0

    def body(n, ms):
        page_phase(phase0 + 1 + n)
        for hh in range(hp):
            mask_row = jnp.where(row0, bias_ref[hh, pl.ds(n, 1), :], 0.0)
            qs_ref[hh, dh:dh + _MOBA_V_PAD, :] = mask_row.astype(BF16)
        ss = scores(n)
        m_new = [jnp.maximum(m, jnp.max(s, axis=0, keepdims=True)) for m, s in zip(ms, ss)]
        alphas = [jnp.exp2(m - mn) for m, mn in zip(ms, m_new)]
        pbs = [jnp.exp2((s - mn).astype(BF16)) for s, mn in zip(ss, m_new)]
        for hh, x in enumerate(values(n, pbs)):
            acc_ref[hh] = alphas[hh] * acc_ref[hh] + x
        return tuple(m_new)

    lax.fori_loop(0, j, body, tuple(ms))
    for hh in range(hp):
        acc = acc_ref[hh]
        o_ref[0, 0, hh * dh:(hh + 1) * dh, :] = (acc[0:dh] / acc[dh:dh + 1]).astype(o_ref.dtype)


def _moba_prompt(qT, kh, vT, kmean, page_table, cache_kT, q_lanes):
    B, nb, _, tq = qT.shape
    H, dh, hp = ATT_HEADS, ATT_HEAD_DIM, _MOBA_HEADS_PER_STEP
    nseq, n_pages = page_table.shape
    ps = cache_kT.shape[-1]
    assert hp == H, "the page stream is scheduled over a (B, 1, nb) grid"
    assert n_pages % _PAGES_PER_CHUNK == 0 and n_pages // _PAGES_PER_BLOCK <= V7X_LANES
    n_chunks = nseq * (n_pages // _PAGES_PER_CHUNK)
    assert _PAGE_RING - 1 <= n_chunks <= B * nb * (nb + 1) // 2, "one page chunk per key-block phase"
    grid_spec = pltpu.PrefetchScalarGridSpec(
        num_scalar_prefetch=1,
        grid=(B, H // hp, nb),
        in_specs=[
            pl.BlockSpec((1, 1, hp * dh, tq), lambda b, g, j, pt: (b, j, g, 0)),
            pl.BlockSpec((1, nb, hp, tq, 2 * dh), lambda b, g, j, pt: (b, 0, g, 0, 0)),
            pl.BlockSpec((1, nb, hp * _MOBA_V_ROWS, tq), lambda b, g, j, pt: (b, 0, g, 0)),
            pl.BlockSpec((1, hp, nb, dh), lambda b, g, j, pt: (b, g, 0, 0)),
            pl.BlockSpec(memory_space=pl.ANY),
            pl.BlockSpec(memory_space=pl.ANY),
        ],
        out_specs=[
            pl.BlockSpec((1, 1, hp * dh, tq), lambda b, g, j, pt: (b, j, g, 0)),
            pl.BlockSpec((nseq, H, V7X_LANES), lambda b, g, j, pt: (0, 0, 0)),
        ],
        scratch_shapes=[
            pltpu.VMEM((hp, nb, tq), F32), pltpu.VMEM((hp, 2 * dh, tq), BF16),
            pltpu.VMEM((hp, _MOBA_V_ROWS, tq), F32),
            pltpu.VMEM((_PAGE_RING, _PAGES_PER_CHUNK, H, dh, ps), F32),
            pltpu.VMEM((2, H, dh, ps), F32),
            pltpu.SemaphoreType.DMA((_PAGE_RING,)), pltpu.SemaphoreType.DMA((2,)),
        ],
    )
    return pl.pallas_call(
        _moba_kernel,
        grid_spec=grid_spec,
        out_shape=[jax.ShapeDtypeStruct((B, nb, ATT_WIDTH, tq), BF16),
                   jax.ShapeDtypeStruct((nseq, H, V7X_LANES), F32)],
        compiler_params=_params("arbitrary", "arbitrary", "arbitrary"),
        name="prompt_moba",
    )(page_table, qT, kh, vT, kmean, cache_kT, q_lanes)


def _log_sigmoid(x):
    return jnp.minimum(x, 0.0) - jnp.log1p(jnp.exp(-jnp.abs(x)))


def _causal_conv(x, tail_ref, w_ref, b_ref):
    L = x.shape[0]
    prev = tail_ref[...]
    row = lax.broadcasted_iota(jnp.int32, (V7X_SUBLANES, x.shape[1]), 0)
    acc = x * w_ref[M_CONV - 1:M_CONV, :] + b_ref[...]
    for d in range(1, M_CONV):
        r = pltpu.roll(x, d, axis=0)
        head = jnp.where(row < d, pltpu.roll(prev, d, axis=0), r[:V7X_SUBLANES])
        shifted = jnp.concatenate([head, r[V7X_SUBLANES:]], axis=0)
        acc += shifted * w_ref[M_CONV - 1 - d:M_CONV - d, :]
    tail_ref[...] = x[L - V7X_SUBLANES:, :]
    return acc


def _mlstm_kernel(qk_ref, v_ref, vT_ref, g_ref, gc_ref, o_ref, wc_ref, bc_ref, ng_ref,
                  h_ref, C_out, n_out, m_out,
                  C_s, n_s, m_s, tail_s):
    c = pl.program_id(1)
    L = qk_ref.shape[1]
    MH, dk, dv = M_HEADS, M_QK_DIM, M_V_DIM
    nt = (((1,), (1,)), ((), ()))

    @pl.when(c == 0)
    def _():
        C_s[...] = jnp.zeros_like(C_s)
        n_s[...] = jnp.zeros_like(n_s)
        m_s[...] = jnp.zeros_like(m_s)
        tail_s[...] = jnp.zeros_like(tail_s)

    qk = _silu(_causal_conv(qk_ref[0], tail_s, wc_ref, bc_ref))

    t_idx = lax.broadcasted_iota(jnp.int32, (L, L), 0)
    s_idx = lax.broadcasted_iota(jnp.int32, (L, L), 1)
    causal = s_idx <= t_idx
    tri = jnp.where(causal, 1.0, 0.0).astype(BF16)
    g_rows = g_ref[0]
    b_rows = sum(lax.dot_general(part, tri, nt, preferred_element_type=F32)
                 for part in _split3(_log_sigmoid(g_rows)))
    b_cols = sum(jnp.dot(tri, part, preferred_element_type=F32)
                 for part in _split3(_log_sigmoid(gc_ref[0])))

    for hd in range(MH):
        qb = qk[:, hd * dk:(hd + 1) * dk].astype(BF16)
        kb = (qk[:, M_QK_WIDTH + hd * dk:M_QK_WIDTH + (hd + 1) * dk] * (dk ** -0.5)).astype(BF16)
        vsl = slice(hd * dv, (hd + 1) * dv)
        b_row = b_rows[MH + hd:MH + hd + 1]
        c_row = g_rows[hd:hd + 1] - b_row
        b_col = b_cols[:, MH + hd:MH + hd + 1]
        m_prev = m_s[hd:hd + 1, 0:1]
        M_col = jnp.maximum(jnp.max(jnp.where(causal, c_row, NEG_SCORE), axis=1, keepdims=True),
                            m_prev)
        M_last = jnp.maximum(jnp.max(c_row, axis=1, keepdims=True), m_prev)
        w = jnp.where(causal, jnp.exp(c_row - M_col), 0.0)
        a_col = jnp.exp(m_prev - M_col)

        s = lax.dot_general(qb, kb, nt, preferred_element_type=F32) * w
        inter = lax.dot_general(qb, C_s[hd].astype(BF16), nt, preferred_element_type=F32)
        num = a_col * inter + jnp.dot(s.astype(BF16), v_ref[0, :, vsl], preferred_element_type=F32)
        n_b = n_s[hd:hd + 1, :].astype(BF16).astype(F32)
        qn = jnp.sum(qb.astype(F32) * n_b, axis=1, keepdims=True)
        den = a_col * qn + jnp.sum(s, axis=1, keepdims=True)
        hh = num / jnp.maximum(jnp.abs(den), jnp.exp(-(b_col + M_col)))
        hh = hh * lax.rsqrt(jnp.mean(hh * hh, axis=-1, keepdims=True) + NORM_EPS)
        gate = jax.nn.sigmoid(o_ref[0, :, vsl].astype(F32))
        h_ref[0, :, vsl] = (hh * ng_ref[:, vsl] * gate).astype(h_ref.dtype)

        ws_row = jnp.exp(c_row - M_last)
        a_end = jnp.exp(m_prev - M_last)
        vwT = (vT_ref[0, 0, vsl, :].astype(F32) * ws_row).astype(BF16)
        C_s[hd] = a_end * C_s[hd] + jnp.dot(vwT, kb, preferred_element_type=F32)
        ws8 = jnp.broadcast_to(ws_row, (V7X_SUBLANES, L)).astype(BF16)
        n_s[hd:hd + 1, :] = (a_end * n_s[hd:hd + 1, :]
                             + jnp.dot(ws8, kb, preferred_element_type=F32)[0:1])
        m_s[hd:hd + 1, :] = jnp.broadcast_to(b_row[:, L - 1:L] + M_last, (1, m_s.shape[1]))

    @pl.when(c == pl.num_programs(1) - 1)
    def _():
        C_out[0] = C_s[...]
        n_out[0] = n_s[...]
        m_out[0] = m_s[...]


def _mlstm_prompt(mqk, mv, mvT, gT, gc, mo, w_conv, b_conv, m_norm_g):
    B, S, W = mqk.shape
    L = mvT.shape[3]
    nc = S // L
    MH, dk, dv = M_HEADS, M_QK_DIM, M_V_DIM
    const = lambda b, c: (0, 0)
    tok = lambda w: pl.BlockSpec((1, L, w), lambda b, c: (b, c, 0))
    in_specs = [
        tok(W), tok(M_WIDTH),
        pl.BlockSpec((1, 1, M_WIDTH, L), lambda b, c: (b, c, 0, 0)),
        pl.BlockSpec((1, 2 * MH, L), lambda b, c: (b, 0, c)),
        tok(2 * MH), tok(M_WIDTH),
        pl.BlockSpec((M_CONV, W), const),
        pl.BlockSpec((1, W), const),
        pl.BlockSpec((1, M_WIDTH), const),
    ]
    out_shape = [
        jax.ShapeDtypeStruct((B, S, M_WIDTH), BF16),
        jax.ShapeDtypeStruct((B, MH, dv, dk), F32),
        jax.ShapeDtypeStruct((B, V7X_SUBLANES, dk), F32),
        jax.ShapeDtypeStruct((B, V7X_SUBLANES, V7X_LANES), F32),
    ]
    out_specs = [
        tok(M_WIDTH),
        pl.BlockSpec((1, MH, dv, dk), lambda b, c: (b, 0, 0, 0)),
        pl.BlockSpec((1, V7X_SUBLANES, dk), lambda b, c: (b, 0, 0)),
        pl.BlockSpec((1, V7X_SUBLANES, V7X_LANES), lambda b, c: (b, 0, 0)),
    ]
    scratch = [
        pltpu.VMEM((MH, dv, dk), F32), pltpu.VMEM((V7X_SUBLANES, dk), F32),
        pltpu.VMEM((V7X_SUBLANES, V7X_LANES), F32), pltpu.VMEM((V7X_SUBLANES, W), F32),
    ]
    return pl.pallas_call(
        _mlstm_kernel,
        grid=(B, nc),
        in_specs=in_specs,
        out_specs=out_specs,
        out_shape=out_shape,
        scratch_shapes=scratch,
        compiler_params=_params("arbitrary", "arbitrary"),
        name="prompt_mlstm",
    )(mqk, mv, mvT, gT, gc, mo, w_conv, b_conv.reshape(1, W), m_norm_g.reshape(1, M_WIDTH))


def _merge_kernel(yaT_ref, az_ref, hm_ref, mz_ref, ga_ref, gm_ref, x_ref, mod_ref,
                  watt_ref, wml_ref, wout_ref, fg_ref, y_ref):
    ya = yaT_ref[0, 0].astype(F32).T * _silu(az_ref[0].astype(F32))
    ya = jnp.dot(ya.astype(BF16), watt_ref[...], preferred_element_type=F32)
    ym = hm_ref[0].astype(F32) * _silu(mz_ref[0].astype(F32))
    ym = jnp.dot(ym.astype(BF16), wml_ref[...], preferred_element_type=F32)
    u = (jax.nn.sigmoid(ga_ref[0].astype(F32)) * ya
         + jax.nn.sigmoid(gm_ref[0].astype(F32)) * ym)
    upd = jnp.dot(u.astype(BF16), wout_ref[...], preferred_element_type=F32)
    y = x_ref[0] + mod_ref[0, 2:3, :] * upd
    y = y * lax.rsqrt(jnp.mean(y * y, axis=-1, keepdims=True) + NORM_EPS)
    y_ref[0] = y * fg_ref[...]


def _merge_prompt(yaT, az, hm, mz, ga, gm, x, mod3, w_att, w_mlstm, w_out, final_g):
    B, S, D = x.shape
    tm = yaT.shape[3]
    nb = S // tm
    const = lambda b, j: (0, 0)
    tok = lambda w: pl.BlockSpec((1, tm, w), lambda b, j: (b, j, 0))
    return pl.pallas_call(
        _merge_kernel,
        grid=(B, nb),
        in_specs=[
            pl.BlockSpec((1, 1, ATT_WIDTH, tm), lambda b, j: (b, j, 0, 0)),
            tok(ATT_WIDTH), tok(M_WIDTH), tok(M_WIDTH), tok(D), tok(D), tok(D),
            pl.BlockSpec((1, 3, D), lambda b, j: (b, 0, 0)),
            pl.BlockSpec(w_att.shape, const),
            pl.BlockSpec(w_mlstm.shape, const),
            pl.BlockSpec(w_out.shape, const),
            pl.BlockSpec((1, D), const),
        ],
        out_specs=tok(D),
        out_shape=jax.ShapeDtypeStruct((B, S, D), F32),
        compiler_params=_params("arbitrary", "arbitrary"),
        name="prompt_merge",
    )(yaT, az, hm, mz, ga, gm, x, mod3, w_att, w_mlstm, w_out, final_g.reshape(1, D))


def _prompt_layer(x, mod3, norm_g, w_main, w_ifT, b_if, w_conv, b_conv, m_norm_g,
                  w_att, w_mlstm, w_out, final_g, page_table, cache_kT, q_lanes):
    B, S, _ = x.shape
    (qT, kh, vT, kf, vf, kmean, az, mqk, mv, mo, mz, ga, gm, gT, mvT, gc) = _inproj(
        x, mod3, norm_g, w_main, w_ifT, b_if)
    yaT, block_scores = _moba_prompt(qT, kh, vT, kmean, page_table, cache_kT, q_lanes)
    hm, C, n, m = _mlstm_prompt(mqk, mv, mvT, gT, gc, mo, w_conv, b_conv, m_norm_g)
    y = _merge_prompt(yaT, az, hm, mz, ga, gm, x, mod3, w_att, w_mlstm, w_out, final_g)
    k_rows = kf.reshape(B, S, ATT_HEADS, ATT_HEAD_DIM)
    v_rows = vf.reshape(B, S, ATT_HEADS, ATT_HEAD_DIM)
    conv_new = mqk[:, S - (M_CONV - 1):, :]
    return y, k_rows, v_rows, C, n[:, :M_HEADS, :], m[:, :M_HEADS, 0], conv_new, block_scores


def _inproj_sample_kernel(x_ref, mod_ref, ng_ref, w_ref, wif_ref, bif_ref, z_ref, g_ref):
    x = x_ref[...]
    d = x.shape[1]
    xn = x * lax.rsqrt(jnp.mean(x * x, axis=-1, keepdims=True) + NORM_EPS) * ng_ref[...]
    h = (xn * (1.0 + mod_ref[:, d:2 * d]) + mod_ref[:, 0:d]).astype(BF16)
    z_ref[...] = jnp.dot(h, w_ref[...], preferred_element_type=F32)
    g = lax.dot_general(h, wif_ref[...], (((1,), (1,)), ((), ())), preferred_element_type=F32)
    g_ref[...] = g + bif_ref[...]


def _inproj_sample(x, mod, norm_g, w_main, w_ifT, b_if):
    n, d = x.shape
    bn = 1024
    const = lambda i: (0, 0)
    return pl.pallas_call(
        _inproj_sample_kernel,
        grid=(w_main.shape[1] // bn,),
        in_specs=[pl.BlockSpec((n, d), const),
                  pl.BlockSpec(mod.shape, const),
                  pl.BlockSpec((1, d), const),
                  pl.BlockSpec((d, bn), lambda i: (0, i)),
                  pl.BlockSpec(w_ifT.shape, const),
                  pl.BlockSpec((1, 2 * M_HEADS), const)],
        out_specs=[pl.BlockSpec((n, bn), lambda i: (0, i)),
                   pl.BlockSpec((n, 2 * M_HEADS), const)],
        out_shape=[jax.ShapeDtypeStruct((n, w_main.shape[1]), F32),
                   jax.ShapeDtypeStruct((n, 2 * M_HEADS), F32)],
        compiler_params=_params("arbitrary"),
        name="sample_inproj",
    )(x, mod, norm_g.reshape(1, d), w_main, w_ifT, b_if.reshape(1, 2 * M_HEADS))


_PAGES_PER_CHUNK = 8
_PAGE_RING = 4
_PAGES_PER_BLOCK = MOBA_BLOCK // PAGE_SIZE


def _attend_sample_kernel(pt_ref, sel_ref, ck_ref, cv_ref, q_ref, kn_ref, vn_ref, o_ref,
                          kbuf, vbuf, sem):
    b = pl.program_id(0)
    nseq = pl.num_programs(0)
    H, dh = ATT_HEADS, ATT_HEAD_DIM
    npg = MOBA_TOPK * _PAGES_PER_BLOCK

    def copies(bb, slot):
        out = []
        for hh in range(H):
            for r in range(MOBA_TOPK):
                blk = sel_ref[(bb * H + hh) * MOBA_TOPK + r]
                for i in range(_PAGES_PER_BLOCK):
                    page = pt_ref[bb, blk * _PAGES_PER_BLOCK + i]
                    pg = r * _PAGES_PER_BLOCK + i
                    out.append(pltpu.make_async_copy(
                        ck_ref.at[0, page, hh], kbuf.at[slot, hh, pg], sem.at[0, slot]))
                    out.append(pltpu.make_async_copy(
                        cv_ref.at[0, page, hh], vbuf.at[slot, hh, pg], sem.at[1, slot]))
        return out

    @pl.when(b == 0)
    def _():
        for cp in copies(0, 0):
            cp.start()

    slot = b % 2

    @pl.when(b + 1 < nseq)
    def _():
        for cp in copies(b + 1, 1 - slot):
            cp.start()

    for cp in copies(b, slot):
        cp.wait()

    for hh in range(H):
        q8 = jnp.broadcast_to(q_ref[0, hh:hh + 1, :], (V7X_SUBLANES, dh)).astype(BF16)
        kn = kn_ref[0, hh:hh + 1, :].astype(BF16).astype(F32)
        vn = vn_ref[0, hh:hh + 1, :].astype(BF16).astype(F32)
        scale = ATT_HEAD_DIM ** -0.5
        s = jnp.concatenate(
            [jnp.dot(q8, kbuf[slot, hh, pg].astype(BF16), preferred_element_type=F32)[0:1]
             for pg in range(npg)], axis=1) * scale
        s_new = jnp.sum(q8[0:1].astype(F32) * kn, axis=1, keepdims=True) * scale
        m = jnp.maximum(jnp.max(s, axis=1, keepdims=True), s_new)
        p = jnp.exp(s - m)
        p_new = jnp.exp(s_new - m)
        l = jnp.sum(p, axis=1, keepdims=True) + p_new
        p8 = jnp.broadcast_to(p, (V7X_SUBLANES, p.shape[1])).astype(BF16)
        acc = p_new.astype(BF16).astype(F32) * vn
        for pg in range(npg):
            acc = acc + lax.dot_general(
                p8[:, pg * PAGE_SIZE:(pg + 1) * PAGE_SIZE], vbuf[slot, hh, pg].astype(BF16),
                (((1,), (1,)), ((), ())), preferred_element_type=F32)[0:1]
        o_ref[0, :, hh * dh:(hh + 1) * dh] = acc / l


def _attend_sample(cache_k, cache_v, page_table, sel_flat, q, k_new, v_new):
    nseq, n_pages = page_table.shape
    H, dh = ATT_HEADS, ATT_HEAD_DIM
    npg = MOBA_TOPK * _PAGES_PER_BLOCK
    ps = cache_k.shape[-1]
    grid_spec = pltpu.PrefetchScalarGridSpec(
        num_scalar_prefetch=2,
        grid=(nseq,),
        in_specs=[pl.BlockSpec(memory_space=pl.ANY),
                  pl.BlockSpec(memory_space=pl.ANY),
                  pl.BlockSpec((1, H, dh), lambda b, pt, sel: (b, 0, 0)),
                  pl.BlockSpec((1, H, dh), lambda b, pt, sel: (b, 0, 0)),
                  pl.BlockSpec((1, H, dh), lambda b, pt, sel: (b, 0, 0))],
        out_specs=pl.BlockSpec((1, 1, ATT_WIDTH), lambda b, pt, sel: (b, 0, 0)),
        scratch_shapes=[pltpu.VMEM((2, H, npg, dh, ps), F32),
                        pltpu.VMEM((2, H, npg, dh, ps), F32),
                        pltpu.SemaphoreType.DMA((2, 2))],
    )
    return pl.pallas_call(
        _attend_sample_kernel,
        grid_spec=grid_spec,
        out_shape=jax.ShapeDtypeStruct((nseq, 1, ATT_WIDTH), F32),
        compiler_params=_params("arbitrary"),
        name="sample_attend",
    )(page_table, sel_flat, cache_k, cache_v, q, k_new, v_new)


def _mlstm_sample_kernel(qk_ref, v_ref, o_ref, g_ref, conv_ref, C_ref, n_ref, wc_ref, bc_ref, ng_ref,
                         h_ref, C_out, n_out, m_out, conv_out):
    x = qk_ref[0]
    conv = conv_ref[0]
    acc = x * wc_ref[M_CONV - 1:M_CONV, :] + bc_ref[...]
    for jj in range(M_CONV - 1):
        acc += conv[jj:jj + 1, :] * wc_ref[jj:jj + 1, :]
    qk = _silu(acc)
    conv_out[0] = jnp.concatenate([conv[1:], x], axis=0)
    g = g_ref[0]
    m_new = []
    for hh in range(M_HEADS):
        q = qk[:, hh * M_QK_DIM:(hh + 1) * M_QK_DIM]
        k = qk[:, M_QK_WIDTH + hh * M_QK_DIM:M_QK_WIDTH + (hh + 1) * M_QK_DIM] * (M_QK_DIM ** -0.5)
        qb = q.astype(BF16)
        kb = k.astype(BF16)
        v = v_ref[0][:, hh * M_V_DIM:(hh + 1) * M_V_DIM]
        i_pre = g[:, hh:hh + 1]
        f_pre = g[:, M_HEADS + hh:M_HEADS + hh + 1]
        m_prev = g[:, 2 * M_HEADS + hh:2 * M_HEADS + hh + 1]
        bb = jnp.minimum(f_pre, 0.0) - jnp.log1p(jnp.exp(-jnp.abs(f_pre)))
        cc = i_pre - bb
        M = jnp.maximum(m_prev, cc)
        w = jnp.exp(cc - M)
        a = jnp.exp(m_prev - M)
        C = C_ref[0, hh]
        n = n_ref[0, hh:hh + 1, :]
        qf = qb.astype(F32)
        s = jnp.sum(qf * kb.astype(F32), axis=1, keepdims=True) * w
        q8 = jnp.broadcast_to(qb, (V7X_SUBLANES, M_QK_DIM))
        inter = lax.dot_general(q8, C.astype(BF16), (((1,), (1,)), ((), ())),
                                preferred_element_type=F32)[0:1]
        vb = v.astype(BF16).astype(F32)
        num = a * inter + s.astype(BF16).astype(F32) * vb
        den = a * jnp.sum(qf * n.astype(BF16).astype(F32), axis=1, keepdims=True) + s
        hv = num / jnp.maximum(jnp.abs(den), jnp.exp(-(bb + M)))
        hv = hv * lax.rsqrt(jnp.mean(hv * hv, axis=-1, keepdims=True) + NORM_EPS)
        sl = slice(hh * M_V_DIM, (hh + 1) * M_V_DIM)
        h_ref[0, :, sl] = hv * ng_ref[:, sl] * jax.nn.sigmoid(o_ref[0][:, sl])
        wv = jnp.broadcast_to(w * v, (V7X_SUBLANES, M_V_DIM))
        wv = jnp.concatenate([wv, jnp.zeros((V7X_LANES - V7X_SUBLANES, M_V_DIM), F32)], axis=0)
        wv_col = wv.T[:, 0:1]
        C_out[0, hh] = a * C + wv_col * k
        n_out[0, hh:hh + 1, :] = a * n + w * k
        m_new.append(bb + M)
    lane = lax.broadcasted_iota(jnp.int32, (1, V7X_LANES), 1)
    mrow = jnp.zeros((1, V7X_LANES), F32)
    for hh in range(M_HEADS):
        mrow = jnp.where(lane == hh, m_new[hh], mrow)
    m_out[0] = mrow


def _mlstm_sample(z3, g16, state_conv, state_C, state_n, w_conv, b_conv, m_norm_g):
    nseq = z3.shape[0]
    MH, dk, dv = M_HEADS, M_QK_DIM, M_V_DIM
    W = 2 * M_QK_WIDTH
    const = lambda b: (0, 0)
    return pl.pallas_call(
        _mlstm_sample_kernel,
        grid=(nseq,),
        in_specs=[
            pl.BlockSpec((1, 1, W), lambda b: (b, 0, _C_MQK // W)),
            pl.BlockSpec((1, 1, M_WIDTH), lambda b: (b, 0, _C_MV // M_WIDTH)),
            pl.BlockSpec((1, 1, M_WIDTH), lambda b: (b, 0, _C_MO // M_WIDTH)),
            pl.BlockSpec((1, 1, 4 * MH), lambda b: (b, 0, 0)),
            pl.BlockSpec((1, M_CONV - 1, W), lambda b: (b, 0, 0)),
            pl.BlockSpec((1, MH, dv, dk), lambda b: (b, 0, 0, 0)),
            pl.BlockSpec((1, MH, dk), lambda b: (b, 0, 0)),
            pl.BlockSpec((M_CONV, W), const),
            pl.BlockSpec((1, W), const),
            pl.BlockSpec((1, M_WIDTH), const),
        ],
        out_specs=[
            pl.BlockSpec((1, 1, M_WIDTH), lambda b: (b, 0, 0)),
            pl.BlockSpec((1, MH, dv, dk), lambda b: (b, 0, 0, 0)),
            pl.BlockSpec((1, MH, dk), lambda b: (b, 0, 0)),
            pl.BlockSpec((1, 1, V7X_LANES), lambda b: (b, 0, 0)),
            pl.BlockSpec((1, M_CONV - 1, W), lambda b: (b, 0, 0)),
        ],
        out_shape=[
            jax.ShapeDtypeStruct((nseq, 1, M_WIDTH), F32),
            jax.ShapeDtypeStruct((nseq, MH, dv, dk), F32),
            jax.ShapeDtypeStruct((nseq, MH, dk), F32),
            jax.ShapeDtypeStruct((nseq, 1, V7X_LANES), F32),
            jax.ShapeDtypeStruct((nseq, M_CONV - 1, W), F32),
        ],
        compiler_params=_params("arbitrary"),
        name="sample_mlstm",
    )(z3, z3, z3, g16, state_conv, state_C, state_n, w_conv, b_conv.reshape(1, W),
      m_norm_g.reshape(1, M_WIDTH))


def _merge_sample_kernel(ya_ref, hm_ref, z_ref, x_ref, mod_ref, watt_ref, wml_ref, wout_ref, fg_ref,
                         y_ref):
    d = x_ref.shape[1]
    ya = ya_ref[...] * _silu(z_ref[:, _C_AZ:_C_MQK])
    ya = jnp.dot(ya.astype(BF16), watt_ref[...], preferred_element_type=F32)
    ym = hm_ref[...] * _silu(z_ref[:, _C_MZ:_C_MIF])
    ym = jnp.dot(ym.astype(BF16), wml_ref[...], preferred_element_type=F32)
    u = jax.nn.sigmoid(z_ref[:, _P_GA:_P_GM]) * ya + jax.nn.sigmoid(z_ref[:, _P_GM:_P_END]) * ym
    upd = jnp.dot(u.astype(BF16), wout_ref[...], preferred_element_type=F32)
    y = x_ref[...] + mod_ref[:, 2 * d:3 * d] * upd
    y = y * lax.rsqrt(jnp.mean(y * y, axis=-1, keepdims=True) + NORM_EPS)
    y_ref[...] = y * fg_ref[...]


def _merge_sample(ya, hm, z, x, mod, w_att, w_mlstm, w_out, final_g):
    n, d = x.shape
    full = lambda a: pl.BlockSpec(a.shape, lambda i: (0,) * a.ndim)
    fg = final_g.reshape(1, d)
    args = (ya, hm, z, x, mod, w_att, w_mlstm, w_out, fg)
    return pl.pallas_call(
        _merge_sample_kernel,
        grid=(1,),
        in_specs=[full(a) for a in args],
        out_specs=pl.BlockSpec((n, d), lambda i: (0, 0)),
        out_shape=jax.ShapeDtypeStruct((n, d), F32),
        compiler_params=_params("arbitrary"),
        name="sample_merge",
    )(*args)


def _gate_scores_kernel(sc_ref, idx_ref, *, nblk):
    sc = sc_ref[0] / MOBA_BLOCK
    lane = lax.broadcasted_iota(jnp.int32, sc.shape, 1)
    cnt = jnp.zeros(sc.shape, jnp.int32)
    for m in range(nblk):
        col = sc[:, m:m + 1]
        beats = (col > sc) | ((col == sc) & (lane > m))
        cnt += beats.astype(jnp.int32)
    out = jnp.zeros(sc.shape, jnp.int32)
    for r in range(MOBA_TOPK):
        idx = jnp.sum(jnp.where((cnt == r) & (lane < nblk), lane, 0), axis=1, keepdims=True)
        out = jnp.where(lane == r, idx, out)
    idx_ref[0] = out


def _gate_scores(block_scores, nblk):
    nseq, H, lanes = block_scores.shape
    return pl.pallas_call(
        functools.partial(_gate_scores_kernel, nblk=nblk),
        grid=(nseq,),
        in_specs=[pl.BlockSpec((1, H, lanes), lambda b: (b, 0, 0))],
        out_specs=pl.BlockSpec((1, H, lanes), lambda b: (b, 0, 0)),
        out_shape=jax.ShapeDtypeStruct((nseq, H, lanes), jnp.int32),
        compiler_params=_params("arbitrary"),
        name="sample_gate",
    )(block_scores)


def _sample_inputs(x, mod, norm_g, w_main, w_ifT, b_if):
    nseq = x.shape[0]
    H, dh = ATT_HEADS, ATT_HEAD_DIM
    z, g = _inproj_sample(x, mod, norm_g, w_main, w_ifT, b_if)
    q = z[:, _C_AQ:_C_AK].reshape(nseq, H, dh)
    q_lanes = jnp.broadcast_to(q[..., None], (nseq, H, dh, PAGE_SIZE))
    return z, g, q, q_lanes


def _sample_layer(x, mod, z, g, q, block_scores, cache_kT, cache_vT, page_table,
                  state_C, state_n, state_m, state_conv,
                  w_conv, b_conv, m_norm_g, w_att, w_mlstm, w_out, final_g):
    nseq = x.shape[0]
    H, dh = ATT_HEADS, ATT_HEAD_DIM
    nblk = page_table.shape[1] // _PAGES_PER_BLOCK
    assert nblk >= MOBA_TOPK
    k_new = z[:, _C_AK:_C_AV].reshape(nseq, H, dh)
    v_new = z[:, _C_AV:_C_AZ].reshape(nseq, H, dh)
    sel = _gate_scores(block_scores, nblk)
    sel_flat = sel[:, :, :MOBA_TOPK].reshape(-1)
    ya = _attend_sample(cache_kT, cache_vT, page_table, sel_flat, q, k_new, v_new)
    g16 = jnp.concatenate([g, state_m, jnp.zeros_like(state_m)], axis=1).reshape(nseq, 1, 4 * M_HEADS)
    hm, C, n, m, conv_new = _mlstm_sample(z.reshape(nseq, 1, -1), g16, state_conv, state_C, state_n,
                                          w_conv, b_conv, m_norm_g)
    y = _merge_sample(ya.reshape(nseq, ATT_WIDTH), hm.reshape(nseq, M_WIDTH), z, x, mod,
                      w_att, w_mlstm, w_out, final_g)
    return (y.reshape(nseq, 1, -1), k_new.reshape(nseq, 1, H, dh), v_new.reshape(nseq, 1, H, dh),
            C, n, m[:, 0, :M_HEADS], conv_new)


def kernel(x_prompt, x_sample, cache_k, cache_v, state_C, state_n, state_m, state_conv, page_table,
           c_prompt, c_sample, w_ada, b_ada, norm_g, w_in, b_if, w_conv, b_conv, m_norm_g,
           w_att, w_mlstm, w_out, final_g):
    depth = w_in.shape[0]
    assert depth == 1, "single-layer step"
    B = x_prompt.shape[0]
    nseq = x_sample.shape[0]
    pad = (-(B + nseq)) % V7X_SUBLANES
    c_all = jnp.concatenate([c_prompt, c_sample, jnp.zeros((pad, c_prompt.shape[1]), F32)], axis=0)
    mod = _ada(c_all, w_ada[0], b_ada[0])
    mod_p = mod[:B].reshape(B, 3, D_MODEL)
    mod_s = mod[B:B + nseq]
    w_main, w_ifT = _pack_w_in(w_in[0])
    wa, wm, wo = w_att[0].astype(BF16), w_mlstm[0].astype(BF16), w_out[0].astype(BF16)
    cache_kT = jnp.transpose(cache_k, (0, 1, 3, 4, 2))
    cache_vT = jnp.transpose(cache_v, (0, 1, 3, 4, 2))
    xs = x_sample[:, 0, :]
    z, g, q, q_lanes = _sample_inputs(xs, mod_s, norm_g[0], w_main, w_ifT, b_if[0])
    *outs_p, block_scores = _prompt_layer(
        x_prompt, mod_p, norm_g[0], w_main, w_ifT, b_if[0], w_conv[0], b_conv[0], m_norm_g[0],
        wa, wm, wo, final_g, page_table, cache_kT, q_lanes)
    outs_s = _sample_layer(xs, mod_s, z, g, q, block_scores, cache_kT, cache_vT, page_table,
                           state_C[0], state_n[0], state_m[0], state_conv[0],
                           w_conv[0], b_conv[0], m_norm_g[0], wa, wm, wo, final_g)
    y_p, *st_p = outs_p
    y_s, *st_s = outs_s
    return (y_p, y_s) + tuple(a[None] for a in st_p) + tuple(a[None] for a in st_s)
```

```python
import functools

import jax
import jax.numpy as jnp
from jax import lax
from jax.experimental import pallas as pl
from jax.experimental.pallas import tpu as pltpu

F32 = jnp.float32
BF16 = jnp.bfloat16

D_MODEL = 1024
ATT_HEADS = 8
ATT_HEAD_DIM = 64
ATT_WIDTH = ATT_HEADS * ATT_HEAD_DIM
MOBA_BLOCK = 256
MOBA_TOPK = 3
M_HEADS = 4
M_V_DIM = 256
M_QK_DIM = 128
M_QK_WIDTH = M_HEADS * M_QK_DIM
M_WIDTH = M_HEADS * M_V_DIM
M_CONV = 4
PAGE_SIZE = 128
NORM_EPS = 1e-6
NEG_SCORE = -1e30
_LOG2E = 1.4426950408889634

V7X_LANES = 128
V7X_SUBLANES = 8
V7X_VMEM_LIMIT_BYTES = 56 * 1024 * 1024

_C_AQ, _C_AK, _C_AV, _C_AZ = 0, 512, 1024, 1536
_C_MQK, _C_MV, _C_MO, _C_MZ = 2048, 3072, 4096, 5120
_C_MIF, _C_GA, _C_GM, _C_END = 6144, 6152, 7176, 8200


def _silu(x):
    return x * jax.nn.sigmoid(x)


def _split3(x):
    hi = x.astype(BF16)
    r1 = x - hi.astype(F32)
    mid = r1.astype(BF16)
    lo = (r1 - mid.astype(F32)).astype(BF16)
    return hi, mid, lo


def _pack_w_in(w_in):
    pad = (-w_in.shape[1]) % V7X_LANES
    w_main = jnp.pad(w_in.astype(BF16), ((0, 0), (0, pad)))
    w_ifT = w_in[:, _C_MIF:_C_GA].T.astype(BF16)
    return w_main, w_ifT


def _params(*sem):
    return pltpu.CompilerParams(dimension_semantics=sem, vmem_limit_bytes=V7X_VMEM_LIMIT_BYTES)


def _ada_kernel(c_ref, w_ref, b_ref, o_ref):
    a = _silu(c_ref[...])
    a_hi = a.astype(BF16)
    a_lo = (a - a_hi.astype(F32)).astype(BF16)
    w = w_ref[...]
    w_hi = w.astype(BF16)
    w_lo = (w - w_hi.astype(F32)).astype(BF16)
    acc = jnp.dot(a_hi, w_hi, preferred_element_type=F32)
    acc += jnp.dot(a_hi, w_lo, preferred_element_type=F32)
    acc += jnp.dot(a_lo, w_hi, preferred_element_type=F32)
    o_ref[...] = acc + b_ref[...]


def _ada(c, w_ada, b_ada):
    n, d = c.shape
    d3 = w_ada.shape[1]
    bn = 512
    return pl.pallas_call(
        _ada_kernel,
        grid=(d3 // bn,),
        in_specs=[pl.BlockSpec((n, d), lambda i: (0, 0)),
                  pl.BlockSpec((d, bn), lambda i: (0, i)),
                  pl.BlockSpec((1, bn), lambda i: (0, i))],
        out_specs=pl.BlockSpec((n, bn), lambda i: (0, i)),
        out_shape=jax.ShapeDtypeStruct((n, d3), F32),
        compiler_params=_params("arbitrary"),
        name="ada_mod",
    )(c, w_ada, b_ada.reshape(1, d3))


def _inproj_kernel(x_ref, mod_ref, ng_ref, w_ref, wif_ref, bif_ref, bifn_ref, wc_ref, bc_ref,
                   qT_ref, kh_ref, vT_ref, kf_ref, vf_ref, kmean_ref,
                   az_ref, qk_ref, mv_ref, mo_ref, mz_ref, ga_ref, gm_ref, gT_ref, mvT_ref, gc_ref,
                   ctail_ref, tail_s):
    j = pl.program_id(1)
    x = x_ref[0]
    xn = x * lax.rsqrt(jnp.mean(x * x, axis=-1, keepdims=True) + NORM_EPS) * ng_ref[...]
    shift = mod_ref[0, 0:1, :]
    scale = mod_ref[0, 1:2, :]
    h = (xn * (1.0 + scale) + shift).astype(BF16)

    def proj(a, b):
        return jnp.dot(h, w_ref[:, a:b], preferred_element_type=F32)

    @pl.when(j == 0)
    def _():
        tail_s[...] = jnp.zeros_like(tail_s)

    mqk = proj(_C_MQK, _C_MV)
    ctail_ref[0] = mqk[mqk.shape[0] - V7X_SUBLANES:, :]
    qk = _silu(_causal_conv(mqk, tail_s, wc_ref, bc_ref))
    qk_ref[0, :, 0:M_QK_WIDTH] = qk[:, 0:M_QK_WIDTH].astype(BF16)
    qk_ref[0, :, M_QK_WIDTH:] = (qk[:, M_QK_WIDTH:] * (M_QK_DIM ** -0.5)).astype(BF16)

    q = proj(_C_AQ, _C_AK) * (ATT_HEAD_DIM ** -0.5 * _LOG2E)
    qT_ref[0, 0] = q.T.astype(BF16)
    k = proj(_C_AK, _C_AV)
    kf_ref[0] = k
    ksum = jnp.sum(k, axis=0, keepdims=True) / MOBA_BLOCK
    dh = ATT_HEAD_DIM
    tm = k.shape[0]
    one_col = jnp.where(lax.broadcasted_iota(jnp.int32, (tm, dh), 1) == 0, 1.0, 0.0)
    one_row = jnp.where(lax.broadcasted_iota(jnp.int32, (_MOBA_V_PAD, tm), 0) == 0, 1.0, 0.0)
    v = proj(_C_AV, _C_AZ)
    vf_ref[0] = v
    vT = v.T
    for hh in range(ATT_HEADS):
        sl = slice(hh * dh, (hh + 1) * dh)
        kh_ref[0, 0, hh] = jnp.concatenate([k[:, sl], one_col], axis=1).astype(BF16)
        kmean_ref[0, hh, pl.ds(j, 1), :] = ksum[:, sl]
        r0 = hh * _MOBA_V_ROWS
        vT_ref[0, 0, r0:r0 + dh, :] = vT[sl].astype(BF16)
        vT_ref[0, 0, r0 + dh:r0 + _MOBA_V_ROWS, :] = one_row.astype(BF16)
    az_ref[0] = proj(_C_AZ, _C_MQK).astype(BF16)
    mv = proj(_C_MV, _C_MO)
    mv_ref[0] = mv.astype(BF16)
    mvT_ref[0, 0] = mv.T.astype(BF16)
    mo_ref[0] = proj(_C_MO, _C_MZ).astype(BF16)
    mz_ref[0] = proj(_C_MZ, _C_MIF).astype(BF16)
    tail = proj(_C_MIF, w_ref.shape[1])
    gc_ref[0] = tail[:, 0:_C_GA - _C_MIF] + bifn_ref[...]
    ga_ref[0] = tail[:, _C_GA - _C_MIF:_C_GM - _C_MIF].astype(BF16)
    gm_ref[0] = tail[:, _C_GM - _C_MIF:_C_END - _C_MIF].astype(BF16)
    gT = lax.dot_general(wif_ref[...], h, (((1,), (1,)), ((), ())), preferred_element_type=F32)
    gT_ref[0] = gT + bif_ref[...]


def _inproj(x, mod3, norm_g, w_main, w_ifT, b_if, w_conv, b_conv):
    B, S, D = x.shape
    W = 2 * M_QK_WIDTH
    tm = MOBA_BLOCK
    nb = S // tm
    H, dh = ATT_HEADS, ATT_HEAD_DIM
    const = lambda b, j: (0, 0)
    tok = lambda w: pl.BlockSpec((1, tm, w), lambda b, j: (b, j, 0))
    blk = pl.BlockSpec((1, 1, ATT_WIDTH, tm), lambda b, j: (b, j, 0, 0))
    out_shape = [
        jax.ShapeDtypeStruct((B, nb, ATT_WIDTH, tm), BF16),
        jax.ShapeDtypeStruct((B, nb, H, tm, 2 * dh), BF16),
        jax.ShapeDtypeStruct((B, nb, H * _MOBA_V_ROWS, tm), BF16),
        jax.ShapeDtypeStruct((B, S, ATT_WIDTH), F32),
        jax.ShapeDtypeStruct((B, S, ATT_WIDTH), F32),
        jax.ShapeDtypeStruct((B, H, nb, dh), F32),
        jax.ShapeDtypeStruct((B, S, ATT_WIDTH), BF16),
        jax.ShapeDtypeStruct((B, S, W), BF16),
        jax.ShapeDtypeStruct((B, S, M_WIDTH), BF16),
        jax.ShapeDtypeStruct((B, S, M_WIDTH), BF16),
        jax.ShapeDtypeStruct((B, S, M_WIDTH), BF16),
        jax.ShapeDtypeStruct((B, S, D), BF16),
        jax.ShapeDtypeStruct((B, S, D), BF16),
        jax.ShapeDtypeStruct((B, 2 * M_HEADS, S), F32),
        jax.ShapeDtypeStruct((B, nb, M_WIDTH, tm), BF16),
        jax.ShapeDtypeStruct((B, S, 2 * M_HEADS), F32),
        jax.ShapeDtypeStruct((B, V7X_SUBLANES, W), F32),
    ]
    out_specs = [
        blk,
        pl.BlockSpec((1, 1, H, tm, 2 * dh), lambda b, j: (b, j, 0, 0, 0)),
        pl.BlockSpec((1, 1, H * _MOBA_V_ROWS, tm), lambda b, j: (b, j, 0, 0)),
        tok(ATT_WIDTH), tok(ATT_WIDTH),
        pl.BlockSpec((1, H, nb, dh), lambda b, j: (b, 0, 0, 0)),
        tok(ATT_WIDTH), tok(2 * M_QK_WIDTH), tok(M_WIDTH), tok(M_WIDTH), tok(M_WIDTH), tok(D), tok(D),
        pl.BlockSpec((1, 2 * M_HEADS, tm), lambda b, j: (b, 0, j)),
        pl.BlockSpec((1, 1, M_WIDTH, tm), lambda b, j: (b, j, 0, 0)),
        tok(2 * M_HEADS),
        pl.BlockSpec((1, V7X_SUBLANES, W), lambda b, j: (b, 0, 0)),
    ]
    in_specs = [
        pl.BlockSpec((1, tm, D), lambda b, j: (b, j, 0)),
        pl.BlockSpec((1, 3, D), lambda b, j: (b, 0, 0)),
        pl.BlockSpec((1, D), const),
        pl.BlockSpec(w_main.shape, const, pipeline_mode=pl.Buffered(1)),
        pl.BlockSpec(w_ifT.shape, const),
        pl.BlockSpec((2 * M_HEADS, 1), const),
        pl.BlockSpec((1, 2 * M_HEADS), const),
        pl.BlockSpec((M_CONV, W), const),
        pl.BlockSpec((1, W), const),
    ]
    return pl.pallas_call(
        _inproj_kernel,
        grid=(B, nb),
        in_specs=in_specs,
        out_specs=out_specs,
        out_shape=out_shape,
        scratch_shapes=[pltpu.VMEM((V7X_SUBLANES, W), F32)],
        compiler_params=_params("arbitrary", "arbitrary"),
        name="prompt_inproj",
    )(x, mod3, norm_g.reshape(1, D), w_main, w_ifT, b_if.reshape(2 * M_HEADS, 1),
      b_if.reshape(1, 2 * M_HEADS), w_conv, b_conv.reshape(1, W))


_MOBA_HEADS_PER_STEP = 8
_MOBA_V_PAD = 16
_MOBA_V_ROWS = ATT_HEAD_DIM + _MOBA_V_PAD


def _moba_kernel(pt_ref, qT_ref, kh_ref, vT_ref, kmean_ref, ck_ref, qb_ref, o_ref, sc_ref,
                 bias_ref, qs_ref, acc_ref, pbuf, qbuf, psem, qsem):
    b = pl.program_id(0)
    j = pl.program_id(2)
    nbq = pl.num_programs(2)
    nseq, n_pages = pt_ref.shape
    cpb = n_pages // _PAGES_PER_CHUNK
    bpc = _PAGES_PER_CHUNK // _PAGES_PER_BLOCK
    n_chunks = nseq * cpb
    ring = pbuf.shape[0]
    ahead = ring - 1

    def chunk_copy(g, i):
        page = pt_ref[g // cpb, (g % cpb) * _PAGES_PER_CHUNK + i]
        slot = g % ring
        return pltpu.make_async_copy(ck_ref.at[0, page], pbuf.at[slot, i], psem.at[slot])

    def q_copy(seq):
        return pltpu.make_async_copy(qb_ref.at[seq], qbuf.at[seq % 2], qsem.at[seq % 2])

    def start_chunk(g):
        @pl.when(g < n_chunks)
        def _():
            for i in range(_PAGES_PER_CHUNK):
                chunk_copy(g, i).start()

    @pl.when((b == 0) & (j == 0))
    def _():
        sc_ref[...] = jnp.zeros_like(sc_ref)
        q_copy(0).start()
        for g in range(ahead):
            start_chunk(g)

    lane_blk = lax.broadcasted_iota(jnp.int32, sc_ref.shape[1:], 1)

    def page_phase(g):
        live = g < n_chunks
        gc = jnp.minimum(g, n_chunks - 1)
        seq = gc // cpb
        ci = gc % cpb
        slot = gc % ring

        @pl.when(live & (ci == 0))
        def _():
            q_copy(seq).wait()

        @pl.when(live & (ci == 0) & (seq + 1 < nseq))
        def _():
            q_copy(seq + 1).start()

        start_chunk(g + ahead)

        @pl.when(live)
        def _():
            for i in range(_PAGES_PER_CHUNK):
                chunk_copy(gc, i).wait()

        qb = qbuf[seq % 2]
        row = sc_ref[seq]
        for blk in range(bpc):
            p = pbuf[slot, blk * _PAGES_PER_BLOCK]
            for i in range(1, _PAGES_PER_BLOCK):
                p = p + pbuf[slot, blk * _PAGES_PER_BLOCK + i]
            part = jnp.sum(p * qb, axis=1)
            val = jnp.sum(part, axis=1, keepdims=True)
            row = jnp.where(lane_blk == ci * bpc + blk, val, row)
        sc_ref[seq] = row

    phase0 = b * (nbq * (nbq + 1) // 2) + j * (j + 1) // 2
    hp = kmean_ref.shape[1]
    nb = kmean_ref.shape[2]
    tq = qT_ref.shape[3]
    dh = ATT_HEAD_DIM
    vr = _MOBA_V_ROWS
    qTs = [qT_ref[0, 0, hh * dh:(hh + 1) * dh, :] for hh in range(hp)]
    for hh in range(hp):
        qs_ref[hh, 0:dh, :] = qTs[hh]
        qs_ref[hh, dh:2 * dh, :] = jnp.zeros((dh, tq), BF16)

    blk = lax.broadcasted_iota(jnp.int32, (nb, tq), 0)
    past = blk < j
    for hh in range(hp):
        sg = sum(jnp.dot(part, qTs[hh], preferred_element_type=F32)
                 for part in _split3(kmean_ref[0, hh]))
        sg = jnp.where(past, sg, NEG_SCORE)
        cnt = jnp.zeros((nb, tq), jnp.int32)
        for m in range(nb):
            row = sg[m:m + 1, :]
            beats = (row > sg) | ((row == sg) & (blk > m))
            cnt += beats.astype(jnp.int32)
        bias_ref[hh] = jnp.where((cnt < MOBA_TOPK) & past, 0.0, NEG_SCORE)

    kpos = lax.broadcasted_iota(jnp.int32, (tq, tq), 0)
    qpos = lax.broadcasted_iota(jnp.int32, (tq, tq), 1)
    causal = kpos <= qpos

    def scores(n):
        return [jnp.dot(kh_ref[0, n, hh], qs_ref[hh], preferred_element_type=F32)
                for hh in range(hp)]

    def values(n, ps):
        return [jnp.dot(vT_ref[0, n, hh * vr:(hh + 1) * vr, :], ps[hh], preferred_element_type=F32)
                for hh in range(hp)]

    ss = [jnp.where(causal, s, NEG_SCORE) for s in scores(j)]
    ms = [jnp.max(s, axis=0, keepdims=True) for s in ss]
    pbs = [jnp.exp2((s - m).astype(BF16)) for s, m in zip(ss, ms)]
    for hh, x in enumerate(values(j, pbs)):
        acc_ref[hh] = x
    page_phase(phase0)

    row0 = lax.broadcasted_iota(jnp.int32, (_MOBA_V_PAD, tq), 0) == 0

    def body(n, ms):
        page_phase(phase0 + 1 + n)
        for hh in range(hp):
            mask_row = jnp.where(row0, bias_ref[hh, pl.ds(n, 1), :], 0.0)
            qs_ref[hh, dh:dh + _MOBA_V_PAD, :] = mask_row.astype(BF16)
        ss = scores(n)
        m_new = [jnp.maximum(m, jnp.max(s, axis=0, keepdims=True)) for m, s in zip(ms, ss)]
        alphas = [jnp.exp2(m - mn) for m, mn in zip(ms, m_new)]
        pbs = [jnp.exp2((s - mn).astype(BF16)) for s, mn in zip(ss, m_new)]
        for hh, x in enumerate(values(n, pbs)):
            acc_ref[hh] = alphas[hh] * acc_ref[hh] + x
        return tuple(m_new)

    lax.fori_loop(0, j, body, tuple(ms))
    for hh in range(hp):
        acc = acc_ref[hh]
        o_ref[0, 0, hh * dh:(hh + 1) * dh, :] = (acc[0:dh] / acc[dh:dh + 1]).astype(o_ref.dtype)


def _moba_prompt(qT, kh, vT, kmean, page_table, cache_kT, q_lanes):
    B, nb, _, tq = qT.shape
    H, dh, hp = ATT_HEADS, ATT_HEAD_DIM, _MOBA_HEADS_PER_STEP
    nseq, n_pages = page_table.shape
    ps = cache_kT.shape[-1]
    assert hp == H, "the page stream is scheduled over a (B, 1, nb) grid"
    assert n_pages % _PAGES_PER_CHUNK == 0 and n_pages // _PAGES_PER_BLOCK <= V7X_LANES
    n_chunks = nseq * (n_pages // _PAGES_PER_CHUNK)
    assert _PAGE_RING - 1 <= n_chunks <= B * nb * (nb + 1) // 2, "one page chunk per key-block phase"
    grid_spec = pltpu.PrefetchScalarGridSpec(
        num_scalar_prefetch=1,
        grid=(B, H // hp, nb),
        in_specs=[
            pl.BlockSpec((1, 1, hp * dh, tq), lambda b, g, j, pt: (b, j, g, 0)),
            pl.BlockSpec((1, nb, hp, tq, 2 * dh), lambda b, g, j, pt: (b, 0, g, 0, 0)),
            pl.BlockSpec((1, nb, hp * _MOBA_V_ROWS, tq), lambda b, g, j, pt: (b, 0, g, 0)),
            pl.BlockSpec((1, hp, nb, dh), lambda b, g, j, pt: (b, g, 0, 0)),
            pl.BlockSpec(memory_space=pl.ANY),
            pl.BlockSpec(memory_space=pl.ANY),
        ],
        out_specs=[
            pl.BlockSpec((1, 1, hp * dh, tq), lambda b, g, j, pt: (b, j, g, 0)),
            pl.BlockSpec((nseq, H, V7X_LANES), lambda b, g, j, pt: (0, 0, 0)),
        ],
        scratch_shapes=[
            pltpu.VMEM((hp, nb, tq), F32), pltpu.VMEM((hp, 2 * dh, tq), BF16),
            pltpu.VMEM((hp, _MOBA_V_ROWS, tq), F32),
            pltpu.VMEM((_PAGE_RING, _PAGES_PER_CHUNK, H, dh, ps), F32),
            pltpu.VMEM((2, H, dh, ps), F32),
            pltpu.SemaphoreType.DMA((_PAGE_RING,)), pltpu.SemaphoreType.DMA((2,)),
        ],
    )
    return pl.pallas_call(
        _moba_kernel,
        grid_spec=grid_spec,
        out_shape=[jax.ShapeDtypeStruct((B, nb, ATT_WIDTH, tq), BF16),
                   jax.ShapeDtypeStruct((nseq, H, V7X_LANES), F32)],
        compiler_params=_params("arbitrary", "arbitrary", "arbitrary"),
        name="prompt_moba",
    )(page_table, qT, kh, vT, kmean, cache_kT, q_lanes)


def _log_sigmoid(x):
    return jnp.minimum(x, 0.0) - jnp.log1p(jnp.exp(-jnp.abs(x)))


def _causal_conv(x, tail_ref, w_ref, b_ref):
    L = x.shape[0]
    prev = tail_ref[...]
    row = lax.broadcasted_iota(jnp.int32, (V7X_SUBLANES, x.shape[1]), 0)
    acc = x * w_ref[M_CONV - 1:M_CONV, :] + b_ref[...]
    for d in range(1, M_CONV):
        r = pltpu.roll(x, d, axis=0)
        head = jnp.where(row < d, pltpu.roll(prev, d, axis=0), r[:V7X_SUBLANES])
        shifted = jnp.concatenate([head, r[V7X_SUBLANES:]], axis=0)
        acc += shifted * w_ref[M_CONV - 1 - d:M_CONV - d, :]
    tail_ref[...] = x[L - V7X_SUBLANES:, :]
    return acc


def _mlstm_kernel(qk_ref, v_ref, vT_ref, g_ref, gc_ref, o_ref, ng_ref,
                  h_ref, C_out, n_out, m_out,
                  C_s, n_s, m_s):
    c = pl.program_id(1)
    L = qk_ref.shape[1]
    MH, dk, dv = M_HEADS, M_QK_DIM, M_V_DIM
    nt = (((1,), (1,)), ((), ()))

    @pl.when(c == 0)
    def _():
        C_s[...] = jnp.zeros_like(C_s)
        n_s[...] = jnp.zeros_like(n_s)
        m_s[...] = jnp.zeros_like(m_s)

    t_idx = lax.broadcasted_iota(jnp.int32, (L, L), 0)
    s_idx = lax.broadcasted_iota(jnp.int32, (L, L), 1)
    causal = s_idx <= t_idx
    tri = jnp.where(causal, 1.0, 0.0).astype(BF16)
    g_rows = g_ref[0]
    b_rows = sum(lax.dot_general(part, tri, nt, preferred_element_type=F32)
                 for part in _split3(_log_sigmoid(g_rows)))
    b_cols = sum(jnp.dot(tri, part, preferred_element_type=F32)
                 for part in _split3(_log_sigmoid(gc_ref[0])))

    for hd in range(MH):
        qb = qk_ref[0, :, hd * dk:(hd + 1) * dk]
        kb = qk_ref[0, :, M_QK_WIDTH + hd * dk:M_QK_WIDTH + (hd + 1) * dk]
        vsl = slice(hd * dv, (hd + 1) * dv)
        b_row = b_rows[MH + hd:MH + hd + 1]
        c_row = g_rows[hd:hd + 1] - b_row
        b_col = b_cols[:, MH + hd:MH + hd + 1]
        m_prev = m_s[hd:hd + 1, 0:1]
        M_col = jnp.maximum(jnp.max(jnp.where(causal, c_row, NEG_SCORE), axis=1, keepdims=True),
                            m_prev)
        M_last = jnp.maximum(jnp.max(c_row, axis=1, keepdims=True), m_prev)
        w = jnp.where(causal, jnp.exp(c_row - M_col), 0.0)
        a_col = jnp.exp(m_prev - M_col)

        s = lax.dot_general(qb, kb, nt, preferred_element_type=F32) * w
        inter = lax.dot_general(qb, C_s[hd].astype(BF16), nt, preferred_element_type=F32)
        num = a_col * inter + jnp.dot(s.astype(BF16), v_ref[0, :, vsl], preferred_element_type=F32)
        n_b = n_s[hd:hd + 1, :].astype(BF16).astype(F32)
        qn = jnp.sum(qb.astype(F32) * n_b, axis=1, keepdims=True)
        den = a_col * qn + jnp.sum(s, axis=1, keepdims=True)
        hh = num / jnp.maximum(jnp.abs(den), jnp.exp(-(b_col + M_col)))
        hh = hh * lax.rsqrt(jnp.mean(hh * hh, axis=-1, keepdims=True) + NORM_EPS)
        gate = jax.nn.sigmoid(o_ref[0, :, vsl].astype(F32))
        h_ref[0, :, vsl] = (hh * ng_ref[:, vsl] * gate).astype(h_ref.dtype)

        ws_row = jnp.exp(c_row - M_last)
        a_end = jnp.exp(m_prev - M_last)
        vwT = (vT_ref[0, 0, vsl, :].astype(F32) * ws_row).astype(BF16)
        C_s[hd] = a_end * C_s[hd] + jnp.dot(vwT, kb, preferred_element_type=F32)
        ws8 = jnp.broadcast_to(ws_row, (V7X_SUBLANES, L)).astype(BF16)
        n_s[hd:hd + 1, :] = (a_end * n_s[hd:hd + 1, :]
                             + jnp.dot(ws8, kb, preferred_element_type=F32)[0:1])
        m_s[hd:hd + 1, :] = jnp.broadcast_to(b_row[:, L - 1:L] + M_last, (1, m_s.shape[1]))

    @pl.when(c == pl.num_programs(1) - 1)
    def _():
        C_out[0] = C_s[...]
        n_out[0] = n_s[...]
        m_out[0] = m_s[...]


def _mlstm_prompt(qk, mv, mvT, gT, gc, mo, m_norm_g):
    B, S, W = qk.shape
    L = mvT.shape[3]
    nc = S // L
    MH, dk, dv = M_HEADS, M_QK_DIM, M_V_DIM
    const = lambda b, c: (0, 0)
    tok = lambda w: pl.BlockSpec((1, L, w), lambda b, c: (b, c, 0))
    in_specs = [
        tok(W), tok(M_WIDTH),
        pl.BlockSpec((1, 1, M_WIDTH, L), lambda b, c: (b, c, 0, 0)),
        pl.BlockSpec((1, 2 * MH, L), lambda b, c: (b, 0, c)),
        tok(2 * MH), tok(M_WIDTH),
        pl.BlockSpec((1, M_WIDTH), const),
    ]
    out_shape = [
        jax.ShapeDtypeStruct((B, S, M_WIDTH), BF16),
        jax.ShapeDtypeStruct((B, MH, dv, dk), F32),
        jax.ShapeDtypeStruct((B, V7X_SUBLANES, dk), F32),
        jax.ShapeDtypeStruct((B, V7X_SUBLANES, V7X_LANES), F32),
    ]
    out_specs = [
        tok(M_WIDTH),
        pl.BlockSpec((1, MH, dv, dk), lambda b, c: (b, 0, 0, 0)),
        pl.BlockSpec((1, V7X_SUBLANES, dk), lambda b, c: (b, 0, 0)),
        pl.BlockSpec((1, V7X_SUBLANES, V7X_LANES), lambda b, c: (b, 0, 0)),
    ]
    scratch = [
        pltpu.VMEM((MH, dv, dk), F32), pltpu.VMEM((V7X_SUBLANES, dk), F32),
        pltpu.VMEM((V7X_SUBLANES, V7X_LANES), F32),
    ]
    return pl.pallas_call(
        _mlstm_kernel,
        grid=(B, nc),
        in_specs=in_specs,
        out_specs=out_specs,
        out_shape=out_shape,
        scratch_shapes=scratch,
        compiler_params=_params("arbitrary", "arbitrary"),
        name="prompt_mlstm",
    )(qk, mv, mvT, gT, gc, mo, m_norm_g.reshape(1, M_WIDTH))


def _merge_kernel(yaT_ref, az_ref, hm_ref, mz_ref, ga_ref, gm_ref, x_ref, mod_ref,
                  watt_ref, wml_ref, wout_ref, fg_ref, y_ref):
    ya = yaT_ref[0, 0].astype(F32).T * _silu(az_ref[0].astype(F32))
    ya = jnp.dot(ya.astype(BF16), watt_ref[...], preferred_element_type=F32)
    ym = hm_ref[0].astype(F32) * _silu(mz_ref[0].astype(F32))
    ym = jnp.dot(ym.astype(BF16), wml_ref[...], preferred_element_type=F32)
    u = (jax.nn.sigmoid(ga_ref[0].astype(F32)) * ya
         + jax.nn.sigmoid(gm_ref[0].astype(F32)) * ym)
    upd = jnp.dot(u.astype(BF16), wout_ref[...], preferred_element_type=F32)
    y = x_ref[0] + mod_ref[0, 2:3, :] * upd
    y = y * lax.rsqrt(jnp.mean(y * y, axis=-1, keepdims=True) + NORM_EPS)
    y_ref[0] = y * fg_ref[...]


def _merge_prompt(yaT, az, hm, mz, ga, gm, x, mod3, w_att, w_mlstm, w_out, final_g):
    B, S, D = x.shape
    tm = yaT.shape[3]
    nb = S // tm
    const = lambda b, j: (0, 0)
    tok = lambda w: pl.BlockSpec((1, tm, w), lambda b, j: (b, j, 0))
    return pl.pallas_call(
        _merge_kernel,
        grid=(B, nb),
        in_specs=[
            pl.BlockSpec((1, 1, ATT_WIDTH, tm), lambda b, j: (b, j, 0, 0)),
            tok(ATT_WIDTH), tok(M_WIDTH), tok(M_WIDTH), tok(D), tok(D), tok(D),
            pl.BlockSpec((1, 3, D), lambda b, j: (b, 0, 0)),
            pl.BlockSpec(w_att.shape, const),
            pl.BlockSpec(w_mlstm.shape, const),
            pl.BlockSpec(w_out.shape, const),
            pl.BlockSpec((1, D), const),
        ],
        out_specs=tok(D),
        out_shape=jax.ShapeDtypeStruct((B, S, D), F32),
        compiler_params=_params("arbitrary", "arbitrary"),
        name="prompt_merge",
    )(yaT, az, hm, mz, ga, gm, x, mod3, w_att, w_mlstm, w_out, final_g.reshape(1, D))


def _prompt_layer(x, mod3, norm_g, w_main, w_ifT, b_if, w_conv, b_conv, m_norm_g,
                  w_att, w_mlstm, w_out, final_g, page_table, cache_kT, q_lanes):
    B, S, _ = x.shape
    (qT, kh, vT, kf, vf, kmean, az, qk, mv, mo, mz, ga, gm, gT, mvT, gc, ctail) = _inproj(
        x, mod3, norm_g, w_main, w_ifT, b_if, w_conv, b_conv)
    yaT, block_scores = _moba_prompt(qT, kh, vT, kmean, page_table, cache_kT, q_lanes)
    hm, C, n, m = _mlstm_prompt(qk, mv, mvT, gT, gc, mo, m_norm_g)
    y = _merge_prompt(yaT, az, hm, mz, ga, gm, x, mod3, w_att, w_mlstm, w_out, final_g)
    k_rows = kf.reshape(B, S, ATT_HEADS, ATT_HEAD_DIM)
    v_rows = vf.reshape(B, S, ATT_HEADS, ATT_HEAD_DIM)
    conv_new = ctail[:, V7X_SUBLANES - (M_CONV - 1):, :]
    return y, k_rows, v_rows, C, n[:, :M_HEADS, :], m[:, :M_HEADS, 0], conv_new, block_scores


def _inproj_sample_kernel(x_ref, mod_ref, ng_ref, w_ref, wif_ref, bif_ref, z_ref, g_ref):
    x = x_ref[...]
    d = x.shape[1]
    xn = x * lax.rsqrt(jnp.mean(x * x, axis=-1, keepdims=True) + NORM_EPS) * ng_ref[...]
    h = (xn * (1.0 + mod_ref[:, d:2 * d]) + mod_ref[:, 0:d]).astype(BF16)
    z_ref[...] = jnp.dot(h, w_ref[...], preferred_element_type=F32)
    g = lax.dot_general(h, wif_ref[...], (((1,), (1,)), ((), ())), preferred_element_type=F32)
    g_ref[...] = g + bif_ref[...]


def _inproj_sample(x, mod, norm_g, w_main, w_ifT, b_if):
    n, d = x.shape
    bn = 13 * V7X_LANES
    assert w_main.shape[1] % bn == 0
    const = lambda i: (0, 0)
    return pl.pallas_call(
        _inproj_sample_kernel,
        grid=(w_main.shape[1] // bn,),
        in_specs=[pl.BlockSpec((n, d), const),
                  pl.BlockSpec(mod.shape, const),
                  pl.BlockSpec((1, d), const),
                  pl.BlockSpec((d, bn), lambda i: (0, i)),
                  pl.BlockSpec(w_ifT.shape, const),
                  pl.BlockSpec((1, 2 * M_HEADS), const)],
        out_specs=[pl.BlockSpec((n, bn), lambda i: (0, i)),
                   pl.BlockSpec((n, 2 * M_HEADS), const)],
        out_shape=[jax.ShapeDtypeStruct((n, w_main.shape[1]), F32),
                   jax.ShapeDtypeStruct((n, 2 * M_HEADS), F32)],
        compiler_params=_params("arbitrary"),
        name="sample_inproj",
    )(x, mod, norm_g.reshape(1, d), w_main, w_ifT, b_if.reshape(1, 2 * M_HEADS))


_PAGES_PER_CHUNK = 8
_PAGE_RING = 4
_PAGES_PER_BLOCK = MOBA_BLOCK // PAGE_SIZE
_SAMPLE_SEQS_PER_STEP = 4


def _attend_sample_kernel(pt_ref, sel_ref, ck_ref, cv_ref, q_ref, kn_ref, vn_ref, o_ref,
                          kbuf, vbuf, sem):
    b = pl.program_id(0)
    nseq = pl.num_programs(0)
    H, dh = ATT_HEADS, ATT_HEAD_DIM
    npg = MOBA_TOPK * _PAGES_PER_BLOCK

    def copies(bb, slot):
        out = []
        for hh in range(H):
            for r in range(MOBA_TOPK):
                blk = sel_ref[(bb * H + hh) * MOBA_TOPK + r]
                for i in range(_PAGES_PER_BLOCK):
                    page = pt_ref[bb, blk * _PAGES_PER_BLOCK + i]
                    pg = r * _PAGES_PER_BLOCK + i
                    out.append(pltpu.make_async_copy(
                        ck_ref.at[0, page, hh], kbuf.at[slot, hh, pg], sem.at[0, slot]))
                    out.append(pltpu.make_async_copy(
                        cv_ref.at[0, page, hh], vbuf.at[slot, hh, pg], sem.at[1, slot]))
        return out

    @pl.when(b == 0)
    def _():
        for cp in copies(0, 0):
            cp.start()

    slot = b % 2

    @pl.when(b + 1 < nseq)
    def _():
        for cp in copies(b + 1, 1 - slot):
            cp.start()

    for cp in copies(b, slot):
        cp.wait()

    scale = ATT_HEAD_DIM ** -0.5
    nt = (((1,), (1,)), ((), ()))
    q8s = [jnp.broadcast_to(q_ref[0, hh:hh + 1, :], (V7X_SUBLANES, dh)).astype(BF16)
           for hh in range(H)]
    ss = [jnp.concatenate(
        [jnp.dot(q8s[hh], kbuf[slot, hh, pg].astype(BF16), preferred_element_type=F32)[0:1]
         for pg in range(npg)], axis=1) * scale for hh in range(H)]
    p8s, p_news, ls = [], [], []
    for hh in range(H):
        kn = kn_ref[0, hh:hh + 1, :].astype(BF16).astype(F32)
        s_new = jnp.sum(q8s[hh][0:1].astype(F32) * kn, axis=1, keepdims=True) * scale
        m = jnp.maximum(jnp.max(ss[hh], axis=1, keepdims=True), s_new)
        p = jnp.exp(ss[hh] - m)
        p_new = jnp.exp(s_new - m)
        ls.append(jnp.sum(p, axis=1, keepdims=True) + p_new)
        p_news.append(p_new)
        p8s.append(jnp.broadcast_to(p, (V7X_SUBLANES, p.shape[1])).astype(BF16))
    pvs = [[lax.dot_general(p8s[hh][:, pg * PAGE_SIZE:(pg + 1) * PAGE_SIZE],
                            vbuf[slot, hh, pg].astype(BF16), nt, preferred_element_type=F32)[0:1]
            for pg in range(npg)] for hh in range(H)]
    for hh in range(H):
        vn = vn_ref[0, hh:hh + 1, :].astype(BF16).astype(F32)
        acc = p_news[hh].astype(BF16).astype(F32) * vn
        for x in pvs[hh]:
            acc = acc + x
        o_ref[0, :, hh * dh:(hh + 1) * dh] = acc / ls[hh]


def _attend_sample(cache_k, cache_v, page_table, sel_flat, q, k_new, v_new):
    nseq, n_pages = page_table.shape
    H, dh = ATT_HEADS, ATT_HEAD_DIM
    npg = MOBA_TOPK * _PAGES_PER_BLOCK
    ps = cache_k.shape[-1]
    grid_spec = pltpu.PrefetchScalarGridSpec(
        num_scalar_prefetch=2,
        grid=(nseq,),
        in_specs=[pl.BlockSpec(memory_space=pl.ANY),
                  pl.BlockSpec(memory_space=pl.ANY),
                  pl.BlockSpec((1, H, dh), lambda b, pt, sel: (b, 0, 0)),
                  pl.BlockSpec((1, H, dh), lambda b, pt, sel: (b, 0, 0)),
                  pl.BlockSpec((1, H, dh), lambda b, pt, sel: (b, 0, 0))],
        out_specs=pl.BlockSpec((1, 1, ATT_WIDTH), lambda b, pt, sel: (b, 0, 0)),
        scratch_shapes=[pltpu.VMEM((2, H, npg, dh, ps), F32),
                        pltpu.VMEM((2, H, npg, dh, ps), F32),
                        pltpu.SemaphoreType.DMA((2, 2))],
    )
    return pl.pallas_call(
        _attend_sample_kernel,
        grid_spec=grid_spec,
        out_shape=jax.ShapeDtypeStruct((nseq, 1, ATT_WIDTH), F32),
        compiler_params=_params("arbitrary"),
        name="sample_attend",
    )(page_table, sel_flat, cache_k, cache_v, q, k_new, v_new)


def _mlstm_sample_kernel(qk_ref, v_ref, o_ref, g_ref, conv_ref, C_ref, n_ref, wc_ref, bc_ref, ng_ref,
                         h_ref, C_out, n_out, m_out, conv_out):
    for sq in range(qk_ref.shape[0]):
        x = qk_ref[sq]
        conv = conv_ref[sq]
        acc = x * wc_ref[M_CONV - 1:M_CONV, :] + bc_ref[...]
        for jj in range(M_CONV - 1):
            acc += conv[jj:jj + 1, :] * wc_ref[jj:jj + 1, :]
        qk = _silu(acc)
        conv_out[sq] = jnp.concatenate([conv[1:], x], axis=0)
        g = g_ref[sq]
        m_new = []
        for hh in range(M_HEADS):
            q = qk[:, hh * M_QK_DIM:(hh + 1) * M_QK_DIM]
            k = qk[:, M_QK_WIDTH + hh * M_QK_DIM:M_QK_WIDTH + (hh + 1) * M_QK_DIM] * (M_QK_DIM ** -0.5)
            qb = q.astype(BF16)
            kb = k.astype(BF16)
            v = v_ref[sq][:, hh * M_V_DIM:(hh + 1) * M_V_DIM]
            i_pre = g[:, hh:hh + 1]
            f_pre = g[:, M_HEADS + hh:M_HEADS + hh + 1]
            m_prev = g[:, 2 * M_HEADS + hh:2 * M_HEADS + hh + 1]
            bb = _log_sigmoid(f_pre)
            cc = i_pre - bb
            M = jnp.maximum(m_prev, cc)
            w = jnp.exp(cc - M)
            a = jnp.exp(m_prev - M)
            C = C_ref[sq, hh]
            n = n_ref[sq, hh:hh + 1, :]
            qf = qb.astype(F32)
            s = jnp.sum(qf * kb.astype(F32), axis=1, keepdims=True) * w
            q8 = jnp.broadcast_to(qb, (V7X_SUBLANES, M_QK_DIM))
            inter = lax.dot_general(q8, C.astype(BF16), (((1,), (1,)), ((), ())),
                                    preferred_element_type=F32)[0:1]
            vb = v.astype(BF16).astype(F32)
            num = a * inter + s.astype(BF16).astype(F32) * vb
            den = a * jnp.sum(qf * n.astype(BF16).astype(F32), axis=1, keepdims=True) + s
            hv = num / jnp.maximum(jnp.abs(den), jnp.exp(-(bb + M)))
            hv = hv * lax.rsqrt(jnp.mean(hv * hv, axis=-1, keepdims=True) + NORM_EPS)
            sl = slice(hh * M_V_DIM, (hh + 1) * M_V_DIM)
            h_ref[sq, :, sl] = hv * ng_ref[:, sl] * jax.nn.sigmoid(o_ref[sq][:, sl])
            wv = jnp.broadcast_to(w * v, (V7X_SUBLANES, M_V_DIM))
            wv = jnp.concatenate([wv, jnp.zeros((V7X_LANES - V7X_SUBLANES, M_V_DIM), F32)], axis=0)
            wv_col = wv.T[:, 0:1]
            C_out[sq, hh] = a * C + wv_col * k
            n_out[sq, hh:hh + 1, :] = a * n + w * k
            m_new.append(bb + M)
        lane = lax.broadcasted_iota(jnp.int32, (1, V7X_LANES), 1)
        mrow = jnp.zeros((1, V7X_LANES), F32)
        for hh in range(M_HEADS):
            mrow = jnp.where(lane == hh, m_new[hh], mrow)
        m_out[sq] = mrow


def _mlstm_sample(z3, g16, state_conv, state_C, state_n, w_conv, b_conv, m_norm_g):
    nseq = z3.shape[0]
    MH, dk, dv = M_HEADS, M_QK_DIM, M_V_DIM
    W = 2 * M_QK_WIDTH
    sb = _SAMPLE_SEQS_PER_STEP
    assert nseq % sb == 0
    const = lambda b: (0, 0)
    return pl.pallas_call(
        _mlstm_sample_kernel,
        grid=(nseq // sb,),
        in_specs=[
            pl.BlockSpec((sb, 1, W), lambda b: (b, 0, _C_MQK // W)),
            pl.BlockSpec((sb, 1, M_WIDTH), lambda b: (b, 0, _C_MV // M_WIDTH)),
            pl.BlockSpec((sb, 1, M_WIDTH), lambda b: (b, 0, _C_MO // M_WIDTH)),
            pl.BlockSpec((sb, 1, 4 * MH), lambda b: (b, 0, 0)),
            pl.BlockSpec((sb, M_CONV - 1, W), lambda b: (b, 0, 0)),
            pl.BlockSpec((sb, MH, dv, dk), lambda b: (b, 0, 0, 0)),
            pl.BlockSpec((sb, MH, dk), lambda b: (b, 0, 0)),
            pl.BlockSpec((M_CONV, W), const),
            pl.BlockSpec((1, W), const),
            pl.BlockSpec((1, M_WIDTH), const),
        ],
        out_specs=[
            pl.BlockSpec((sb, 1, M_WIDTH), lambda b: (b, 0, 0)),
            pl.BlockSpec((sb, MH, dv, dk), lambda b: (b, 0, 0, 0)),
            pl.BlockSpec((sb, MH, dk), lambda b: (b, 0, 0)),
            pl.BlockSpec((sb, 1, V7X_LANES), lambda b: (b, 0, 0)),
            pl.BlockSpec((sb, M_CONV - 1, W), lambda b: (b, 0, 0)),
        ],
        out_shape=[
            jax.ShapeDtypeStruct((nseq, 1, M_WIDTH), F32),
            jax.ShapeDtypeStruct((nseq, MH, dv, dk), F32),
            jax.ShapeDtypeStruct((nseq, MH, dk), F32),
            jax.ShapeDtypeStruct((nseq, 1, V7X_LANES), F32),
            jax.ShapeDtypeStruct((nseq, M_CONV - 1, W), F32),
        ],
        compiler_params=_params("arbitrary"),
        name="sample_mlstm",
    )(z3, z3, z3, g16, state_conv, state_C, state_n, w_conv, b_conv.reshape(1, W),
      m_norm_g.reshape(1, M_WIDTH))


def _merge_sample_kernel(ya_ref, hm_ref, z_ref, x_ref, mod_ref, watt_ref, wml_ref, wout_ref, fg_ref,
                         y_ref):
    d = x_ref.shape[1]
    ya = ya_ref[...] * _silu(z_ref[:, _C_AZ:_C_MQK])
    ya = jnp.dot(ya.astype(BF16), watt_ref[...], preferred_element_type=F32)
    ym = hm_ref[...] * _silu(z_ref[:, _C_MZ:_C_MIF])
    ym = jnp.dot(ym.astype(BF16), wml_ref[...], preferred_element_type=F32)
    tail = z_ref[:, _C_MIF:]
    ga = tail[:, _C_GA - _C_MIF:_C_GM - _C_MIF]
    gm = tail[:, _C_GM - _C_MIF:_C_END - _C_MIF]
    u = jax.nn.sigmoid(ga) * ya + jax.nn.sigmoid(gm) * ym
    upd = jnp.dot(u.astype(BF16), wout_ref[...], preferred_element_type=F32)
    y = x_ref[...] + mod_ref[:, 2 * d:3 * d] * upd
    y = y * lax.rsqrt(jnp.mean(y * y, axis=-1, keepdims=True) + NORM_EPS)
    y_ref[...] = y * fg_ref[...]


def _merge_sample(ya, hm, z, x, mod, w_att, w_mlstm, w_out, final_g):
    n, d = x.shape
    full = lambda a: pl.BlockSpec(a.shape, lambda i: (0,) * a.ndim)
    fg = final_g.reshape(1, d)
    args = (ya, hm, z, x, mod, w_att, w_mlstm, w_out, fg)
    return pl.pallas_call(
        _merge_sample_kernel,
        grid=(1,),
        in_specs=[full(a) for a in args],
        out_specs=pl.BlockSpec((n, d), lambda i: (0, 0)),
        out_shape=jax.ShapeDtypeStruct((n, d), F32),
        compiler_params=_params("arbitrary"),
        name="sample_merge",
    )(*args)


def _gate_scores_kernel(sc_ref, idx_ref, *, nblk):
    sc = sc_ref[0] / MOBA_BLOCK
    lane = lax.broadcasted_iota(jnp.int32, sc.shape, 1)
    cnt = jnp.zeros(sc.shape, jnp.int32)
    for m in range(nblk):
        col = sc[:, m:m + 1]
        beats = (col > sc) | ((col == sc) & (lane > m))
        cnt += beats.astype(jnp.int32)
    out = jnp.zeros(sc.shape, jnp.int32)
    for r in range(MOBA_TOPK):
        idx = jnp.sum(jnp.where((cnt == r) & (lane < nblk), lane, 0), axis=1, keepdims=True)
        out = jnp.where(lane == r, idx, out)
    idx_ref[0] = out


def _gate_scores(block_scores, nblk):
    nseq, H, lanes = block_scores.shape
    return pl.pallas_call(
        functools.partial(_gate_scores_kernel, nblk=nblk),
        grid=(nseq,),
        in_specs=[pl.BlockSpec((1, H, lanes), lambda b: (b, 0, 0))],
        out_specs=pl.BlockSpec((1, H, lanes), lambda b: (b, 0, 0)),
        out_shape=jax.ShapeDtypeStruct((nseq, H, lanes), jnp.int32),
        compiler_params=_params("arbitrary"),
        name="sample_gate",
    )(block_scores)


def _sample_inputs(x, mod, norm_g, w_main, w_ifT, b_if):
    nseq = x.shape[0]
    H, dh = ATT_HEADS, ATT_HEAD_DIM
    z, g = _inproj_sample(x, mod, norm_g, w_main, w_ifT, b_if)
    q = z[:, _C_AQ:_C_AK].reshape(nseq, H, dh)
    q_lanes = jnp.broadcast_to(q[..., None], (nseq, H, dh, PAGE_SIZE))
    return z, g, q, q_lanes


def _sample_layer(x, mod, z, g, q, block_scores, cache_kT, cache_vT, page_table,
                  state_C, state_n, state_m, state_conv,
                  w_conv, b_conv, m_norm_g, w_att, w_mlstm, w_out, final_g):
    nseq = x.shape[0]
    H, dh = ATT_HEADS, ATT_HEAD_DIM
    nblk = page_table.shape[1] // _PAGES_PER_BLOCK
    assert nblk >= MOBA_TOPK
    k_new = z[:, _C_AK:_C_AV].reshape(nseq, H, dh)
    v_new = z[:, _C_AV:_C_AZ].reshape(nseq, H, dh)
    sel = _gate_scores(block_scores, nblk)
    sel_flat = sel[:, :, :MOBA_TOPK].reshape(-1)
    ya = _attend_sample(cache_kT, cache_vT, page_table, sel_flat, q, k_new, v_new)
    g16 = jnp.concatenate([g, state_m, jnp.zeros_like(state_m)], axis=1).reshape(nseq, 1, 4 * M_HEADS)
    hm, C, n, m, conv_new = _mlstm_sample(z.reshape(nseq, 1, -1), g16, state_conv, state_C, state_n,
                                          w_conv, b_conv, m_norm_g)
    y = _merge_sample(ya.reshape(nseq, ATT_WIDTH), hm.reshape(nseq, M_WIDTH), z, x, mod,
                      w_att, w_mlstm, w_out, final_g)
    return (y.reshape(nseq, 1, -1), k_new.reshape(nseq, 1, H, dh), v_new.reshape(nseq, 1, H, dh),
            C, n, m[:, 0, :M_HEADS], conv_new)


def kernel(x_prompt, x_sample, cache_k, cache_v, state_C, state_n, state_m, state_conv, page_table,
           c_prompt, c_sample, w_ada, b_ada, norm_g, w_in, b_if, w_conv, b_conv, m_norm_g,
           w_att, w_mlstm, w_out, final_g):
    depth = w_in.shape[0]
    assert depth == 1, "single-layer step"
    B = x_prompt.shape[0]
    nseq = x_sample.shape[0]
    pad = (-(B + nseq)) % V7X_SUBLANES
    c_all = jnp.concatenate([c_prompt, c_sample, jnp.zeros((pad, c_prompt.shape[1]), F32)], axis=0)
    mod = _ada(c_all, w_ada[0], b_ada[0])
    mod_p = mod[:B].reshape(B, 3, D_MODEL)
    mod_s = mod[B:B + nseq]
    w_main, w_ifT = _pack_w_in(w_in[0])
    wa, wm, wo = w_att[0].astype(BF16), w_mlstm[0].astype(BF16), w_out[0].astype(BF16)
    cache_kT = jnp.transpose(cache_k, (0, 1, 3, 4, 2))
    cache_vT = jnp.transpose(cache_v, (0, 1, 3, 4, 2))
    xs = x_sample[:, 0, :]
    z, g, q, q_lanes = _sample_inputs(xs, mod_s, norm_g[0], w_main, w_ifT, b_if[0])
    *outs_p, block_scores = _prompt_layer(
        x_prompt, mod_p, norm_g[0], w_main, w_ifT, b_if[0], w_conv[0], b_conv[0], m_norm_g[0],
        wa, wm, wo, final_g, page_table, cache_kT, q_lanes)
    outs_s = _sample_layer(xs, mod_s, z, g, q, block_scores, cache_kT, cache_vT, page_table,
                           state_C[0], state_n[0], state_m[0], state_conv[0],
                           w_conv[0], b_conv[0], m_norm_g[0], wa, wm, wo, final_g)
    y_p, *st_p = outs_p
    y_s, *st_s = outs_s
    return (y_p, y_s) + tuple(a[None] for a in st_p) + tuple(a[None] for a in st_s)
```

```python
import functools

import jax
import jax.numpy as jnp
from jax import lax
from jax.experimental import pallas as pl
from jax.experimental.pallas import tpu as pltpu

F32 = jnp.float32
BF16 = jnp.bfloat16

D_MODEL = 1024
ATT_HEADS = 8
ATT_HEAD_DIM = 64
ATT_WIDTH = ATT_HEADS * ATT_HEAD_DIM
MOBA_BLOCK = 256
MOBA_TOPK = 3
M_HEADS = 4
M_V_DIM = 256
M_QK_DIM = 128
M_QK_WIDTH = M_HEADS * M_QK_DIM
M_WIDTH = M_HEADS * M_V_DIM
M_CONV = 4
PAGE_SIZE = 128
NORM_EPS = 1e-6
NEG_SCORE = -1e30
_LOG2E = 1.4426950408889634

V7X_LANES = 128
V7X_SUBLANES = 8
V7X_VMEM_LIMIT_BYTES = 56 * 1024 * 1024

_C_AQ, _C_AK, _C_AV, _C_AZ = 0, 512, 1024, 1536
_C_MQK, _C_MV, _C_MO, _C_MZ = 2048, 3072, 4096, 5120
_C_MIF, _C_GA, _C_GM, _C_END = 6144, 6152, 7176, 8200


def _silu(x):
    return x * jax.nn.sigmoid(x)


def _split3(x):
    hi = x.astype(BF16)
    r1 = x - hi.astype(F32)
    mid = r1.astype(BF16)
    lo = (r1 - mid.astype(F32)).astype(BF16)
    return hi, mid, lo


def _pack_w_in(w_in):
    pad = (-w_in.shape[1]) % V7X_LANES
    w_main = jnp.concatenate([w_in.astype(BF16), jnp.zeros((w_in.shape[0], pad), BF16)], axis=1)
    w_ifT = w_main[:, _C_MIF:_C_GA].T
    return w_main, w_ifT


def _params(*sem):
    return pltpu.CompilerParams(dimension_semantics=sem, vmem_limit_bytes=V7X_VMEM_LIMIT_BYTES)


def _ada_kernel(c_ref, w_ref, b_ref, o_ref):
    a = _silu(c_ref[...])
    a_hi = a.astype(BF16)
    a_lo = (a - a_hi.astype(F32)).astype(BF16)
    w = w_ref[...]
    w_hi = w.astype(BF16)
    w_lo = (w - w_hi.astype(F32)).astype(BF16)
    acc = jnp.dot(a_hi, w_hi, preferred_element_type=F32)
    acc += jnp.dot(a_hi, w_lo, preferred_element_type=F32)
    acc += jnp.dot(a_lo, w_hi, preferred_element_type=F32)
    o_ref[...] = acc + b_ref[...]


def _ada(c, w_ada, b_ada):
    n, d = c.shape
    d3 = w_ada.shape[1]
    bn = 512
    return pl.pallas_call(
        _ada_kernel,
        grid=(d3 // bn,),
        in_specs=[pl.BlockSpec((n, d), lambda i: (0, 0)),
                  pl.BlockSpec((d, bn), lambda i: (0, i)),
                  pl.BlockSpec((1, bn), lambda i: (0, i))],
        out_specs=pl.BlockSpec((n, bn), lambda i: (0, i)),
        out_shape=jax.ShapeDtypeStruct((n, d3), F32),
        compiler_params=_params("arbitrary"),
        name="ada_mod",
    )(c, w_ada, b_ada.reshape(1, d3))


def _inproj_kernel(x_ref, mod_ref, ng_ref, w_ref, wif_ref, bif_ref, bifn_ref, wc_ref, bc_ref,
                   qT_ref, kh_ref, vT_ref, kf_ref, vf_ref, kmean_ref,
                   az_ref, qk_ref, mv_ref, mo_ref, mz_ref, ga_ref, gm_ref, gT_ref, mvT_ref, gc_ref,
                   ctail_ref, tail_s):
    j = pl.program_id(1)
    x = x_ref[0]
    xn = x * lax.rsqrt(jnp.mean(x * x, axis=-1, keepdims=True) + NORM_EPS) * ng_ref[...]
    shift = mod_ref[0, 0:1, :]
    scale = mod_ref[0, 1:2, :]
    h = (xn * (1.0 + scale) + shift).astype(BF16)

    def proj(a, b):
        return jnp.dot(h, w_ref[:, a:b], preferred_element_type=F32)

    @pl.when(j == 0)
    def _():
        tail_s[...] = jnp.zeros_like(tail_s)

    mqk = proj(_C_MQK, _C_MV)
    ctail_ref[0] = mqk[mqk.shape[0] - V7X_SUBLANES:, :]
    qk = _silu(_causal_conv(mqk, tail_s, wc_ref, bc_ref))
    qk_ref[0, :, 0:M_QK_WIDTH] = qk[:, 0:M_QK_WIDTH].astype(BF16)
    qk_ref[0, :, M_QK_WIDTH:] = (qk[:, M_QK_WIDTH:] * (M_QK_DIM ** -0.5)).astype(BF16)

    mo_ref[0] = proj(_C_MO, _C_MZ).astype(BF16)

    q = proj(_C_AQ, _C_AK) * (ATT_HEAD_DIM ** -0.5 * _LOG2E)
    qT_ref[0, 0] = q.T.astype(BF16)
    mz_ref[0] = proj(_C_MZ, _C_MIF).astype(BF16)

    dh = ATT_HEAD_DIM
    tm = x.shape[0]
    one_col = jnp.where(lax.broadcasted_iota(jnp.int32, (tm, dh), 1) == 0, 1.0, 0.0)
    one_row = jnp.where(lax.broadcasted_iota(jnp.int32, (_MOBA_V_PAD, tm), 0) == 0, 1.0, 0.0)
    k = proj(_C_AK, _C_AV)
    kf_ref[0] = k
    ksum = jnp.sum(k, axis=0, keepdims=True) / MOBA_BLOCK
    for hh in range(ATT_HEADS):
        sl = slice(hh * dh, (hh + 1) * dh)
        kh_ref[0, 0, hh] = jnp.concatenate([k[:, sl], one_col], axis=1).astype(BF16)
        kmean_ref[0, hh, pl.ds(j, 1), :] = ksum[:, sl]
    az_ref[0] = proj(_C_AZ, _C_MQK).astype(BF16)

    v = proj(_C_AV, _C_AZ)
    vf_ref[0] = v
    vT = v.T
    for hh in range(ATT_HEADS):
        r0 = hh * _MOBA_V_ROWS
        vT_ref[0, 0, r0:r0 + dh, :] = vT[hh * dh:(hh + 1) * dh].astype(BF16)
        vT_ref[0, 0, r0 + dh:r0 + _MOBA_V_ROWS, :] = one_row.astype(BF16)

    tail = proj(_C_MIF, w_ref.shape[1])
    gc_ref[0] = tail[:, 0:_C_GA - _C_MIF] + bifn_ref[...]
    ga_ref[0] = tail[:, _C_GA - _C_MIF:_C_GM - _C_MIF].astype(BF16)
    gm_ref[0] = tail[:, _C_GM - _C_MIF:_C_END - _C_MIF].astype(BF16)
    gT = lax.dot_general(wif_ref[...], h, (((1,), (1,)), ((), ())), preferred_element_type=F32)
    gT_ref[0] = gT + bif_ref[...]

    mv = proj(_C_MV, _C_MO)
    mv_ref[0] = mv.astype(BF16)
    mvT_ref[0, 0] = mv.T.astype(BF16)


def _inproj(x, mod3, norm_g, w_main, w_ifT, b_if, w_conv, b_conv):
    B, S, D = x.shape
    W = 2 * M_QK_WIDTH
    tm = MOBA_BLOCK
    nb = S // tm
    H, dh = ATT_HEADS, ATT_HEAD_DIM
    const = lambda b, j: (0, 0)
    tok = lambda w: pl.BlockSpec((1, tm, w), lambda b, j: (b, j, 0))
    blk = pl.BlockSpec((1, 1, ATT_WIDTH, tm), lambda b, j: (b, j, 0, 0))
    out_shape = [
        jax.ShapeDtypeStruct((B, nb, ATT_WIDTH, tm), BF16),
        jax.ShapeDtypeStruct((B, nb, H, tm, 2 * dh), BF16),
        jax.ShapeDtypeStruct((B, nb, H * _MOBA_V_ROWS, tm), BF16),
        jax.ShapeDtypeStruct((B, S, ATT_WIDTH), F32),
        jax.ShapeDtypeStruct((B, S, ATT_WIDTH), F32),
        jax.ShapeDtypeStruct((B, H, nb, dh), F32),
        jax.ShapeDtypeStruct((B, S, ATT_WIDTH), BF16),
        jax.ShapeDtypeStruct((B, S, W), BF16),
        jax.ShapeDtypeStruct((B, S, M_WIDTH), BF16),
        jax.ShapeDtypeStruct((B, S, M_WIDTH), BF16),
        jax.ShapeDtypeStruct((B, S, M_WIDTH), BF16),
        jax.ShapeDtypeStruct((B, S, D), BF16),
        jax.ShapeDtypeStruct((B, S, D), BF16),
        jax.ShapeDtypeStruct((B, 2 * M_HEADS, S), F32),
        jax.ShapeDtypeStruct((B, nb, M_WIDTH, tm), BF16),
        jax.ShapeDtypeStruct((B, S, 2 * M_HEADS), F32),
        jax.ShapeDtypeStruct((B, V7X_SUBLANES, W), F32),
    ]
    out_specs = [
        blk,
        pl.BlockSpec((1, 1, H, tm, 2 * dh), lambda b, j: (b, j, 0, 0, 0)),
        pl.BlockSpec((1, 1, H * _MOBA_V_ROWS, tm), lambda b, j: (b, j, 0, 0)),
        tok(ATT_WIDTH), tok(ATT_WIDTH),
        pl.BlockSpec((1, H, nb, dh), lambda b, j: (b, 0, 0, 0)),
        tok(ATT_WIDTH), tok(2 * M_QK_WIDTH), tok(M_WIDTH), tok(M_WIDTH), tok(M_WIDTH), tok(D), tok(D),
        pl.BlockSpec((1, 2 * M_HEADS, tm), lambda b, j: (b, 0, j)),
        pl.BlockSpec((1, 1, M_WIDTH, tm), lambda b, j: (b, j, 0, 0)),
        tok(2 * M_HEADS),
        pl.BlockSpec((1, V7X_SUBLANES, W), lambda b, j: (b, 0, 0)),
    ]
    in_specs = [
        pl.BlockSpec((1, tm, D), lambda b, j: (b, j, 0)),
        pl.BlockSpec((1, 3, D), lambda b, j: (b, 0, 0)),
        pl.BlockSpec((1, D), const),
        pl.BlockSpec(w_main.shape, const, pipeline_mode=pl.Buffered(1)),
        pl.BlockSpec(w_ifT.shape, const),
        pl.BlockSpec((2 * M_HEADS, 1), const),
        pl.BlockSpec((1, 2 * M_HEADS), const),
        pl.BlockSpec((M_CONV, W), const),
        pl.BlockSpec((1, W), const),
    ]
    return pl.pallas_call(
        _inproj_kernel,
        grid=(B, nb),
        in_specs=in_specs,
        out_specs=out_specs,
        out_shape=out_shape,
        scratch_shapes=[pltpu.VMEM((V7X_SUBLANES, W), F32)],
        compiler_params=_params("arbitrary", "arbitrary"),
        name="prompt_inproj",
    )(x, mod3, norm_g.reshape(1, D), w_main, w_ifT, b_if.reshape(2 * M_HEADS, 1),
      b_if.reshape(1, 2 * M_HEADS), w_conv, b_conv.reshape(1, W))


_MOBA_HEADS_PER_STEP = 8
_MOBA_HEAD_GROUP = 8
_MOBA_V_PAD = 16
_MOBA_V_ROWS = ATT_HEAD_DIM + _MOBA_V_PAD


def _moba_phases(j):
    return j + 2 * ((j * j) // 4)


def _moba_kernel(pt_ref, qT_ref, kh_ref, vT_ref, kmean_ref, ck_ref, qb_ref, o_ref, sc_ref,
                 bias_ref, qs_ref, acc_ref, pbuf, qbuf, psem, qsem):
    b = pl.program_id(0)
    j = pl.program_id(2)
    nbq = pl.num_programs(2)
    nseq, n_pages = pt_ref.shape
    cpb = n_pages // _PAGES_PER_CHUNK
    bpc = _PAGES_PER_CHUNK // _PAGES_PER_BLOCK
    n_chunks = nseq * cpb
    ring = pbuf.shape[0]
    ahead = ring - 2

    def chunk_copy(g, i):
        page = pt_ref[g // cpb, (g % cpb) * _PAGES_PER_CHUNK + i]
        slot = g % ring
        return pltpu.make_async_copy(ck_ref.at[0, page], pbuf.at[slot, i], psem.at[slot])

    qslots = qbuf.shape[0]

    def q_copy(seq):
        s = seq % qslots
        return pltpu.make_async_copy(qb_ref.at[seq], qbuf.at[s], qsem.at[s])

    def start_chunk(g):
        @pl.when(g < n_chunks)
        def _():
            for i in range(_PAGES_PER_CHUNK):
                chunk_copy(g, i).start()

    step = b * pl.num_programs(1) + pl.program_id(1)

    @pl.when((step == 0) & (j == 0))
    def _():
        sc_ref[...] = jnp.zeros_like(sc_ref)
        q_copy(0).start()
        for g in range(ahead):
            start_chunk(g)

    lane_blk = lax.broadcasted_iota(jnp.int32, sc_ref.shape[1:], 1)

    def page_sync(g):
        live = g < n_chunks
        gc = jnp.minimum(g, n_chunks - 1)
        seq = gc // cpb
        ci = gc % cpb

        @pl.when(live & (ci == 0))
        def _():
            q_copy(seq).wait()

        @pl.when(live & (ci == 0) & (seq + 1 < nseq))
        def _():
            q_copy(seq + 1).start()

        start_chunk(g + ahead)

        @pl.when(live)
        def _():
            for i in range(_PAGES_PER_CHUNK):
                chunk_copy(gc, i).wait()

    def page_scores(g):
        gc = jnp.minimum(g, n_chunks - 1)
        seq = gc // cpb
        ci = gc % cpb
        slot = gc % ring
        qb = qbuf[seq % qslots]
        row = sc_ref[seq]
        for blk in range(bpc):
            p = pbuf[slot, blk * _PAGES_PER_BLOCK]
            for i in range(1, _PAGES_PER_BLOCK):
                p = p + pbuf[slot, blk * _PAGES_PER_BLOCK + i]
            part = jnp.sum(p * qb, axis=1)
            val = jnp.sum(part, axis=1, keepdims=True)
            row = jnp.where(lane_blk == ci * bpc + blk, val, row)
        sc_ref[seq] = row

    phase0 = step * _moba_phases(nbq) + _moba_phases(j)
    hp = kmean_ref.shape[1]
    nb = kmean_ref.shape[2]
    tq = qT_ref.shape[3]
    dh = ATT_HEAD_DIM
    vr = _MOBA_V_ROWS
    qTs = [qT_ref[0, 0, hh * dh:(hh + 1) * dh, :] for hh in range(hp)]
    for hh in range(hp):
        for slot in range(2):
            qs_ref[slot, hh, 0:dh, :] = qTs[hh]
            qs_ref[slot, hh, dh:2 * dh, :] = jnp.zeros((dh, tq), BF16)

    blk = lax.broadcasted_iota(jnp.int32, (nb, tq), 0)
    past = blk < j
    for hh in range(hp):
        sg = sum(jnp.dot(part, qTs[hh], preferred_element_type=F32)
                 for part in _split3(kmean_ref[0, hh]))
        sg = jnp.where(past, sg, NEG_SCORE)
        cnt = jnp.zeros((nb, tq), jnp.int32)
        for m in range(nb):
            row = sg[m:m + 1, :]
            beats = (row > sg) | ((row == sg) & (blk > m))
            cnt += beats.astype(jnp.int32)
        bias_ref[hh] = jnp.where((cnt < MOBA_TOPK) & past, 0.0, NEG_SCORE)

    kpos = lax.broadcasted_iota(jnp.int32, (tq, tq), 0)
    qpos = lax.broadcasted_iota(jnp.int32, (tq, tq), 1)
    causal = kpos <= qpos

    def score(n, slot, hh):
        return jnp.dot(kh_ref[0, n, hh], qs_ref[slot, hh], preferred_element_type=F32)

    def value(n, hh, p):
        return jnp.dot(vT_ref[0, n, hh * vr:(hh + 1) * vr, :], p, preferred_element_type=F32)

    def scores(n, slot=0):
        return [score(n, slot, hh) for hh in range(hp)]

    def values(n, ps):
        return [value(n, hh, ps[hh]) for hh in range(hp)]

    ss = [jnp.where(causal, s, NEG_SCORE) for s in scores(j)]
    ms = [jnp.max(s, axis=0, keepdims=True) for s in ss]
    pbs = [jnp.exp2((s - m).astype(BF16)) for s, m in zip(ss, ms)]
    for hh, x in enumerate(values(j, pbs)):
        acc_ref[hh] = x
    page_sync(phase0)
    page_scores(phase0)

    row0 = lax.broadcasted_iota(jnp.int32, (_MOBA_V_PAD, tq), 0) == 0

    def body(t, ms):
        n0 = 2 * t
        real1 = n0 + 1 < j
        n1 = jnp.minimum(n0 + 1, j - 1)
        g = phase0 + 1 + 2 * t
        page_sync(g)
        page_sync(g + 1)
        for hh in range(hp):
            mask0 = jnp.where(row0, bias_ref[hh, pl.ds(n0, 1), :], 0.0)
            qs_ref[0, hh, dh:dh + _MOBA_V_PAD, :] = mask0.astype(BF16)
            bias1 = jnp.where(real1, bias_ref[hh, pl.ds(n1, 1), :], NEG_SCORE)
            qs_ref[1, hh, dh:dh + _MOBA_V_PAD, :] = jnp.where(row0, bias1, 0.0).astype(BF16)
        page_scores(g)
        page_scores(g + 1)
        m_out = []
        for h0 in range(0, hp, _MOBA_HEAD_GROUP):
            hs = range(h0, min(h0 + _MOBA_HEAD_GROUP, hp))
            ss0 = [score(n0, 0, hh) for hh in hs]
            ss1 = [score(n1, 1, hh) for hh in hs]
            m0 = [ms[hh] for hh in hs]
            m1 = [jnp.maximum(m, jnp.max(s, axis=0, keepdims=True)) for m, s in zip(m0, ss0)]
            a0 = [jnp.exp2(m - mn) for m, mn in zip(m0, m1)]
            pv0 = [value(n0, hh, jnp.exp2((s - mn).astype(BF16)))
                   for hh, s, mn in zip(hs, ss0, m1)]
            m2 = [jnp.maximum(m, jnp.max(s, axis=0, keepdims=True)) for m, s in zip(m1, ss1)]
            a1 = [jnp.exp2(m - mn) for m, mn in zip(m1, m2)]
            pv1 = [value(n1, hh, jnp.exp2((s - mn).astype(BF16)))
                   for hh, s, mn in zip(hs, ss1, m2)]
            for i, hh in enumerate(hs):
                acc_ref[hh] = (a0[i] * acc_ref[hh] + pv0[i]) * a1[i] + pv1[i]
            m_out += m2
        return tuple(m_out)

    lax.fori_loop(0, (j + 1) // 2, body, tuple(ms))
    for hh in range(hp):
        acc = acc_ref[hh]
        o_ref[0, 0, hh * dh:(hh + 1) * dh, :] = (acc[0:dh] / acc[dh:dh + 1]).astype(o_ref.dtype)


def _moba_prompt(qT, kh, vT, kmean, page_table, cache_kT, q_lanes):
    B, nb, _, tq = qT.shape
    H, dh, hp = ATT_HEADS, ATT_HEAD_DIM, _MOBA_HEADS_PER_STEP
    nseq, n_pages = page_table.shape
    ps = cache_kT.shape[-1]
    assert n_pages % _PAGES_PER_CHUNK == 0 and n_pages // _PAGES_PER_BLOCK <= V7X_LANES
    n_chunks = nseq * (n_pages // _PAGES_PER_CHUNK)
    assert _PAGE_RING - 2 <= n_chunks <= B * (H // hp) * _moba_phases(nb), \
        "one page chunk per key-block phase"
    grid_spec = pltpu.PrefetchScalarGridSpec(
        num_scalar_prefetch=1,
        grid=(B, H // hp, nb),
        in_specs=[
            pl.BlockSpec((1, 1, hp * dh, tq), lambda b, g, j, pt: (b, j, g, 0)),
            pl.BlockSpec((1, nb, hp, tq, 2 * dh), lambda b, g, j, pt: (b, 0, g, 0, 0)),
            pl.BlockSpec((1, nb, hp * _MOBA_V_ROWS, tq), lambda b, g, j, pt: (b, 0, g, 0)),
            pl.BlockSpec((1, hp, nb, dh), lambda b, g, j, pt: (b, g, 0, 0)),
            pl.BlockSpec(memory_space=pl.ANY),
            pl.BlockSpec(memory_space=pl.ANY),
        ],
        out_specs=[
            pl.BlockSpec((1, 1, hp * dh, tq), lambda b, g, j, pt: (b, j, g, 0)),
            pl.BlockSpec((nseq, H, V7X_LANES), lambda b, g, j, pt: (0, 0, 0)),
        ],
        scratch_shapes=[
            pltpu.VMEM((hp, nb, tq), F32), pltpu.VMEM((2, hp, 2 * dh, tq), BF16),
            pltpu.VMEM((hp, _MOBA_V_ROWS, tq), F32),
            pltpu.VMEM((_PAGE_RING, _PAGES_PER_CHUNK, H, dh, ps), F32),
            pltpu.VMEM((3, H, dh, ps), F32),
            pltpu.SemaphoreType.DMA((_PAGE_RING,)), pltpu.SemaphoreType.DMA((3,)),
        ],
    )
    return pl.pallas_call(
        _moba_kernel,
        grid_spec=grid_spec,
        out_shape=[jax.ShapeDtypeStruct((B, nb, ATT_WIDTH, tq), BF16),
                   jax.ShapeDtypeStruct((nseq, H, V7X_LANES), F32)],
        compiler_params=_params("arbitrary", "arbitrary", "arbitrary"),
        name="prompt_moba",
    )(page_table, qT, kh, vT, kmean, cache_kT, q_lanes)


def _log_sigmoid(x):
    return jnp.minimum(x, 0.0) - jnp.log1p(jnp.exp(-jnp.abs(x)))


def _causal_conv(x, tail_ref, w_ref, b_ref):
    L = x.shape[0]
    prev = tail_ref[...]
    row = lax.broadcasted_iota(jnp.int32, (V7X_SUBLANES, x.shape[1]), 0)
    acc = x * w_ref[M_CONV - 1:M_CONV, :] + b_ref[...]
    for d in range(1, M_CONV):
        r = pltpu.roll(x, d, axis=0)
        head = jnp.where(row < d, pltpu.roll(prev, d, axis=0), r[:V7X_SUBLANES])
        shifted = jnp.concatenate([head, r[V7X_SUBLANES:]], axis=0)
        acc += shifted * w_ref[M_CONV - 1 - d:M_CONV - d, :]
    tail_ref[...] = x[L - V7X_SUBLANES:, :]
    return acc


def _mlstm_kernel(qk_ref, v_ref, vT_ref, g_ref, gc_ref, o_ref, ng_ref,
                  h_ref, C_out, n_out, m_out,
                  C_s, n_s, m_s):
    c = pl.program_id(1)
    L = qk_ref.shape[1]
    MH, dk, dv = M_HEADS, M_QK_DIM, M_V_DIM
    nt = (((1,), (1,)), ((), ()))

    @pl.when(c == 0)
    def _():
        C_s[...] = jnp.zeros_like(C_s)
        n_s[...] = jnp.zeros_like(n_s)
        m_s[...] = jnp.zeros_like(m_s)

    t_idx = lax.broadcasted_iota(jnp.int32, (L, L), 0)
    s_idx = lax.broadcasted_iota(jnp.int32, (L, L), 1)
    causal = s_idx <= t_idx
    tri = jnp.where(causal, 1.0, 0.0).astype(BF16)
    g_rows = g_ref[0]
    b_rows = sum(lax.dot_general(part, tri, nt, preferred_element_type=F32)
                 for part in _split3(_log_sigmoid(g_rows)))
    b_cols = sum(jnp.dot(tri, part, preferred_element_type=F32)
                 for part in _split3(_log_sigmoid(gc_ref[0])))

    for hd in range(MH):
        qb = qk_ref[0, :, hd * dk:(hd + 1) * dk]
        kb = qk_ref[0, :, M_QK_WIDTH + hd * dk:M_QK_WIDTH + (hd + 1) * dk]
        vsl = slice(hd * dv, (hd + 1) * dv)
        b_row = b_rows[MH + hd:MH + hd + 1]
        c_row = g_rows[hd:hd + 1] - b_row
        b_col = b_cols[:, MH + hd:MH + hd + 1]
        m_prev = m_s[hd:hd + 1, 0:1]
        M_col = jnp.maximum(jnp.max(jnp.where(causal, c_row, NEG_SCORE), axis=1, keepdims=True),
                            m_prev)
        M_last = jnp.maximum(jnp.max(c_row, axis=1, keepdims=True), m_prev)
        w = jnp.where(causal, jnp.exp(c_row - M_col), 0.0)
        a_col = jnp.exp(m_prev - M_col)

        s = lax.dot_general(qb, kb, nt, preferred_element_type=F32) * w
        inter = lax.dot_general(qb, C_s[hd].astype(BF16), nt, preferred_element_type=F32)
        num = a_col * inter + jnp.dot(s.astype(BF16), v_ref[0, :, vsl], preferred_element_type=F32)
        n_b = n_s[hd:hd + 1, :].astype(BF16).astype(F32)
        qn = jnp.sum(qb.astype(F32) * n_b, axis=1, keepdims=True)
        den = a_col * qn + jnp.sum(s, axis=1, keepdims=True)
        hh = num / jnp.maximum(jnp.abs(den), jnp.exp(-(b_col + M_col)))
        hh = hh * lax.rsqrt(jnp.mean(hh * hh, axis=-1, keepdims=True) + NORM_EPS)
        gate = jax.nn.sigmoid(o_ref[0, :, vsl].astype(F32))
        h_ref[0, :, vsl] = (hh * ng_ref[:, vsl] * gate).astype(h_ref.dtype)

        ws_row = jnp.exp(c_row - M_last)
        a_end = jnp.exp(m_prev - M_last)
        vwT = (vT_ref[0, 0, vsl, :].astype(F32) * ws_row).astype(BF16)
        C_s[hd] = a_end * C_s[hd] + jnp.dot(vwT, kb, preferred_element_type=F32)
        ws8 = jnp.broadcast_to(ws_row, (V7X_SUBLANES, L)).astype(BF16)
        n_s[hd:hd + 1, :] = (a_end * n_s[hd:hd + 1, :]
                             + jnp.dot(ws8, kb, preferred_element_type=F32)[0:1])
        m_s[hd:hd + 1, :] = jnp.broadcast_to(b_row[:, L - 1:L] + M_last, (1, m_s.shape[1]))

    @pl.when(c == pl.num_programs(1) - 1)
    def _():
        C_out[0] = C_s[...]
        n_out[0] = n_s[...]
        m_out[0] = m_s[...]


def _mlstm_prompt(qk, mv, mvT, gT, gc, mo, m_norm_g):
    B, S, W = qk.shape
    L = mvT.shape[3]
    nc = S // L
    MH, dk, dv = M_HEADS, M_QK_DIM, M_V_DIM
    const = lambda b, c: (0, 0)
    tok = lambda w: pl.BlockSpec((1, L, w), lambda b, c: (b, c, 0))
    in_specs = [
        tok(W), tok(M_WIDTH),
        pl.BlockSpec((1, 1, M_WIDTH, L), lambda b, c: (b, c, 0, 0)),
        pl.BlockSpec((1, 2 * MH, L), lambda b, c: (b, 0, c)),
        tok(2 * MH), tok(M_WIDTH),
        pl.BlockSpec((1, M_WIDTH), const),
    ]
    out_shape = [
        jax.ShapeDtypeStruct((B, S, M_WIDTH), BF16),
        jax.ShapeDtypeStruct((B, MH, dv, dk), F32),
        jax.ShapeDtypeStruct((B, V7X_SUBLANES, dk), F32),
        jax.ShapeDtypeStruct((B, V7X_SUBLANES, V7X_LANES), F32),
    ]
    out_specs = [
        tok(M_WIDTH),
        pl.BlockSpec((1, MH, dv, dk), lambda b, c: (b, 0, 0, 0)),
        pl.BlockSpec((1, V7X_SUBLANES, dk), lambda b, c: (b, 0, 0)),
        pl.BlockSpec((1, V7X_SUBLANES, V7X_LANES), lambda b, c: (b, 0, 0)),
    ]
    scratch = [
        pltpu.VMEM((MH, dv, dk), F32), pltpu.VMEM((V7X_SUBLANES, dk), F32),
        pltpu.VMEM((V7X_SUBLANES, V7X_LANES), F32),
    ]
    return pl.pallas_call(
        _mlstm_kernel,
        grid=(B, nc),
        in_specs=in_specs,
        out_specs=out_specs,
        out_shape=out_shape,
        scratch_shapes=scratch,
        compiler_params=_params("arbitrary", "arbitrary"),
        name="prompt_mlstm",
    )(qk, mv, mvT, gT, gc, mo, m_norm_g.reshape(1, M_WIDTH))


def _merge_kernel(yaT_ref, az_ref, hm_ref, mz_ref, ga_ref, gm_ref, x_ref, mod_ref,
                  watt_ref, wml_ref, wout_ref, fg_ref, y_ref):
    ya = yaT_ref[0, 0].astype(F32).T * _silu(az_ref[0].astype(F32))
    ya = jnp.dot(ya.astype(BF16), watt_ref[...], preferred_element_type=F32)
    ym = hm_ref[0].astype(F32) * _silu(mz_ref[0].astype(F32))
    ym = jnp.dot(ym.astype(BF16), wml_ref[...], preferred_element_type=F32)
    u = (jax.nn.sigmoid(ga_ref[0].astype(F32)) * ya
         + jax.nn.sigmoid(gm_ref[0].astype(F32)) * ym)
    upd = jnp.dot(u.astype(BF16), wout_ref[...], preferred_element_type=F32)
    y = x_ref[0] + mod_ref[0, 2:3, :] * upd
    y = y * lax.rsqrt(jnp.mean(y * y, axis=-1, keepdims=True) + NORM_EPS)
    y_ref[0] = y * fg_ref[...]


def _merge_prompt(yaT, az, hm, mz, ga, gm, x, mod3, w_att, w_mlstm, w_out, final_g):
    B, S, D = x.shape
    tm = yaT.shape[3]
    nb = S // tm
    const = lambda b, j: (0, 0)
    tok = lambda w: pl.BlockSpec((1, tm, w), lambda b, j: (b, j, 0))
    return pl.pallas_call(
        _merge_kernel,
        grid=(B, nb),
        in_specs=[
            pl.BlockSpec((1, 1, ATT_WIDTH, tm), lambda b, j: (b, j, 0, 0)),
            tok(ATT_WIDTH), tok(M_WIDTH), tok(M_WIDTH), tok(D), tok(D), tok(D),
            pl.BlockSpec((1, 3, D), lambda b, j: (b, 0, 0)),
            pl.BlockSpec(w_att.shape, const),
            pl.BlockSpec(w_mlstm.shape, const),
            pl.BlockSpec(w_out.shape, const),
            pl.BlockSpec((1, D), const),
        ],
        out_specs=tok(D),
        out_shape=jax.ShapeDtypeStruct((B, S, D), F32),
        compiler_params=_params("arbitrary", "arbitrary"),
        name="prompt_merge",
    )(yaT, az, hm, mz, ga, gm, x, mod3, w_att, w_mlstm, w_out, final_g.reshape(1, D))


def _prompt_layer(x, mod3, norm_g, w_main, w_ifT, b_if, w_conv, b_conv, m_norm_g,
                  w_att, w_mlstm, w_out, final_g, page_table, cache_kT, q_lanes):
    B, S, _ = x.shape
    (qT, kh, vT, kf, vf, kmean, az, qk, mv, mo, mz, ga, gm, gT, mvT, gc, ctail) = _inproj(
        x, mod3, norm_g, w_main, w_ifT, b_if, w_conv, b_conv)
    yaT, block_scores = _moba_prompt(qT, kh, vT, kmean, page_table, cache_kT, q_lanes)
    hm, C, n, m = _mlstm_prompt(qk, mv, mvT, gT, gc, mo, m_norm_g)
    y = _merge_prompt(yaT, az, hm, mz, ga, gm, x, mod3, w_att, w_mlstm, w_out, final_g)
    k_rows = kf.reshape(B, S, ATT_HEADS, ATT_HEAD_DIM)
    v_rows = vf.reshape(B, S, ATT_HEADS, ATT_HEAD_DIM)
    conv_new = ctail[:, V7X_SUBLANES - (M_CONV - 1):, :]
    return y, k_rows, v_rows, C, n[:, :M_HEADS, :], m[:, :M_HEADS, 0], conv_new, block_scores


def _inproj_sample_kernel(x_ref, mod_ref, ng_ref, w_ref, wif_ref, bif_ref, z_ref, g_ref):
    x = x_ref[...]
    d = x.shape[1]
    xn = x * lax.rsqrt(jnp.mean(x * x, axis=-1, keepdims=True) + NORM_EPS) * ng_ref[...]
    h = (xn * (1.0 + mod_ref[:, d:2 * d]) + mod_ref[:, 0:d]).astype(BF16)
    z_ref[...] = jnp.dot(h, w_ref[...], preferred_element_type=F32)
    g = lax.dot_general(h, wif_ref[...], (((1,), (1,)), ((), ())), preferred_element_type=F32)
    g_ref[...] = g + bif_ref[...]


def _inproj_sample(x, mod, norm_g, w_main, w_ifT, b_if):
    n, d = x.shape
    bn = 13 * V7X_LANES
    assert w_main.shape[1] % bn == 0
    const = lambda i: (0, 0)
    return pl.pallas_call(
        _inproj_sample_kernel,
        grid=(w_main.shape[1] // bn,),
        in_specs=[pl.BlockSpec((n, d), const),
                  pl.BlockSpec(mod.shape, const),
                  pl.BlockSpec((1, d), const),
                  pl.BlockSpec((d, bn), lambda i: (0, i)),
                  pl.BlockSpec(w_ifT.shape, const),
                  pl.BlockSpec((1, 2 * M_HEADS), const)],
        out_specs=[pl.BlockSpec((n, bn), lambda i: (0, i)),
                   pl.BlockSpec((n, 2 * M_HEADS), const)],
        out_shape=[jax.ShapeDtypeStruct((n, w_main.shape[1]), F32),
                   jax.ShapeDtypeStruct((n, 2 * M_HEADS), F32)],
        compiler_params=_params("arbitrary"),
        name="sample_inproj",
    )(x, mod, norm_g.reshape(1, d), w_main, w_ifT, b_if.reshape(1, 2 * M_HEADS))


_PAGES_PER_CHUNK = 8
_PAGE_RING = 5
_PAGES_PER_BLOCK = MOBA_BLOCK // PAGE_SIZE
_SAMPLE_SEQS_PER_STEP = 1


def _attend_sample_kernel(pt_ref, sel_ref, ck_ref, cv_ref, q_ref, kn_ref, vn_ref, o_ref,
                          kbuf, vbuf, sem):
    b = pl.program_id(0)
    nseq = pl.num_programs(0)
    H, dh = ATT_HEADS, ATT_HEAD_DIM
    npg = MOBA_TOPK * _PAGES_PER_BLOCK

    def copies(bb, slot):
        out = []
        for hh in range(H):
            for r in range(MOBA_TOPK):
                blk = sel_ref[(bb * H + hh) * MOBA_TOPK + r]
                for i in range(_PAGES_PER_BLOCK):
                    page = pt_ref[bb, blk * _PAGES_PER_BLOCK + i]
                    pg = r * _PAGES_PER_BLOCK + i
                    out.append(pltpu.make_async_copy(
                        ck_ref.at[0, page, hh], kbuf.at[slot, hh, pg], sem.at[0, slot]))
                    out.append(pltpu.make_async_copy(
                        cv_ref.at[0, page, hh], vbuf.at[slot, hh, pg], sem.at[1, slot]))
        return out

    @pl.when(b == 0)
    def _():
        for cp in copies(0, 0):
            cp.start()

    slot = b % 2

    @pl.when(b + 1 < nseq)
    def _():
        for cp in copies(b + 1, 1 - slot):
            cp.start()

    for cp in copies(b, slot):
        cp.wait()

    scale = ATT_HEAD_DIM ** -0.5
    nt = (((1,), (1,)), ((), ()))
    q8s = [jnp.broadcast_to(q_ref[0, hh:hh + 1, :], (V7X_SUBLANES, dh)).astype(BF16)
           for hh in range(H)]
    ss = [jnp.concatenate(
        [jnp.dot(q8s[hh], kbuf[slot, hh, pg].astype(BF16), preferred_element_type=F32)[0:1]
         for pg in range(npg)], axis=1) * scale for hh in range(H)]
    p8s, p_news, ls = [], [], []
    for hh in range(H):
        kn = kn_ref[0, hh:hh + 1, :].astype(BF16).astype(F32)
        s_new = jnp.sum(q8s[hh][0:1].astype(F32) * kn, axis=1, keepdims=True) * scale
        m = jnp.maximum(jnp.max(ss[hh], axis=1, keepdims=True), s_new)
        p = jnp.exp(ss[hh] - m)
        p_new = jnp.exp(s_new - m)
        ls.append(jnp.sum(p, axis=1, keepdims=True) + p_new)
        p_news.append(p_new)
        p8s.append(jnp.broadcast_to(p, (V7X_SUBLANES, p.shape[1])).astype(BF16))
    pvs = [[lax.dot_general(p8s[hh][:, pg * PAGE_SIZE:(pg + 1) * PAGE_SIZE],
                            vbuf[slot, hh, pg].astype(BF16), nt, preferred_element_type=F32)[0:1]
            for pg in range(npg)] for hh in range(H)]
    for hh in range(H):
        vn = vn_ref[0, hh:hh + 1, :].astype(BF16).astype(F32)
        acc = p_news[hh].astype(BF16).astype(F32) * vn
        for x in pvs[hh]:
            acc = acc + x
        o_ref[0, :, hh * dh:(hh + 1) * dh] = acc / ls[hh]


def _attend_sample(cache_k, cache_v, page_table, sel_flat, q, k_new, v_new):
    nseq, n_pages = page_table.shape
    H, dh = ATT_HEADS, ATT_HEAD_DIM
    npg = MOBA_TOPK * _PAGES_PER_BLOCK
    ps = cache_k.shape[-1]
    grid_spec = pltpu.PrefetchScalarGridSpec(
        num_scalar_prefetch=2,
        grid=(nseq,),
        in_specs=[pl.BlockSpec(memory_space=pl.ANY),
                  pl.BlockSpec(memory_space=pl.ANY),
                  pl.BlockSpec((1, H, dh), lambda b, pt, sel: (b, 0, 0)),
                  pl.BlockSpec((1, H, dh), lambda b, pt, sel: (b, 0, 0)),
                  pl.BlockSpec((1, H, dh), lambda b, pt, sel: (b, 0, 0))],
        out_specs=pl.BlockSpec((1, 1, ATT_WIDTH), lambda b, pt, sel: (b, 0, 0)),
        scratch_shapes=[pltpu.VMEM((2, H, npg, dh, ps), F32),
                        pltpu.VMEM((2, H, npg, dh, ps), F32),
                        pltpu.SemaphoreType.DMA((2, 2))],
    )
    return pl.pallas_call(
        _attend_sample_kernel,
        grid_spec=grid_spec,
        out_shape=jax.ShapeDtypeStruct((nseq, 1, ATT_WIDTH), F32),
        compiler_params=_params("arbitrary"),
        name="sample_attend",
    )(page_table, sel_flat, cache_k, cache_v, q, k_new, v_new)


def _mlstm_sample_kernel(qk_ref, v_ref, o_ref, g_ref, conv_ref, C_ref, n_ref, wc_ref, bc_ref, ng_ref,
                         h_ref, C_out, n_out, m_out, conv_out):
    for sq in range(qk_ref.shape[0]):
        x = qk_ref[sq]
        conv = conv_ref[sq]
        acc = x * wc_ref[M_CONV - 1:M_CONV, :] + bc_ref[...]
        for jj in range(M_CONV - 1):
            acc += conv[jj:jj + 1, :] * wc_ref[jj:jj + 1, :]
        qk = _silu(acc)
        conv_out[sq] = jnp.concatenate([conv[1:], x], axis=0)
        g = g_ref[sq]
        m_new = []
        for hh in range(M_HEADS):
            q = qk[:, hh * M_QK_DIM:(hh + 1) * M_QK_DIM]
            k = qk[:, M_QK_WIDTH + hh * M_QK_DIM:M_QK_WIDTH + (hh + 1) * M_QK_DIM] * (M_QK_DIM ** -0.5)
            qb = q.astype(BF16)
            kb = k.astype(BF16)
            v = v_ref[sq][:, hh * M_V_DIM:(hh + 1) * M_V_DIM]
            i_pre = g[:, hh:hh + 1]
            f_pre = g[:, M_HEADS + hh:M_HEADS + hh + 1]
            m_prev = g[:, 2 * M_HEADS + hh:2 * M_HEADS + hh + 1]
            bb = _log_sigmoid(f_pre)
            cc = i_pre - bb
            M = jnp.maximum(m_prev, cc)
            w = jnp.exp(cc - M)
            a = jnp.exp(m_prev - M)
            C = C_ref[sq, hh]
            n = n_ref[sq, hh:hh + 1, :]
            qf = qb.astype(F32)
            s = jnp.sum(qf * kb.astype(F32), axis=1, keepdims=True) * w
            q8 = jnp.broadcast_to(qb, (V7X_SUBLANES, M_QK_DIM))
            inter = lax.dot_general(q8, C.astype(BF16), (((1,), (1,)), ((), ())),
                                    preferred_element_type=F32)[0:1]
            vb = v.astype(BF16).astype(F32)
            num = a * inter + s.astype(BF16).astype(F32) * vb
            den = a * jnp.sum(qf * n.astype(BF16).astype(F32), axis=1, keepdims=True) + s
            hv = num / jnp.maximum(jnp.abs(den), jnp.exp(-(bb + M)))
            hv = hv * lax.rsqrt(jnp.mean(hv * hv, axis=-1, keepdims=True) + NORM_EPS)
            sl = slice(hh * M_V_DIM, (hh + 1) * M_V_DIM)
            h_ref[sq, :, sl] = hv * ng_ref[:, sl] * jax.nn.sigmoid(o_ref[sq][:, sl])
            wv = jnp.broadcast_to(w * v, (V7X_SUBLANES, M_V_DIM))
            wv = jnp.concatenate([wv, jnp.zeros((V7X_LANES - V7X_SUBLANES, M_V_DIM), F32)], axis=0)
            wv_col = wv.T[:, 0:1]
            C_out[sq, hh] = a * C + wv_col * k
            n_out[sq, hh:hh + 1, :] = a * n + w * k
            m_new.append(bb + M)
        lane = lax.broadcasted_iota(jnp.int32, (1, V7X_LANES), 1)
        mrow = jnp.zeros((1, V7X_LANES), F32)
        for hh in range(M_HEADS):
            mrow = jnp.where(lane == hh, m_new[hh], mrow)
        m_out[sq] = mrow


def _mlstm_sample(z3, g16, state_conv, state_C, state_n, w_conv, b_conv, m_norm_g):
    nseq = z3.shape[0]
    MH, dk, dv = M_HEADS, M_QK_DIM, M_V_DIM
    W = 2 * M_QK_WIDTH
    sb = _SAMPLE_SEQS_PER_STEP
    assert nseq % sb == 0
    const = lambda b: (0, 0)
    return pl.pallas_call(
        _mlstm_sample_kernel,
        grid=(nseq // sb,),
        in_specs=[
            pl.BlockSpec((sb, 1, W), lambda b: (b, 0, _C_MQK // W)),
            pl.BlockSpec((sb, 1, M_WIDTH), lambda b: (b, 0, _C_MV // M_WIDTH)),
            pl.BlockSpec((sb, 1, M_WIDTH), lambda b: (b, 0, _C_MO // M_WIDTH)),
            pl.BlockSpec((sb, 1, 4 * MH), lambda b: (b, 0, 0)),
            pl.BlockSpec((sb, M_CONV - 1, W), lambda b: (b, 0, 0)),
            pl.BlockSpec((sb, MH, dv, dk), lambda b: (b, 0, 0, 0)),
            pl.BlockSpec((sb, MH, dk), lambda b: (b, 0, 0)),
            pl.BlockSpec((M_CONV, W), const),
            pl.BlockSpec((1, W), const),
            pl.BlockSpec((1, M_WIDTH), const),
        ],
        out_specs=[
            pl.BlockSpec((sb, 1, M_WIDTH), lambda b: (b, 0, 0)),
            pl.BlockSpec((sb, MH, dv, dk), lambda b: (b, 0, 0, 0)),
            pl.BlockSpec((sb, MH, dk), lambda b: (b, 0, 0)),
            pl.BlockSpec((sb, 1, V7X_LANES), lambda b: (b, 0, 0)),
            pl.BlockSpec((sb, M_CONV - 1, W), lambda b: (b, 0, 0)),
        ],
        out_shape=[
            jax.ShapeDtypeStruct((nseq, 1, M_WIDTH), F32),
            jax.ShapeDtypeStruct((nseq, MH, dv, dk), F32),
            jax.ShapeDtypeStruct((nseq, MH, dk), F32),
            jax.ShapeDtypeStruct((nseq, 1, V7X_LANES), F32),
            jax.ShapeDtypeStruct((nseq, M_CONV - 1, W), F32),
        ],
        compiler_params=_params("arbitrary"),
        name="sample_mlstm",
    )(z3, z3, z3, g16, state_conv, state_C, state_n, w_conv, b_conv.reshape(1, W),
      m_norm_g.reshape(1, M_WIDTH))


def _merge_sample_kernel(ya_ref, hm_ref, z_ref, x_ref, mod_ref, watt_ref, wml_ref, wout_ref, fg_ref,
                         y_ref):
    d = x_ref.shape[1]
    ya = ya_ref[...] * _silu(z_ref[:, _C_AZ:_C_MQK])
    ya = jnp.dot(ya.astype(BF16), watt_ref[...], preferred_element_type=F32)
    ym = hm_ref[...] * _silu(z_ref[:, _C_MZ:_C_MIF])
    ym = jnp.dot(ym.astype(BF16), wml_ref[...], preferred_element_type=F32)
    tail = z_ref[:, _C_MIF:]
    ga = tail[:, _C_GA - _C_MIF:_C_GM - _C_MIF]
    gm = tail[:, _C_GM - _C_MIF:_C_END - _C_MIF]
    u = jax.nn.sigmoid(ga) * ya + jax.nn.sigmoid(gm) * ym
    upd = jnp.dot(u.astype(BF16), wout_ref[...], preferred_element_type=F32)
    y = x_ref[...] + mod_ref[:, 2 * d:3 * d] * upd
    y = y * lax.rsqrt(jnp.mean(y * y, axis=-1, keepdims=True) + NORM_EPS)
    y_ref[...] = y * fg_ref[...]


def _merge_sample(ya, hm, z, x, mod, w_att, w_mlstm, w_out, final_g):
    n, d = x.shape
    full = lambda a: pl.BlockSpec(a.shape, lambda i: (0,) * a.ndim)
    fg = final_g.reshape(1, d)
    args = (ya, hm, z, x, mod, w_att, w_mlstm, w_out, fg)
    return pl.pallas_call(
        _merge_sample_kernel,
        grid=(1,),
        in_specs=[full(a) for a in args],
        out_specs=pl.BlockSpec((n, d), lambda i: (0, 0)),
        out_shape=jax.ShapeDtypeStruct((n, d), F32),
        compiler_params=_params("arbitrary"),
        name="sample_merge",
    )(*args)


def _gate_scores_kernel(sc_ref, idx_ref, *, nblk):
    sc = sc_ref[0] / MOBA_BLOCK
    lane = lax.broadcasted_iota(jnp.int32, sc.shape, 1)
    cnt = jnp.zeros(sc.shape, jnp.int32)
    for m in range(nblk):
        col = sc[:, m:m + 1]
        beats = (col > sc) | ((col == sc) & (lane > m))
        cnt += beats.astype(jnp.int32)
    out = jnp.zeros(sc.shape, jnp.int32)
    for r in range(MOBA_TOPK):
        idx = jnp.sum(jnp.where((cnt == r) & (lane < nblk), lane, 0), axis=1, keepdims=True)
        out = jnp.where(lane == r, idx, out)
    idx_ref[0] = out


def _gate_scores(block_scores, nblk):
    nseq, H, lanes = block_scores.shape
    return pl.pallas_call(
        functools.partial(_gate_scores_kernel, nblk=nblk),
        grid=(nseq,),
        in_specs=[pl.BlockSpec((1, H, lanes), lambda b: (b, 0, 0))],
        out_specs=pl.BlockSpec((1, H, lanes), lambda b: (b, 0, 0)),
        out_shape=jax.ShapeDtypeStruct((nseq, H, lanes), jnp.int32),
        compiler_params=_params("arbitrary"),
        name="sample_gate",
    )(block_scores)


def _sample_inputs(x, mod, norm_g, w_main, w_ifT, b_if):
    nseq = x.shape[0]
    H, dh = ATT_HEADS, ATT_HEAD_DIM
    z, g = _inproj_sample(x, mod, norm_g, w_main, w_ifT, b_if)
    q = z[:, _C_AQ:_C_AK].reshape(nseq, H, dh)
    q_lanes = jnp.broadcast_to(q[..., None], (nseq, H, dh, PAGE_SIZE))
    return z, g, q, q_lanes


def _sample_layer(x, mod, z, g, q, block_scores, cache_kT, cache_vT, page_table,
                  state_C, state_n, state_m, state_conv,
                  w_conv, b_conv, m_norm_g, w_att, w_mlstm, w_out, final_g):
    nseq = x.shape[0]
    H, dh = ATT_HEADS, ATT_HEAD_DIM
    nblk = page_table.shape[1] // _PAGES_PER_BLOCK
    assert nblk >= MOBA_TOPK
    k_new = z[:, _C_AK:_C_AV].reshape(nseq, H, dh)
    v_new = z[:, _C_AV:_C_AZ].reshape(nseq, H, dh)
    sel = _gate_scores(block_scores, nblk)
    sel_flat = sel[:, :, :MOBA_TOPK].reshape(-1)
    ya = _attend_sample(cache_kT, cache_vT, page_table, sel_flat, q, k_new, v_new)
    g16 = jnp.concatenate([g, state_m, jnp.zeros_like(state_m)], axis=1).reshape(nseq, 1, 4 * M_HEADS)
    hm, C, n, m, conv_new = _mlstm_sample(z.reshape(nseq, 1, -1), g16, state_conv, state_C, state_n,
                                          w_conv, b_conv, m_norm_g)
    y = _merge_sample(ya.reshape(nseq, ATT_WIDTH), hm.reshape(nseq, M_WIDTH), z, x, mod,
                      w_att, w_mlstm, w_out, final_g)
    return (y.reshape(nseq, 1, -1), k_new.reshape(nseq, 1, H, dh), v_new.reshape(nseq, 1, H, dh),
            C, n, m[:, 0, :M_HEADS], conv_new)


def kernel(x_prompt, x_sample, cache_k, cache_v, state_C, state_n, state_m, state_conv, page_table,
           c_prompt, c_sample, w_ada, b_ada, norm_g, w_in, b_if, w_conv, b_conv, m_norm_g,
           w_att, w_mlstm, w_out, final_g):
    depth = w_in.shape[0]
    assert depth == 1, "single-layer step"
    B = x_prompt.shape[0]
    nseq = x_sample.shape[0]
    pad = (-(B + nseq)) % V7X_SUBLANES
    c_all = jnp.concatenate([c_prompt, c_sample, jnp.zeros((pad, c_prompt.shape[1]), F32)], axis=0)
    mod = _ada(c_all, w_ada[0], b_ada[0])
    mod_p = mod[:B].reshape(B, 3, D_MODEL)
    mod_s = mod[B:B + nseq]
    w_main, w_ifT = _pack_w_in(w_in[0])
    wa, wm, wo = w_att[0].astype(BF16), w_mlstm[0].astype(BF16), w_out[0].astype(BF16)
    cache_kT = jnp.transpose(cache_k, (0, 1, 3, 4, 2))
    cache_vT = jnp.transpose(cache_v, (0, 1, 3, 4, 2))
    xs = x_sample[:, 0, :]
    z, g, q, q_lanes = _sample_inputs(xs, mod_s, norm_g[0], w_main, w_ifT, b_if[0])
    *outs_p, block_scores = _prompt_layer(
        x_prompt, mod_p, norm_g[0], w_main, w_ifT, b_if[0], w_conv[0], b_conv[0], m_norm_g[0],
        wa, wm, wo, final_g, page_table, cache_kT, q_lanes)
    outs_s = _sample_layer(xs, mod_s, z, g, q, block_scores, cache_kT, cache_vT, page_table,
                           state_C[0], state_n[0], state_m[0], state_conv[0],
                           w_conv[0], b_conv[0], m_norm_g[0], wa, wm, wo, final_g)
    y_p, *st_p = outs_p
    y_s, *st_s = outs_s
    return (y_p, y_s) + tuple(a[None] for a in st_p) + tuple(a[None] for a in st_s)
```

```python
import functools

import jax
import jax.numpy as jnp
from jax import lax
from jax.experimental import pallas as pl
from jax.experimental.pallas import tpu as pltpu

F32 = jnp.float32
BF16 = jnp.bfloat16

D_MODEL = 1024
ATT_HEADS = 8
ATT_HEAD_DIM = 64
ATT_WIDTH = ATT_HEADS * ATT_HEAD_DIM
MOBA_BLOCK = 256
MOBA_TOPK = 3
M_HEADS = 4
M_V_DIM = 256
M_QK_DIM = 128
M_QK_WIDTH = M_HEADS * M_QK_DIM
M_WIDTH = M_HEADS * M_V_DIM
M_CONV = 4
PAGE_SIZE = 128
NORM_EPS = 1e-6
NEG_SCORE = -1e30
_LOG2E = 1.4426950408889634

V7X_LANES = 128
V7X_SUBLANES = 8
V7X_VMEM_LIMIT_BYTES = 56 * 1024 * 1024

_C_AQ, _C_AK, _C_AV, _C_AZ = 0, 512, 1024, 1536
_C_MQK, _C_MV, _C_MO, _C_MZ = 2048, 3072, 4096, 5120
_C_MIF, _C_GA, _C_GM, _C_END = 6144, 6152, 7176, 8200


def _silu(x):
    return x * jax.nn.sigmoid(x)


def _split3(x):
    hi = x.astype(BF16)
    r1 = x - hi.astype(F32)
    mid = r1.astype(BF16)
    lo = (r1 - mid.astype(F32)).astype(BF16)
    return hi, mid, lo


def _pack_w_in(w_in):
    pad = (-w_in.shape[1]) % V7X_LANES
    w_main = jnp.concatenate([w_in.astype(BF16), jnp.zeros((w_in.shape[0], pad), BF16)], axis=1)
    w_ifT = w_main[:, _C_MIF:_C_GA].T
    return w_main, w_ifT


def _params(*sem):
    return pltpu.CompilerParams(dimension_semantics=sem, vmem_limit_bytes=V7X_VMEM_LIMIT_BYTES)


def _ada_kernel(c_ref, w_ref, b_ref, o_ref):
    a = _silu(c_ref[...]).astype(BF16)
    acc = jnp.dot(a, w_ref[...].astype(BF16), preferred_element_type=F32)
    o_ref[...] = acc + b_ref[...]


def _ada(c, w_ada, b_ada):
    n, d = c.shape
    d3 = w_ada.shape[1]
    bn = 512
    return pl.pallas_call(
        _ada_kernel,
        grid=(d3 // bn,),
        in_specs=[pl.BlockSpec((n, d), lambda i: (0, 0)),
                  pl.BlockSpec((d, bn), lambda i: (0, i)),
                  pl.BlockSpec((1, bn), lambda i: (0, i))],
        out_specs=pl.BlockSpec((n, bn), lambda i: (0, i)),
        out_shape=jax.ShapeDtypeStruct((n, d3), F32),
        compiler_params=_params("arbitrary"),
        name="ada_mod",
    )(c, w_ada, b_ada.reshape(1, d3))


def _inproj_kernel(x_ref, mod_ref, ng_ref, w_ref, wif_ref, bif_ref, bifn_ref, wc_ref, bc_ref,
                   qT_ref, kh_ref, vT_ref, kf_ref, vf_ref, kmean_ref,
                   az_ref, qk_ref, mv_ref, mo_ref, mz_ref, ga_ref, gm_ref, gT_ref, mvT_ref, gc_ref,
                   ctail_ref, tail_s):
    j = pl.program_id(1)
    x = x_ref[0]
    xn = x * lax.rsqrt(jnp.mean(x * x, axis=-1, keepdims=True) + NORM_EPS) * ng_ref[...]
    shift = mod_ref[0, 0:1, :]
    scale = mod_ref[0, 1:2, :]
    h = (xn * (1.0 + scale) + shift).astype(BF16)

    def proj(a, b):
        return jnp.dot(h, w_ref[:, a:b], preferred_element_type=F32)

    @pl.when(j == 0)
    def _():
        tail_s[...] = jnp.zeros_like(tail_s)

    mqk = proj(_C_MQK, _C_MV)
    ctail_ref[0] = mqk[mqk.shape[0] - V7X_SUBLANES:, :]
    qk = _silu(_causal_conv(mqk, tail_s, wc_ref, bc_ref))
    qk_ref[0, :, 0:M_QK_WIDTH] = qk[:, 0:M_QK_WIDTH].astype(BF16)
    qk_ref[0, :, M_QK_WIDTH:] = (qk[:, M_QK_WIDTH:] * (M_QK_DIM ** -0.5)).astype(BF16)

    mo_ref[0] = proj(_C_MO, _C_MZ).astype(BF16)

    q = proj(_C_AQ, _C_AK) * (ATT_HEAD_DIM ** -0.5 * _LOG2E)
    qT_ref[0, 0] = q.T.astype(BF16)
    mz_ref[0] = proj(_C_MZ, _C_MIF).astype(BF16)

    dh = ATT_HEAD_DIM
    tm = x.shape[0]
    one_col = jnp.where(lax.broadcasted_iota(jnp.int32, (tm, dh), 1) == 0, 1.0, 0.0)
    one_row = jnp.where(lax.broadcasted_iota(jnp.int32, (_MOBA_V_PAD, tm), 0) == 0, 1.0, 0.0)
    k = proj(_C_AK, _C_AV)
    kf_ref[0] = k
    ksum = jnp.sum(k, axis=0, keepdims=True) / MOBA_BLOCK
    for hh in range(ATT_HEADS):
        sl = slice(hh * dh, (hh + 1) * dh)
        kh_ref[0, 0, hh] = jnp.concatenate([k[:, sl], one_col], axis=1).astype(BF16)
        kmean_ref[0, hh, pl.ds(j, 1), :] = ksum[:, sl]
    az_ref[0] = proj(_C_AZ, _C_MQK).astype(BF16)

    v = proj(_C_AV, _C_AZ)
    vf_ref[0] = v
    vT = v.T
    for hh in range(ATT_HEADS):
        r0 = hh * _MOBA_V_ROWS
        vT_ref[0, 0, r0:r0 + dh, :] = vT[hh * dh:(hh + 1) * dh].astype(BF16)
        vT_ref[0, 0, r0 + dh:r0 + _MOBA_V_ROWS, :] = one_row.astype(BF16)

    tail = proj(_C_MIF, w_ref.shape[1])
    gc_ref[0] = tail[:, 0:_C_GA - _C_MIF] + bifn_ref[...]
    ga_ref[0] = tail[:, _C_GA - _C_MIF:_C_GM - _C_MIF].astype(BF16)
    gm_ref[0] = tail[:, _C_GM - _C_MIF:_C_END - _C_MIF].astype(BF16)
    gT = lax.dot_general(wif_ref[...], h, (((1,), (1,)), ((), ())), preferred_element_type=F32)
    gT_ref[0] = gT + bif_ref[...]

    mv = proj(_C_MV, _C_MO)
    mv_ref[0] = mv.astype(BF16)
    mvT_ref[0, 0] = mv.T.astype(BF16)


def _inproj(x, mod3, norm_g, w_main, w_ifT, b_if, w_conv, b_conv):
    B, S, D = x.shape
    W = 2 * M_QK_WIDTH
    tm = MOBA_BLOCK
    nb = S // tm
    H, dh = ATT_HEADS, ATT_HEAD_DIM
    const = lambda b, j: (0, 0)
    tok = lambda w: pl.BlockSpec((1, tm, w), lambda b, j: (b, j, 0))
    blk = pl.BlockSpec((1, 1, ATT_WIDTH, tm), lambda b, j: (b, j, 0, 0))
    out_shape = [
        jax.ShapeDtypeStruct((B, nb, ATT_WIDTH, tm), BF16),
        jax.ShapeDtypeStruct((B, nb, H, tm, 2 * dh), BF16),
        jax.ShapeDtypeStruct((B, nb, H * _MOBA_V_ROWS, tm), BF16),
        jax.ShapeDtypeStruct((B, S, ATT_WIDTH), F32),
        jax.ShapeDtypeStruct((B, S, ATT_WIDTH), F32),
        jax.ShapeDtypeStruct((B, H, nb, dh), F32),
        jax.ShapeDtypeStruct((B, S, ATT_WIDTH), BF16),
        jax.ShapeDtypeStruct((B, S, W), BF16),
        jax.ShapeDtypeStruct((B, S, M_WIDTH), BF16),
        jax.ShapeDtypeStruct((B, S, M_WIDTH), BF16),
        jax.ShapeDtypeStruct((B, S, M_WIDTH), BF16),
        jax.ShapeDtypeStruct((B, S, D), BF16),
        jax.ShapeDtypeStruct((B, S, D), BF16),
        jax.ShapeDtypeStruct((B, 2 * M_HEADS, S), F32),
        jax.ShapeDtypeStruct((B, nb, M_WIDTH, tm), BF16),
        jax.ShapeDtypeStruct((B, S, 2 * M_HEADS), F32),
        jax.ShapeDtypeStruct((B, V7X_SUBLANES, W), F32),
    ]
    out_specs = [
        blk,
        pl.BlockSpec((1, 1, H, tm, 2 * dh), lambda b, j: (b, j, 0, 0, 0)),
        pl.BlockSpec((1, 1, H * _MOBA_V_ROWS, tm), lambda b, j: (b, j, 0, 0)),
        tok(ATT_WIDTH), tok(ATT_WIDTH),
        pl.BlockSpec((1, H, nb, dh), lambda b, j: (b, 0, 0, 0)),
        tok(ATT_WIDTH), tok(2 * M_QK_WIDTH), tok(M_WIDTH), tok(M_WIDTH), tok(M_WIDTH), tok(D), tok(D),
        pl.BlockSpec((1, 2 * M_HEADS, tm), lambda b, j: (b, 0, j)),
        pl.BlockSpec((1, 1, M_WIDTH, tm), lambda b, j: (b, j, 0, 0)),
        tok(2 * M_HEADS),
        pl.BlockSpec((1, V7X_SUBLANES, W), lambda b, j: (b, 0, 0)),
    ]
    in_specs = [
        pl.BlockSpec((1, tm, D), lambda b, j: (b, j, 0)),
        pl.BlockSpec((1, 3, D), lambda b, j: (b, 0, 0)),
        pl.BlockSpec((1, D), const),
        pl.BlockSpec(w_main.shape, const, pipeline_mode=pl.Buffered(1)),
        pl.BlockSpec(w_ifT.shape, const),
        pl.BlockSpec((2 * M_HEADS, 1), const),
        pl.BlockSpec((1, 2 * M_HEADS), const),
        pl.BlockSpec((M_CONV, W), const),
        pl.BlockSpec((1, W), const),
    ]
    return pl.pallas_call(
        _inproj_kernel,
        grid=(B, nb),
        in_specs=in_specs,
        out_specs=out_specs,
        out_shape=out_shape,
        scratch_shapes=[pltpu.VMEM((V7X_SUBLANES, W), F32)],
        compiler_params=_params("arbitrary", "arbitrary"),
        name="prompt_inproj",
    )(x, mod3, norm_g.reshape(1, D), w_main, w_ifT, b_if.reshape(2 * M_HEADS, 1),
      b_if.reshape(1, 2 * M_HEADS), w_conv, b_conv.reshape(1, W))


_MOBA_HEADS_PER_STEP = 8
_MOBA_HEAD_GROUP = 8
_MOBA_V_PAD = 16
_MOBA_V_ROWS = ATT_HEAD_DIM + _MOBA_V_PAD


def _moba_phases(j):
    return j + 2 * ((j * j) // 4)


def _moba_kernel(pt_ref, qT_ref, kh_ref, vT_ref, kmean_ref, ck_ref, qb_ref, o_ref, sc_ref,
                 bias_ref, qs_ref, acc_ref, pbuf, qbuf, psem, qsem):
    b = pl.program_id(0)
    j = pl.program_id(2)
    nbq = pl.num_programs(2)
    nseq, n_pages = pt_ref.shape
    cpb = n_pages // _PAGES_PER_CHUNK
    bpc = _PAGES_PER_CHUNK // _PAGES_PER_BLOCK
    n_chunks = nseq * cpb
    ring = pbuf.shape[0]
    ahead = ring - 2

    def chunk_copy(g, i):
        page = pt_ref[g // cpb, (g % cpb) * _PAGES_PER_CHUNK + i]
        slot = g % ring
        return pltpu.make_async_copy(ck_ref.at[0, page], pbuf.at[slot, i], psem.at[slot])

    qslots = qbuf.shape[0]

    def q_copy(seq):
        s = seq % qslots
        return pltpu.make_async_copy(qb_ref.at[seq], qbuf.at[s], qsem.at[s])

    def start_chunk(g):
        @pl.when(g < n_chunks)
        def _():
            for i in range(_PAGES_PER_CHUNK):
                chunk_copy(g, i).start()

    step = b * pl.num_programs(1) + pl.program_id(1)

    @pl.when((step == 0) & (j == 0))
    def _():
        sc_ref[...] = jnp.zeros_like(sc_ref)
        q_copy(0).start()
        for g in range(ahead):
            start_chunk(g)

    lane_blk = lax.broadcasted_iota(jnp.int32, sc_ref.shape[1:], 1)

    def page_sync(g):
        live = g < n_chunks
        gc = jnp.minimum(g, n_chunks - 1)
        seq = gc // cpb
        ci = gc % cpb

        @pl.when(live & (ci == 0))
        def _():
            q_copy(seq).wait()

        @pl.when(live & (ci == 0) & (seq + 1 < nseq))
        def _():
            q_copy(seq + 1).start()

        start_chunk(g + ahead)

        @pl.when(live)
        def _():
            for i in range(_PAGES_PER_CHUNK):
                chunk_copy(gc, i).wait()

    def page_scores(g):
        gc = jnp.minimum(g, n_chunks - 1)
        seq = gc // cpb
        ci = gc % cpb
        slot = gc % ring
        qb = qbuf[seq % qslots]
        row = sc_ref[seq]
        for blk in range(bpc):
            p = pbuf[slot, blk * _PAGES_PER_BLOCK]
            for i in range(1, _PAGES_PER_BLOCK):
                p = p + pbuf[slot, blk * _PAGES_PER_BLOCK + i]
            part = jnp.sum(p * qb, axis=1)
            val = jnp.sum(part, axis=1, keepdims=True)
            row = jnp.where(lane_blk == ci * bpc + blk, val, row)
        sc_ref[seq] = row

    phase0 = step * _moba_phases(nbq) + _moba_phases(j)
    hp = kmean_ref.shape[1]
    nb = kmean_ref.shape[2]
    tq = qT_ref.shape[3]
    dh = ATT_HEAD_DIM
    vr = _MOBA_V_ROWS
    qTs = [qT_ref[0, 0, hh * dh:(hh + 1) * dh, :] for hh in range(hp)]
    for hh in range(hp):
        for slot in range(2):
            qs_ref[slot, hh, 0:dh, :] = qTs[hh]
            qs_ref[slot, hh, dh:2 * dh, :] = jnp.zeros((dh, tq), BF16)

    blk = lax.broadcasted_iota(jnp.int32, (nb, tq), 0)
    past = blk < j
    for hh in range(hp):
        sg = sum(jnp.dot(part, qTs[hh], preferred_element_type=F32)
                 for part in _split3(kmean_ref[0, hh]))
        sg = jnp.where(past, sg, NEG_SCORE)
        cnt = jnp.zeros((nb, tq), jnp.int32)
        for m in range(nb):
            row = sg[m:m + 1, :]
            beats = (row > sg) | ((row == sg) & (blk > m))
            cnt += beats.astype(jnp.int32)
        bias_ref[hh] = jnp.where((cnt < MOBA_TOPK) & past, 0.0, NEG_SCORE)

    kpos = lax.broadcasted_iota(jnp.int32, (tq, tq), 0)
    qpos = lax.broadcasted_iota(jnp.int32, (tq, tq), 1)
    causal = kpos <= qpos

    def score(n, slot, hh):
        return jnp.dot(kh_ref[0, n, hh], qs_ref[slot, hh], preferred_element_type=F32)

    def value(n, hh, p):
        return jnp.dot(vT_ref[0, n, hh * vr:(hh + 1) * vr, :], p, preferred_element_type=F32)

    def scores(n, slot=0):
        return [score(n, slot, hh) for hh in range(hp)]

    def values(n, ps):
        return [value(n, hh, ps[hh]) for hh in range(hp)]

    ss = [jnp.where(causal, s, NEG_SCORE) for s in scores(j)]
    ms = [jnp.max(s, axis=0, keepdims=True) for s in ss]
    pbs = [jnp.exp2((s - m).astype(BF16)) for s, m in zip(ss, ms)]
    for hh, x in enumerate(values(j, pbs)):
        acc_ref[hh] = x
    page_sync(phase0)
    page_scores(phase0)

    row0 = lax.broadcasted_iota(jnp.int32, (_MOBA_V_PAD, tq), 0) == 0

    def body(t, ms):
        n0 = 2 * t
        real1 = n0 + 1 < j
        n1 = jnp.minimum(n0 + 1, j - 1)
        g = phase0 + 1 + 2 * t
        page_sync(g)
        page_sync(g + 1)
        for hh in range(hp):
            mask0 = jnp.where(row0, bias_ref[hh, pl.ds(n0, 1), :], 0.0)
            qs_ref[0, hh, dh:dh + _MOBA_V_PAD, :] = mask0.astype(BF16)
            bias1 = jnp.where(real1, bias_ref[hh, pl.ds(n1, 1), :], NEG_SCORE)
            qs_ref[1, hh, dh:dh + _MOBA_V_PAD, :] = jnp.where(row0, bias1, 0.0).astype(BF16)
        page_scores(g)
        page_scores(g + 1)
        m_out = []
        for h0 in range(0, hp, _MOBA_HEAD_GROUP):
            hs = range(h0, min(h0 + _MOBA_HEAD_GROUP, hp))
            ss0 = [score(n0, 0, hh) for hh in hs]
            ss1 = [score(n1, 1, hh) for hh in hs]
            m0 = [ms[hh] for hh in hs]
            m1 = [jnp.maximum(m, jnp.max(s, axis=0, keepdims=True)) for m, s in zip(m0, ss0)]
            a0 = [jnp.exp2(m - mn) for m, mn in zip(m0, m1)]
            pv0 = [value(n0, hh, jnp.exp2((s - mn).astype(BF16)))
                   for hh, s, mn in zip(hs, ss0, m1)]
            m2 = [jnp.maximum(m, jnp.max(s, axis=0, keepdims=True)) for m, s in zip(m1, ss1)]
            a1 = [jnp.exp2(m - mn) for m, mn in zip(m1, m2)]
            pv1 = [value(n1, hh, jnp.exp2((s - mn).astype(BF16)))
                   for hh, s, mn in zip(hs, ss1, m2)]
            for i, hh in enumerate(hs):
                acc_ref[hh] = (a0[i] * acc_ref[hh] + pv0[i]) * a1[i] + pv1[i]
            m_out += m2
        return tuple(m_out)

    lax.fori_loop(0, (j + 1) // 2, body, tuple(ms))
    for hh in range(hp):
        acc = acc_ref[hh]
        o_ref[0, 0, hh * dh:(hh + 1) * dh, :] = (acc[0:dh] / acc[dh:dh + 1]).astype(o_ref.dtype)


def _moba_prompt(qT, kh, vT, kmean, page_table, cache_kT, q_lanes):
    B, nb, _, tq = qT.shape
    H, dh, hp = ATT_HEADS, ATT_HEAD_DIM, _MOBA_HEADS_PER_STEP
    nseq, n_pages = page_table.shape
    ps = cache_kT.shape[-1]
    assert n_pages % _PAGES_PER_CHUNK == 0 and n_pages // _PAGES_PER_BLOCK <= V7X_LANES
    n_chunks = nseq * (n_pages // _PAGES_PER_CHUNK)
    assert _PAGE_RING - 2 <= n_chunks <= B * (H // hp) * _moba_phases(nb), \
        "one page chunk per key-block phase"
    grid_spec = pltpu.PrefetchScalarGridSpec(
        num_scalar_prefetch=1,
        grid=(B, H // hp, nb),
        in_specs=[
            pl.BlockSpec((1, 1, hp * dh, tq), lambda b, g, j, pt: (b, j, g, 0)),
            pl.BlockSpec((1, nb, hp, tq, 2 * dh), lambda b, g, j, pt: (b, 0, g, 0, 0)),
            pl.BlockSpec((1, nb, hp * _MOBA_V_ROWS, tq), lambda b, g, j, pt: (b, 0, g, 0)),
            pl.BlockSpec((1, hp, nb, dh), lambda b, g, j, pt: (b, g, 0, 0)),
            pl.BlockSpec(memory_space=pl.ANY),
            pl.BlockSpec(memory_space=pl.ANY),
        ],
        out_specs=[
            pl.BlockSpec((1, 1, hp * dh, tq), lambda b, g, j, pt: (b, j, g, 0)),
            pl.BlockSpec((nseq, H, V7X_LANES), lambda b, g, j, pt: (0, 0, 0)),
        ],
        scratch_shapes=[
            pltpu.VMEM((hp, nb, tq), F32), pltpu.VMEM((2, hp, 2 * dh, tq), BF16),
            pltpu.VMEM((hp, _MOBA_V_ROWS, tq), F32),
            pltpu.VMEM((_PAGE_RING, _PAGES_PER_CHUNK, H, dh, ps), F32),
            pltpu.VMEM((3, H, dh, ps), F32),
            pltpu.SemaphoreType.DMA((_PAGE_RING,)), pltpu.SemaphoreType.DMA((3,)),
        ],
    )
    return pl.pallas_call(
        _moba_kernel,
        grid_spec=grid_spec,
        out_shape=[jax.ShapeDtypeStruct((B, nb, ATT_WIDTH, tq), BF16),
                   jax.ShapeDtypeStruct((nseq, H, V7X_LANES), F32)],
        compiler_params=_params("arbitrary", "arbitrary", "arbitrary"),
        name="prompt_moba",
    )(page_table, qT, kh, vT, kmean, cache_kT, q_lanes)


def _log_sigmoid(x):
    return jnp.minimum(x, 0.0) - jnp.log1p(jnp.exp(-jnp.abs(x)))


def _causal_conv(x, tail_ref, w_ref, b_ref):
    L = x.shape[0]
    prev = tail_ref[...]
    row = lax.broadcasted_iota(jnp.int32, (V7X_SUBLANES, x.shape[1]), 0)
    acc = x * w_ref[M_CONV - 1:M_CONV, :] + b_ref[...]
    for d in range(1, M_CONV):
        r = pltpu.roll(x, d, axis=0)
        head = jnp.where(row < d, pltpu.roll(prev, d, axis=0), r[:V7X_SUBLANES])
        shifted = jnp.concatenate([head, r[V7X_SUBLANES:]], axis=0)
        acc += shifted * w_ref[M_CONV - 1 - d:M_CONV - d, :]
    tail_ref[...] = x[L - V7X_SUBLANES:, :]
    return acc


def _mlstm_kernel(qk_ref, v_ref, vT_ref, g_ref, gc_ref, o_ref, ng_ref,
                  h_ref, C_out, n_out, m_out,
                  C_s, n_s, m_s):
    c = pl.program_id(1)
    L = qk_ref.shape[1]
    MH, dk, dv = M_HEADS, M_QK_DIM, M_V_DIM
    nt = (((1,), (1,)), ((), ()))

    @pl.when(c == 0)
    def _():
        C_s[...] = jnp.zeros_like(C_s)
        n_s[...] = jnp.zeros_like(n_s)
        m_s[...] = jnp.zeros_like(m_s)

    t_idx = lax.broadcasted_iota(jnp.int32, (L, L), 0)
    s_idx = lax.broadcasted_iota(jnp.int32, (L, L), 1)
    causal = s_idx <= t_idx
    tri = jnp.where(causal, 1.0, 0.0).astype(BF16)
    g_rows = g_ref[0]
    b_rows = sum(lax.dot_general(part, tri, nt, preferred_element_type=F32)
                 for part in _split3(_log_sigmoid(g_rows)))
    b_cols = sum(jnp.dot(tri, part, preferred_element_type=F32)
                 for part in _split3(_log_sigmoid(gc_ref[0])))

    for hd in range(MH):
        qb = qk_ref[0, :, hd * dk:(hd + 1) * dk]
        kb = qk_ref[0, :, M_QK_WIDTH + hd * dk:M_QK_WIDTH + (hd + 1) * dk]
        vsl = slice(hd * dv, (hd + 1) * dv)
        b_row = b_rows[MH + hd:MH + hd + 1]
        c_row = g_rows[hd:hd + 1] - b_row
        b_col = b_cols[:, MH + hd:MH + hd + 1]
        m_prev = m_s[hd:hd + 1, 0:1]
        M_col = jnp.maximum(jnp.max(jnp.where(causal, c_row, NEG_SCORE), axis=1, keepdims=True),
                            m_prev)
        M_last = jnp.maximum(jnp.max(c_row, axis=1, keepdims=True), m_prev)
        w = jnp.where(causal, jnp.exp(c_row - M_col), 0.0)
        a_col = jnp.exp(m_prev - M_col)

        s = lax.dot_general(qb, kb, nt, preferred_element_type=F32) * w
        inter = lax.dot_general(qb, C_s[hd].astype(BF16), nt, preferred_element_type=F32)
        num = a_col * inter + jnp.dot(s.astype(BF16), v_ref[0, :, vsl], preferred_element_type=F32)
        n_b = n_s[hd:hd + 1, :].astype(BF16).astype(F32)
        qn = jnp.sum(qb.astype(F32) * n_b, axis=1, keepdims=True)
        den = a_col * qn + jnp.sum(s, axis=1, keepdims=True)
        hh = num / jnp.maximum(jnp.abs(den), jnp.exp(-(b_col + M_col)))
        hh = hh * lax.rsqrt(jnp.mean(hh * hh, axis=-1, keepdims=True) + NORM_EPS)
        gate = jax.nn.sigmoid(o_ref[0, :, vsl].astype(F32))
        h_ref[0, :, vsl] = (hh * ng_ref[:, vsl] * gate).astype(h_ref.dtype)

        ws_row = jnp.exp(c_row - M_last)
        a_end = jnp.exp(m_prev - M_last)
        vwT = (vT_ref[0, 0, vsl, :].astype(F32) * ws_row).astype(BF16)
        C_s[hd] = a_end * C_s[hd] + jnp.dot(vwT, kb, preferred_element_type=F32)
        ws8 = jnp.broadcast_to(ws_row, (V7X_SUBLANES, L)).astype(BF16)
        n_s[hd:hd + 1, :] = (a_end * n_s[hd:hd + 1, :]
                             + jnp.dot(ws8, kb, preferred_element_type=F32)[0:1])
        m_s[hd:hd + 1, :] = jnp.broadcast_to(b_row[:, L - 1:L] + M_last, (1, m_s.shape[1]))

    @pl.when(c == pl.num_programs(1) - 1)
    def _():
        C_out[0] = C_s[...]
        n_out[0] = n_s[...]
        m_out[0] = m_s[...]


def _mlstm_prompt(qk, mv, mvT, gT, gc, mo, m_norm_g):
    B, S, W = qk.shape
    L = mvT.shape[3]
    nc = S // L
    MH, dk, dv = M_HEADS, M_QK_DIM, M_V_DIM
    const = lambda b, c: (0, 0)
    tok = lambda w: pl.BlockSpec((1, L, w), lambda b, c: (b, c, 0))
    in_specs = [
        tok(W), tok(M_WIDTH),
        pl.BlockSpec((1, 1, M_WIDTH, L), lambda b, c: (b, c, 0, 0)),
        pl.BlockSpec((1, 2 * MH, L), lambda b, c: (b, 0, c)),
        tok(2 * MH), tok(M_WIDTH),
        pl.BlockSpec((1, M_WIDTH), const),
    ]
    out_shape = [
        jax.ShapeDtypeStruct((B, S, M_WIDTH), BF16),
        jax.ShapeDtypeStruct((B, MH, dv, dk), F32),
        jax.ShapeDtypeStruct((B, V7X_SUBLANES, dk), F32),
        jax.ShapeDtypeStruct((B, V7X_SUBLANES, V7X_LANES), F32),
    ]
    out_specs = [
        tok(M_WIDTH),
        pl.BlockSpec((1, MH, dv, dk), lambda b, c: (b, 0, 0, 0)),
        pl.BlockSpec((1, V7X_SUBLANES, dk), lambda b, c: (b, 0, 0)),
        pl.BlockSpec((1, V7X_SUBLANES, V7X_LANES), lambda b, c: (b, 0, 0)),
    ]
    scratch = [
        pltpu.VMEM((MH, dv, dk), F32), pltpu.VMEM((V7X_SUBLANES, dk), F32),
        pltpu.VMEM((V7X_SUBLANES, V7X_LANES), F32),
    ]
    return pl.pallas_call(
        _mlstm_kernel,
        grid=(B, nc),
        in_specs=in_specs,
        out_specs=out_specs,
        out_shape=out_shape,
        scratch_shapes=scratch,
        compiler_params=_params("arbitrary", "arbitrary"),
        name="prompt_mlstm",
    )(qk, mv, mvT, gT, gc, mo, m_norm_g.reshape(1, M_WIDTH))


def _merge_kernel(yaT_ref, az_ref, hm_ref, mz_ref, ga_ref, gm_ref, x_ref, mod_ref,
                  watt_ref, wml_ref, wout_ref, fg_ref, y_ref, watt_s, wml_s, wout_s):
    @pl.when((pl.program_id(0) == 0) & (pl.program_id(1) == 0))
    def _():
        watt_s[...] = watt_ref[...].astype(BF16)
        wml_s[...] = wml_ref[...].astype(BF16)
        wout_s[...] = wout_ref[...].astype(BF16)

    ya = yaT_ref[0, 0].astype(F32).T * _silu(az_ref[0].astype(F32))
    ya = jnp.dot(ya.astype(BF16), watt_s[...], preferred_element_type=F32)
    ym = hm_ref[0].astype(F32) * _silu(mz_ref[0].astype(F32))
    ym = jnp.dot(ym.astype(BF16), wml_s[...], preferred_element_type=F32)
    u = (jax.nn.sigmoid(ga_ref[0].astype(F32)) * ya
         + jax.nn.sigmoid(gm_ref[0].astype(F32)) * ym)
    upd = jnp.dot(u.astype(BF16), wout_s[...], preferred_element_type=F32)
    y = x_ref[0] + mod_ref[0, 2:3, :] * upd
    y = y * lax.rsqrt(jnp.mean(y * y, axis=-1, keepdims=True) + NORM_EPS)
    y_ref[0] = y * fg_ref[...]


def _merge_prompt(yaT, az, hm, mz, ga, gm, x, mod3, w_att, w_mlstm, w_out, final_g):
    B, S, D = x.shape
    tm = yaT.shape[3]
    nb = S // tm
    const = lambda b, j: (0, 0)
    tok = lambda w: pl.BlockSpec((1, tm, w), lambda b, j: (b, j, 0))
    return pl.pallas_call(
        _merge_kernel,
        grid=(B, nb),
        in_specs=[
            pl.BlockSpec((1, 1, ATT_WIDTH, tm), lambda b, j: (b, j, 0, 0)),
            tok(ATT_WIDTH), tok(M_WIDTH), tok(M_WIDTH), tok(D), tok(D), tok(D),
            pl.BlockSpec((1, 3, D), lambda b, j: (b, 0, 0)),
            pl.BlockSpec(w_att.shape, const, pipeline_mode=pl.Buffered(1)),
            pl.BlockSpec(w_mlstm.shape, const, pipeline_mode=pl.Buffered(1)),
            pl.BlockSpec(w_out.shape, const, pipeline_mode=pl.Buffered(1)),
            pl.BlockSpec((1, D), const),
        ],
        out_specs=tok(D),
        out_shape=jax.ShapeDtypeStruct((B, S, D), F32),
        scratch_shapes=[pltpu.VMEM(w_att.shape, BF16), pltpu.VMEM(w_mlstm.shape, BF16),
                        pltpu.VMEM(w_out.shape, BF16)],
        compiler_params=_params("arbitrary", "arbitrary"),
        name="prompt_merge",
    )(yaT, az, hm, mz, ga, gm, x, mod3, w_att, w_mlstm, w_out, final_g.reshape(1, D))


def _prompt_layer(x, mod3, norm_g, w_main, w_ifT, b_if, w_conv, b_conv, m_norm_g,
                  w_att, w_mlstm, w_out, final_g, page_table, cache_kT, q_lanes):
    B, S, _ = x.shape
    (qT, kh, vT, kf, vf, kmean, az, qk, mv, mo, mz, ga, gm, gT, mvT, gc, ctail) = _inproj(
        x, mod3, norm_g, w_main, w_ifT, b_if, w_conv, b_conv)
    yaT, block_scores = _moba_prompt(qT, kh, vT, kmean, page_table, cache_kT, q_lanes)
    hm, C, n, m = _mlstm_prompt(qk, mv, mvT, gT, gc, mo, m_norm_g)
    y = _merge_prompt(yaT, az, hm, mz, ga, gm, x, mod3, w_att, w_mlstm, w_out, final_g)
    k_rows = kf.reshape(B, S, ATT_HEADS, ATT_HEAD_DIM)
    v_rows = vf.reshape(B, S, ATT_HEADS, ATT_HEAD_DIM)
    conv_new = ctail[:, V7X_SUBLANES - (M_CONV - 1):, :]
    return y, k_rows, v_rows, C, n[:, :M_HEADS, :], m[:, :M_HEADS, 0], conv_new, block_scores


def _inproj_sample_kernel(x_ref, mod_ref, ng_ref, w_ref, wif_ref, bif_ref, z_ref, g_ref):
    x = x_ref[...]
    d = x.shape[1]
    xn = x * lax.rsqrt(jnp.mean(x * x, axis=-1, keepdims=True) + NORM_EPS) * ng_ref[...]
    h = (xn * (1.0 + mod_ref[:, d:2 * d]) + mod_ref[:, 0:d]).astype(BF16)
    z_ref[...] = jnp.dot(h, w_ref[...], preferred_element_type=F32)
    g = lax.dot_general(h, wif_ref[...], (((1,), (1,)), ((), ())), preferred_element_type=F32)
    g_ref[...] = g + bif_ref[...]


def _inproj_sample(x, mod, norm_g, w_main, w_ifT, b_if):
    n, d = x.shape
    bn = 13 * V7X_LANES
    assert w_main.shape[1] % bn == 0
    const = lambda i: (0, 0)
    return pl.pallas_call(
        _inproj_sample_kernel,
        grid=(w_main.shape[1] // bn,),
        in_specs=[pl.BlockSpec((n, d), const),
                  pl.BlockSpec(mod.shape, const),
                  pl.BlockSpec((1, d), const),
                  pl.BlockSpec((d, bn), lambda i: (0, i)),
                  pl.BlockSpec(w_ifT.shape, const),
                  pl.BlockSpec((1, 2 * M_HEADS), const)],
        out_specs=[pl.BlockSpec((n, bn), lambda i: (0, i)),
                   pl.BlockSpec((n, 2 * M_HEADS), const)],
        out_shape=[jax.ShapeDtypeStruct((n, w_main.shape[1]), F32),
                   jax.ShapeDtypeStruct((n, 2 * M_HEADS), F32)],
        compiler_params=_params("arbitrary"),
        name="sample_inproj",
    )(x, mod, norm_g.reshape(1, d), w_main, w_ifT, b_if.reshape(1, 2 * M_HEADS))


_PAGES_PER_CHUNK = 8
_PAGE_RING = 5
_PAGES_PER_BLOCK = MOBA_BLOCK // PAGE_SIZE
_SAMPLE_SEQS_PER_STEP = 1


def _attend_sample_kernel(pt_ref, sel_ref, ck_ref, cv_ref, q_ref, kn_ref, vn_ref, o_ref,
                          kbuf, vbuf, sem):
    b = pl.program_id(0)
    nseq = pl.num_programs(0)
    H, dh = ATT_HEADS, ATT_HEAD_DIM
    npg = MOBA_TOPK * _PAGES_PER_BLOCK

    def copies(bb, slot):
        out = []
        for hh in range(H):
            for r in range(MOBA_TOPK):
                blk = sel_ref[(bb * H + hh) * MOBA_TOPK + r]
                for i in range(_PAGES_PER_BLOCK):
                    page = pt_ref[bb, blk * _PAGES_PER_BLOCK + i]
                    pg = r * _PAGES_PER_BLOCK + i
                    out.append(pltpu.make_async_copy(
                        ck_ref.at[0, page, hh], kbuf.at[slot, hh, pg], sem.at[0, slot]))
                    out.append(pltpu.make_async_copy(
                        cv_ref.at[0, page, hh], vbuf.at[slot, hh, pg], sem.at[1, slot]))
        return out

    @pl.when(b == 0)
    def _():
        for cp in copies(0, 0):
            cp.start()

    slot = b % 2

    @pl.when(b + 1 < nseq)
    def _():
        for cp in copies(b + 1, 1 - slot):
            cp.start()

    for cp in copies(b, slot):
        cp.wait()

    scale = ATT_HEAD_DIM ** -0.5
    nt = (((1,), (1,)), ((), ()))
    q8s = [jnp.broadcast_to(q_ref[0, hh:hh + 1, :], (V7X_SUBLANES, dh)).astype(BF16)
           for hh in range(H)]
    ss = [jnp.concatenate(
        [jnp.dot(q8s[hh], kbuf[slot, hh, pg].astype(BF16), preferred_element_type=F32)[0:1]
         for pg in range(npg)], axis=1) * scale for hh in range(H)]
    p8s, p_news, ls = [], [], []
    for hh in range(H):
        kn = kn_ref[0, hh:hh + 1, :].astype(BF16).astype(F32)
        s_new = jnp.sum(q8s[hh][0:1].astype(F32) * kn, axis=1, keepdims=True) * scale
        m = jnp.maximum(jnp.max(ss[hh], axis=1, keepdims=True), s_new)
        p = jnp.exp(ss[hh] - m)
        p_new = jnp.exp(s_new - m)
        ls.append(jnp.sum(p, axis=1, keepdims=True) + p_new)
        p_news.append(p_new)
        p8s.append(jnp.broadcast_to(p, (V7X_SUBLANES, p.shape[1])).astype(BF16))
    pvs = [[lax.dot_general(p8s[hh][:, pg * PAGE_SIZE:(pg + 1) * PAGE_SIZE],
                            vbuf[slot, hh, pg].astype(BF16), nt, preferred_element_type=F32)[0:1]
            for pg in range(npg)] for hh in range(H)]
    for hh in range(H):
        vn = vn_ref[0, hh:hh + 1, :].astype(BF16).astype(F32)
        acc = p_news[hh].astype(BF16).astype(F32) * vn
        for x in pvs[hh]:
            acc = acc + x
        o_ref[0, :, hh * dh:(hh + 1) * dh] = acc / ls[hh]


def _attend_sample(cache_k, cache_v, page_table, sel_flat, q, k_new, v_new):
    nseq, n_pages = page_table.shape
    H, dh = ATT_HEADS, ATT_HEAD_DIM
    npg = MOBA_TOPK * _PAGES_PER_BLOCK
    ps = cache_k.shape[-1]
    grid_spec = pltpu.PrefetchScalarGridSpec(
        num_scalar_prefetch=2,
        grid=(nseq,),
        in_specs=[pl.BlockSpec(memory_space=pl.ANY),
                  pl.BlockSpec(memory_space=pl.ANY),
                  pl.BlockSpec((1, H, dh), lambda b, pt, sel: (b, 0, 0)),
                  pl.BlockSpec((1, H, dh), lambda b, pt, sel: (b, 0, 0)),
                  pl.BlockSpec((1, H, dh), lambda b, pt, sel: (b, 0, 0))],
        out_specs=pl.BlockSpec((1, 1, ATT_WIDTH), lambda b, pt, sel: (b, 0, 0)),
        scratch_shapes=[pltpu.VMEM((2, H, npg, dh, ps), F32),
                        pltpu.VMEM((2, H, npg, dh, ps), F32),
                        pltpu.SemaphoreType.DMA((2, 2))],
    )
    return pl.pallas_call(
        _attend_sample_kernel,
        grid_spec=grid_spec,
        out_shape=jax.ShapeDtypeStruct((nseq, 1, ATT_WIDTH), F32),
        compiler_params=_params("arbitrary"),
        name="sample_attend",
    )(page_table, sel_flat, cache_k, cache_v, q, k_new, v_new)


def _mlstm_sample_kernel(qk_ref, v_ref, o_ref, g_ref, conv_ref, C_ref, n_ref, wc_ref, bc_ref, ng_ref,
                         h_ref, C_out, n_out, m_out, conv_out):
    for sq in range(qk_ref.shape[0]):
        x = qk_ref[sq]
        conv = conv_ref[sq]
        acc = x * wc_ref[M_CONV - 1:M_CONV, :] + bc_ref[...]
        for jj in range(M_CONV - 1):
            acc += conv[jj:jj + 1, :] * wc_ref[jj:jj + 1, :]
        qk = _silu(acc)
        conv_out[sq] = jnp.concatenate([conv[1:], x], axis=0)
        g = g_ref[sq]
        m_new = []
        for hh in range(M_HEADS):
            q = qk[:, hh * M_QK_DIM:(hh + 1) * M_QK_DIM]
            k = qk[:, M_QK_WIDTH + hh * M_QK_DIM:M_QK_WIDTH + (hh + 1) * M_QK_DIM] * (M_QK_DIM ** -0.5)
            qb = q.astype(BF16)
            kb = k.astype(BF16)
            v = v_ref[sq][:, hh * M_V_DIM:(hh + 1) * M_V_DIM]
            i_pre = g[:, hh:hh + 1]
            f_pre = g[:, M_HEADS + hh:M_HEADS + hh + 1]
            m_prev = g[:, 2 * M_HEADS + hh:2 * M_HEADS + hh + 1]
            bb = _log_sigmoid(f_pre)
            cc = i_pre - bb
            M = jnp.maximum(m_prev, cc)
            w = jnp.exp(cc - M)
            a = jnp.exp(m_prev - M)
            C = C_ref[sq, hh]
            n = n_ref[sq, hh:hh + 1, :]
            qf = qb.astype(F32)
            s = jnp.sum(qf * kb.astype(F32), axis=1, keepdims=True) * w
            q8 = jnp.broadcast_to(qb, (V7X_SUBLANES, M_QK_DIM))
            inter = lax.dot_general(q8, C.astype(BF16), (((1,), (1,)), ((), ())),
                                    preferred_element_type=F32)[0:1]
            vb = v.astype(BF16).astype(F32)
            num = a * inter + s.astype(BF16).astype(F32) * vb
            den = a * jnp.sum(qf * n.astype(BF16).astype(F32), axis=1, keepdims=True) + s
            hv = num / jnp.maximum(jnp.abs(den), jnp.exp(-(bb + M)))
            hv = hv * lax.rsqrt(jnp.mean(hv * hv, axis=-1, keepdims=True) + NORM_EPS)
            sl = slice(hh * M_V_DIM, (hh + 1) * M_V_DIM)
            h_ref[sq, :, sl] = hv * ng_ref[:, sl] * jax.nn.sigmoid(o_ref[sq][:, sl])
            wv = jnp.broadcast_to(w * v, (V7X_SUBLANES, M_V_DIM))
            wv = jnp.concatenate([wv, jnp.zeros((V7X_LANES - V7X_SUBLANES, M_V_DIM), F32)], axis=0)
            wv_col = wv.T[:, 0:1]
            C_out[sq, hh] = a * C + wv_col * k
            n_out[sq, hh:hh + 1, :] = a * n + w * k
            m_new.append(bb + M)
        lane = lax.broadcasted_iota(jnp.int32, (1, V7X_LANES), 1)
        mrow = jnp.zeros((1, V7X_LANES), F32)
        for hh in range(M_HEADS):
            mrow = jnp.where(lane == hh, m_new[hh], mrow)
        m_out[sq] = mrow


def _mlstm_sample(z3, g16, state_conv, state_C, state_n, w_conv, b_conv, m_norm_g):
    nseq = z3.shape[0]
    MH, dk, dv = M_HEADS, M_QK_DIM, M_V_DIM
    W = 2 * M_QK_WIDTH
    sb = _SAMPLE_SEQS_PER_STEP
    assert nseq % sb == 0
    const = lambda b: (0, 0)
    return pl.pallas_call(
        _mlstm_sample_kernel,
        grid=(nseq // sb,),
        in_specs=[
            pl.BlockSpec((sb, 1, W), lambda b: (b, 0, _C_MQK // W)),
            pl.BlockSpec((sb, 1, M_WIDTH), lambda b: (b, 0, _C_MV // M_WIDTH)),
            pl.BlockSpec((sb, 1, M_WIDTH), lambda b: (b, 0, _C_MO // M_WIDTH)),
            pl.BlockSpec((sb, 1, 4 * MH), lambda b: (b, 0, 0)),
            pl.BlockSpec((sb, M_CONV - 1, W), lambda b: (b, 0, 0)),
            pl.BlockSpec((sb, MH, dv, dk), lambda b: (b, 0, 0, 0)),
            pl.BlockSpec((sb, MH, dk), lambda b: (b, 0, 0)),
            pl.BlockSpec((M_CONV, W), const),
            pl.BlockSpec((1, W), const),
            pl.BlockSpec((1, M_WIDTH), const),
        ],
        out_specs=[
            pl.BlockSpec((sb, 1, M_WIDTH), lambda b: (b, 0, 0)),
            pl.BlockSpec((sb, MH, dv, dk), lambda b: (b, 0, 0, 0)),
            pl.BlockSpec((sb, MH, dk), lambda b: (b, 0, 0)),
            pl.BlockSpec((sb, 1, V7X_LANES), lambda b: (b, 0, 0)),
            pl.BlockSpec((sb, M_CONV - 1, W), lambda b: (b, 0, 0)),
        ],
        out_shape=[
            jax.ShapeDtypeStruct((nseq, 1, M_WIDTH), F32),
            jax.ShapeDtypeStruct((nseq, MH, dv, dk), F32),
            jax.ShapeDtypeStruct((nseq, MH, dk), F32),
            jax.ShapeDtypeStruct((nseq, 1, V7X_LANES), F32),
            jax.ShapeDtypeStruct((nseq, M_CONV - 1, W), F32),
        ],
        compiler_params=_params("arbitrary"),
        name="sample_mlstm",
    )(z3, z3, z3, g16, state_conv, state_C, state_n, w_conv, b_conv.reshape(1, W),
      m_norm_g.reshape(1, M_WIDTH))


def _merge_sample_kernel(ya_ref, hm_ref, z_ref, x_ref, mod_ref, watt_ref, wml_ref, wout_ref, fg_ref,
                         y_ref):
    d = x_ref.shape[1]
    ya = ya_ref[...] * _silu(z_ref[:, _C_AZ:_C_MQK])
    ya = jnp.dot(ya.astype(BF16), watt_ref[...].astype(BF16), preferred_element_type=F32)
    ym = hm_ref[...] * _silu(z_ref[:, _C_MZ:_C_MIF])
    ym = jnp.dot(ym.astype(BF16), wml_ref[...].astype(BF16), preferred_element_type=F32)
    tail = z_ref[:, _C_MIF:]
    ga = tail[:, _C_GA - _C_MIF:_C_GM - _C_MIF]
    gm = tail[:, _C_GM - _C_MIF:_C_END - _C_MIF]
    u = jax.nn.sigmoid(ga) * ya + jax.nn.sigmoid(gm) * ym
    upd = jnp.dot(u.astype(BF16), wout_ref[...].astype(BF16), preferred_element_type=F32)
    y = x_ref[...] + mod_ref[:, 2 * d:3 * d] * upd
    y = y * lax.rsqrt(jnp.mean(y * y, axis=-1, keepdims=True) + NORM_EPS)
    y_ref[...] = y * fg_ref[...]


def _merge_sample(ya, hm, z, x, mod, w_att, w_mlstm, w_out, final_g):
    n, d = x.shape
    full = lambda a: pl.BlockSpec(a.shape, lambda i: (0,) * a.ndim)
    fg = final_g.reshape(1, d)
    args = (ya, hm, z, x, mod, w_att, w_mlstm, w_out, fg)
    return pl.pallas_call(
        _merge_sample_kernel,
        grid=(1,),
        in_specs=[full(a) for a in args],
        out_specs=pl.BlockSpec((n, d), lambda i: (0, 0)),
        out_shape=jax.ShapeDtypeStruct((n, d), F32),
        compiler_params=_params("arbitrary"),
        name="sample_merge",
    )(*args)


def _gate_scores_kernel(sc_ref, idx_ref, *, nblk):
    sc = sc_ref[...] / MOBA_BLOCK
    lane = lax.broadcasted_iota(jnp.int32, sc.shape, 1)
    cnt = jnp.zeros(sc.shape, jnp.int32)
    for m in range(nblk):
        col = sc[:, m:m + 1]
        beats = (col > sc) | ((col == sc) & (lane > m))
        cnt += beats.astype(jnp.int32)
    out = jnp.zeros(sc.shape, jnp.int32)
    for r in range(MOBA_TOPK):
        idx = jnp.sum(jnp.where((cnt == r) & (lane < nblk), lane, 0), axis=1, keepdims=True)
        out = jnp.where(lane == r, idx, out)
    idx_ref[...] = out


def _gate_scores(block_scores, nblk):
    nseq, H, lanes = block_scores.shape
    rows = nseq * H
    out = pl.pallas_call(
        functools.partial(_gate_scores_kernel, nblk=nblk),
        grid=(1,),
        in_specs=[pl.BlockSpec((rows, lanes), lambda i: (0, 0))],
        out_specs=pl.BlockSpec((rows, lanes), lambda i: (0, 0)),
        out_shape=jax.ShapeDtypeStruct((rows, lanes), jnp.int32),
        compiler_params=_params("arbitrary"),
        name="sample_gate",
    )(block_scores.reshape(rows, lanes))
    return out.reshape(nseq, H, lanes)


def _sample_inputs(x, mod, norm_g, w_main, w_ifT, b_if):
    nseq = x.shape[0]
    H, dh = ATT_HEADS, ATT_HEAD_DIM
    z, g = _inproj_sample(x, mod, norm_g, w_main, w_ifT, b_if)
    q = z[:, _C_AQ:_C_AK].reshape(nseq, H, dh)
    q_lanes = jnp.broadcast_to(q[..., None], (nseq, H, dh, PAGE_SIZE))
    return z, g, q, q_lanes


def _sample_layer(x, mod, z, g, q, block_scores, cache_kT, cache_vT, page_table,
                  state_C, state_n, state_m, state_conv,
                  w_conv, b_conv, m_norm_g, w_att, w_mlstm, w_out, final_g):
    nseq = x.shape[0]
    H, dh = ATT_HEADS, ATT_HEAD_DIM
    nblk = page_table.shape[1] // _PAGES_PER_BLOCK
    assert nblk >= MOBA_TOPK
    k_new = z[:, _C_AK:_C_AV].reshape(nseq, H, dh)
    v_new = z[:, _C_AV:_C_AZ].reshape(nseq, H, dh)
    sel = _gate_scores(block_scores, nblk)
    sel_flat = sel[:, :, :MOBA_TOPK].reshape(-1)
    ya = _attend_sample(cache_kT, cache_vT, page_table, sel_flat, q, k_new, v_new)
    g16 = jnp.concatenate([g, state_m, jnp.zeros_like(state_m)], axis=1).reshape(nseq, 1, 4 * M_HEADS)
    hm, C, n, m, conv_new = _mlstm_sample(z.reshape(nseq, 1, -1), g16, state_conv, state_C, state_n,
                                          w_conv, b_conv, m_norm_g)
    y = _merge_sample(ya.reshape(nseq, ATT_WIDTH), hm.reshape(nseq, M_WIDTH), z, x, mod,
                      w_att, w_mlstm, w_out, final_g)
    return (y.reshape(nseq, 1, -1), k_new.reshape(nseq, 1, H, dh), v_new.reshape(nseq, 1, H, dh),
            C, n, m[:, 0, :M_HEADS], conv_new)


def kernel(x_prompt, x_sample, cache_k, cache_v, state_C, state_n, state_m, state_conv, page_table,
           c_prompt, c_sample, w_ada, b_ada, norm_g, w_in, b_if, w_conv, b_conv, m_norm_g,
           w_att, w_mlstm, w_out, final_g):
    depth = w_in.shape[0]
    assert depth == 1, "single-layer step"
    B = x_prompt.shape[0]
    nseq = x_sample.shape[0]
    pad = (-(B + nseq)) % V7X_SUBLANES
    c_all = jnp.concatenate([c_prompt, c_sample, jnp.zeros((pad, c_prompt.shape[1]), F32)], axis=0)
    mod = _ada(c_all, w_ada[0], b_ada[0])
    mod_p = mod[:B].reshape(B, 3, D_MODEL)
    mod_s = mod[B:B + nseq]
    w_main, w_ifT = _pack_w_in(w_in[0])
    wa, wm, wo = w_att[0], w_mlstm[0], w_out[0]
    cache_kT = jnp.transpose(cache_k, (0, 1, 3, 4, 2))
    cache_vT = jnp.transpose(cache_v, (0, 1, 3, 4, 2))
    xs = x_sample[:, 0, :]
    z, g, q, q_lanes = _sample_inputs(xs, mod_s, norm_g[0], w_main, w_ifT, b_if[0])
    *outs_p, block_scores = _prompt_layer(
        x_prompt, mod_p, norm_g[0], w_main, w_ifT, b_if[0], w_conv[0], b_conv[0], m_norm_g[0],
        wa, wm, wo, final_g, page_table, cache_kT, q_lanes)
    outs_s = _sample_layer(xs, mod_s, z, g, q, block_scores, cache_kT, cache_vT, page_table,
                           state_C[0], state_n[0], state_m[0], state_conv[0],
                           w_conv[0], b_conv[0], m_norm_g[0], wa, wm, wo, final_g)
    y_p, *st_p = outs_p
    y_s, *st_s = outs_s
    return (y_p, y_s) + tuple(a[None] for a in st_p) + tuple(a[None] for a in st_s)
```

```python
import functools

import jax
import jax.numpy as jnp
from jax import lax
from jax.experimental import pallas as pl
from jax.experimental.pallas import tpu as pltpu

F32 = jnp.float32
BF16 = jnp.bfloat16

D_MODEL = 1024
ATT_HEADS = 8
ATT_HEAD_DIM = 64
ATT_WIDTH = ATT_HEADS * ATT_HEAD_DIM
MOBA_BLOCK = 256
MOBA_TOPK = 3
M_HEADS = 4
M_V_DIM = 256
M_QK_DIM = 128
M_QK_WIDTH = M_HEADS * M_QK_DIM
M_WIDTH = M_HEADS * M_V_DIM
M_CONV = 4
PAGE_SIZE = 128
NORM_EPS = 1e-6
NEG_SCORE = -1e30
_LOG2E = 1.4426950408889634

V7X_LANES = 128
V7X_SUBLANES = 8
V7X_VMEM_LIMIT_BYTES = 56 * 1024 * 1024

_C_AQ, _C_AK, _C_AV, _C_AZ = 0, 512, 1024, 1536
_C_MQK, _C_MV, _C_MO, _C_MZ = 2048, 3072, 4096, 5120
_C_MIF, _C_GA, _C_GM, _C_END = 6144, 6152, 7176, 8200


def _silu(x):
    return x * jax.nn.sigmoid(x)


def _split3(x):
    hi = x.astype(BF16)
    r1 = x - hi.astype(F32)
    mid = r1.astype(BF16)
    lo = (r1 - mid.astype(F32)).astype(BF16)
    return hi, mid, lo


def _pack_w_in(w_in):
    pad = (-w_in.shape[1]) % V7X_LANES
    w_main = jnp.concatenate([w_in.astype(BF16), jnp.zeros((w_in.shape[0], pad), BF16)], axis=1)
    w_ifT = w_main[:, _C_MIF:_C_GA].T
    return w_main, w_ifT


def _params(*sem):
    return pltpu.CompilerParams(dimension_semantics=sem, vmem_limit_bytes=V7X_VMEM_LIMIT_BYTES)


def _ada_kernel(c_ref, w_ref, b_ref, o_ref):
    a = _silu(c_ref[...]).astype(BF16)
    acc = jnp.dot(a, w_ref[...].astype(BF16), preferred_element_type=F32)
    o_ref[...] = acc + b_ref[...]


def _ada(c, w_ada, b_ada):
    n, d = c.shape
    d3 = w_ada.shape[1]
    bn = 512
    return pl.pallas_call(
        _ada_kernel,
        grid=(d3 // bn,),
        in_specs=[pl.BlockSpec((n, d), lambda i: (0, 0)),
                  pl.BlockSpec((d, bn), lambda i: (0, i)),
                  pl.BlockSpec((1, bn), lambda i: (0, i))],
        out_specs=pl.BlockSpec((n, bn), lambda i: (0, i)),
        out_shape=jax.ShapeDtypeStruct((n, d3), F32),
        compiler_params=_params("arbitrary"),
        name="ada_mod",
    )(c, w_ada, b_ada.reshape(1, d3))


def _inproj_kernel(x_ref, mod_ref, ng_ref, w_ref, wif_ref, bif_ref, bifn_ref, wc_ref, bc_ref,
                   qT_ref, kh_ref, vT_ref, kf_ref, vf_ref, kmean_ref,
                   az_ref, qk_ref, mv_ref, mo_ref, mz_ref, ga_ref, gm_ref, gT_ref, mvT_ref, gc_ref,
                   ctail_ref, tail_s):
    j = pl.program_id(1)
    x = x_ref[0]
    xn = x * lax.rsqrt(jnp.mean(x * x, axis=-1, keepdims=True) + NORM_EPS) * ng_ref[...]
    shift = mod_ref[0, 0:1, :]
    scale = mod_ref[0, 1:2, :]
    h = (xn * (1.0 + scale) + shift).astype(BF16)

    def proj(a, b):
        return jnp.dot(h, w_ref[:, a:b], preferred_element_type=F32)

    @pl.when(j == 0)
    def _():
        tail_s[...] = jnp.zeros_like(tail_s)

    mqk = proj(_C_MQK, _C_MV)
    ctail_ref[0] = mqk[mqk.shape[0] - V7X_SUBLANES:, :]
    qk = _silu(_causal_conv(mqk, tail_s, wc_ref, bc_ref))
    qk_ref[0, :, 0:M_QK_WIDTH] = qk[:, 0:M_QK_WIDTH].astype(BF16)
    qk_ref[0, :, M_QK_WIDTH:] = (qk[:, M_QK_WIDTH:] * (M_QK_DIM ** -0.5)).astype(BF16)

    mo_ref[0] = proj(_C_MO, _C_MZ).astype(BF16)

    q = proj(_C_AQ, _C_AK) * (ATT_HEAD_DIM ** -0.5 * _LOG2E)
    qT_ref[0, 0] = q.T.astype(BF16)
    mz_ref[0] = proj(_C_MZ, _C_MIF).astype(BF16)

    dh = ATT_HEAD_DIM
    tm = x.shape[0]
    one_col = jnp.where(lax.broadcasted_iota(jnp.int32, (tm, dh), 1) == 0, 1.0, 0.0)
    one_row = jnp.where(lax.broadcasted_iota(jnp.int32, (_MOBA_V_PAD, tm), 0) == 0, 1.0, 0.0)
    k = proj(_C_AK, _C_AV)
    kf_ref[0] = k
    ksum = jnp.sum(k, axis=0, keepdims=True) / MOBA_BLOCK
    for hh in range(ATT_HEADS):
        sl = slice(hh * dh, (hh + 1) * dh)
        kh_ref[0, 0, hh] = jnp.concatenate([k[:, sl], one_col], axis=1).astype(BF16)
        kmean_ref[0, hh, pl.ds(j, 1), :] = ksum[:, sl]
    az_ref[0] = proj(_C_AZ, _C_MQK).astype(BF16)

    v = proj(_C_AV, _C_AZ)
    vf_ref[0] = v
    vT = v.T
    for hh in range(ATT_HEADS):
        r0 = hh * _MOBA_V_ROWS
        vT_ref[0, 0, r0:r0 + dh, :] = vT[hh * dh:(hh + 1) * dh].astype(BF16)
        vT_ref[0, 0, r0 + dh:r0 + _MOBA_V_ROWS, :] = one_row.astype(BF16)

    tail = proj(_C_MIF, w_ref.shape[1])
    gc_ref[0] = tail[:, 0:_C_GA - _C_MIF] + bifn_ref[...]
    ga_ref[0] = tail[:, _C_GA - _C_MIF:_C_GM - _C_MIF].astype(BF16)
    gm_ref[0] = tail[:, _C_GM - _C_MIF:_C_END - _C_MIF].astype(BF16)
    gT = lax.dot_general(wif_ref[...], h, (((1,), (1,)), ((), ())), preferred_element_type=F32)
    gT_ref[0] = gT + bif_ref[...]

    mv = proj(_C_MV, _C_MO)
    mv_ref[0] = mv.astype(BF16)
    mvT_ref[0, 0] = mv.T.astype(BF16)


def _inproj(x, mod3, norm_g, w_main, w_ifT, b_if, w_conv, b_conv):
    B, S, D = x.shape
    W = 2 * M_QK_WIDTH
    tm = MOBA_BLOCK
    nb = S // tm
    H, dh = ATT_HEADS, ATT_HEAD_DIM
    const = lambda b, j: (0, 0)
    tok = lambda w: pl.BlockSpec((1, tm, w), lambda b, j: (b, j, 0))
    blk = pl.BlockSpec((1, 1, ATT_WIDTH, tm), lambda b, j: (b, j, 0, 0))
    out_shape = [
        jax.ShapeDtypeStruct((B, nb, ATT_WIDTH, tm), BF16),
        jax.ShapeDtypeStruct((B, nb, H, tm, 2 * dh), BF16),
        jax.ShapeDtypeStruct((B, nb, H * _MOBA_V_ROWS, tm), BF16),
        jax.ShapeDtypeStruct((B, S, ATT_WIDTH), F32),
        jax.ShapeDtypeStruct((B, S, ATT_WIDTH), F32),
        jax.ShapeDtypeStruct((B, H, nb, dh), F32),
        jax.ShapeDtypeStruct((B, S, ATT_WIDTH), BF16),
        jax.ShapeDtypeStruct((B, S, W), BF16),
        jax.ShapeDtypeStruct((B, S, M_WIDTH), BF16),
        jax.ShapeDtypeStruct((B, S, M_WIDTH), BF16),
        jax.ShapeDtypeStruct((B, S, M_WIDTH), BF16),
        jax.ShapeDtypeStruct((B, S, D), BF16),
        jax.ShapeDtypeStruct((B, S, D), BF16),
        jax.ShapeDtypeStruct((B, 2 * M_HEADS, S), F32),
        jax.ShapeDtypeStruct((B, nb, M_WIDTH, tm), BF16),
        jax.ShapeDtypeStruct((B, S, 2 * M_HEADS), F32),
        jax.ShapeDtypeStruct((B, V7X_SUBLANES, W), F32),
    ]
    out_specs = [
        blk,
        pl.BlockSpec((1, 1, H, tm, 2 * dh), lambda b, j: (b, j, 0, 0, 0)),
        pl.BlockSpec((1, 1, H * _MOBA_V_ROWS, tm), lambda b, j: (b, j, 0, 0)),
        tok(ATT_WIDTH), tok(ATT_WIDTH),
        pl.BlockSpec((1, H, nb, dh), lambda b, j: (b, 0, 0, 0)),
        tok(ATT_WIDTH), tok(2 * M_QK_WIDTH), tok(M_WIDTH), tok(M_WIDTH), tok(M_WIDTH), tok(D), tok(D),
        pl.BlockSpec((1, 2 * M_HEADS, tm), lambda b, j: (b, 0, j)),
        pl.BlockSpec((1, 1, M_WIDTH, tm), lambda b, j: (b, j, 0, 0)),
        tok(2 * M_HEADS),
        pl.BlockSpec((1, V7X_SUBLANES, W), lambda b, j: (b, 0, 0)),
    ]
    in_specs = [
        pl.BlockSpec((1, tm, D), lambda b, j: (b, j, 0)),
        pl.BlockSpec((1, 3, D), lambda b, j: (b, 0, 0)),
        pl.BlockSpec((1, D), const),
        pl.BlockSpec(w_main.shape, const, pipeline_mode=pl.Buffered(1)),
        pl.BlockSpec(w_ifT.shape, const),
        pl.BlockSpec((2 * M_HEADS, 1), const),
        pl.BlockSpec((1, 2 * M_HEADS), const),
        pl.BlockSpec((M_CONV, W), const),
        pl.BlockSpec((1, W), const),
    ]
    return pl.pallas_call(
        _inproj_kernel,
        grid=(B, nb),
        in_specs=in_specs,
        out_specs=out_specs,
        out_shape=out_shape,
        scratch_shapes=[pltpu.VMEM((V7X_SUBLANES, W), F32)],
        compiler_params=_params("arbitrary", "arbitrary"),
        name="prompt_inproj",
    )(x, mod3, norm_g.reshape(1, D), w_main, w_ifT, b_if.reshape(2 * M_HEADS, 1),
      b_if.reshape(1, 2 * M_HEADS), w_conv, b_conv.reshape(1, W))


_MOBA_HEADS_PER_STEP = 8
_MOBA_HEAD_GROUP = 8
_MOBA_V_PAD = 16
_MOBA_V_ROWS = ATT_HEAD_DIM + _MOBA_V_PAD


def _moba_phases(j):
    return j + 2 * ((j * j) // 4)


def _moba_kernel(pt_ref, qT_ref, kh_ref, vT_ref, kmean_ref, ck_ref, qb_ref, o_ref, sc_ref,
                 bias_ref, qs_ref, acc_ref, pbuf, qbuf, psem, qsem):
    b = pl.program_id(0)
    j = pl.program_id(2)
    nbq = pl.num_programs(2)
    nseq, n_pages = pt_ref.shape
    cpb = n_pages // _PAGES_PER_CHUNK
    bpc = _PAGES_PER_CHUNK // _PAGES_PER_BLOCK
    n_chunks = nseq * cpb
    ring = pbuf.shape[0]
    ahead = ring - 2

    def chunk_copy(g, i):
        page = pt_ref[g // cpb, (g % cpb) * _PAGES_PER_CHUNK + i]
        slot = g % ring
        return pltpu.make_async_copy(ck_ref.at[0, page], pbuf.at[slot, i], psem.at[slot])

    qslots = qbuf.shape[0]

    def q_copy(seq):
        s = seq % qslots
        return pltpu.make_async_copy(qb_ref.at[seq], qbuf.at[s], qsem.at[s])

    def start_chunk(g):
        @pl.when(g < n_chunks)
        def _():
            for i in range(_PAGES_PER_CHUNK):
                chunk_copy(g, i).start()

    step = b * pl.num_programs(1) + pl.program_id(1)

    @pl.when((step == 0) & (j == 0))
    def _():
        sc_ref[...] = jnp.zeros_like(sc_ref)
        q_copy(0).start()
        for g in range(ahead):
            start_chunk(g)

    lane_blk = lax.broadcasted_iota(jnp.int32, sc_ref.shape[1:], 1)

    def page_sync(g):
        live = g < n_chunks
        gc = jnp.minimum(g, n_chunks - 1)
        seq = gc // cpb
        ci = gc % cpb

        @pl.when(live & (ci == 0))
        def _():
            q_copy(seq).wait()

        @pl.when(live & (ci == 0) & (seq + 1 < nseq))
        def _():
            q_copy(seq + 1).start()

        start_chunk(g + ahead)

        @pl.when(live)
        def _():
            for i in range(_PAGES_PER_CHUNK):
                chunk_copy(gc, i).wait()

    def page_scores(g):
        gc = jnp.minimum(g, n_chunks - 1)
        seq = gc // cpb
        ci = gc % cpb
        slot = gc % ring
        qb = qbuf[seq % qslots]
        row = sc_ref[seq]
        for blk in range(bpc):
            p = pbuf[slot, blk * _PAGES_PER_BLOCK]
            for i in range(1, _PAGES_PER_BLOCK):
                p = p + pbuf[slot, blk * _PAGES_PER_BLOCK + i]
            part = jnp.sum(p * qb, axis=1)
            val = jnp.sum(part, axis=1, keepdims=True)
            row = jnp.where(lane_blk == ci * bpc + blk, val, row)
        sc_ref[seq] = row

    phase0 = step * _moba_phases(nbq) + _moba_phases(j)
    hp = kmean_ref.shape[1]
    nb = kmean_ref.shape[2]
    tq = qT_ref.shape[3]
    dh = ATT_HEAD_DIM
    vr = _MOBA_V_ROWS
    qTs = [qT_ref[0, 0, hh * dh:(hh + 1) * dh, :] for hh in range(hp)]
    for hh in range(hp):
        for slot in range(2):
            qs_ref[slot, hh, 0:dh, :] = qTs[hh]
            qs_ref[slot, hh, dh:2 * dh, :] = jnp.zeros((dh, tq), BF16)

    blk = lax.broadcasted_iota(jnp.int32, (nb, tq), 0)
    blk_f = blk.astype(F32)
    past = blk < j
    for hh in range(hp):
        sg = sum(jnp.dot(part, qTs[hh], preferred_element_type=F32)
                 for part in _split3(kmean_ref[0, hh]))
        work = jnp.where(past, sg, NEG_SCORE)
        bias = jnp.full((nb, tq), NEG_SCORE, F32)
        for _ in range(MOBA_TOPK):
            mx = jnp.max(work, axis=0, keepdims=True)
            first = jnp.min(jnp.where(work == mx, blk_f, float(nb)), axis=0, keepdims=True)
            chosen = blk_f == first
            bias = jnp.where(chosen & past, 0.0, bias)
            work = jnp.where(chosen, NEG_SCORE, work)
        bias_ref[hh] = bias

    kpos = lax.broadcasted_iota(jnp.int32, (tq, tq), 0)
    qpos = lax.broadcasted_iota(jnp.int32, (tq, tq), 1)
    causal = kpos <= qpos

    def score(n, slot, hh):
        return jnp.dot(kh_ref[0, n, hh], qs_ref[slot, hh], preferred_element_type=F32)

    def value(n, hh, p):
        return jnp.dot(vT_ref[0, n, hh * vr:(hh + 1) * vr, :], p, preferred_element_type=F32)

    def scores(n, slot=0):
        return [score(n, slot, hh) for hh in range(hp)]

    def values(n, ps):
        return [value(n, hh, ps[hh]) for hh in range(hp)]

    ss = [jnp.where(causal, s, NEG_SCORE) for s in scores(j)]
    ms = [jnp.max(s, axis=0, keepdims=True) for s in ss]
    pbs = [jnp.exp2((s - m).astype(BF16)) for s, m in zip(ss, ms)]
    for hh, x in enumerate(values(j, pbs)):
        acc_ref[hh] = x
    page_sync(phase0)
    page_scores(phase0)

    row0 = lax.broadcasted_iota(jnp.int32, (_MOBA_V_PAD, tq), 0) == 0

    def body(t, ms):
        n0 = 2 * t
        real1 = n0 + 1 < j
        n1 = jnp.minimum(n0 + 1, j - 1)
        g = phase0 + 1 + 2 * t
        page_sync(g)
        page_sync(g + 1)
        for hh in range(hp):
            mask0 = jnp.where(row0, bias_ref[hh, pl.ds(n0, 1), :], 0.0)
            qs_ref[0, hh, dh:dh + _MOBA_V_PAD, :] = mask0.astype(BF16)
            bias1 = jnp.where(real1, bias_ref[hh, pl.ds(n1, 1), :], NEG_SCORE)
            qs_ref[1, hh, dh:dh + _MOBA_V_PAD, :] = jnp.where(row0, bias1, 0.0).astype(BF16)
        page_scores(g)
        page_scores(g + 1)
        m_out = []
        for h0 in range(0, hp, _MOBA_HEAD_GROUP):
            hs = range(h0, min(h0 + _MOBA_HEAD_GROUP, hp))
            ss0 = [score(n0, 0, hh) for hh in hs]
            ss1 = [score(n1, 1, hh) for hh in hs]
            m0 = [ms[hh] for hh in hs]
            m1 = [jnp.maximum(m, jnp.max(s, axis=0, keepdims=True)) for m, s in zip(m0, ss0)]
            a0 = [jnp.exp2(m - mn) for m, mn in zip(m0, m1)]
            pv0 = [value(n0, hh, jnp.exp2((s - mn).astype(BF16)))
                   for hh, s, mn in zip(hs, ss0, m1)]
            m2 = [jnp.maximum(m, jnp.max(s, axis=0, keepdims=True)) for m, s in zip(m1, ss1)]
            a1 = [jnp.exp2(m - mn) for m, mn in zip(m1, m2)]
            pv1 = [value(n1, hh, jnp.exp2((s - mn).astype(BF16)))
                   for hh, s, mn in zip(hs, ss1, m2)]
            for i, hh in enumerate(hs):
                acc_ref[hh] = (a0[i] * acc_ref[hh] + pv0[i]) * a1[i] + pv1[i]
            m_out += m2
        return tuple(m_out)

    lax.fori_loop(0, (j + 1) // 2, body, tuple(ms))
    for hh in range(hp):
        acc = acc_ref[hh]
        o_ref[0, 0, hh * dh:(hh + 1) * dh, :] = (acc[0:dh] / acc[dh:dh + 1]).astype(o_ref.dtype)


def _moba_prompt(qT, kh, vT, kmean, page_table, cache_kT, q_lanes):
    B, nb, _, tq = qT.shape
    H, dh, hp = ATT_HEADS, ATT_HEAD_DIM, _MOBA_HEADS_PER_STEP
    nseq, n_pages = page_table.shape
    ps = cache_kT.shape[-1]
    assert n_pages % _PAGES_PER_CHUNK == 0 and n_pages // _PAGES_PER_BLOCK <= V7X_LANES
    n_chunks = nseq * (n_pages // _PAGES_PER_CHUNK)
    assert _PAGE_RING - 2 <= n_chunks <= B * (H // hp) * _moba_phases(nb), \
        "one page chunk per key-block phase"
    grid_spec = pltpu.PrefetchScalarGridSpec(
        num_scalar_prefetch=1,
        grid=(B, H // hp, nb),
        in_specs=[
            pl.BlockSpec((1, 1, hp * dh, tq), lambda b, g, j, pt: (b, j, g, 0)),
            pl.BlockSpec((1, nb, hp, tq, 2 * dh), lambda b, g, j, pt: (b, 0, g, 0, 0)),
            pl.BlockSpec((1, nb, hp * _MOBA_V_ROWS, tq), lambda b, g, j, pt: (b, 0, g, 0)),
            pl.BlockSpec((1, hp, nb, dh), lambda b, g, j, pt: (b, g, 0, 0)),
            pl.BlockSpec(memory_space=pl.ANY),
            pl.BlockSpec(memory_space=pl.ANY),
        ],
        out_specs=[
            pl.BlockSpec((1, 1, hp * dh, tq), lambda b, g, j, pt: (b, j, g, 0)),
            pl.BlockSpec((nseq, H, V7X_LANES), lambda b, g, j, pt: (0, 0, 0)),
        ],
        scratch_shapes=[
            pltpu.VMEM((hp, nb, tq), F32), pltpu.VMEM((2, hp, 2 * dh, tq), BF16),
            pltpu.VMEM((hp, _MOBA_V_ROWS, tq), F32),
            pltpu.VMEM((_PAGE_RING, _PAGES_PER_CHUNK, H, dh, ps), F32),
            pltpu.VMEM((3, H, dh, ps), F32),
            pltpu.SemaphoreType.DMA((_PAGE_RING,)), pltpu.SemaphoreType.DMA((3,)),
        ],
    )
    return pl.pallas_call(
        _moba_kernel,
        grid_spec=grid_spec,
        out_shape=[jax.ShapeDtypeStruct((B, nb, ATT_WIDTH, tq), BF16),
                   jax.ShapeDtypeStruct((nseq, H, V7X_LANES), F32)],
        compiler_params=_params("arbitrary", "arbitrary", "arbitrary"),
        name="prompt_moba",
    )(page_table, qT, kh, vT, kmean, cache_kT, q_lanes)


def _log_sigmoid(x):
    return jnp.minimum(x, 0.0) - jnp.log1p(jnp.exp(-jnp.abs(x)))


def _causal_conv(x, tail_ref, w_ref, b_ref):
    L = x.shape[0]
    prev = tail_ref[...]
    row = lax.broadcasted_iota(jnp.int32, (V7X_SUBLANES, x.shape[1]), 0)
    acc = x * w_ref[M_CONV - 1:M_CONV, :] + b_ref[...]
    for d in range(1, M_CONV):
        r = pltpu.roll(x, d, axis=0)
        head = jnp.where(row < d, pltpu.roll(prev, d, axis=0), r[:V7X_SUBLANES])
        shifted = jnp.concatenate([head, r[V7X_SUBLANES:]], axis=0)
        acc += shifted * w_ref[M_CONV - 1 - d:M_CONV - d, :]
    tail_ref[...] = x[L - V7X_SUBLANES:, :]
    return acc


def _mlstm_kernel(qk_ref, v_ref, vT_ref, g_ref, gc_ref, o_ref, ng_ref,
                  h_ref, C_out, n_out, m_out,
                  C_s, n_s, m_s):
    c = pl.program_id(1)
    L = qk_ref.shape[1]
    MH, dk, dv = M_HEADS, M_QK_DIM, M_V_DIM
    nt = (((1,), (1,)), ((), ()))

    @pl.when(c == 0)
    def _():
        C_s[...] = jnp.zeros_like(C_s)
        n_s[...] = jnp.zeros_like(n_s)
        m_s[...] = jnp.zeros_like(m_s)

    t_idx = lax.broadcasted_iota(jnp.int32, (L, L), 0)
    s_idx = lax.broadcasted_iota(jnp.int32, (L, L), 1)
    causal = s_idx <= t_idx
    tri = jnp.where(causal, 1.0, 0.0).astype(BF16)
    g_rows = g_ref[0]
    b_rows = sum(lax.dot_general(part, tri, nt, preferred_element_type=F32)
                 for part in _split3(_log_sigmoid(g_rows)))
    b_cols = sum(jnp.dot(tri, part, preferred_element_type=F32)
                 for part in _split3(_log_sigmoid(gc_ref[0])))

    for hd in range(MH):
        qb = qk_ref[0, :, hd * dk:(hd + 1) * dk]
        kb = qk_ref[0, :, M_QK_WIDTH + hd * dk:M_QK_WIDTH + (hd + 1) * dk]
        vsl = slice(hd * dv, (hd + 1) * dv)
        b_row = b_rows[MH + hd:MH + hd + 1]
        c_row = g_rows[hd:hd + 1] - b_row
        b_col = b_cols[:, MH + hd:MH + hd + 1]
        m_prev = m_s[hd:hd + 1, 0:1]
        M_col = jnp.maximum(jnp.max(jnp.where(causal, c_row, NEG_SCORE), axis=1, keepdims=True),
                            m_prev)
        M_last = jnp.maximum(jnp.max(c_row, axis=1, keepdims=True), m_prev)
        w = jnp.where(causal, jnp.exp(c_row - M_col), 0.0)
        a_col = jnp.exp(m_prev - M_col)

        s = lax.dot_general(qb, kb, nt, preferred_element_type=F32) * w
        inter = lax.dot_general(qb, C_s[hd].astype(BF16), nt, preferred_element_type=F32)
        num = a_col * inter + jnp.dot(s.astype(BF16), v_ref[0, :, vsl], preferred_element_type=F32)
        n_b = n_s[hd:hd + 1, :].astype(BF16).astype(F32)
        qn = jnp.sum(qb.astype(F32) * n_b, axis=1, keepdims=True)
        den = a_col * qn + jnp.sum(s, axis=1, keepdims=True)
        hh = num / jnp.maximum(jnp.abs(den), jnp.exp(-(b_col + M_col)))
        hh = hh * lax.rsqrt(jnp.mean(hh * hh, axis=-1, keepdims=True) + NORM_EPS)
        gate = jax.nn.sigmoid(o_ref[0, :, vsl].astype(F32))
        h_ref[0, :, vsl] = (hh * ng_ref[:, vsl] * gate).astype(h_ref.dtype)

        ws_row = jnp.exp(c_row - M_last)
        a_end = jnp.exp(m_prev - M_last)
        vwT = (vT_ref[0, 0, vsl, :].astype(F32) * ws_row).astype(BF16)
        C_s[hd] = a_end * C_s[hd] + jnp.dot(vwT, kb, preferred_element_type=F32)
        ws8 = jnp.broadcast_to(ws_row, (V7X_SUBLANES, L)).astype(BF16)
        n_s[hd:hd + 1, :] = (a_end * n_s[hd:hd + 1, :]
                             + jnp.dot(ws8, kb, preferred_element_type=F32)[0:1])
        m_s[hd:hd + 1, :] = jnp.broadcast_to(b_row[:, L - 1:L] + M_last, (1, m_s.shape[1]))

    @pl.when(c == pl.num_programs(1) - 1)
    def _():
        C_out[0] = C_s[...]
        n_out[0] = n_s[...]
        m_out[0] = m_s[...]


def _mlstm_prompt(qk, mv, mvT, gT, gc, mo, m_norm_g):
    B, S, W = qk.shape
    L = mvT.shape[3]
    nc = S // L
    MH, dk, dv = M_HEADS, M_QK_DIM, M_V_DIM
    const = lambda b, c: (0, 0)
    tok = lambda w: pl.BlockSpec((1, L, w), lambda b, c: (b, c, 0))
    in_specs = [
        tok(W), tok(M_WIDTH),
        pl.BlockSpec((1, 1, M_WIDTH, L), lambda b, c: (b, c, 0, 0)),
        pl.BlockSpec((1, 2 * MH, L), lambda b, c: (b, 0, c)),
        tok(2 * MH), tok(M_WIDTH),
        pl.BlockSpec((1, M_WIDTH), const),
    ]
    out_shape = [
        jax.ShapeDtypeStruct((B, S, M_WIDTH), BF16),
        jax.ShapeDtypeStruct((B, MH, dv, dk), F32),
        jax.ShapeDtypeStruct((B, V7X_SUBLANES, dk), F32),
        jax.ShapeDtypeStruct((B, V7X_SUBLANES, V7X_LANES), F32),
    ]
    out_specs = [
        tok(M_WIDTH),
        pl.BlockSpec((1, MH, dv, dk), lambda b, c: (b, 0, 0, 0)),
        pl.BlockSpec((1, V7X_SUBLANES, dk), lambda b, c: (b, 0, 0)),
        pl.BlockSpec((1, V7X_SUBLANES, V7X_LANES), lambda b, c: (b, 0, 0)),
    ]
    scratch = [
        pltpu.VMEM((MH, dv, dk), F32), pltpu.VMEM((V7X_SUBLANES, dk), F32),
        pltpu.VMEM((V7X_SUBLANES, V7X_LANES), F32),
    ]
    return pl.pallas_call(
        _mlstm_kernel,
        grid=(B, nc),
        in_specs=in_specs,
        out_specs=out_specs,
        out_shape=out_shape,
        scratch_shapes=scratch,
        compiler_params=_params("arbitrary", "arbitrary"),
        name="prompt_mlstm",
    )(qk, mv, mvT, gT, gc, mo, m_norm_g.reshape(1, M_WIDTH))


def _merge_kernel(yaT_ref, az_ref, hm_ref, mz_ref, ga_ref, gm_ref, x_ref, mod_ref,
                  watt_ref, wml_ref, wout_ref, fg_ref, y_ref, watt_s, wml_s, wout_s):
    @pl.when((pl.program_id(0) == 0) & (pl.program_id(1) == 0))
    def _():
        watt_s[...] = watt_ref[...].astype(BF16)
        wml_s[...] = wml_ref[...].astype(BF16)
        wout_s[...] = wout_ref[...].astype(BF16)

    ya = yaT_ref[0, 0].astype(F32).T * _silu(az_ref[0].astype(F32))
    ya = jnp.dot(ya.astype(BF16), watt_s[...], preferred_element_type=F32)
    ym = hm_ref[0].astype(F32) * _silu(mz_ref[0].astype(F32))
    ym = jnp.dot(ym.astype(BF16), wml_s[...], preferred_element_type=F32)
    u = (jax.nn.sigmoid(ga_ref[0].astype(F32)) * ya
         + jax.nn.sigmoid(gm_ref[0].astype(F32)) * ym)
    upd = jnp.dot(u.astype(BF16), wout_s[...], preferred_element_type=F32)
    y = x_ref[0] + mod_ref[0, 2:3, :] * upd
    y = y * lax.rsqrt(jnp.mean(y * y, axis=-1, keepdims=True) + NORM_EPS)
    y_ref[0] = y * fg_ref[...]


def _merge_prompt(yaT, az, hm, mz, ga, gm, x, mod3, w_att, w_mlstm, w_out, final_g):
    B, S, D = x.shape
    tm = yaT.shape[3]
    nb = S // tm
    const = lambda b, j: (0, 0)
    tok = lambda w: pl.BlockSpec((1, tm, w), lambda b, j: (b, j, 0))
    return pl.pallas_call(
        _merge_kernel,
        grid=(B, nb),
        in_specs=[
            pl.BlockSpec((1, 1, ATT_WIDTH, tm), lambda b, j: (b, j, 0, 0)),
            tok(ATT_WIDTH), tok(M_WIDTH), tok(M_WIDTH), tok(D), tok(D), tok(D),
            pl.BlockSpec((1, 3, D), lambda b, j: (b, 0, 0)),
            pl.BlockSpec(w_att.shape, const, pipeline_mode=pl.Buffered(1)),
            pl.BlockSpec(w_mlstm.shape, const, pipeline_mode=pl.Buffered(1)),
            pl.BlockSpec(w_out.shape, const, pipeline_mode=pl.Buffered(1)),
            pl.BlockSpec((1, D), const),
        ],
        out_specs=tok(D),
        out_shape=jax.ShapeDtypeStruct((B, S, D), F32),
        scratch_shapes=[pltpu.VMEM(w_att.shape, BF16), pltpu.VMEM(w_mlstm.shape, BF16),
                        pltpu.VMEM(w_out.shape, BF16)],
        compiler_params=_params("arbitrary", "arbitrary"),
        name="prompt_merge",
    )(yaT, az, hm, mz, ga, gm, x, mod3, w_att, w_mlstm, w_out, final_g.reshape(1, D))


def _prompt_layer(x, mod3, norm_g, w_main, w_ifT, b_if, w_conv, b_conv, m_norm_g,
                  w_att, w_mlstm, w_out, final_g, page_table, cache_kT, q_lanes):
    B, S, _ = x.shape
    (qT, kh, vT, kf, vf, kmean, az, qk, mv, mo, mz, ga, gm, gT, mvT, gc, ctail) = _inproj(
        x, mod3, norm_g, w_main, w_ifT, b_if, w_conv, b_conv)
    yaT, block_scores = _moba_prompt(qT, kh, vT, kmean, page_table, cache_kT, q_lanes)
    hm, C, n, m = _mlstm_prompt(qk, mv, mvT, gT, gc, mo, m_norm_g)
    y = _merge_prompt(yaT, az, hm, mz, ga, gm, x, mod3, w_att, w_mlstm, w_out, final_g)
    k_rows = kf.reshape(B, S, ATT_HEADS, ATT_HEAD_DIM)
    v_rows = vf.reshape(B, S, ATT_HEADS, ATT_HEAD_DIM)
    conv_new = ctail[:, V7X_SUBLANES - (M_CONV - 1):, :]
    return y, k_rows, v_rows, C, n[:, :M_HEADS, :], m[:, :M_HEADS, 0], conv_new, block_scores


def _inproj_sample_kernel(x_ref, mod_ref, ng_ref, w_ref, wif_ref, bif_ref, z_ref, g_ref):
    x = x_ref[...]
    d = x.shape[1]
    xn = x * lax.rsqrt(jnp.mean(x * x, axis=-1, keepdims=True) + NORM_EPS) * ng_ref[...]
    h = (xn * (1.0 + mod_ref[:, d:2 * d]) + mod_ref[:, 0:d]).astype(BF16)
    z_ref[...] = jnp.dot(h, w_ref[...], preferred_element_type=F32)
    g = lax.dot_general(h, wif_ref[...], (((1,), (1,)), ((), ())), preferred_element_type=F32)
    g_ref[...] = g + bif_ref[...]


def _inproj_sample(x, mod, norm_g, w_main, w_ifT, b_if):
    n, d = x.shape
    bn = 13 * V7X_LANES
    assert w_main.shape[1] % bn == 0
    const = lambda i: (0, 0)
    return pl.pallas_call(
        _inproj_sample_kernel,
        grid=(w_main.shape[1] // bn,),
        in_specs=[pl.BlockSpec((n, d), const),
                  pl.BlockSpec(mod.shape, const),
                  pl.BlockSpec((1, d), const),
                  pl.BlockSpec((d, bn), lambda i: (0, i)),
                  pl.BlockSpec(w_ifT.shape, const),
                  pl.BlockSpec((1, 2 * M_HEADS), const)],
        out_specs=[pl.BlockSpec((n, bn), lambda i: (0, i)),
                   pl.BlockSpec((n, 2 * M_HEADS), const)],
        out_shape=[jax.ShapeDtypeStruct((n, w_main.shape[1]), F32),
                   jax.ShapeDtypeStruct((n, 2 * M_HEADS), F32)],
        compiler_params=_params("arbitrary"),
        name="sample_inproj",
    )(x, mod, norm_g.reshape(1, d), w_main, w_ifT, b_if.reshape(1, 2 * M_HEADS))


_PAGES_PER_CHUNK = 8
_PAGE_RING = 5
_PAGES_PER_BLOCK = MOBA_BLOCK // PAGE_SIZE
_SAMPLE_SEQS_PER_STEP = 1


def _attend_sample_kernel(pt_ref, sel_ref, ck_ref, cv_ref, q_ref, kn_ref, vn_ref, o_ref,
                          kbuf, vbuf, sem):
    b = pl.program_id(0)
    nseq = pl.num_programs(0)
    H, dh = ATT_HEADS, ATT_HEAD_DIM
    npg = MOBA_TOPK * _PAGES_PER_BLOCK

    def copies(bb, slot):
        out = []
        for hh in range(H):
            for r in range(MOBA_TOPK):
                blk = sel_ref[(bb * H + hh) * MOBA_TOPK + r]
                for i in range(_PAGES_PER_BLOCK):
                    page = pt_ref[bb, blk * _PAGES_PER_BLOCK + i]
                    pg = r * _PAGES_PER_BLOCK + i
                    out.append(pltpu.make_async_copy(
                        ck_ref.at[0, page, hh], kbuf.at[slot, hh, pg], sem.at[0, slot]))
                    out.append(pltpu.make_async_copy(
                        cv_ref.at[0, page, hh], vbuf.at[slot, hh, pg], sem.at[1, slot]))
        return out

    @pl.when(b == 0)
    def _():
        for cp in copies(0, 0):
            cp.start()

    slot = b % 2

    @pl.when(b + 1 < nseq)
    def _():
        for cp in copies(b + 1, 1 - slot):
            cp.start()

    for cp in copies(b, slot):
        cp.wait()

    scale = ATT_HEAD_DIM ** -0.5
    nt = (((1,), (1,)), ((), ()))
    q8s = [jnp.broadcast_to(q_ref[0, hh:hh + 1, :], (V7X_SUBLANES, dh)).astype(BF16)
           for hh in range(H)]
    ss = [jnp.concatenate(
        [jnp.dot(q8s[hh], kbuf[slot, hh, pg].astype(BF16), preferred_element_type=F32)[0:1]
         for pg in range(npg)], axis=1) * scale for hh in range(H)]
    p8s, p_news, ls = [], [], []
    for hh in range(H):
        kn = kn_ref[0, hh:hh + 1, :].astype(BF16).astype(F32)
        s_new = jnp.sum(q8s[hh][0:1].astype(F32) * kn, axis=1, keepdims=True) * scale
        m = jnp.maximum(jnp.max(ss[hh], axis=1, keepdims=True), s_new)
        p = jnp.exp(ss[hh] - m)
        p_new = jnp.exp(s_new - m)
        ls.append(jnp.sum(p, axis=1, keepdims=True) + p_new)
        p_news.append(p_new)
        p8s.append(jnp.broadcast_to(p, (V7X_SUBLANES, p.shape[1])).astype(BF16))
    pvs = [[lax.dot_general(p8s[hh][:, pg * PAGE_SIZE:(pg + 1) * PAGE_SIZE],
                            vbuf[slot, hh, pg].astype(BF16), nt, preferred_element_type=F32)[0:1]
            for pg in range(npg)] for hh in range(H)]
    for hh in range(H):
        vn = vn_ref[0, hh:hh + 1, :].astype(BF16).astype(F32)
        acc = p_news[hh].astype(BF16).astype(F32) * vn
        for x in pvs[hh]:
            acc = acc + x
        o_ref[0, :, hh * dh:(hh + 1) * dh] = acc / ls[hh]


def _attend_sample(cache_k, cache_v, page_table, sel_flat, q, k_new, v_new):
    nseq, n_pages = page_table.shape
    H, dh = ATT_HEADS, ATT_HEAD_DIM
    npg = MOBA_TOPK * _PAGES_PER_BLOCK
    ps = cache_k.shape[-1]
    grid_spec = pltpu.PrefetchScalarGridSpec(
        num_scalar_prefetch=2,
        grid=(nseq,),
        in_specs=[pl.BlockSpec(memory_space=pl.ANY),
                  pl.BlockSpec(memory_space=pl.ANY),
                  pl.BlockSpec((1, H, dh), lambda b, pt, sel: (b, 0, 0)),
                  pl.BlockSpec((1, H, dh), lambda b, pt, sel: (b, 0, 0)),
                  pl.BlockSpec((1, H, dh), lambda b, pt, sel: (b, 0, 0))],
        out_specs=pl.BlockSpec((1, 1, ATT_WIDTH), lambda b, pt, sel: (b, 0, 0)),
        scratch_shapes=[pltpu.VMEM((2, H, npg, dh, ps), F32),
                        pltpu.VMEM((2, H, npg, dh, ps), F32),
                        pltpu.SemaphoreType.DMA((2, 2))],
    )
    return pl.pallas_call(
        _attend_sample_kernel,
        grid_spec=grid_spec,
        out_shape=jax.ShapeDtypeStruct((nseq, 1, ATT_WIDTH), F32),
        compiler_params=_params("arbitrary"),
        name="sample_attend",
    )(page_table, sel_flat, cache_k, cache_v, q, k_new, v_new)


def _mlstm_sample_kernel(qk_ref, v_ref, o_ref, g_ref, conv_ref, C_ref, n_ref, wc_ref, bc_ref, ng_ref,
                         h_ref, C_out, n_out, m_out, conv_out):
    for sq in range(qk_ref.shape[0]):
        x = qk_ref[sq]
        conv = conv_ref[sq]
        acc = x * wc_ref[M_CONV - 1:M_CONV, :] + bc_ref[...]
        for jj in range(M_CONV - 1):
            acc += conv[jj:jj + 1, :] * wc_ref[jj:jj + 1, :]
        qk = _silu(acc)
        conv_out[sq] = jnp.concatenate([conv[1:], x], axis=0)
        g = g_ref[sq]
        MH = M_HEADS
        bb4 = _log_sigmoid(g[:, MH:2 * MH])
        m_prev4 = g[:, 2 * MH:3 * MH]
        cc4 = g[:, 0:MH] - bb4
        M4 = jnp.maximum(m_prev4, cc4)
        w4 = jnp.exp(cc4 - M4)
        a4 = jnp.exp(m_prev4 - M4)
        floor4 = jnp.exp(-(bb4 + M4))
        m_new4 = bb4 + M4
        nt = (((1,), (1,)), ((), ()))
        heads = range(MH)
        qbs = [qk[:, hh * M_QK_DIM:(hh + 1) * M_QK_DIM].astype(BF16) for hh in heads]
        ks = [qk[:, M_QK_WIDTH + hh * M_QK_DIM:M_QK_WIDTH + (hh + 1) * M_QK_DIM] * (M_QK_DIM ** -0.5)
              for hh in heads]
        vs = [v_ref[sq][:, hh * M_V_DIM:(hh + 1) * M_V_DIM] for hh in heads]
        Cs = [C_ref[sq, hh] for hh in heads]
        ns = [n_ref[sq, hh:hh + 1, :] for hh in heads]
        inters = [lax.dot_general(jnp.broadcast_to(qbs[hh], (V7X_SUBLANES, M_QK_DIM)),
                                  Cs[hh].astype(BF16), nt, preferred_element_type=F32)[0:1]
                  for hh in heads]
        wv_cols = []
        for hh in heads:
            wv = jnp.broadcast_to(w4[:, hh:hh + 1] * vs[hh], (V7X_SUBLANES, M_V_DIM))
            wv = jnp.concatenate([wv, jnp.zeros((V7X_LANES - V7X_SUBLANES, M_V_DIM), F32)], axis=0)
            wv_cols.append(wv.T[:, 0:1])
        for hh in heads:
            w, a = w4[:, hh:hh + 1], a4[:, hh:hh + 1]
            qf = qbs[hh].astype(F32)
            s = jnp.sum(qf * ks[hh].astype(BF16).astype(F32), axis=1, keepdims=True) * w
            vb = vs[hh].astype(BF16).astype(F32)
            num = a * inters[hh] + s.astype(BF16).astype(F32) * vb
            den = a * jnp.sum(qf * ns[hh].astype(BF16).astype(F32), axis=1, keepdims=True) + s
            hv = num / jnp.maximum(jnp.abs(den), floor4[:, hh:hh + 1])
            hv = hv * lax.rsqrt(jnp.mean(hv * hv, axis=-1, keepdims=True) + NORM_EPS)
            sl = slice(hh * M_V_DIM, (hh + 1) * M_V_DIM)
            h_ref[sq, :, sl] = hv * ng_ref[:, sl] * jax.nn.sigmoid(o_ref[sq][:, sl])
        for hh in heads:
            w, a = w4[:, hh:hh + 1], a4[:, hh:hh + 1]
            C_out[sq, hh] = a * Cs[hh] + wv_cols[hh] * ks[hh]
            n_out[sq, hh:hh + 1, :] = a * ns[hh] + w * ks[hh]
        m_out[sq] = jnp.concatenate(
            [m_new4, jnp.zeros((1, V7X_LANES - MH), F32)], axis=1)


def _mlstm_sample(z3, g16, state_conv, state_C, state_n, w_conv, b_conv, m_norm_g):
    nseq = z3.shape[0]
    MH, dk, dv = M_HEADS, M_QK_DIM, M_V_DIM
    W = 2 * M_QK_WIDTH
    sb = _SAMPLE_SEQS_PER_STEP
    assert nseq % sb == 0
    const = lambda b: (0, 0)
    return pl.pallas_call(
        _mlstm_sample_kernel,
        grid=(nseq // sb,),
        in_specs=[
            pl.BlockSpec((sb, 1, W), lambda b: (b, 0, _C_MQK // W)),
            pl.BlockSpec((sb, 1, M_WIDTH), lambda b: (b, 0, _C_MV // M_WIDTH)),
            pl.BlockSpec((sb, 1, M_WIDTH), lambda b: (b, 0, _C_MO // M_WIDTH)),
            pl.BlockSpec((sb, 1, 4 * MH), lambda b: (b, 0, 0)),
            pl.BlockSpec((sb, M_CONV - 1, W), lambda b: (b, 0, 0)),
            pl.BlockSpec((sb, MH, dv, dk), lambda b: (b, 0, 0, 0)),
            pl.BlockSpec((sb, MH, dk), lambda b: (b, 0, 0)),
            pl.BlockSpec((M_CONV, W), const),
            pl.BlockSpec((1, W), const),
            pl.BlockSpec((1, M_WIDTH), const),
        ],
        out_specs=[
            pl.BlockSpec((sb, 1, M_WIDTH), lambda b: (b, 0, 0)),
            pl.BlockSpec((sb, MH, dv, dk), lambda b: (b, 0, 0, 0)),
            pl.BlockSpec((sb, MH, dk), lambda b: (b, 0, 0)),
            pl.BlockSpec((sb, 1, V7X_LANES), lambda b: (b, 0, 0)),
            pl.BlockSpec((sb, M_CONV - 1, W), lambda b: (b, 0, 0)),
        ],
        out_shape=[
            jax.ShapeDtypeStruct((nseq, 1, M_WIDTH), F32),
            jax.ShapeDtypeStruct((nseq, MH, dv, dk), F32),
            jax.ShapeDtypeStruct((nseq, MH, dk), F32),
            jax.ShapeDtypeStruct((nseq, 1, V7X_LANES), F32),
            jax.ShapeDtypeStruct((nseq, M_CONV - 1, W), F32),
        ],
        compiler_params=_params("arbitrary"),
        name="sample_mlstm",
    )(z3, z3, z3, g16, state_conv, state_C, state_n, w_conv, b_conv.reshape(1, W),
      m_norm_g.reshape(1, M_WIDTH))


def _merge_sample_kernel(ya_ref, hm_ref, z_ref, x_ref, mod_ref, watt_ref, wml_ref, wout_ref, fg_ref,
                         y_ref):
    d = x_ref.shape[1]
    ya = ya_ref[...] * _silu(z_ref[:, _C_AZ:_C_MQK])
    ya = jnp.dot(ya.astype(BF16), watt_ref[...].astype(BF16), preferred_element_type=F32)
    ym = hm_ref[...] * _silu(z_ref[:, _C_MZ:_C_MIF])
    ym = jnp.dot(ym.astype(BF16), wml_ref[...].astype(BF16), preferred_element_type=F32)
    tail = z_ref[:, _C_MIF:]
    ga = tail[:, _C_GA - _C_MIF:_C_GM - _C_MIF]
    gm = tail[:, _C_GM - _C_MIF:_C_END - _C_MIF]
    u = jax.nn.sigmoid(ga) * ya + jax.nn.sigmoid(gm) * ym
    upd = jnp.dot(u.astype(BF16), wout_ref[...].astype(BF16), preferred_element_type=F32)
    y = x_ref[...] + mod_ref[:, 2 * d:3 * d] * upd
    y = y * lax.rsqrt(jnp.mean(y * y, axis=-1, keepdims=True) + NORM_EPS)
    y_ref[...] = y * fg_ref[...]


def _merge_sample(ya, hm, z, x, mod, w_att, w_mlstm, w_out, final_g):
    n, d = x.shape
    full = lambda a: pl.BlockSpec(a.shape, lambda i: (0,) * a.ndim)
    fg = final_g.reshape(1, d)
    args = (ya, hm, z, x, mod, w_att, w_mlstm, w_out, fg)
    return pl.pallas_call(
        _merge_sample_kernel,
        grid=(1,),
        in_specs=[full(a) for a in args],
        out_specs=pl.BlockSpec((n, d), lambda i: (0, 0)),
        out_shape=jax.ShapeDtypeStruct((n, d), F32),
        compiler_params=_params("arbitrary"),
        name="sample_merge",
    )(*args)


def _gate_scores_kernel(sc_ref, idx_ref, *, nblk):
    sc = sc_ref[...] / MOBA_BLOCK
    lane = lax.broadcasted_iota(jnp.int32, sc.shape, 1)
    cnt = jnp.zeros(sc.shape, jnp.int32)
    for m in range(nblk):
        col = sc[:, m:m + 1]
        beats = (col > sc) | ((col == sc) & (lane > m))
        cnt += beats.astype(jnp.int32)
    out = jnp.zeros(sc.shape, jnp.int32)
    for r in range(MOBA_TOPK):
        idx = jnp.sum(jnp.where((cnt == r) & (lane < nblk), lane, 0), axis=1, keepdims=True)
        out = jnp.where(lane == r, idx, out)
    idx_ref[...] = out


def _gate_scores(block_scores, nblk):
    nseq, H, lanes = block_scores.shape
    rows = nseq * H
    out = pl.pallas_call(
        functools.partial(_gate_scores_kernel, nblk=nblk),
        grid=(1,),
        in_specs=[pl.BlockSpec((rows, lanes), lambda i: (0, 0))],
        out_specs=pl.BlockSpec((rows, lanes), lambda i: (0, 0)),
        out_shape=jax.ShapeDtypeStruct((rows, lanes), jnp.int32),
        compiler_params=_params("arbitrary"),
        name="sample_gate",
    )(block_scores.reshape(rows, lanes))
    return out.reshape(nseq, H, lanes)


def _sample_inputs(x, mod, norm_g, w_main, w_ifT, b_if):
    nseq = x.shape[0]
    H, dh = ATT_HEADS, ATT_HEAD_DIM
    z, g = _inproj_sample(x, mod, norm_g, w_main, w_ifT, b_if)
    q = z[:, _C_AQ:_C_AK].reshape(nseq, H, dh)
    q_lanes = jnp.broadcast_to(q[..., None], (nseq, H, dh, PAGE_SIZE))
    return z, g, q, q_lanes


def _sample_layer(x, mod, z, g, q, block_scores, cache_kT, cache_vT, page_table,
                  state_C, state_n, state_m, state_conv,
                  w_conv, b_conv, m_norm_g, w_att, w_mlstm, w_out, final_g):
    nseq = x.shape[0]
    H, dh = ATT_HEADS, ATT_HEAD_DIM
    nblk = page_table.shape[1] // _PAGES_PER_BLOCK
    assert nblk >= MOBA_TOPK
    k_new = z[:, _C_AK:_C_AV].reshape(nseq, H, dh)
    v_new = z[:, _C_AV:_C_AZ].reshape(nseq, H, dh)
    sel = _gate_scores(block_scores, nblk)
    sel_flat = sel[:, :, :MOBA_TOPK].reshape(-1)
    ya = _attend_sample(cache_kT, cache_vT, page_table, sel_flat, q, k_new, v_new)
    g16 = jnp.concatenate([g, state_m, jnp.zeros_like(state_m)], axis=1).reshape(nseq, 1, 4 * M_HEADS)
    hm, C, n, m, conv_new = _mlstm_sample(z.reshape(nseq, 1, -1), g16, state_conv, state_C, state_n,
                                          w_conv, b_conv, m_norm_g)
    y = _merge_sample(ya.reshape(nseq, ATT_WIDTH), hm.reshape(nseq, M_WIDTH), z, x, mod,
                      w_att, w_mlstm, w_out, final_g)
    return (y.reshape(nseq, 1, -1), k_new.reshape(nseq, 1, H, dh), v_new.reshape(nseq, 1, H, dh),
            C, n, m[:, 0, :M_HEADS], conv_new)


def kernel(x_prompt, x_sample, cache_k, cache_v, state_C, state_n, state_m, state_conv, page_table,
           c_prompt, c_sample, w_ada, b_ada, norm_g, w_in, b_if, w_conv, b_conv, m_norm_g,
           w_att, w_mlstm, w_out, final_g):
    depth = w_in.shape[0]
    assert depth == 1, "single-layer step"
    B = x_prompt.shape[0]
    nseq = x_sample.shape[0]
    pad = (-(B + nseq)) % V7X_SUBLANES
    c_all = jnp.concatenate([c_prompt, c_sample, jnp.zeros((pad, c_prompt.shape[1]), F32)], axis=0)
    mod = _ada(c_all, w_ada[0], b_ada[0])
    mod_p = mod[:B].reshape(B, 3, D_MODEL)
    mod_s = mod[B:B + nseq]
    w_main, w_ifT = _pack_w_in(w_in[0])
    wa, wm, wo = w_att[0], w_mlstm[0], w_out[0]
    cache_kT = jnp.transpose(cache_k, (0, 1, 3, 4, 2))
    cache_vT = jnp.transpose(cache_v, (0, 1, 3, 4, 2))
    xs = x_sample[:, 0, :]
    z, g, q, q_lanes = _sample_inputs(xs, mod_s, norm_g[0], w_main, w_ifT, b_if[0])
    *outs_p, block_scores = _prompt_layer(
        x_prompt, mod_p, norm_g[0], w_main, w_ifT, b_if[0], w_conv[0], b_conv[0], m_norm_g[0],
        wa, wm, wo, final_g, page_table, cache_kT, q_lanes)
    outs_s = _sample_layer(xs, mod_s, z, g, q, block_scores, cache_kT, cache_vT, page_table,
                           state_C[0], state_n[0], state_m[0], state_conv[0],
                           w_conv[0], b_conv[0], m_norm_g[0], wa, wm, wo, final_g)
    y_p, *st_p = outs_p
    y_s, *st_s = outs_s
    return (y_p, y_s) + tuple(a[None] for a in st_p) + tuple(a[None] for a in st_s)
```

```python
import functools

import jax
import jax.numpy as jnp
from jax import lax
from jax.experimental import pallas as pl
from jax.experimental.pallas import tpu as pltpu

F32 = jnp.float32
BF16 = jnp.bfloat16

D_MODEL = 1024
ATT_HEADS = 8
ATT_HEAD_DIM = 64
ATT_WIDTH = ATT_HEADS * ATT_HEAD_DIM
MOBA_BLOCK = 256
MOBA_TOPK = 3
M_HEADS = 4
M_V_DIM = 256
M_QK_DIM = 128
M_QK_WIDTH = M_HEADS * M_QK_DIM
M_WIDTH = M_HEADS * M_V_DIM
M_CONV = 4
PAGE_SIZE = 128
NORM_EPS = 1e-6
NEG_SCORE = -1e30
_LOG2E = 1.4426950408889634

V7X_LANES = 128
V7X_SUBLANES = 8
V7X_VMEM_LIMIT_BYTES = 56 * 1024 * 1024

_C_AQ, _C_AK, _C_AV, _C_AZ = 0, 512, 1024, 1536
_C_MQK, _C_MV, _C_MO, _C_MZ = 2048, 3072, 4096, 5120
_C_MIF, _C_GA, _C_GM, _C_END = 6144, 6152, 7176, 8200


def _silu(x):
    return x * jax.nn.sigmoid(x)


def _split3(x):
    hi = x.astype(BF16)
    r1 = x - hi.astype(F32)
    mid = r1.astype(BF16)
    lo = (r1 - mid.astype(F32)).astype(BF16)
    return hi, mid, lo


def _pack_w_in(w_in):
    pad = (-w_in.shape[1]) % V7X_LANES
    w_main = jnp.concatenate([w_in.astype(BF16), jnp.zeros((w_in.shape[0], pad), BF16)], axis=1)
    w_ifT = w_main[:, _C_MIF:_C_GA].T
    return w_main, w_ifT


def _params(*sem):
    return pltpu.CompilerParams(dimension_semantics=sem, vmem_limit_bytes=V7X_VMEM_LIMIT_BYTES)


def _ada_kernel(c_ref, w_ref, b_ref, o_ref):
    a = _silu(c_ref[...]).astype(BF16)
    acc = jnp.dot(a, w_ref[...].astype(BF16), preferred_element_type=F32)
    o_ref[...] = acc + b_ref[...]


def _ada(c, w_ada, b_ada):
    n, d = c.shape
    d3 = w_ada.shape[1]
    bn = 512
    return pl.pallas_call(
        _ada_kernel,
        grid=(d3 // bn,),
        in_specs=[pl.BlockSpec((n, d), lambda i: (0, 0)),
                  pl.BlockSpec((d, bn), lambda i: (0, i)),
                  pl.BlockSpec((1, bn), lambda i: (0, i))],
        out_specs=pl.BlockSpec((n, bn), lambda i: (0, i)),
        out_shape=jax.ShapeDtypeStruct((n, d3), F32),
        compiler_params=_params("arbitrary"),
        name="ada_mod",
    )(c, w_ada, b_ada.reshape(1, d3))


def _inproj_kernel(x_ref, mod_ref, ng_ref, w_ref, wif_ref, bif_ref, bifn_ref, wc_ref, bc_ref,
                   qT_ref, kh_ref, vT_ref, kf_ref, vf_ref, kmean_ref,
                   az_ref, qk_ref, mv_ref, mo_ref, mz_ref, ga_ref, gm_ref, gT_ref, mvT_ref, gc_ref,
                   ctail_ref, tail_s):
    j = pl.program_id(1)
    x = x_ref[0]
    xn = x * lax.rsqrt(jnp.mean(x * x, axis=-1, keepdims=True) + NORM_EPS) * ng_ref[...]
    shift = mod_ref[0, 0:1, :]
    scale = mod_ref[0, 1:2, :]
    h = (xn * (1.0 + scale) + shift).astype(BF16)

    def proj(a, b):
        return jnp.dot(h, w_ref[:, a:b], preferred_element_type=F32)

    @pl.when(j == 0)
    def _():
        tail_s[...] = jnp.zeros_like(tail_s)

    mqk = proj(_C_MQK, _C_MV)
    ctail_ref[0] = mqk[mqk.shape[0] - V7X_SUBLANES:, :]
    qk = _silu(_causal_conv(mqk, tail_s, wc_ref, bc_ref))
    qk_ref[0, :, 0:M_QK_WIDTH] = qk[:, 0:M_QK_WIDTH].astype(BF16)
    qk_ref[0, :, M_QK_WIDTH:] = (qk[:, M_QK_WIDTH:] * (M_QK_DIM ** -0.5)).astype(BF16)

    mo_ref[0] = proj(_C_MO, _C_MZ).astype(BF16)

    q = proj(_C_AQ, _C_AK) * (ATT_HEAD_DIM ** -0.5 * _LOG2E)
    qT_ref[0, 0] = q.T.astype(BF16)
    mz_ref[0] = proj(_C_MZ, _C_MIF).astype(BF16)

    dh = ATT_HEAD_DIM
    tm = x.shape[0]
    one_col = jnp.where(lax.broadcasted_iota(jnp.int32, (tm, dh), 1) == 0, 1.0, 0.0)
    one_row = jnp.where(lax.broadcasted_iota(jnp.int32, (_MOBA_V_PAD, tm), 0) == 0, 1.0, 0.0)
    k = proj(_C_AK, _C_AV)
    kf_ref[0] = k
    ksum = jnp.sum(k, axis=0, keepdims=True) / MOBA_BLOCK
    for hh in range(ATT_HEADS):
        sl = slice(hh * dh, (hh + 1) * dh)
        kh_ref[0, 0, hh] = jnp.concatenate([k[:, sl], one_col], axis=1).astype(BF16)
        kmean_ref[0, hh, pl.ds(j, 1), :] = ksum[:, sl]
    az_ref[0] = proj(_C_AZ, _C_MQK).astype(BF16)

    v = proj(_C_AV, _C_AZ)
    vf_ref[0] = v
    vT = v.T
    for hh in range(ATT_HEADS):
        r0 = hh * _MOBA_V_ROWS
        vT_ref[0, 0, r0:r0 + dh, :] = vT[hh * dh:(hh + 1) * dh].astype(BF16)
        vT_ref[0, 0, r0 + dh:r0 + _MOBA_V_ROWS, :] = one_row.astype(BF16)

    tail = proj(_C_MIF, w_ref.shape[1])
    gc_ref[0] = tail[:, 0:_C_GA - _C_MIF] + bifn_ref[...]
    ga_ref[0] = tail[:, _C_GA - _C_MIF:_C_GM - _C_MIF].astype(BF16)
    gm_ref[0] = tail[:, _C_GM - _C_MIF:_C_END - _C_MIF].astype(BF16)
    gT = lax.dot_general(wif_ref[...], h, (((1,), (1,)), ((), ())), preferred_element_type=F32)
    gT_ref[0] = gT + bif_ref[...]

    mv = proj(_C_MV, _C_MO)
    mv_ref[0] = mv.astype(BF16)
    mvT_ref[0, 0] = mv.T.astype(BF16)


def _inproj(x, mod3, norm_g, w_main, w_ifT, b_if, w_conv, b_conv):
    B, S, D = x.shape
    W = 2 * M_QK_WIDTH
    tm = MOBA_BLOCK
    nb = S // tm
    H, dh = ATT_HEADS, ATT_HEAD_DIM
    const = lambda b, j: (0, 0)
    tok = lambda w: pl.BlockSpec((1, tm, w), lambda b, j: (b, j, 0))
    blk = pl.BlockSpec((1, 1, ATT_WIDTH, tm), lambda b, j: (b, j, 0, 0))
    out_shape = [
        jax.ShapeDtypeStruct((B, nb, ATT_WIDTH, tm), BF16),
        jax.ShapeDtypeStruct((B, nb, H, tm, 2 * dh), BF16),
        jax.ShapeDtypeStruct((B, nb, H * _MOBA_V_ROWS, tm), BF16),
        jax.ShapeDtypeStruct((B, S, ATT_WIDTH), F32),
        jax.ShapeDtypeStruct((B, S, ATT_WIDTH), F32),
        jax.ShapeDtypeStruct((B, H, nb, dh), F32),
        jax.ShapeDtypeStruct((B, S, ATT_WIDTH), BF16),
        jax.ShapeDtypeStruct((B, S, W), BF16),
        jax.ShapeDtypeStruct((B, S, M_WIDTH), BF16),
        jax.ShapeDtypeStruct((B, S, M_WIDTH), BF16),
        jax.ShapeDtypeStruct((B, S, M_WIDTH), BF16),
        jax.ShapeDtypeStruct((B, S, D), BF16),
        jax.ShapeDtypeStruct((B, S, D), BF16),
        jax.ShapeDtypeStruct((B, 2 * M_HEADS, S), F32),
        jax.ShapeDtypeStruct((B, nb, M_WIDTH, tm), BF16),
        jax.ShapeDtypeStruct((B, S, 2 * M_HEADS), F32),
        jax.ShapeDtypeStruct((B, V7X_SUBLANES, W), F32),
    ]
    out_specs = [
        blk,
        pl.BlockSpec((1, 1, H, tm, 2 * dh), lambda b, j: (b, j, 0, 0, 0)),
        pl.BlockSpec((1, 1, H * _MOBA_V_ROWS, tm), lambda b, j: (b, j, 0, 0)),
        tok(ATT_WIDTH), tok(ATT_WIDTH),
        pl.BlockSpec((1, H, nb, dh), lambda b, j: (b, 0, 0, 0)),
        tok(ATT_WIDTH), tok(2 * M_QK_WIDTH), tok(M_WIDTH), tok(M_WIDTH), tok(M_WIDTH), tok(D), tok(D),
        pl.BlockSpec((1, 2 * M_HEADS, tm), lambda b, j: (b, 0, j)),
        pl.BlockSpec((1, 1, M_WIDTH, tm), lambda b, j: (b, j, 0, 0)),
        tok(2 * M_HEADS),
        pl.BlockSpec((1, V7X_SUBLANES, W), lambda b, j: (b, 0, 0)),
    ]
    in_specs = [
        pl.BlockSpec((1, tm, D), lambda b, j: (b, j, 0)),
        pl.BlockSpec((1, 3, D), lambda b, j: (b, 0, 0)),
        pl.BlockSpec((1, D), const),
        pl.BlockSpec(w_main.shape, const, pipeline_mode=pl.Buffered(1)),
        pl.BlockSpec(w_ifT.shape, const),
        pl.BlockSpec((2 * M_HEADS, 1), const),
        pl.BlockSpec((1, 2 * M_HEADS), const),
        pl.BlockSpec((M_CONV, W), const),
        pl.BlockSpec((1, W), const),
    ]
    return pl.pallas_call(
        _inproj_kernel,
        grid=(B, nb),
        in_specs=in_specs,
        out_specs=out_specs,
        out_shape=out_shape,
        scratch_shapes=[pltpu.VMEM((V7X_SUBLANES, W), F32)],
        compiler_params=_params("arbitrary", "arbitrary"),
        name="prompt_inproj",
    )(x, mod3, norm_g.reshape(1, D), w_main, w_ifT, b_if.reshape(2 * M_HEADS, 1),
      b_if.reshape(1, 2 * M_HEADS), w_conv, b_conv.reshape(1, W))


_MOBA_HEADS_PER_STEP = 8
_MOBA_HEAD_GROUP = 8
_MOBA_V_PAD = 16
_MOBA_V_ROWS = ATT_HEAD_DIM + _MOBA_V_PAD


def _moba_phases(j):
    return j + 2 * ((j * j) // 4)


def _moba_kernel(pt_ref, qT_ref, kh_ref, vT_ref, kmean_ref, ck_ref, qb_ref, o_ref, sc_ref,
                 bias_ref, qs_ref, acc_ref, pbuf, qbuf, psem, qsem):
    b = pl.program_id(0)
    j = pl.program_id(2)
    nbq = pl.num_programs(2)
    nseq, n_pages = pt_ref.shape
    cpb = n_pages // _PAGES_PER_CHUNK
    bpc = _PAGES_PER_CHUNK // _PAGES_PER_BLOCK
    n_chunks = nseq * cpb
    ring = pbuf.shape[0]
    ahead = ring - 2

    def chunk_copy(g, i):
        page = pt_ref[g // cpb, (g % cpb) * _PAGES_PER_CHUNK + i]
        slot = g % ring
        return pltpu.make_async_copy(ck_ref.at[0, page], pbuf.at[slot, i], psem.at[slot])

    qslots = qbuf.shape[0]

    def q_copy(seq):
        s = seq % qslots
        return pltpu.make_async_copy(qb_ref.at[seq], qbuf.at[s], qsem.at[s])

    def start_chunk(g):
        @pl.when(g < n_chunks)
        def _():
            for i in range(_PAGES_PER_CHUNK):
                chunk_copy(g, i).start()

    step = b * pl.num_programs(1) + pl.program_id(1)

    @pl.when((step == 0) & (j == 0))
    def _():
        sc_ref[...] = jnp.zeros_like(sc_ref)
        q_copy(0).start()
        for g in range(ahead):
            start_chunk(g)

    lane_blk = lax.broadcasted_iota(jnp.int32, sc_ref.shape[1:], 1)

    def page_sync(g):
        live = g < n_chunks
        gc = jnp.minimum(g, n_chunks - 1)
        seq = gc // cpb
        ci = gc % cpb

        @pl.when(live & (ci == 0))
        def _():
            q_copy(seq).wait()

        @pl.when(live & (ci == 0) & (seq + 1 < nseq))
        def _():
            q_copy(seq + 1).start()

        start_chunk(g + ahead)

        @pl.when(live)
        def _():
            for i in range(_PAGES_PER_CHUNK):
                chunk_copy(gc, i).wait()

    def page_scores(g):
        gc = jnp.minimum(g, n_chunks - 1)
        seq = gc // cpb
        ci = gc % cpb
        slot = gc % ring
        qb = qbuf[seq % qslots]
        row = sc_ref[seq]
        for blk in range(bpc):
            p = pbuf[slot, blk * _PAGES_PER_BLOCK]
            for i in range(1, _PAGES_PER_BLOCK):
                p = p + pbuf[slot, blk * _PAGES_PER_BLOCK + i]
            part = jnp.sum(p * qb, axis=1)
            val = jnp.sum(part, axis=1, keepdims=True)
            row = jnp.where(lane_blk == ci * bpc + blk, val, row)
        sc_ref[seq] = row

    phase0 = step * _moba_phases(nbq) + _moba_phases(j)
    hp = kmean_ref.shape[1]
    nb = kmean_ref.shape[2]
    tq = qT_ref.shape[3]
    dh = ATT_HEAD_DIM
    vr = _MOBA_V_ROWS
    qTs = [qT_ref[0, 0, hh * dh:(hh + 1) * dh, :] for hh in range(hp)]
    for hh in range(hp):
        for slot in range(2):
            qs_ref[slot, hh, 0:dh, :] = qTs[hh]
            qs_ref[slot, hh, dh:2 * dh, :] = jnp.zeros((dh, tq), BF16)

    blk = lax.broadcasted_iota(jnp.int32, (nb, tq), 0)
    blk_f = blk.astype(F32)
    past = blk < j
    for hh in range(hp):
        sg = sum(jnp.dot(part, qTs[hh], preferred_element_type=F32)
                 for part in _split3(kmean_ref[0, hh]))
        work = jnp.where(past, sg, NEG_SCORE)
        bias = jnp.full((nb, tq), NEG_SCORE, F32)
        for _ in range(MOBA_TOPK):
            mx = jnp.max(work, axis=0, keepdims=True)
            first = jnp.min(jnp.where(work == mx, blk_f, float(nb)), axis=0, keepdims=True)
            chosen = blk_f == first
            bias = jnp.where(chosen & past, 0.0, bias)
            work = jnp.where(chosen, NEG_SCORE, work)
        bias_ref[hh] = bias

    kpos = lax.broadcasted_iota(jnp.int32, (tq, tq), 0)
    qpos = lax.broadcasted_iota(jnp.int32, (tq, tq), 1)
    causal = kpos <= qpos

    def score(n, slot, hh):
        return jnp.dot(kh_ref[0, n, hh], qs_ref[slot, hh], preferred_element_type=F32)

    def value(n, hh, p):
        return jnp.dot(vT_ref[0, n, hh * vr:(hh + 1) * vr, :], p, preferred_element_type=F32)

    def scores(n, slot=0):
        return [score(n, slot, hh) for hh in range(hp)]

    def values(n, ps):
        return [value(n, hh, ps[hh]) for hh in range(hp)]

    ss = [jnp.where(causal, s, NEG_SCORE) for s in scores(j)]
    ms = [jnp.max(s, axis=0, keepdims=True) for s in ss]
    pbs = [jnp.exp2((s - m).astype(BF16)) for s, m in zip(ss, ms)]
    for hh, x in enumerate(values(j, pbs)):
        acc_ref[hh] = x
    page_sync(phase0)
    page_scores(phase0)

    row0 = lax.broadcasted_iota(jnp.int32, (_MOBA_V_PAD, tq), 0) == 0

    def body(t, ms):
        n0 = 2 * t
        real1 = n0 + 1 < j
        n1 = jnp.minimum(n0 + 1, j - 1)
        g = phase0 + 1 + 2 * t
        page_sync(g)
        page_sync(g + 1)
        for hh in range(hp):
            mask0 = jnp.where(row0, bias_ref[hh, pl.ds(n0, 1), :], 0.0)
            qs_ref[0, hh, dh:dh + _MOBA_V_PAD, :] = mask0.astype(BF16)
            bias1 = jnp.where(real1, bias_ref[hh, pl.ds(n1, 1), :], NEG_SCORE)
            qs_ref[1, hh, dh:dh + _MOBA_V_PAD, :] = jnp.where(row0, bias1, 0.0).astype(BF16)
        page_scores(g)
        page_scores(g + 1)
        m_out = []
        for h0 in range(0, hp, _MOBA_HEAD_GROUP):
            hs = range(h0, min(h0 + _MOBA_HEAD_GROUP, hp))
            ss0 = [score(n0, 0, hh) for hh in hs]
            ss1 = [score(n1, 1, hh) for hh in hs]
            m0 = [ms[hh] for hh in hs]
            m1 = [jnp.maximum(m, jnp.max(s, axis=0, keepdims=True)) for m, s in zip(m0, ss0)]
            a0 = [jnp.exp2(m - mn) for m, mn in zip(m0, m1)]
            pv0 = [value(n0, hh, jnp.exp2((s - mn).astype(BF16)))
                   for hh, s, mn in zip(hs, ss0, m1)]
            m2 = [jnp.maximum(m, jnp.max(s, axis=0, keepdims=True)) for m, s in zip(m1, ss1)]
            a1 = [jnp.exp2(m - mn) for m, mn in zip(m1, m2)]
            pv1 = [value(n1, hh, jnp.exp2((s - mn).astype(BF16)))
                   for hh, s, mn in zip(hs, ss1, m2)]
            for i, hh in enumerate(hs):
                acc_ref[hh] = (a0[i] * acc_ref[hh] + pv0[i]) * a1[i] + pv1[i]
            m_out += m2
        return tuple(m_out)

    lax.fori_loop(0, (j + 1) // 2, body, tuple(ms))
    for hh in range(hp):
        acc = acc_ref[hh]
        o_ref[0, 0, hh * dh:(hh + 1) * dh, :] = (acc[0:dh] / acc[dh:dh + 1]).astype(o_ref.dtype)


def _moba_prompt(qT, kh, vT, kmean, page_table, cache_kT, q_lanes):
    B, nb, _, tq = qT.shape
    H, dh, hp = ATT_HEADS, ATT_HEAD_DIM, _MOBA_HEADS_PER_STEP
    nseq, n_pages = page_table.shape
    ps = cache_kT.shape[-1]
    assert n_pages % _PAGES_PER_CHUNK == 0 and n_pages // _PAGES_PER_BLOCK <= V7X_LANES
    n_chunks = nseq * (n_pages // _PAGES_PER_CHUNK)
    assert _PAGE_RING - 2 <= n_chunks <= B * (H // hp) * _moba_phases(nb), \
        "one page chunk per key-block phase"
    grid_spec = pltpu.PrefetchScalarGridSpec(
        num_scalar_prefetch=1,
        grid=(B, H // hp, nb),
        in_specs=[
            pl.BlockSpec((1, 1, hp * dh, tq), lambda b, g, j, pt: (b, j, g, 0)),
            pl.BlockSpec((1, nb, hp, tq, 2 * dh), lambda b, g, j, pt: (b, 0, g, 0, 0)),
            pl.BlockSpec((1, nb, hp * _MOBA_V_ROWS, tq), lambda b, g, j, pt: (b, 0, g, 0)),
            pl.BlockSpec((1, hp, nb, dh), lambda b, g, j, pt: (b, g, 0, 0)),
            pl.BlockSpec(memory_space=pl.ANY),
            pl.BlockSpec(memory_space=pl.ANY),
        ],
        out_specs=[
            pl.BlockSpec((1, 1, hp * dh, tq), lambda b, g, j, pt: (b, j, g, 0)),
            pl.BlockSpec((nseq, H, V7X_LANES), lambda b, g, j, pt: (0, 0, 0)),
        ],
        scratch_shapes=[
            pltpu.VMEM((hp, nb, tq), F32), pltpu.VMEM((2, hp, 2 * dh, tq), BF16),
            pltpu.VMEM((hp, _MOBA_V_ROWS, tq), F32),
            pltpu.VMEM((_PAGE_RING, _PAGES_PER_CHUNK, H, dh, ps), F32),
            pltpu.VMEM((3, H, dh, ps), F32),
            pltpu.SemaphoreType.DMA((_PAGE_RING,)), pltpu.SemaphoreType.DMA((3,)),
        ],
    )
    return pl.pallas_call(
        _moba_kernel,
        grid_spec=grid_spec,
        out_shape=[jax.ShapeDtypeStruct((B, nb, ATT_WIDTH, tq), BF16),
                   jax.ShapeDtypeStruct((nseq, H, V7X_LANES), F32)],
        compiler_params=_params("arbitrary", "arbitrary", "arbitrary"),
        name="prompt_moba",
    )(page_table, qT, kh, vT, kmean, cache_kT, q_lanes)


def _log_sigmoid(x):
    return jnp.minimum(x, 0.0) - jnp.log1p(jnp.exp(-jnp.abs(x)))


def _causal_conv(x, tail_ref, w_ref, b_ref):
    L = x.shape[0]
    prev = tail_ref[...]
    row = lax.broadcasted_iota(jnp.int32, (V7X_SUBLANES, x.shape[1]), 0)
    acc = x * w_ref[M_CONV - 1:M_CONV, :] + b_ref[...]
    for d in range(1, M_CONV):
        r = pltpu.roll(x, d, axis=0)
        head = jnp.where(row < d, pltpu.roll(prev, d, axis=0), r[:V7X_SUBLANES])
        shifted = jnp.concatenate([head, r[V7X_SUBLANES:]], axis=0)
        acc += shifted * w_ref[M_CONV - 1 - d:M_CONV - d, :]
    tail_ref[...] = x[L - V7X_SUBLANES:, :]
    return acc


def _mlstm_kernel(qk_ref, v_ref, vT_ref, g_ref, gc_ref, o_ref, ng_ref,
                  h_ref, C_out, n_out, m_out,
                  C_s, n_s, m_s):
    c = pl.program_id(1)
    L = qk_ref.shape[1]
    MH, dk, dv = M_HEADS, M_QK_DIM, M_V_DIM
    nt = (((1,), (1,)), ((), ()))

    @pl.when(c == 0)
    def _():
        C_s[...] = jnp.zeros_like(C_s)
        n_s[...] = jnp.zeros_like(n_s)
        m_s[...] = jnp.zeros_like(m_s)

    t_idx = lax.broadcasted_iota(jnp.int32, (L, L), 0)
    s_idx = lax.broadcasted_iota(jnp.int32, (L, L), 1)
    causal = s_idx <= t_idx
    tri = jnp.where(causal, 1.0, 0.0).astype(BF16)
    g_rows = g_ref[0]
    b_rows = sum(lax.dot_general(part, tri, nt, preferred_element_type=F32)
                 for part in _split3(_log_sigmoid(g_rows)))
    b_cols = sum(jnp.dot(tri, part, preferred_element_type=F32)
                 for part in _split3(_log_sigmoid(gc_ref[0])))

    for hd in range(MH):
        qb = qk_ref[0, :, hd * dk:(hd + 1) * dk]
        kb = qk_ref[0, :, M_QK_WIDTH + hd * dk:M_QK_WIDTH + (hd + 1) * dk]
        vsl = slice(hd * dv, (hd + 1) * dv)
        b_row = b_rows[MH + hd:MH + hd + 1]
        c_row = g_rows[hd:hd + 1] - b_row
        b_col = b_cols[:, MH + hd:MH + hd + 1]
        m_prev = m_s[hd:hd + 1, 0:1]
        M_col = jnp.maximum(jnp.max(jnp.where(causal, c_row, NEG_SCORE), axis=1, keepdims=True),
                            m_prev)
        M_last = jnp.maximum(jnp.max(c_row, axis=1, keepdims=True), m_prev)
        w = jnp.where(causal, jnp.exp(c_row - M_col), 0.0)
        a_col = jnp.exp(m_prev - M_col)

        s = lax.dot_general(qb, kb, nt, preferred_element_type=F32) * w
        inter = lax.dot_general(qb, C_s[hd].astype(BF16), nt, preferred_element_type=F32)
        num = a_col * inter + jnp.dot(s.astype(BF16), v_ref[0, :, vsl], preferred_element_type=F32)
        n_b = n_s[hd:hd + 1, :].astype(BF16).astype(F32)
        qn = jnp.sum(qb.astype(F32) * n_b, axis=1, keepdims=True)
        den = a_col * qn + jnp.sum(s, axis=1, keepdims=True)
        hh = num / jnp.maximum(jnp.abs(den), jnp.exp(-(b_col + M_col)))
        hh = hh * lax.rsqrt(jnp.mean(hh * hh, axis=-1, keepdims=True) + NORM_EPS)
        gate = jax.nn.sigmoid(o_ref[0, :, vsl].astype(F32))
        h_ref[0, :, vsl] = (hh * ng_ref[:, vsl] * gate).astype(h_ref.dtype)

        ws_row = jnp.exp(c_row - M_last)
        a_end = jnp.exp(m_prev - M_last)
        vwT = (vT_ref[0, 0, vsl, :].astype(F32) * ws_row).astype(BF16)
        C_s[hd] = a_end * C_s[hd] + jnp.dot(vwT, kb, preferred_element_type=F32)
        ws8 = jnp.broadcast_to(ws_row, (V7X_SUBLANES, L)).astype(BF16)
        n_s[hd:hd + 1, :] = (a_end * n_s[hd:hd + 1, :]
                             + jnp.dot(ws8, kb, preferred_element_type=F32)[0:1])
        m_s[hd:hd + 1, :] = jnp.broadcast_to(b_row[:, L - 1:L] + M_last, (1, m_s.shape[1]))

    @pl.when(c == pl.num_programs(1) - 1)
    def _():
        C_out[0] = C_s[...]
        n_out[0] = n_s[...]
        m_out[0] = m_s[...]


def _mlstm_prompt(qk, mv, mvT, gT, gc, mo, m_norm_g):
    B, S, W = qk.shape
    L = mvT.shape[3]
    nc = S // L
    MH, dk, dv = M_HEADS, M_QK_DIM, M_V_DIM
    const = lambda b, c: (0, 0)
    tok = lambda w: pl.BlockSpec((1, L, w), lambda b, c: (b, c, 0))
    in_specs = [
        tok(W), tok(M_WIDTH),
        pl.BlockSpec((1, 1, M_WIDTH, L), lambda b, c: (b, c, 0, 0)),
        pl.BlockSpec((1, 2 * MH, L), lambda b, c: (b, 0, c)),
        tok(2 * MH), tok(M_WIDTH),
        pl.BlockSpec((1, M_WIDTH), const),
    ]
    out_shape = [
        jax.ShapeDtypeStruct((B, S, M_WIDTH), BF16),
        jax.ShapeDtypeStruct((B, MH, dv, dk), F32),
        jax.ShapeDtypeStruct((B, V7X_SUBLANES, dk), F32),
        jax.ShapeDtypeStruct((B, V7X_SUBLANES, V7X_LANES), F32),
    ]
    out_specs = [
        tok(M_WIDTH),
        pl.BlockSpec((1, MH, dv, dk), lambda b, c: (b, 0, 0, 0)),
        pl.BlockSpec((1, V7X_SUBLANES, dk), lambda b, c: (b, 0, 0)),
        pl.BlockSpec((1, V7X_SUBLANES, V7X_LANES), lambda b, c: (b, 0, 0)),
    ]
    scratch = [
        pltpu.VMEM((MH, dv, dk), F32), pltpu.VMEM((V7X_SUBLANES, dk), F32),
        pltpu.VMEM((V7X_SUBLANES, V7X_LANES), F32),
    ]
    return pl.pallas_call(
        _mlstm_kernel,
        grid=(B, nc),
        in_specs=in_specs,
        out_specs=out_specs,
        out_shape=out_shape,
        scratch_shapes=scratch,
        compiler_params=_params("arbitrary", "arbitrary"),
        name="prompt_mlstm",
    )(qk, mv, mvT, gT, gc, mo, m_norm_g.reshape(1, M_WIDTH))


def _merge_kernel(yaT_ref, az_ref, hm_ref, mz_ref, ga_ref, gm_ref, x_ref, mod_ref,
                  watt_ref, wml_ref, wout_ref, fg_ref, y_ref, watt_s, wml_s, wout_s):
    @pl.when((pl.program_id(0) == 0) & (pl.program_id(1) == 0))
    def _():
        watt_s[...] = watt_ref[...].astype(BF16)
        wml_s[...] = wml_ref[...].astype(BF16)
        wout_s[...] = wout_ref[...].astype(BF16)

    ya = yaT_ref[0, 0].astype(F32).T * _silu(az_ref[0].astype(F32))
    ya = jnp.dot(ya.astype(BF16), watt_s[...], preferred_element_type=F32)
    ym = hm_ref[0].astype(F32) * _silu(mz_ref[0].astype(F32))
    ym = jnp.dot(ym.astype(BF16), wml_s[...], preferred_element_type=F32)
    u = (jax.nn.sigmoid(ga_ref[0].astype(F32)) * ya
         + jax.nn.sigmoid(gm_ref[0].astype(F32)) * ym)
    upd = jnp.dot(u.astype(BF16), wout_s[...], preferred_element_type=F32)
    y = x_ref[0] + mod_ref[0, 2:3, :] * upd
    y = y * lax.rsqrt(jnp.mean(y * y, axis=-1, keepdims=True) + NORM_EPS)
    y_ref[0] = y * fg_ref[...]


def _merge_prompt(yaT, az, hm, mz, ga, gm, x, mod3, w_att, w_mlstm, w_out, final_g):
    B, S, D = x.shape
    tm = yaT.shape[3]
    nb = S // tm
    const = lambda b, j: (0, 0)
    tok = lambda w: pl.BlockSpec((1, tm, w), lambda b, j: (b, j, 0))
    return pl.pallas_call(
        _merge_kernel,
        grid=(B, nb),
        in_specs=[
            pl.BlockSpec((1, 1, ATT_WIDTH, tm), lambda b, j: (b, j, 0, 0)),
            tok(ATT_WIDTH), tok(M_WIDTH), tok(M_WIDTH), tok(D), tok(D), tok(D),
            pl.BlockSpec((1, 3, D), lambda b, j: (b, 0, 0)),
            pl.BlockSpec(w_att.shape, const, pipeline_mode=pl.Buffered(1)),
            pl.BlockSpec(w_mlstm.shape, const, pipeline_mode=pl.Buffered(1)),
            pl.BlockSpec(w_out.shape, const, pipeline_mode=pl.Buffered(1)),
            pl.BlockSpec((1, D), const),
        ],
        out_specs=tok(D),
        out_shape=jax.ShapeDtypeStruct((B, S, D), F32),
        scratch_shapes=[pltpu.VMEM(w_att.shape, BF16), pltpu.VMEM(w_mlstm.shape, BF16),
                        pltpu.VMEM(w_out.shape, BF16)],
        compiler_params=_params("arbitrary", "arbitrary"),
        name="prompt_merge",
    )(yaT, az, hm, mz, ga, gm, x, mod3, w_att, w_mlstm, w_out, final_g.reshape(1, D))


def _prompt_layer(x, mod3, norm_g, w_main, w_ifT, b_if, w_conv, b_conv, m_norm_g,
                  w_att, w_mlstm, w_out, final_g, page_table, cache_kT, q_lanes):
    B, S, _ = x.shape
    (qT, kh, vT, kf, vf, kmean, az, qk, mv, mo, mz, ga, gm, gT, mvT, gc, ctail) = _inproj(
        x, mod3, norm_g, w_main, w_ifT, b_if, w_conv, b_conv)
    yaT, block_scores = _moba_prompt(qT, kh, vT, kmean, page_table, cache_kT, q_lanes)
    hm, C, n, m = _mlstm_prompt(qk, mv, mvT, gT, gc, mo, m_norm_g)
    y = _merge_prompt(yaT, az, hm, mz, ga, gm, x, mod3, w_att, w_mlstm, w_out, final_g)
    k_rows = kf.reshape(B, S, ATT_HEADS, ATT_HEAD_DIM)
    v_rows = vf.reshape(B, S, ATT_HEADS, ATT_HEAD_DIM)
    conv_new = ctail[:, V7X_SUBLANES - (M_CONV - 1):, :]
    return y, k_rows, v_rows, C, n[:, :M_HEADS, :], m[:, :M_HEADS, 0], conv_new, block_scores


def _inproj_sample_kernel(x_ref, mod_ref, ng_ref, w_ref, wif_ref, bif_ref, z_ref, g_ref):
    x = x_ref[...]
    d = x.shape[1]
    xn = x * lax.rsqrt(jnp.mean(x * x, axis=-1, keepdims=True) + NORM_EPS) * ng_ref[...]
    h = (xn * (1.0 + mod_ref[:, d:2 * d]) + mod_ref[:, 0:d]).astype(BF16)
    z_ref[...] = jnp.dot(h, w_ref[...], preferred_element_type=F32)
    g = lax.dot_general(h, wif_ref[...], (((1,), (1,)), ((), ())), preferred_element_type=F32)
    g_ref[...] = g + bif_ref[...]


def _inproj_sample(x, mod, norm_g, w_main, w_ifT, b_if):
    n, d = x.shape
    bn = 13 * V7X_LANES
    assert w_main.shape[1] % bn == 0
    const = lambda i: (0, 0)
    return pl.pallas_call(
        _inproj_sample_kernel,
        grid=(w_main.shape[1] // bn,),
        in_specs=[pl.BlockSpec((n, d), const),
                  pl.BlockSpec(mod.shape, const),
                  pl.BlockSpec((1, d), const),
                  pl.BlockSpec((d, bn), lambda i: (0, i)),
                  pl.BlockSpec(w_ifT.shape, const),
                  pl.BlockSpec((1, 2 * M_HEADS), const)],
        out_specs=[pl.BlockSpec((n, bn), lambda i: (0, i)),
                   pl.BlockSpec((n, 2 * M_HEADS), const)],
        out_shape=[jax.ShapeDtypeStruct((n, w_main.shape[1]), F32),
                   jax.ShapeDtypeStruct((n, 2 * M_HEADS), F32)],
        compiler_params=_params("arbitrary"),
        name="sample_inproj",
    )(x, mod, norm_g.reshape(1, d), w_main, w_ifT, b_if.reshape(1, 2 * M_HEADS))


_PAGES_PER_CHUNK = 8
_PAGE_RING = 5
_PAGES_PER_BLOCK = MOBA_BLOCK // PAGE_SIZE
_SAMPLE_SEQS_PER_STEP = 1


def _attend_sample_kernel(pt_ref, sel_ref, ck_ref, cv_ref, q_ref, ql_ref, kn_ref, vn_ref, o_ref,
                          kbuf, vbuf, sem):
    b = pl.program_id(0)
    nseq = pl.num_programs(0)
    H, dh = ATT_HEADS, ATT_HEAD_DIM
    npg = MOBA_TOPK * _PAGES_PER_BLOCK

    def copies(bb, slot):
        out = []
        for hh in range(H):
            for r in range(MOBA_TOPK):
                blk = sel_ref[(bb * H + hh) * MOBA_TOPK + r]
                for i in range(_PAGES_PER_BLOCK):
                    page = pt_ref[bb, blk * _PAGES_PER_BLOCK + i]
                    pg = r * _PAGES_PER_BLOCK + i
                    out.append(pltpu.make_async_copy(
                        ck_ref.at[0, page, hh], kbuf.at[slot, hh, pg], sem.at[0, slot]))
                    out.append(pltpu.make_async_copy(
                        cv_ref.at[0, page, hh], vbuf.at[slot, hh, pg], sem.at[1, slot]))
        return out

    @pl.when(b == 0)
    def _():
        for cp in copies(0, 0):
            cp.start()

    slot = b % 2

    @pl.when(b + 1 < nseq)
    def _():
        for cp in copies(b + 1, 1 - slot):
            cp.start()

    for cp in copies(b, slot):
        cp.wait()

    @pl.when(b == 0)
    def _():
        o_ref[...] = jnp.zeros_like(o_ref)

    scale = ATT_HEAD_DIM ** -0.5
    lane = lax.broadcasted_iota(jnp.int32, (dh, ql_ref.shape[3]), 1)
    eye = lax.broadcasted_iota(jnp.int32, lane.shape, 0) == lane
    for hh in range(H):
        qb = ql_ref[0, hh]
        s = jnp.concatenate([jnp.sum(kbuf[slot, hh, pg] * qb, axis=0, keepdims=True)
                             for pg in range(npg)], axis=1) * scale
        s_new = jnp.sum(q_ref[0, hh:hh + 1, :] * kn_ref[0, hh:hh + 1, :], axis=1,
                        keepdims=True) * scale
        m = jnp.maximum(jnp.max(s, axis=1, keepdims=True), s_new)
        p = jnp.exp(s - m)
        p_new = jnp.exp(s_new - m)
        l = jnp.sum(p, axis=1, keepdims=True) + p_new
        tile = vbuf[slot, hh, 0] * p[:, 0:PAGE_SIZE]
        for pg in range(1, npg):
            tile = tile + vbuf[slot, hh, pg] * p[:, pg * PAGE_SIZE:(pg + 1) * PAGE_SIZE]
        vn_b = jnp.broadcast_to(
            jnp.concatenate([vn_ref[0, hh:hh + 1, :], jnp.zeros((1, lane.shape[1] - dh), F32)],
                            axis=1), lane.shape)
        col = jnp.sum(tile + jnp.where(eye, vn_b * p_new, 0.0), axis=1, keepdims=True) / l
        sl = slice(hh * dh, (hh + 1) * dh)
        o_ref[sl, :] = jnp.where(lane == b, col, o_ref[sl, :])


def _attend_sample(cache_k, cache_v, page_table, sel_flat, q, q_lanes, k_new, v_new):
    nseq, n_pages = page_table.shape
    H, dh = ATT_HEADS, ATT_HEAD_DIM
    npg = MOBA_TOPK * _PAGES_PER_BLOCK
    ps = cache_k.shape[-1]
    assert nseq <= V7X_LANES and q_lanes.shape[3] == ps
    row = lambda: pl.BlockSpec((1, H, dh), lambda b, pt, sel: (b, 0, 0))
    grid_spec = pltpu.PrefetchScalarGridSpec(
        num_scalar_prefetch=2,
        grid=(nseq,),
        in_specs=[pl.BlockSpec(memory_space=pl.ANY),
                  pl.BlockSpec(memory_space=pl.ANY),
                  row(),
                  pl.BlockSpec((1, H, dh, ps), lambda b, pt, sel: (b, 0, 0, 0)),
                  row(), row()],
        out_specs=pl.BlockSpec((ATT_WIDTH, V7X_LANES), lambda b, pt, sel: (0, 0)),
        scratch_shapes=[pltpu.VMEM((2, H, npg, dh, ps), F32),
                        pltpu.VMEM((2, H, npg, dh, ps), F32),
                        pltpu.SemaphoreType.DMA((2, 2))],
    )
    return pl.pallas_call(
        _attend_sample_kernel,
        grid_spec=grid_spec,
        out_shape=jax.ShapeDtypeStruct((ATT_WIDTH, V7X_LANES), F32),
        compiler_params=_params("arbitrary"),
        name="sample_attend",
    )(page_table, sel_flat, cache_k, cache_v, q, q_lanes, k_new, v_new)


def _mlstm_sample_kernel(qk_ref, v_ref, o_ref, g_ref, conv_ref, C_ref, n_ref, wc_ref, bc_ref, ng_ref,
                         h_ref, C_out, n_out, m_out, conv_out):
    for sq in range(qk_ref.shape[0]):
        x = qk_ref[sq]
        conv = conv_ref[sq]
        acc = x * wc_ref[M_CONV - 1:M_CONV, :] + bc_ref[...]
        for jj in range(M_CONV - 1):
            acc += conv[jj:jj + 1, :] * wc_ref[jj:jj + 1, :]
        qk = _silu(acc)
        conv_out[sq] = jnp.concatenate([conv[1:], x], axis=0)
        g = g_ref[sq]
        MH = M_HEADS
        bb4 = _log_sigmoid(g[:, MH:2 * MH])
        m_prev4 = g[:, 2 * MH:3 * MH]
        cc4 = g[:, 0:MH] - bb4
        M4 = jnp.maximum(m_prev4, cc4)
        w4 = jnp.exp(cc4 - M4)
        a4 = jnp.exp(m_prev4 - M4)
        floor4 = jnp.exp(-(bb4 + M4))
        m_new4 = bb4 + M4
        nt = (((1,), (1,)), ((), ()))
        heads = range(MH)
        qbs = [qk[:, hh * M_QK_DIM:(hh + 1) * M_QK_DIM].astype(BF16) for hh in heads]
        ks = [qk[:, M_QK_WIDTH + hh * M_QK_DIM:M_QK_WIDTH + (hh + 1) * M_QK_DIM] * (M_QK_DIM ** -0.5)
              for hh in heads]
        vs = [v_ref[sq][:, hh * M_V_DIM:(hh + 1) * M_V_DIM] for hh in heads]
        Cs = [C_ref[sq, hh] for hh in heads]
        ns = [n_ref[sq, hh:hh + 1, :] for hh in heads]
        inters = [lax.dot_general(jnp.broadcast_to(qbs[hh], (V7X_SUBLANES, M_QK_DIM)),
                                  Cs[hh].astype(BF16), nt, preferred_element_type=F32)[0:1]
                  for hh in heads]
        wv_cols = []
        for hh in heads:
            wv = jnp.broadcast_to(w4[:, hh:hh + 1] * vs[hh], (V7X_SUBLANES, M_V_DIM))
            wv = jnp.concatenate([wv, jnp.zeros((V7X_LANES - V7X_SUBLANES, M_V_DIM), F32)], axis=0)
            wv_cols.append(wv.T[:, 0:1])
        for hh in heads:
            w, a = w4[:, hh:hh + 1], a4[:, hh:hh + 1]
            qf = qbs[hh].astype(F32)
            s = jnp.sum(qf * ks[hh].astype(BF16).astype(F32), axis=1, keepdims=True) * w
            vb = vs[hh].astype(BF16).astype(F32)
            num = a * inters[hh] + s.astype(BF16).astype(F32) * vb
            den = a * jnp.sum(qf * ns[hh].astype(BF16).astype(F32), axis=1, keepdims=True) + s
            hv = num / jnp.maximum(jnp.abs(den), floor4[:, hh:hh + 1])
            hv = hv * lax.rsqrt(jnp.mean(hv * hv, axis=-1, keepdims=True) + NORM_EPS)
            sl = slice(hh * M_V_DIM, (hh + 1) * M_V_DIM)
            h_ref[sq, :, sl] = hv * ng_ref[:, sl] * jax.nn.sigmoid(o_ref[sq][:, sl])
        for hh in heads:
            w, a = w4[:, hh:hh + 1], a4[:, hh:hh + 1]
            C_out[sq, hh] = a * Cs[hh] + wv_cols[hh] * ks[hh]
            n_out[sq, hh:hh + 1, :] = a * ns[hh] + w * ks[hh]
        m_out[sq] = jnp.concatenate(
            [m_new4, jnp.zeros((1, V7X_LANES - MH), F32)], axis=1)


def _mlstm_sample(z3, g16, state_conv, state_C, state_n, w_conv, b_conv, m_norm_g):
    nseq = z3.shape[0]
    MH, dk, dv = M_HEADS, M_QK_DIM, M_V_DIM
    W = 2 * M_QK_WIDTH
    sb = _SAMPLE_SEQS_PER_STEP
    assert nseq % sb == 0
    const = lambda b: (0, 0)
    return pl.pallas_call(
        _mlstm_sample_kernel,
        grid=(nseq // sb,),
        in_specs=[
            pl.BlockSpec((sb, 1, W), lambda b: (b, 0, _C_MQK // W)),
            pl.BlockSpec((sb, 1, M_WIDTH), lambda b: (b, 0, _C_MV // M_WIDTH)),
            pl.BlockSpec((sb, 1, M_WIDTH), lambda b: (b, 0, _C_MO // M_WIDTH)),
            pl.BlockSpec((sb, 1, 4 * MH), lambda b: (b, 0, 0)),
            pl.BlockSpec((sb, M_CONV - 1, W), lambda b: (b, 0, 0)),
            pl.BlockSpec((sb, MH, dv, dk), lambda b: (b, 0, 0, 0)),
            pl.BlockSpec((sb, MH, dk), lambda b: (b, 0, 0)),
            pl.BlockSpec((M_CONV, W), const),
            pl.BlockSpec((1, W), const),
            pl.BlockSpec((1, M_WIDTH), const),
        ],
        out_specs=[
            pl.BlockSpec((sb, 1, M_WIDTH), lambda b: (b, 0, 0)),
            pl.BlockSpec((sb, MH, dv, dk), lambda b: (b, 0, 0, 0)),
            pl.BlockSpec((sb, MH, dk), lambda b: (b, 0, 0)),
            pl.BlockSpec((sb, 1, V7X_LANES), lambda b: (b, 0, 0)),
            pl.BlockSpec((sb, M_CONV - 1, W), lambda b: (b, 0, 0)),
        ],
        out_shape=[
            jax.ShapeDtypeStruct((nseq, 1, M_WIDTH), F32),
            jax.ShapeDtypeStruct((nseq, MH, dv, dk), F32),
            jax.ShapeDtypeStruct((nseq, MH, dk), F32),
            jax.ShapeDtypeStruct((nseq, 1, V7X_LANES), F32),
            jax.ShapeDtypeStruct((nseq, M_CONV - 1, W), F32),
        ],
        compiler_params=_params("arbitrary"),
        name="sample_mlstm",
    )(z3, z3, z3, g16, state_conv, state_C, state_n, w_conv, b_conv.reshape(1, W),
      m_norm_g.reshape(1, M_WIDTH))


def _merge_sample_kernel(ya_ref, hm_ref, z_ref, x_ref, mod_ref, watt_ref, wml_ref, wout_ref, fg_ref,
                         y_ref):
    d = x_ref.shape[1]
    n_rows = x_ref.shape[0]
    ya = ya_ref[...].T[0:n_rows, :] * _silu(z_ref[:, _C_AZ:_C_MQK])
    ya = jnp.dot(ya.astype(BF16), watt_ref[...].astype(BF16), preferred_element_type=F32)
    ym = hm_ref[...] * _silu(z_ref[:, _C_MZ:_C_MIF])
    ym = jnp.dot(ym.astype(BF16), wml_ref[...].astype(BF16), preferred_element_type=F32)
    tail = z_ref[:, _C_MIF:]
    ga = tail[:, _C_GA - _C_MIF:_C_GM - _C_MIF]
    gm = tail[:, _C_GM - _C_MIF:_C_END - _C_MIF]
    u = jax.nn.sigmoid(ga) * ya + jax.nn.sigmoid(gm) * ym
    upd = jnp.dot(u.astype(BF16), wout_ref[...].astype(BF16), preferred_element_type=F32)
    y = x_ref[...] + mod_ref[:, 2 * d:3 * d] * upd
    y = y * lax.rsqrt(jnp.mean(y * y, axis=-1, keepdims=True) + NORM_EPS)
    y_ref[...] = y * fg_ref[...]


def _merge_sample(ya, hm, z, x, mod, w_att, w_mlstm, w_out, final_g):
    n, d = x.shape
    full = lambda a: pl.BlockSpec(a.shape, lambda i: (0,) * a.ndim)
    fg = final_g.reshape(1, d)
    args = (ya, hm, z, x, mod, w_att, w_mlstm, w_out, fg)
    return pl.pallas_call(
        _merge_sample_kernel,
        grid=(1,),
        in_specs=[full(a) for a in args],
        out_specs=pl.BlockSpec((n, d), lambda i: (0, 0)),
        out_shape=jax.ShapeDtypeStruct((n, d), F32),
        compiler_params=_params("arbitrary"),
        name="sample_merge",
    )(*args)


def _gate_scores_kernel(sc_ref, idx_ref, *, nblk):
    sc = sc_ref[...] / MOBA_BLOCK
    lane = lax.broadcasted_iota(jnp.int32, sc.shape, 1)
    cnt = jnp.zeros(sc.shape, jnp.int32)
    for m in range(nblk):
        col = sc[:, m:m + 1]
        beats = (col > sc) | ((col == sc) & (lane > m))
        cnt += beats.astype(jnp.int32)
    out = jnp.zeros(sc.shape, jnp.int32)
    for r in range(MOBA_TOPK):
        idx = jnp.sum(jnp.where((cnt == r) & (lane < nblk), lane, 0), axis=1, keepdims=True)
        out = jnp.where(lane == r, idx, out)
    idx_ref[...] = out


def _gate_scores(block_scores, nblk):
    nseq, H, lanes = block_scores.shape
    rows = nseq * H
    out = pl.pallas_call(
        functools.partial(_gate_scores_kernel, nblk=nblk),
        grid=(1,),
        in_specs=[pl.BlockSpec((rows, lanes), lambda i: (0, 0))],
        out_specs=pl.BlockSpec((rows, lanes), lambda i: (0, 0)),
        out_shape=jax.ShapeDtypeStruct((rows, lanes), jnp.int32),
        compiler_params=_params("arbitrary"),
        name="sample_gate",
    )(block_scores.reshape(rows, lanes))
    return out.reshape(nseq, H, lanes)


def _sample_inputs(x, mod, norm_g, w_main, w_ifT, b_if):
    nseq = x.shape[0]
    H, dh = ATT_HEADS, ATT_HEAD_DIM
    z, g = _inproj_sample(x, mod, norm_g, w_main, w_ifT, b_if)
    q = z[:, _C_AQ:_C_AK].reshape(nseq, H, dh)
    q_lanes = jnp.broadcast_to(q[..., None], (nseq, H, dh, PAGE_SIZE))
    return z, g, q, q_lanes


def _sample_layer(x, mod, z, g, q, q_lanes, block_scores, cache_kT, cache_vT, page_table,
                  state_C, state_n, state_m, state_conv,
                  w_conv, b_conv, m_norm_g, w_att, w_mlstm, w_out, final_g):
    nseq = x.shape[0]
    H, dh = ATT_HEADS, ATT_HEAD_DIM
    nblk = page_table.shape[1] // _PAGES_PER_BLOCK
    assert nblk >= MOBA_TOPK
    k_new = z[:, _C_AK:_C_AV].reshape(nseq, H, dh)
    v_new = z[:, _C_AV:_C_AZ].reshape(nseq, H, dh)
    sel = _gate_scores(block_scores, nblk)
    sel_flat = sel[:, :, :MOBA_TOPK].reshape(-1)
    yaT = _attend_sample(cache_kT, cache_vT, page_table, sel_flat, q, q_lanes, k_new, v_new)
    g16 = jnp.concatenate([g, state_m, jnp.zeros_like(state_m)], axis=1).reshape(nseq, 1, 4 * M_HEADS)
    hm, C, n, m, conv_new = _mlstm_sample(z.reshape(nseq, 1, -1), g16, state_conv, state_C, state_n,
                                          w_conv, b_conv, m_norm_g)
    y = _merge_sample(yaT, hm.reshape(nseq, M_WIDTH), z, x, mod,
                      w_att, w_mlstm, w_out, final_g)
    return (y.reshape(nseq, 1, -1), k_new.reshape(nseq, 1, H, dh), v_new.reshape(nseq, 1, H, dh),
            C, n, m[:, 0, :M_HEADS], conv_new)


def kernel(x_prompt, x_sample, cache_k, cache_v, state_C, state_n, state_m, state_conv, page_table,
           c_prompt, c_sample, w_ada, b_ada, norm_g, w_in, b_if, w_conv, b_conv, m_norm_g,
           w_att, w_mlstm, w_out, final_g):
    depth = w_in.shape[0]
    assert depth == 1, "single-layer step"
    B = x_prompt.shape[0]
    nseq = x_sample.shape[0]
    pad = (-(B + nseq)) % V7X_SUBLANES
    c_all = jnp.concatenate([c_prompt, c_sample, jnp.zeros((pad, c_prompt.shape[1]), F32)], axis=0)
    mod = _ada(c_all, w_ada[0], b_ada[0])
    mod_p = mod[:B].reshape(B, 3, D_MODEL)
    mod_s = mod[B:B + nseq]
    w_main, w_ifT = _pack_w_in(w_in[0])
    wa, wm, wo = w_att[0], w_mlstm[0], w_out[0]
    cache_kT = jnp.transpose(cache_k, (0, 1, 3, 4, 2))
    cache_vT = jnp.transpose(cache_v, (0, 1, 3, 4, 2))
    xs = x_sample[:, 0, :]
    z, g, q, q_lanes = _sample_inputs(xs, mod_s, norm_g[0], w_main, w_ifT, b_if[0])
    *outs_p, block_scores = _prompt_layer(
        x_prompt, mod_p, norm_g[0], w_main, w_ifT, b_if[0], w_conv[0], b_conv[0], m_norm_g[0],
        wa, wm, wo, final_g, page_table, cache_kT, q_lanes)
    outs_s = _sample_layer(xs, mod_s, z, g, q, q_lanes, block_scores, cache_kT, cache_vT, page_table,
                           state_C[0], state_n[0], state_m[0], state_conv[0],
                           w_conv[0], b_conv[0], m_norm_g[0], wa, wm, wo, final_g)
    y_p, *st_p = outs_p
    y_s, *st_s = outs_s
    return (y_p, y_s) + tuple(a[None] for a in st_p) + tuple(a[None] for a in st_s)
```

```python
import functools

import jax
import jax.numpy as jnp
from jax import lax
from jax.experimental import pallas as pl
from jax.experimental.pallas import tpu as pltpu

F32 = jnp.float32
BF16 = jnp.bfloat16

D_MODEL = 1024
ATT_HEADS = 8
ATT_HEAD_DIM = 64
ATT_WIDTH = ATT_HEADS * ATT_HEAD_DIM
MOBA_BLOCK = 256
MOBA_TOPK = 3
M_HEADS = 4
M_V_DIM = 256
M_QK_DIM = 128
M_QK_WIDTH = M_HEADS * M_QK_DIM
M_WIDTH = M_HEADS * M_V_DIM
M_CONV = 4
PAGE_SIZE = 128
NORM_EPS = 1e-6
NEG_SCORE = -1e30
_LOG2E = 1.4426950408889634

V7X_LANES = 128
V7X_SUBLANES = 8
V7X_VMEM_LIMIT_BYTES = 56 * 1024 * 1024

_C_AQ, _C_AK, _C_AV, _C_AZ = 0, 512, 1024, 1536
_C_MQK, _C_MV, _C_MO, _C_MZ = 2048, 3072, 4096, 5120
_C_MIF, _C_GA, _C_GM, _C_END = 6144, 6152, 7176, 8200


def _silu(x):
    return x * jax.nn.sigmoid(x)


def _split3(x):
    hi = x.astype(BF16)
    r1 = x - hi.astype(F32)
    mid = r1.astype(BF16)
    lo = (r1 - mid.astype(F32)).astype(BF16)
    return hi, mid, lo


def _pack_w_in(w_in):
    pad = (-w_in.shape[1]) % V7X_LANES
    w_main = jnp.concatenate([w_in.astype(BF16), jnp.zeros((w_in.shape[0], pad), BF16)], axis=1)
    w_ifT = w_main[:, _C_MIF:_C_GA].T
    return w_main, w_ifT


def _params(*sem):
    return pltpu.CompilerParams(dimension_semantics=sem, vmem_limit_bytes=V7X_VMEM_LIMIT_BYTES)


def _ada_kernel(c_ref, w_ref, b_ref, o_ref):
    a = _silu(c_ref[...]).astype(BF16)
    acc = jnp.dot(a, w_ref[...].astype(BF16), preferred_element_type=F32)
    o_ref[...] = acc + b_ref[...]


def _ada(c, w_ada, b_ada):
    n, d = c.shape
    d3 = w_ada.shape[1]
    bn = 512
    return pl.pallas_call(
        _ada_kernel,
        grid=(d3 // bn,),
        in_specs=[pl.BlockSpec((n, d), lambda i: (0, 0)),
                  pl.BlockSpec((d, bn), lambda i: (0, i)),
                  pl.BlockSpec((1, bn), lambda i: (0, i))],
        out_specs=pl.BlockSpec((n, bn), lambda i: (0, i)),
        out_shape=jax.ShapeDtypeStruct((n, d3), F32),
        compiler_params=_params("arbitrary"),
        name="ada_mod",
    )(c, w_ada, b_ada.reshape(1, d3))


def _inproj_kernel(x_ref, mod_ref, ng_ref, w_ref, wif_ref, bif_ref, bifn_ref, wc_ref, bc_ref,
                   qT_ref, kh_ref, vT_ref, kf_ref, vf_ref, kmean_ref,
                   az_ref, qk_ref, mv_ref, mo_ref, mz_ref, ga_ref, gm_ref, gT_ref, mvT_ref, gc_ref,
                   ctail_ref, tail_s):
    j = pl.program_id(1)
    x = x_ref[0]
    xn = x * lax.rsqrt(jnp.mean(x * x, axis=-1, keepdims=True) + NORM_EPS) * ng_ref[...]
    shift = mod_ref[0, 0:1, :]
    scale = mod_ref[0, 1:2, :]
    h = (xn * (1.0 + scale) + shift).astype(BF16)

    def proj(a, b):
        return jnp.dot(h, w_ref[:, a:b], preferred_element_type=F32)

    @pl.when(j == 0)
    def _():
        tail_s[...] = jnp.zeros_like(tail_s)

    mqk = proj(_C_MQK, _C_MV)
    ctail_ref[0] = mqk[mqk.shape[0] - V7X_SUBLANES:, :]
    qk = _silu(_causal_conv(mqk, tail_s, wc_ref, bc_ref))
    qk_ref[0, :, 0:M_QK_WIDTH] = qk[:, 0:M_QK_WIDTH].astype(BF16)
    qk_ref[0, :, M_QK_WIDTH:] = (qk[:, M_QK_WIDTH:] * (M_QK_DIM ** -0.5)).astype(BF16)

    mo_ref[0] = proj(_C_MO, _C_MZ).astype(BF16)

    q = proj(_C_AQ, _C_AK) * (ATT_HEAD_DIM ** -0.5 * _LOG2E)
    qT_ref[0, 0] = q.T.astype(BF16)
    mz_ref[0] = proj(_C_MZ, _C_MIF).astype(BF16)

    dh = ATT_HEAD_DIM
    tm = x.shape[0]
    one_col = jnp.where(lax.broadcasted_iota(jnp.int32, (tm, dh), 1) == 0, 1.0, 0.0)
    one_row = jnp.where(lax.broadcasted_iota(jnp.int32, (_MOBA_V_PAD, tm), 0) == 0, 1.0, 0.0)
    k = proj(_C_AK, _C_AV)
    kf_ref[0] = k
    ksum = jnp.sum(k, axis=0, keepdims=True) / MOBA_BLOCK
    for hh in range(ATT_HEADS):
        sl = slice(hh * dh, (hh + 1) * dh)
        kh_ref[0, 0, hh] = jnp.concatenate([k[:, sl], one_col], axis=1).astype(BF16)
        kmean_ref[0, hh, pl.ds(j, 1), :] = ksum[:, sl]
    az_ref[0] = proj(_C_AZ, _C_MQK).astype(BF16)

    v = proj(_C_AV, _C_AZ)
    vf_ref[0] = v
    vT = v.T
    for hh in range(ATT_HEADS):
        r0 = hh * _MOBA_V_ROWS
        vT_ref[0, 0, r0:r0 + dh, :] = vT[hh * dh:(hh + 1) * dh].astype(BF16)
        vT_ref[0, 0, r0 + dh:r0 + _MOBA_V_ROWS, :] = one_row.astype(BF16)

    tail = proj(_C_MIF, w_ref.shape[1])
    gc_ref[0] = tail[:, 0:_C_GA - _C_MIF] + bifn_ref[...]
    ga_ref[0] = tail[:, _C_GA - _C_MIF:_C_GM - _C_MIF].astype(BF16)
    gm_ref[0] = tail[:, _C_GM - _C_MIF:_C_END - _C_MIF].astype(BF16)
    gT = lax.dot_general(wif_ref[...], h, (((1,), (1,)), ((), ())), preferred_element_type=F32)
    gT_ref[0] = gT + bif_ref[...]

    mv = proj(_C_MV, _C_MO)
    mv_ref[0] = mv.astype(BF16)
    mvT_ref[0, 0] = mv.T.astype(BF16)


def _inproj(x, mod3, norm_g, w_main, w_ifT, b_if, w_conv, b_conv):
    B, S, D = x.shape
    W = 2 * M_QK_WIDTH
    tm = MOBA_BLOCK
    nb = S // tm
    H, dh = ATT_HEADS, ATT_HEAD_DIM
    const = lambda b, j: (0, 0)
    tok = lambda w: pl.BlockSpec((1, tm, w), lambda b, j: (b, j, 0))
    blk = pl.BlockSpec((1, 1, ATT_WIDTH, tm), lambda b, j: (b, j, 0, 0))
    out_shape = [
        jax.ShapeDtypeStruct((B, nb, ATT_WIDTH, tm), BF16),
        jax.ShapeDtypeStruct((B, nb, H, tm, 2 * dh), BF16),
        jax.ShapeDtypeStruct((B, nb, H * _MOBA_V_ROWS, tm), BF16),
        jax.ShapeDtypeStruct((B, S, ATT_WIDTH), F32),
        jax.ShapeDtypeStruct((B, S, ATT_WIDTH), F32),
        jax.ShapeDtypeStruct((B, H, nb, dh), F32),
        jax.ShapeDtypeStruct((B, S, ATT_WIDTH), BF16),
        jax.ShapeDtypeStruct((B, S, W), BF16),
        jax.ShapeDtypeStruct((B, S, M_WIDTH), BF16),
        jax.ShapeDtypeStruct((B, S, M_WIDTH), BF16),
        jax.ShapeDtypeStruct((B, S, M_WIDTH), BF16),
        jax.ShapeDtypeStruct((B, S, D), BF16),
        jax.ShapeDtypeStruct((B, S, D), BF16),
        jax.ShapeDtypeStruct((B, 2 * M_HEADS, S), F32),
        jax.ShapeDtypeStruct((B, nb, M_WIDTH, tm), BF16),
        jax.ShapeDtypeStruct((B, S, 2 * M_HEADS), F32),
        jax.ShapeDtypeStruct((B, V7X_SUBLANES, W), F32),
    ]
    out_specs = [
        blk,
        pl.BlockSpec((1, 1, H, tm, 2 * dh), lambda b, j: (b, j, 0, 0, 0)),
        pl.BlockSpec((1, 1, H * _MOBA_V_ROWS, tm), lambda b, j: (b, j, 0, 0)),
        tok(ATT_WIDTH), tok(ATT_WIDTH),
        pl.BlockSpec((1, H, nb, dh), lambda b, j: (b, 0, 0, 0)),
        tok(ATT_WIDTH), tok(2 * M_QK_WIDTH), tok(M_WIDTH), tok(M_WIDTH), tok(M_WIDTH), tok(D), tok(D),
        pl.BlockSpec((1, 2 * M_HEADS, tm), lambda b, j: (b, 0, j)),
        pl.BlockSpec((1, 1, M_WIDTH, tm), lambda b, j: (b, j, 0, 0)),
        tok(2 * M_HEADS),
        pl.BlockSpec((1, V7X_SUBLANES, W), lambda b, j: (b, 0, 0)),
    ]
    in_specs = [
        pl.BlockSpec((1, tm, D), lambda b, j: (b, j, 0)),
        pl.BlockSpec((1, 3, D), lambda b, j: (b, 0, 0)),
        pl.BlockSpec((1, D), const),
        pl.BlockSpec(w_main.shape, const, pipeline_mode=pl.Buffered(1)),
        pl.BlockSpec(w_ifT.shape, const),
        pl.BlockSpec((2 * M_HEADS, 1), const),
        pl.BlockSpec((1, 2 * M_HEADS), const),
        pl.BlockSpec((M_CONV, W), const),
        pl.BlockSpec((1, W), const),
    ]
    return pl.pallas_call(
        _inproj_kernel,
        grid=(B, nb),
        in_specs=in_specs,
        out_specs=out_specs,
        out_shape=out_shape,
        scratch_shapes=[pltpu.VMEM((V7X_SUBLANES, W), F32)],
        compiler_params=_params("arbitrary", "arbitrary"),
        name="prompt_inproj",
    )(x, mod3, norm_g.reshape(1, D), w_main, w_ifT, b_if.reshape(2 * M_HEADS, 1),
      b_if.reshape(1, 2 * M_HEADS), w_conv, b_conv.reshape(1, W))


_MOBA_HEADS_PER_STEP = 8
_MOBA_HEAD_GROUP = 8
_MOBA_V_PAD = 16
_MOBA_V_ROWS = ATT_HEAD_DIM + _MOBA_V_PAD


def _moba_phases(j):
    return j + 2 * ((j * j) // 4)


def _moba_kernel(pt_ref, qT_ref, kh_ref, vT_ref, kmean_ref, ck_ref, qb_ref, o_ref, sc_ref,
                 bias_ref, qs_ref, acc_ref, pbuf, qbuf, psem, qsem):
    b = pl.program_id(0)
    j = pl.program_id(2)
    nbq = pl.num_programs(2)
    nseq, n_pages = pt_ref.shape
    cpb = n_pages // _PAGES_PER_CHUNK
    bpc = _PAGES_PER_CHUNK // _PAGES_PER_BLOCK
    n_chunks = nseq * cpb
    ring = pbuf.shape[0]
    ahead = ring - 2

    def chunk_copy(g, i):
        page = pt_ref[g // cpb, (g % cpb) * _PAGES_PER_CHUNK + i]
        slot = g % ring
        return pltpu.make_async_copy(ck_ref.at[0, page], pbuf.at[slot, i], psem.at[slot])

    qslots = qbuf.shape[0]

    def q_copy(seq):
        s = seq % qslots
        return pltpu.make_async_copy(qb_ref.at[seq], qbuf.at[s], qsem.at[s])

    def start_chunk(g):
        @pl.when(g < n_chunks)
        def _():
            for i in range(_PAGES_PER_CHUNK):
                chunk_copy(g, i).start()

    step = b * pl.num_programs(1) + pl.program_id(1)

    @pl.when((step == 0) & (j == 0))
    def _():
        sc_ref[...] = jnp.zeros_like(sc_ref)
        q_copy(0).start()
        for g in range(ahead):
            start_chunk(g)

    lane_blk = lax.broadcasted_iota(jnp.int32, sc_ref.shape[1:], 1)

    def page_sync(g):
        live = g < n_chunks
        gc = jnp.minimum(g, n_chunks - 1)
        seq = gc // cpb
        ci = gc % cpb

        @pl.when(live & (ci == 0))
        def _():
            q_copy(seq).wait()

        @pl.when(live & (ci == 0) & (seq + 1 < nseq))
        def _():
            q_copy(seq + 1).start()

        start_chunk(g + ahead)

        @pl.when(live)
        def _():
            for i in range(_PAGES_PER_CHUNK):
                chunk_copy(gc, i).wait()

    def page_scores(g):
        gc = jnp.minimum(g, n_chunks - 1)
        seq = gc // cpb
        ci = gc % cpb
        slot = gc % ring
        qb = qbuf[seq % qslots]
        row = sc_ref[seq]
        for blk in range(bpc):
            p = pbuf[slot, blk * _PAGES_PER_BLOCK]
            for i in range(1, _PAGES_PER_BLOCK):
                p = p + pbuf[slot, blk * _PAGES_PER_BLOCK + i]
            part = jnp.sum(p * qb, axis=1)
            val = jnp.sum(part, axis=1, keepdims=True)
            row = jnp.where(lane_blk == ci * bpc + blk, val, row)
        sc_ref[seq] = row

    phase0 = step * _moba_phases(nbq) + _moba_phases(j)
    hp = kmean_ref.shape[1]
    nb = kmean_ref.shape[2]
    tq = qT_ref.shape[3]
    dh = ATT_HEAD_DIM
    vr = _MOBA_V_ROWS
    qTs = [qT_ref[0, 0, hh * dh:(hh + 1) * dh, :] for hh in range(hp)]
    for hh in range(hp):
        for slot in range(2):
            qs_ref[slot, hh, 0:dh, :] = qTs[hh]
            qs_ref[slot, hh, dh:2 * dh, :] = jnp.zeros((dh, tq), BF16)

    blk = lax.broadcasted_iota(jnp.int32, (nb, tq), 0)
    blk_f = blk.astype(F32)
    past = blk < j
    for hh in range(hp):
        sg = sum(jnp.dot(part, qTs[hh], preferred_element_type=F32)
                 for part in _split3(kmean_ref[0, hh]))
        work = jnp.where(past, sg, NEG_SCORE)
        bias = jnp.full((nb, tq), NEG_SCORE, F32)
        for _ in range(MOBA_TOPK):
            mx = jnp.max(work, axis=0, keepdims=True)
            first = jnp.min(jnp.where(work == mx, blk_f, float(nb)), axis=0, keepdims=True)
            chosen = blk_f == first
            bias = jnp.where(chosen & past, 0.0, bias)
            work = jnp.where(chosen, NEG_SCORE, work)
        bias_ref[hh] = bias

    kpos = lax.broadcasted_iota(jnp.int32, (tq, tq), 0)
    qpos = lax.broadcasted_iota(jnp.int32, (tq, tq), 1)
    causal = kpos <= qpos

    def score(n, slot, hh):
        return jnp.dot(kh_ref[0, n, hh], qs_ref[slot, hh], preferred_element_type=F32)

    def value(n, hh, p):
        return jnp.dot(vT_ref[0, n, hh * vr:(hh + 1) * vr, :], p, preferred_element_type=F32)

    def scores(n, slot=0):
        return [score(n, slot, hh) for hh in range(hp)]

    def values(n, ps):
        return [value(n, hh, ps[hh]) for hh in range(hp)]

    ss = [jnp.where(causal, s, NEG_SCORE) for s in scores(j)]
    ms = [jnp.max(s, axis=0, keepdims=True) for s in ss]
    pbs = [jnp.exp2((s - m).astype(BF16)) for s, m in zip(ss, ms)]
    for hh, x in enumerate(values(j, pbs)):
        acc_ref[hh] = x
    page_sync(phase0)
    page_scores(phase0)

    row0 = lax.broadcasted_iota(jnp.int32, (_MOBA_V_PAD, tq), 0) == 0

    def body(t, ms):
        n0 = 2 * t
        real1 = n0 + 1 < j
        n1 = jnp.minimum(n0 + 1, j - 1)
        g = phase0 + 1 + 2 * t
        page_sync(g)
        page_sync(g + 1)
        for hh in range(hp):
            mask0 = jnp.where(row0, bias_ref[hh, pl.ds(n0, 1), :], 0.0)
            qs_ref[0, hh, dh:dh + _MOBA_V_PAD, :] = mask0.astype(BF16)
            bias1 = jnp.where(real1, bias_ref[hh, pl.ds(n1, 1), :], NEG_SCORE)
            qs_ref[1, hh, dh:dh + _MOBA_V_PAD, :] = jnp.where(row0, bias1, 0.0).astype(BF16)
        page_scores(g)
        page_scores(g + 1)
        m_out = []
        for h0 in range(0, hp, _MOBA_HEAD_GROUP):
            hs = range(h0, min(h0 + _MOBA_HEAD_GROUP, hp))
            ss0 = [score(n0, 0, hh) for hh in hs]
            ss1 = [score(n1, 1, hh) for hh in hs]
            m0 = [ms[hh] for hh in hs]
            m1 = [jnp.maximum(m, jnp.max(s, axis=0, keepdims=True)) for m, s in zip(m0, ss0)]
            a0 = [jnp.exp2(m - mn) for m, mn in zip(m0, m1)]
            pv0 = [value(n0, hh, jnp.exp2((s - mn).astype(BF16)))
                   for hh, s, mn in zip(hs, ss0, m1)]
            m2 = [jnp.maximum(m, jnp.max(s, axis=0, keepdims=True)) for m, s in zip(m1, ss1)]
            a1 = [jnp.exp2(m - mn) for m, mn in zip(m1, m2)]
            pv1 = [value(n1, hh, jnp.exp2((s - mn).astype(BF16)))
                   for hh, s, mn in zip(hs, ss1, m2)]
            for i, hh in enumerate(hs):
                acc_ref[hh] = (a0[i] * acc_ref[hh] + pv0[i]) * a1[i] + pv1[i]
            m_out += m2
        return tuple(m_out)

    lax.fori_loop(0, (j + 1) // 2, body, tuple(ms))
    for hh in range(hp):
        acc = acc_ref[hh]
        o_ref[0, 0, hh * dh:(hh + 1) * dh, :] = (acc[0:dh] / acc[dh:dh + 1]).astype(o_ref.dtype)


def _moba_prompt(qT, kh, vT, kmean, page_table, cache_kT, q_lanes):
    B, nb, _, tq = qT.shape
    H, dh, hp = ATT_HEADS, ATT_HEAD_DIM, _MOBA_HEADS_PER_STEP
    nseq, n_pages = page_table.shape
    ps = cache_kT.shape[-1]
    assert n_pages % _PAGES_PER_CHUNK == 0 and n_pages // _PAGES_PER_BLOCK <= V7X_LANES
    n_chunks = nseq * (n_pages // _PAGES_PER_CHUNK)
    assert _PAGE_RING - 2 <= n_chunks <= B * (H // hp) * _moba_phases(nb), \
        "one page chunk per key-block phase"
    grid_spec = pltpu.PrefetchScalarGridSpec(
        num_scalar_prefetch=1,
        grid=(B, H // hp, nb),
        in_specs=[
            pl.BlockSpec((1, 1, hp * dh, tq), lambda b, g, j, pt: (b, j, g, 0)),
            pl.BlockSpec((1, nb, hp, tq, 2 * dh), lambda b, g, j, pt: (b, 0, g, 0, 0)),
            pl.BlockSpec((1, nb, hp * _MOBA_V_ROWS, tq), lambda b, g, j, pt: (b, 0, g, 0)),
            pl.BlockSpec((1, hp, nb, dh), lambda b, g, j, pt: (b, g, 0, 0)),
            pl.BlockSpec(memory_space=pl.ANY),
            pl.BlockSpec(memory_space=pl.ANY),
        ],
        out_specs=[
            pl.BlockSpec((1, 1, hp * dh, tq), lambda b, g, j, pt: (b, j, g, 0)),
            pl.BlockSpec((nseq, H, V7X_LANES), lambda b, g, j, pt: (0, 0, 0)),
        ],
        scratch_shapes=[
            pltpu.VMEM((hp, nb, tq), F32), pltpu.VMEM((2, hp, 2 * dh, tq), BF16),
            pltpu.VMEM((hp, _MOBA_V_ROWS, tq), F32),
            pltpu.VMEM((_PAGE_RING, _PAGES_PER_CHUNK, H, dh, ps), F32),
            pltpu.VMEM((3, H, dh, ps), F32),
            pltpu.SemaphoreType.DMA((_PAGE_RING,)), pltpu.SemaphoreType.DMA((3,)),
        ],
    )
    return pl.pallas_call(
        _moba_kernel,
        grid_spec=grid_spec,
        out_shape=[jax.ShapeDtypeStruct((B, nb, ATT_WIDTH, tq), BF16),
                   jax.ShapeDtypeStruct((nseq, H, V7X_LANES), F32)],
        compiler_params=_params("arbitrary", "arbitrary", "arbitrary"),
        name="prompt_moba",
    )(page_table, qT, kh, vT, kmean, cache_kT, q_lanes)


def _log_sigmoid(x):
    return jnp.minimum(x, 0.0) - jnp.log1p(jnp.exp(-jnp.abs(x)))


def _causal_conv(x, tail_ref, w_ref, b_ref):
    L = x.shape[0]
    prev = tail_ref[...]
    row = lax.broadcasted_iota(jnp.int32, (V7X_SUBLANES, x.shape[1]), 0)
    acc = x * w_ref[M_CONV - 1:M_CONV, :] + b_ref[...]
    for d in range(1, M_CONV):
        r = pltpu.roll(x, d, axis=0)
        head = jnp.where(row < d, pltpu.roll(prev, d, axis=0), r[:V7X_SUBLANES])
        shifted = jnp.concatenate([head, r[V7X_SUBLANES:]], axis=0)
        acc += shifted * w_ref[M_CONV - 1 - d:M_CONV - d, :]
    tail_ref[...] = x[L - V7X_SUBLANES:, :]
    return acc


def _mlstm_kernel(qk_ref, v_ref, vT_ref, g_ref, gc_ref, o_ref, ng_ref,
                  h_ref, C_out, n_out, m_out,
                  C_s, n_s, m_s):
    c = pl.program_id(1)
    L = qk_ref.shape[1]
    MH, dk, dv = M_HEADS, M_QK_DIM, M_V_DIM
    nt = (((1,), (1,)), ((), ()))

    @pl.when(c == 0)
    def _():
        C_s[...] = jnp.zeros_like(C_s)
        n_s[...] = jnp.zeros_like(n_s)
        m_s[...] = jnp.zeros_like(m_s)

    t_idx = lax.broadcasted_iota(jnp.int32, (L, L), 0)
    s_idx = lax.broadcasted_iota(jnp.int32, (L, L), 1)
    causal = s_idx <= t_idx
    tri = jnp.where(causal, 1.0, 0.0).astype(BF16)
    g_rows = g_ref[0]
    b_rows = sum(lax.dot_general(part, tri, nt, preferred_element_type=F32)
                 for part in _split3(_log_sigmoid(g_rows)))
    b_cols = sum(jnp.dot(tri, part, preferred_element_type=F32)
                 for part in _split3(_log_sigmoid(gc_ref[0])))

    for hd in range(MH):
        qb = qk_ref[0, :, hd * dk:(hd + 1) * dk]
        kb = qk_ref[0, :, M_QK_WIDTH + hd * dk:M_QK_WIDTH + (hd + 1) * dk]
        vsl = slice(hd * dv, (hd + 1) * dv)
        b_row = b_rows[MH + hd:MH + hd + 1]
        c_row = g_rows[hd:hd + 1] - b_row
        b_col = b_cols[:, MH + hd:MH + hd + 1]
        m_prev = m_s[hd:hd + 1, 0:1]
        M_col = jnp.maximum(jnp.max(jnp.where(causal, c_row, NEG_SCORE), axis=1, keepdims=True),
                            m_prev)
        M_last = jnp.maximum(jnp.max(c_row, axis=1, keepdims=True), m_prev)
        w = jnp.where(causal, jnp.exp(c_row - M_col), 0.0)
        a_col = jnp.exp(m_prev - M_col)

        s = lax.dot_general(qb, kb, nt, preferred_element_type=F32) * w
        inter = lax.dot_general(qb, C_s[hd].astype(BF16), nt, preferred_element_type=F32)
        num = a_col * inter + jnp.dot(s.astype(BF16), v_ref[0, :, vsl], preferred_element_type=F32)
        n_b = n_s[hd:hd + 1, :].astype(BF16).astype(F32)
        qn = jnp.sum(qb.astype(F32) * n_b, axis=1, keepdims=True)
        den = a_col * qn + jnp.sum(s, axis=1, keepdims=True)
        hh = num / jnp.maximum(jnp.abs(den), jnp.exp(-(b_col + M_col)))
        hh = hh * lax.rsqrt(jnp.mean(hh * hh, axis=-1, keepdims=True) + NORM_EPS)
        gate = jax.nn.sigmoid(o_ref[0, :, vsl].astype(F32))
        h_ref[0, :, vsl] = (hh * ng_ref[:, vsl] * gate).astype(h_ref.dtype)

        ws_row = jnp.exp(c_row - M_last)
        a_end = jnp.exp(m_prev - M_last)
        vwT = (vT_ref[0, 0, vsl, :].astype(F32) * ws_row).astype(BF16)
        C_s[hd] = a_end * C_s[hd] + jnp.dot(vwT, kb, preferred_element_type=F32)
        ws8 = jnp.broadcast_to(ws_row, (V7X_SUBLANES, L)).astype(BF16)
        n_s[hd:hd + 1, :] = (a_end * n_s[hd:hd + 1, :]
                             + jnp.dot(ws8, kb, preferred_element_type=F32)[0:1])
        m_s[hd:hd + 1, :] = jnp.broadcast_to(b_row[:, L - 1:L] + M_last, (1, m_s.shape[1]))

    @pl.when(c == pl.num_programs(1) - 1)
    def _():
        C_out[0] = C_s[...]
        n_out[0] = n_s[...]
        m_out[0] = m_s[...]


def _mlstm_prompt(qk, mv, mvT, gT, gc, mo, m_norm_g):
    B, S, W = qk.shape
    L = mvT.shape[3]
    nc = S // L
    MH, dk, dv = M_HEADS, M_QK_DIM, M_V_DIM
    const = lambda b, c: (0, 0)
    tok = lambda w: pl.BlockSpec((1, L, w), lambda b, c: (b, c, 0))
    in_specs = [
        tok(W), tok(M_WIDTH),
        pl.BlockSpec((1, 1, M_WIDTH, L), lambda b, c: (b, c, 0, 0)),
        pl.BlockSpec((1, 2 * MH, L), lambda b, c: (b, 0, c)),
        tok(2 * MH), tok(M_WIDTH),
        pl.BlockSpec((1, M_WIDTH), const),
    ]
    out_shape = [
        jax.ShapeDtypeStruct((B, S, M_WIDTH), BF16),
        jax.ShapeDtypeStruct((B, MH, dv, dk), F32),
        jax.ShapeDtypeStruct((B, V7X_SUBLANES, dk), F32),
        jax.ShapeDtypeStruct((B, V7X_SUBLANES, V7X_LANES), F32),
    ]
    out_specs = [
        tok(M_WIDTH),
        pl.BlockSpec((1, MH, dv, dk), lambda b, c: (b, 0, 0, 0)),
        pl.BlockSpec((1, V7X_SUBLANES, dk), lambda b, c: (b, 0, 0)),
        pl.BlockSpec((1, V7X_SUBLANES, V7X_LANES), lambda b, c: (b, 0, 0)),
    ]
    scratch = [
        pltpu.VMEM((MH, dv, dk), F32), pltpu.VMEM((V7X_SUBLANES, dk), F32),
        pltpu.VMEM((V7X_SUBLANES, V7X_LANES), F32),
    ]
    return pl.pallas_call(
        _mlstm_kernel,
        grid=(B, nc),
        in_specs=in_specs,
        out_specs=out_specs,
        out_shape=out_shape,
        scratch_shapes=scratch,
        compiler_params=_params("arbitrary", "arbitrary"),
        name="prompt_mlstm",
    )(qk, mv, mvT, gT, gc, mo, m_norm_g.reshape(1, M_WIDTH))


def _merge_kernel(yaT_ref, az_ref, hm_ref, mz_ref, ga_ref, gm_ref, x_ref, mod_ref,
                  watt_ref, wml_ref, wout_ref, fg_ref, y_ref, watt_s, wml_s, wout_s):
    @pl.when((pl.program_id(0) == 0) & (pl.program_id(1) == 0))
    def _():
        watt_s[...] = watt_ref[...].astype(BF16)
        wml_s[...] = wml_ref[...].astype(BF16)
        wout_s[...] = wout_ref[...].astype(BF16)

    ya = yaT_ref[0, 0].astype(F32).T * _silu(az_ref[0].astype(F32))
    ya = jnp.dot(ya.astype(BF16), watt_s[...], preferred_element_type=F32)
    ym = hm_ref[0].astype(F32) * _silu(mz_ref[0].astype(F32))
    ym = jnp.dot(ym.astype(BF16), wml_s[...], preferred_element_type=F32)
    u = (jax.nn.sigmoid(ga_ref[0].astype(F32)) * ya
         + jax.nn.sigmoid(gm_ref[0].astype(F32)) * ym)
    upd = jnp.dot(u.astype(BF16), wout_s[...], preferred_element_type=F32)
    y = x_ref[0] + mod_ref[0, 2:3, :] * upd
    y = y * lax.rsqrt(jnp.mean(y * y, axis=-1, keepdims=True) + NORM_EPS)
    y_ref[0] = y * fg_ref[...]


def _merge_prompt(yaT, az, hm, mz, ga, gm, x, mod3, w_att, w_mlstm, w_out, final_g):
    B, S, D = x.shape
    tm = yaT.shape[3]
    nb = S // tm
    const = lambda b, j: (0, 0)
    tok = lambda w: pl.BlockSpec((1, tm, w), lambda b, j: (b, j, 0))
    return pl.pallas_call(
        _merge_kernel,
        grid=(B, nb),
        in_specs=[
            pl.BlockSpec((1, 1, ATT_WIDTH, tm), lambda b, j: (b, j, 0, 0)),
            tok(ATT_WIDTH), tok(M_WIDTH), tok(M_WIDTH), tok(D), tok(D), tok(D),
            pl.BlockSpec((1, 3, D), lambda b, j: (b, 0, 0)),
            pl.BlockSpec(w_att.shape, const, pipeline_mode=pl.Buffered(1)),
            pl.BlockSpec(w_mlstm.shape, const, pipeline_mode=pl.Buffered(1)),
            pl.BlockSpec(w_out.shape, const, pipeline_mode=pl.Buffered(1)),
            pl.BlockSpec((1, D), const),
        ],
        out_specs=tok(D),
        out_shape=jax.ShapeDtypeStruct((B, S, D), F32),
        scratch_shapes=[pltpu.VMEM(w_att.shape, BF16), pltpu.VMEM(w_mlstm.shape, BF16),
                        pltpu.VMEM(w_out.shape, BF16)],
        compiler_params=_params("arbitrary", "arbitrary"),
        name="prompt_merge",
    )(yaT, az, hm, mz, ga, gm, x, mod3, w_att, w_mlstm, w_out, final_g.reshape(1, D))


def _prompt_layer(x, mod3, norm_g, w_main, w_ifT, b_if, w_conv, b_conv, m_norm_g,
                  w_att, w_mlstm, w_out, final_g, page_table, cache_kT, q_lanes):
    B, S, _ = x.shape
    (qT, kh, vT, kf, vf, kmean, az, qk, mv, mo, mz, ga, gm, gT, mvT, gc, ctail) = _inproj(
        x, mod3, norm_g, w_main, w_ifT, b_if, w_conv, b_conv)
    yaT, block_scores = _moba_prompt(qT, kh, vT, kmean, page_table, cache_kT, q_lanes)
    hm, C, n, m = _mlstm_prompt(qk, mv, mvT, gT, gc, mo, m_norm_g)
    y = _merge_prompt(yaT, az, hm, mz, ga, gm, x, mod3, w_att, w_mlstm, w_out, final_g)
    k_rows = kf.reshape(B, S, ATT_HEADS, ATT_HEAD_DIM)
    v_rows = vf.reshape(B, S, ATT_HEADS, ATT_HEAD_DIM)
    conv_new = ctail[:, V7X_SUBLANES - (M_CONV - 1):, :]
    return y, k_rows, v_rows, C, n[:, :M_HEADS, :], m[:, :M_HEADS, 0], conv_new, block_scores


def _inproj_sample_kernel(x_ref, mod_ref, ng_ref, w_ref, wif_ref, bif_ref, z_ref, g_ref, ql_ref):
    x = x_ref[...]
    n, d = x.shape
    xn = x * lax.rsqrt(jnp.mean(x * x, axis=-1, keepdims=True) + NORM_EPS) * ng_ref[...]
    h = (xn * (1.0 + mod_ref[:, d:2 * d]) + mod_ref[:, 0:d]).astype(BF16)
    z = jnp.dot(h, w_ref[...], preferred_element_type=F32)
    z_ref[...] = z
    g = lax.dot_general(h, wif_ref[...], (((1,), (1,)), ((), ())), preferred_element_type=F32)
    g_ref[...] = g + bif_ref[...]

    @pl.when(pl.program_id(0) == _C_AQ // w_ref.shape[1])
    def _():
        q = z[:, _C_AQ % w_ref.shape[1]:_C_AQ % w_ref.shape[1] + ATT_WIDTH]
        q = jnp.concatenate([q, jnp.zeros((V7X_LANES - n, ATT_WIDTH), F32)], axis=0)
        qT = q.T
        for s in range(n):
            ql_ref[s] = jnp.broadcast_to(qT[:, s:s + 1], ql_ref.shape[1:])


def _inproj_sample(x, mod, norm_g, w_main, w_ifT, b_if):
    n, d = x.shape
    bn = 13 * V7X_LANES
    assert w_main.shape[1] % bn == 0 and n <= V7X_LANES and _C_AQ % bn + ATT_WIDTH <= bn
    const = lambda i: (0, 0)
    return pl.pallas_call(
        _inproj_sample_kernel,
        grid=(w_main.shape[1] // bn,),
        in_specs=[pl.BlockSpec((n, d), const),
                  pl.BlockSpec(mod.shape, const),
                  pl.BlockSpec((1, d), const),
                  pl.BlockSpec((d, bn), lambda i: (0, i)),
                  pl.BlockSpec(w_ifT.shape, const),
                  pl.BlockSpec((1, 2 * M_HEADS), const)],
        out_specs=[pl.BlockSpec((n, bn), lambda i: (0, i)),
                   pl.BlockSpec((n, 2 * M_HEADS), const),
                   pl.BlockSpec((n, ATT_WIDTH, PAGE_SIZE), lambda i: (0, 0, 0))],
        out_shape=[jax.ShapeDtypeStruct((n, w_main.shape[1]), F32),
                   jax.ShapeDtypeStruct((n, 2 * M_HEADS), F32),
                   jax.ShapeDtypeStruct((n, ATT_WIDTH, PAGE_SIZE), F32)],
        compiler_params=_params("arbitrary"),
        name="sample_inproj",
    )(x, mod, norm_g.reshape(1, d), w_main, w_ifT, b_if.reshape(1, 2 * M_HEADS))


_PAGES_PER_CHUNK = 8
_PAGE_RING = 5
_PAGES_PER_BLOCK = MOBA_BLOCK // PAGE_SIZE
_SAMPLE_SEQS_PER_STEP = 1


def _attend_sample_kernel(pt_ref, sel_ref, ck_ref, cv_ref, q_ref, ql_ref, kn_ref, vn_ref, o_ref,
                          kbuf, vbuf, sem):
    b = pl.program_id(0)
    nseq = pl.num_programs(0)
    H, dh = ATT_HEADS, ATT_HEAD_DIM
    npg = MOBA_TOPK * _PAGES_PER_BLOCK

    def copies(bb, slot):
        out = []
        for hh in range(H):
            for r in range(MOBA_TOPK):
                blk = sel_ref[(bb * H + hh) * MOBA_TOPK + r]
                for i in range(_PAGES_PER_BLOCK):
                    page = pt_ref[bb, blk * _PAGES_PER_BLOCK + i]
                    pg = r * _PAGES_PER_BLOCK + i
                    out.append(pltpu.make_async_copy(
                        ck_ref.at[0, page, hh], kbuf.at[slot, hh, pg], sem.at[0, slot]))
                    out.append(pltpu.make_async_copy(
                        cv_ref.at[0, page, hh], vbuf.at[slot, hh, pg], sem.at[1, slot]))
        return out

    def start_all(bb, slot):
        for i, cp in enumerate(copies(bb, slot)):
            cp.start(priority=i % 2)

    @pl.when(b == 0)
    def _():
        start_all(0, 0)

    slot = b % 2

    @pl.when(b + 1 < nseq)
    def _():
        start_all(b + 1, 1 - slot)

    for cp in copies(b, slot):
        cp.wait()

    @pl.when(b == 0)
    def _():
        o_ref[...] = jnp.zeros_like(o_ref)

    scale = ATT_HEAD_DIM ** -0.5
    lane = lax.broadcasted_iota(jnp.int32, (dh, ql_ref.shape[3]), 1)
    eye = lax.broadcasted_iota(jnp.int32, lane.shape, 0) == lane
    for hh in range(H):
        qb = ql_ref[0, hh]
        s = jnp.concatenate([jnp.sum(kbuf[slot, hh, pg] * qb, axis=0, keepdims=True)
                             for pg in range(npg)], axis=1) * scale
        s_new = jnp.sum(q_ref[0, hh:hh + 1, :] * kn_ref[0, hh:hh + 1, :], axis=1,
                        keepdims=True) * scale
        m = jnp.maximum(jnp.max(s, axis=1, keepdims=True), s_new)
        p = jnp.exp(s - m)
        p_new = jnp.exp(s_new - m)
        l = jnp.sum(p, axis=1, keepdims=True) + p_new
        tile = vbuf[slot, hh, 0] * p[:, 0:PAGE_SIZE]
        for pg in range(1, npg):
            tile = tile + vbuf[slot, hh, pg] * p[:, pg * PAGE_SIZE:(pg + 1) * PAGE_SIZE]
        vn_b = jnp.broadcast_to(
            jnp.concatenate([vn_ref[0, hh:hh + 1, :], jnp.zeros((1, lane.shape[1] - dh), F32)],
                            axis=1), lane.shape)
        col = jnp.sum(tile + jnp.where(eye, vn_b * p_new, 0.0), axis=1, keepdims=True) / l
        sl = slice(hh * dh, (hh + 1) * dh)
        o_ref[sl, :] = jnp.where(lane == b, col, o_ref[sl, :])


def _attend_sample(cache_k, cache_v, page_table, sel_flat, q, q_lanes, k_new, v_new):
    nseq, n_pages = page_table.shape
    H, dh = ATT_HEADS, ATT_HEAD_DIM
    npg = MOBA_TOPK * _PAGES_PER_BLOCK
    ps = cache_k.shape[-1]
    assert nseq <= V7X_LANES and q_lanes.shape[3] == ps
    row = lambda: pl.BlockSpec((1, H, dh), lambda b, pt, sel: (b, 0, 0))
    grid_spec = pltpu.PrefetchScalarGridSpec(
        num_scalar_prefetch=2,
        grid=(nseq,),
        in_specs=[pl.BlockSpec(memory_space=pl.ANY),
                  pl.BlockSpec(memory_space=pl.ANY),
                  row(),
                  pl.BlockSpec((1, H, dh, ps), lambda b, pt, sel: (b, 0, 0, 0)),
                  row(), row()],
        out_specs=pl.BlockSpec((ATT_WIDTH, V7X_LANES), lambda b, pt, sel: (0, 0)),
        scratch_shapes=[pltpu.VMEM((2, H, npg, dh, ps), F32),
                        pltpu.VMEM((2, H, npg, dh, ps), F32),
                        pltpu.SemaphoreType.DMA((2, 2))],
    )
    return pl.pallas_call(
        _attend_sample_kernel,
        grid_spec=grid_spec,
        out_shape=jax.ShapeDtypeStruct((ATT_WIDTH, V7X_LANES), F32),
        compiler_params=_params("arbitrary"),
        name="sample_attend",
    )(page_table, sel_flat, cache_k, cache_v, q, q_lanes, k_new, v_new)


def _mlstm_sample_kernel(qk_ref, v_ref, o_ref, g_ref, conv_ref, C_ref, n_ref, wc_ref, bc_ref, ng_ref,
                         h_ref, C_out, n_out, m_out, conv_out):
    for sq in range(qk_ref.shape[0]):
        x = qk_ref[sq]
        conv = conv_ref[sq]
        acc = x * wc_ref[M_CONV - 1:M_CONV, :] + bc_ref[...]
        for jj in range(M_CONV - 1):
            acc += conv[jj:jj + 1, :] * wc_ref[jj:jj + 1, :]
        qk = _silu(acc)
        conv_out[sq] = jnp.concatenate([conv[1:], x], axis=0)
        g = g_ref[sq]
        MH = M_HEADS
        bb4 = _log_sigmoid(g[:, MH:2 * MH])
        m_prev4 = g[:, 2 * MH:3 * MH]
        cc4 = g[:, 0:MH] - bb4
        M4 = jnp.maximum(m_prev4, cc4)
        w4 = jnp.exp(cc4 - M4)
        a4 = jnp.exp(m_prev4 - M4)
        floor4 = jnp.exp(-(bb4 + M4))
        m_new4 = bb4 + M4
        nt = (((1,), (1,)), ((), ()))
        heads = range(MH)
        qbs = [qk[:, hh * M_QK_DIM:(hh + 1) * M_QK_DIM].astype(BF16) for hh in heads]
        ks = [qk[:, M_QK_WIDTH + hh * M_QK_DIM:M_QK_WIDTH + (hh + 1) * M_QK_DIM] * (M_QK_DIM ** -0.5)
              for hh in heads]
        vs = [v_ref[sq][:, hh * M_V_DIM:(hh + 1) * M_V_DIM] for hh in heads]
        Cs = [C_ref[sq, hh] for hh in heads]
        ns = [n_ref[sq, hh:hh + 1, :] for hh in heads]
        inters = [lax.dot_general(jnp.broadcast_to(qbs[hh], (V7X_SUBLANES, M_QK_DIM)),
                                  Cs[hh].astype(BF16), nt, preferred_element_type=F32)[0:1]
                  for hh in heads]
        wv_cols = []
        for hh in heads:
            wv = jnp.broadcast_to(w4[:, hh:hh + 1] * vs[hh], (V7X_SUBLANES, M_V_DIM))
            wv = jnp.concatenate([wv, jnp.zeros((V7X_LANES - V7X_SUBLANES, M_V_DIM), F32)], axis=0)
            wv_cols.append(wv.T[:, 0:1])
        for hh in heads:
            w, a = w4[:, hh:hh + 1], a4[:, hh:hh + 1]
            qf = qbs[hh].astype(F32)
            s = jnp.sum(qf * ks[hh].astype(BF16).astype(F32), axis=1, keepdims=True) * w
            vb = vs[hh].astype(BF16).astype(F32)
            num = a * inters[hh] + s.astype(BF16).astype(F32) * vb
            den = a * jnp.sum(qf * ns[hh].astype(BF16).astype(F32), axis=1, keepdims=True) + s
            hv = num / jnp.maximum(jnp.abs(den), floor4[:, hh:hh + 1])
            hv = hv * lax.rsqrt(jnp.mean(hv * hv, axis=-1, keepdims=True) + NORM_EPS)
            sl = slice(hh * M_V_DIM, (hh + 1) * M_V_DIM)
            h_ref[sq, :, sl] = hv * ng_ref[:, sl] * jax.nn.sigmoid(o_ref[sq][:, sl])
        for hh in heads:
            w, a = w4[:, hh:hh + 1], a4[:, hh:hh + 1]
            C_out[sq, hh] = a * Cs[hh] + wv_cols[hh] * ks[hh]
            n_out[sq, hh:hh + 1, :] = a * ns[hh] + w * ks[hh]
        m_out[sq] = jnp.concatenate(
            [m_new4, jnp.zeros((1, V7X_LANES - MH), F32)], axis=1)


def _mlstm_sample(z3, g16, state_conv, state_C, state_n, w_conv, b_conv, m_norm_g):
    nseq = z3.shape[0]
    MH, dk, dv = M_HEADS, M_QK_DIM, M_V_DIM
    W = 2 * M_QK_WIDTH
    sb = _SAMPLE_SEQS_PER_STEP
    assert nseq % sb == 0
    const = lambda b: (0, 0)
    return pl.pallas_call(
        _mlstm_sample_kernel,
        grid=(nseq // sb,),
        in_specs=[
            pl.BlockSpec((sb, 1, W), lambda b: (b, 0, _C_MQK // W)),
            pl.BlockSpec((sb, 1, M_WIDTH), lambda b: (b, 0, _C_MV // M_WIDTH)),
            pl.BlockSpec((sb, 1, M_WIDTH), lambda b: (b, 0, _C_MO // M_WIDTH)),
            pl.BlockSpec((sb, 1, 4 * MH), lambda b: (b, 0, 0)),
            pl.BlockSpec((sb, M_CONV - 1, W), lambda b: (b, 0, 0)),
            pl.BlockSpec((sb, MH, dv, dk), lambda b: (b, 0, 0, 0)),
            pl.BlockSpec((sb, MH, dk), lambda b: (b, 0, 0)),
            pl.BlockSpec((M_CONV, W), const),
            pl.BlockSpec((1, W), const),
            pl.BlockSpec((1, M_WIDTH), const),
        ],
        out_specs=[
            pl.BlockSpec((sb, 1, M_WIDTH), lambda b: (b, 0, 0)),
            pl.BlockSpec((sb, MH, dv, dk), lambda b: (b, 0, 0, 0)),
            pl.BlockSpec((sb, MH, dk), lambda b: (b, 0, 0)),
            pl.BlockSpec((sb, 1, V7X_LANES), lambda b: (b, 0, 0)),
            pl.BlockSpec((sb, M_CONV - 1, W), lambda b: (b, 0, 0)),
        ],
        out_shape=[
            jax.ShapeDtypeStruct((nseq, 1, M_WIDTH), F32),
            jax.ShapeDtypeStruct((nseq, MH, dv, dk), F32),
            jax.ShapeDtypeStruct((nseq, MH, dk), F32),
            jax.ShapeDtypeStruct((nseq, 1, V7X_LANES), F32),
            jax.ShapeDtypeStruct((nseq, M_CONV - 1, W), F32),
        ],
        compiler_params=_params("arbitrary"),
        name="sample_mlstm",
    )(z3, z3, z3, g16, state_conv, state_C, state_n, w_conv, b_conv.reshape(1, W),
      m_norm_g.reshape(1, M_WIDTH))


def _merge_sample_kernel(ya_ref, hm_ref, z_ref, x_ref, mod_ref, watt_ref, wml_ref, wout_ref, fg_ref,
                         y_ref):
    d = x_ref.shape[1]
    n_rows = x_ref.shape[0]
    ya = ya_ref[...].T[0:n_rows, :] * _silu(z_ref[:, _C_AZ:_C_MQK])
    ya = jnp.dot(ya.astype(BF16), watt_ref[...].astype(BF16), preferred_element_type=F32)
    ym = hm_ref[...] * _silu(z_ref[:, _C_MZ:_C_MIF])
    ym = jnp.dot(ym.astype(BF16), wml_ref[...].astype(BF16), preferred_element_type=F32)
    tail = z_ref[:, _C_MIF:]
    ga = tail[:, _C_GA - _C_MIF:_C_GM - _C_MIF]
    gm = tail[:, _C_GM - _C_MIF:_C_END - _C_MIF]
    u = jax.nn.sigmoid(ga) * ya + jax.nn.sigmoid(gm) * ym
    upd = jnp.dot(u.astype(BF16), wout_ref[...].astype(BF16), preferred_element_type=F32)
    y = x_ref[...] + mod_ref[:, 2 * d:3 * d] * upd
    y = y * lax.rsqrt(jnp.mean(y * y, axis=-1, keepdims=True) + NORM_EPS)
    y_ref[...] = y * fg_ref[...]


def _merge_sample(ya, hm, z, x, mod, w_att, w_mlstm, w_out, final_g):
    n, d = x.shape
    full = lambda a: pl.BlockSpec(a.shape, lambda i: (0,) * a.ndim)
    fg = final_g.reshape(1, d)
    args = (ya, hm, z, x, mod, w_att, w_mlstm, w_out, fg)
    return pl.pallas_call(
        _merge_sample_kernel,
        grid=(1,),
        in_specs=[full(a) for a in args],
        out_specs=pl.BlockSpec((n, d), lambda i: (0, 0)),
        out_shape=jax.ShapeDtypeStruct((n, d), F32),
        compiler_params=_params("arbitrary"),
        name="sample_merge",
    )(*args)


def _gate_scores_kernel(sc_ref, idx_ref, *, nblk):
    sc = sc_ref[...] / MOBA_BLOCK
    lane = lax.broadcasted_iota(jnp.int32, sc.shape, 1)
    cnt = jnp.zeros(sc.shape, jnp.int32)
    for m in range(nblk):
        col = sc[:, m:m + 1]
        beats = (col > sc) | ((col == sc) & (lane > m))
        cnt += beats.astype(jnp.int32)
    out = jnp.zeros(sc.shape, jnp.int32)
    for r in range(MOBA_TOPK):
        idx = jnp.sum(jnp.where((cnt == r) & (lane < nblk), lane, 0), axis=1, keepdims=True)
        out = jnp.where(lane == r, idx, out)
    idx_ref[...] = out


def _gate_scores(block_scores, nblk):
    nseq, H, lanes = block_scores.shape
    rows = nseq * H
    out = pl.pallas_call(
        functools.partial(_gate_scores_kernel, nblk=nblk),
        grid=(1,),
        in_specs=[pl.BlockSpec((rows, lanes), lambda i: (0, 0))],
        out_specs=pl.BlockSpec((rows, lanes), lambda i: (0, 0)),
        out_shape=jax.ShapeDtypeStruct((rows, lanes), jnp.int32),
        compiler_params=_params("arbitrary"),
        name="sample_gate",
    )(block_scores.reshape(rows, lanes))
    return out.reshape(nseq, H, lanes)


def _sample_inputs(x, mod, norm_g, w_main, w_ifT, b_if):
    nseq = x.shape[0]
    H, dh = ATT_HEADS, ATT_HEAD_DIM
    z, g, q_lanes = _inproj_sample(x, mod, norm_g, w_main, w_ifT, b_if)
    q = z[:, _C_AQ:_C_AK].reshape(nseq, H, dh)
    return z, g, q, q_lanes.reshape(nseq, H, dh, PAGE_SIZE)


def _sample_layer(x, mod, z, g, q, q_lanes, block_scores, cache_kT, cache_vT, page_table,
                  state_C, state_n, state_m, state_conv,
                  w_conv, b_conv, m_norm_g, w_att, w_mlstm, w_out, final_g):
    nseq = x.shape[0]
    H, dh = ATT_HEADS, ATT_HEAD_DIM
    nblk = page_table.shape[1] // _PAGES_PER_BLOCK
    assert nblk >= MOBA_TOPK
    k_new = z[:, _C_AK:_C_AV].reshape(nseq, H, dh)
    v_new = z[:, _C_AV:_C_AZ].reshape(nseq, H, dh)
    sel = _gate_scores(block_scores, nblk)
    sel_flat = sel[:, :, :MOBA_TOPK].reshape(-1)
    yaT = _attend_sample(cache_kT, cache_vT, page_table, sel_flat, q, q_lanes, k_new, v_new)
    g16 = jnp.concatenate([g, state_m, jnp.zeros_like(state_m)], axis=1).reshape(nseq, 1, 4 * M_HEADS)
    hm, C, n, m, conv_new = _mlstm_sample(z.reshape(nseq, 1, -1), g16, state_conv, state_C, state_n,
                                          w_conv, b_conv, m_norm_g)
    y = _merge_sample(yaT, hm.reshape(nseq, M_WIDTH), z, x, mod,
                      w_att, w_mlstm, w_out, final_g)
    return (y.reshape(nseq, 1, -1), k_new.reshape(nseq, 1, H, dh), v_new.reshape(nseq, 1, H, dh),
            C, n, m[:, 0, :M_HEADS], conv_new)


def kernel(x_prompt, x_sample, cache_k, cache_v, state_C, state_n, state_m, state_conv, page_table,
           c_prompt, c_sample, w_ada, b_ada, norm_g, w_in, b_if, w_conv, b_conv, m_norm_g,
           w_att, w_mlstm, w_out, final_g):
    depth = w_in.shape[0]
    assert depth == 1, "single-layer step"
    B = x_prompt.shape[0]
    nseq = x_sample.shape[0]
    pad = (-(B + nseq)) % V7X_SUBLANES
    c_all = jnp.concatenate([c_prompt, c_sample, jnp.zeros((pad, c_prompt.shape[1]), F32)], axis=0)
    mod = _ada(c_all, w_ada[0], b_ada[0])
    mod_p = mod[:B].reshape(B, 3, D_MODEL)
    mod_s = mod[B:B + nseq]
    w_main, w_ifT = _pack_w_in(w_in[0])
    wa, wm, wo = w_att[0], w_mlstm[0], w_out[0]
    cache_kT = jnp.transpose(cache_k, (0, 1, 3, 4, 2))
    cache_vT = jnp.transpose(cache_v, (0, 1, 3, 4, 2))
    xs = x_sample[:, 0, :]
    z, g, q, q_lanes = _sample_inputs(xs, mod_s, norm_g[0], w_main, w_ifT, b_if[0])
    *outs_p, block_scores = _prompt_layer(
        x_prompt, mod_p, norm_g[0], w_main, w_ifT, b_if[0], w_conv[0], b_conv[0], m_norm_g[0],
        wa, wm, wo, final_g, page_table, cache_kT, q_lanes)
    outs_s = _sample_layer(xs, mod_s, z, g, q, q_lanes, block_scores, cache_kT, cache_vT, page_table,
                           state_C[0], state_n[0], state_m[0], state_conv[0],
                           w_conv[0], b_conv[0], m_norm_g[0], wa, wm, wo, final_g)
    y_p, *st_p = outs_p
    y_s, *st_s = outs_s
    return (y_p, y_s) + tuple(a[None] for a in st_p) + tuple(a[None] for a in st_s)
```

```python
import functools

import jax
import jax.numpy as jnp
from jax import lax
from jax.experimental import pallas as pl
from jax.experimental.pallas import tpu as pltpu

F32 = jnp.float32
BF16 = jnp.bfloat16

D_MODEL = 1024
ATT_HEADS = 8
ATT_HEAD_DIM = 64
ATT_WIDTH = ATT_HEADS * ATT_HEAD_DIM
MOBA_BLOCK = 256
MOBA_TOPK = 3
M_HEADS = 4
M_V_DIM = 256
M_QK_DIM = 128
M_QK_WIDTH = M_HEADS * M_QK_DIM
M_WIDTH = M_HEADS * M_V_DIM
M_CONV = 4
PAGE_SIZE = 128
NORM_EPS = 1e-6
NEG_SCORE = -1e30
_LOG2E = 1.4426950408889634

V7X_LANES = 128
V7X_SUBLANES = 8
V7X_VMEM_LIMIT_BYTES = 56 * 1024 * 1024

_C_AQ, _C_AK, _C_AV, _C_AZ = 0, 512, 1024, 1536
_C_MQK, _C_MV, _C_MO, _C_MZ = 2048, 3072, 4096, 5120
_C_MIF, _C_GA, _C_GM, _C_END = 6144, 6152, 7176, 8200


def _silu(x):
    return x * jax.nn.sigmoid(x)


def _split3(x):
    hi = x.astype(BF16)
    r1 = x - hi.astype(F32)
    mid = r1.astype(BF16)
    lo = (r1 - mid.astype(F32)).astype(BF16)
    return hi, mid, lo


def _pack_w_in(w_in):
    pad = (-w_in.shape[1]) % V7X_LANES
    w_main = jnp.concatenate([w_in.astype(BF16), jnp.zeros((w_in.shape[0], pad), BF16)], axis=1)
    w_ifT = w_main[:, _C_MIF:_C_GA].T
    return w_main, w_ifT


def _params(*sem):
    return pltpu.CompilerParams(dimension_semantics=sem, vmem_limit_bytes=V7X_VMEM_LIMIT_BYTES)


def _ada_kernel(c_ref, w_ref, b_ref, o_ref):
    a = _silu(c_ref[...]).astype(BF16)
    acc = jnp.dot(a, w_ref[...].astype(BF16), preferred_element_type=F32)
    o_ref[...] = acc + b_ref[...]


def _ada(c, w_ada, b_ada):
    n, d = c.shape
    d3 = w_ada.shape[1]
    bn = 1024
    return pl.pallas_call(
        _ada_kernel,
        grid=(d3 // bn,),
        in_specs=[pl.BlockSpec((n, d), lambda i: (0, 0)),
                  pl.BlockSpec((d, bn), lambda i: (0, i)),
                  pl.BlockSpec((1, bn), lambda i: (0, i))],
        out_specs=pl.BlockSpec((n, bn), lambda i: (0, i)),
        out_shape=jax.ShapeDtypeStruct((n, d3), F32),
        compiler_params=_params("arbitrary"),
        name="ada_mod",
    )(c, w_ada, b_ada.reshape(1, d3))


def _inproj_kernel(x_ref, mod_ref, ng_ref, w_ref, wif_ref, bif_ref, bifn_ref, wc_ref, bc_ref,
                   qT_ref, kh_ref, vT_ref, kf_ref, vf_ref, kmean_ref,
                   az_ref, qk_ref, mv_ref, mo_ref, mz_ref, ga_ref, gm_ref, gT_ref, mvT_ref, gc_ref,
                   ctail_ref, tail_s):
    j = pl.program_id(1)
    x = x_ref[0]
    xn = x * lax.rsqrt(jnp.mean(x * x, axis=-1, keepdims=True) + NORM_EPS) * ng_ref[...]
    shift = mod_ref[0, 0:1, :]
    scale = mod_ref[0, 1:2, :]
    h = (xn * (1.0 + scale) + shift).astype(BF16)

    def proj(a, b):
        return jnp.dot(h, w_ref[:, a:b], preferred_element_type=F32)

    @pl.when(j == 0)
    def _():
        tail_s[...] = jnp.zeros_like(tail_s)

    mqk = proj(_C_MQK, _C_MV)
    ctail_ref[0] = mqk[mqk.shape[0] - V7X_SUBLANES:, :]
    qk = _silu(_causal_conv(mqk, tail_s, wc_ref, bc_ref))
    qk_ref[0, :, 0:M_QK_WIDTH] = qk[:, 0:M_QK_WIDTH].astype(BF16)
    qk_ref[0, :, M_QK_WIDTH:] = (qk[:, M_QK_WIDTH:] * (M_QK_DIM ** -0.5)).astype(BF16)

    mo_ref[0] = proj(_C_MO, _C_MZ).astype(BF16)

    q = proj(_C_AQ, _C_AK) * (ATT_HEAD_DIM ** -0.5 * _LOG2E)
    qT_ref[0, 0] = q.T.astype(BF16)
    mz_ref[0] = proj(_C_MZ, _C_MIF).astype(BF16)

    dh = ATT_HEAD_DIM
    tm = x.shape[0]
    one_col = jnp.where(lax.broadcasted_iota(jnp.int32, (tm, dh), 1) == 0, 1.0, 0.0)
    one_row = jnp.where(lax.broadcasted_iota(jnp.int32, (_MOBA_V_PAD, tm), 0) == 0, 1.0, 0.0)
    k = proj(_C_AK, _C_AV)
    kf_ref[0] = k
    ksum = jnp.sum(k, axis=0, keepdims=True) / MOBA_BLOCK
    for hh in range(ATT_HEADS):
        sl = slice(hh * dh, (hh + 1) * dh)
        kh_ref[0, 0, hh] = jnp.concatenate([k[:, sl], one_col], axis=1).astype(BF16)
        kmean_ref[0, hh, pl.ds(j, 1), :] = ksum[:, sl]
    az_ref[0] = proj(_C_AZ, _C_MQK).astype(BF16)

    v = proj(_C_AV, _C_AZ)
    vf_ref[0] = v
    vT = v.T
    for hh in range(ATT_HEADS):
        r0 = hh * _MOBA_V_ROWS
        vT_ref[0, 0, r0:r0 + dh, :] = vT[hh * dh:(hh + 1) * dh].astype(BF16)
        vT_ref[0, 0, r0 + dh:r0 + _MOBA_V_ROWS, :] = one_row.astype(BF16)

    tail = proj(_C_MIF, w_ref.shape[1])
    gc_ref[0] = tail[:, 0:_C_GA - _C_MIF] + bifn_ref[...]
    ga_ref[0] = tail[:, _C_GA - _C_MIF:_C_GM - _C_MIF].astype(BF16)
    gm_ref[0] = tail[:, _C_GM - _C_MIF:_C_END - _C_MIF].astype(BF16)
    gT = lax.dot_general(wif_ref[...], h, (((1,), (1,)), ((), ())), preferred_element_type=F32)
    gT_ref[0] = gT + bif_ref[...]

    mv = proj(_C_MV, _C_MO)
    mv_ref[0] = mv.astype(BF16)
    mvT_ref[0, 0] = mv.T.astype(BF16)


def _inproj(x, mod3, norm_g, w_main, w_ifT, b_if, w_conv, b_conv):
    B, S, D = x.shape
    W = 2 * M_QK_WIDTH
    tm = MOBA_BLOCK
    nb = S // tm
    H, dh = ATT_HEADS, ATT_HEAD_DIM
    const = lambda b, j: (0, 0)
    tok = lambda w: pl.BlockSpec((1, tm, w), lambda b, j: (b, j, 0))
    blk = pl.BlockSpec((1, 1, ATT_WIDTH, tm), lambda b, j: (b, j, 0, 0))
    out_shape = [
        jax.ShapeDtypeStruct((B, nb, ATT_WIDTH, tm), BF16),
        jax.ShapeDtypeStruct((B, nb, H, tm, 2 * dh), BF16),
        jax.ShapeDtypeStruct((B, nb, H * _MOBA_V_ROWS, tm), BF16),
        jax.ShapeDtypeStruct((B, S, ATT_WIDTH), F32),
        jax.ShapeDtypeStruct((B, S, ATT_WIDTH), F32),
        jax.ShapeDtypeStruct((B, H, nb, dh), F32),
        jax.ShapeDtypeStruct((B, S, ATT_WIDTH), BF16),
        jax.ShapeDtypeStruct((B, S, W), BF16),
        jax.ShapeDtypeStruct((B, S, M_WIDTH), BF16),
        jax.ShapeDtypeStruct((B, S, M_WIDTH), BF16),
        jax.ShapeDtypeStruct((B, S, M_WIDTH), BF16),
        jax.ShapeDtypeStruct((B, S, D), BF16),
        jax.ShapeDtypeStruct((B, S, D), BF16),
        jax.ShapeDtypeStruct((B, 2 * M_HEADS, S), F32),
        jax.ShapeDtypeStruct((B, nb, M_WIDTH, tm), BF16),
        jax.ShapeDtypeStruct((B, S, 2 * M_HEADS), F32),
        jax.ShapeDtypeStruct((B, V7X_SUBLANES, W), F32),
    ]
    out_specs = [
        blk,
        pl.BlockSpec((1, 1, H, tm, 2 * dh), lambda b, j: (b, j, 0, 0, 0)),
        pl.BlockSpec((1, 1, H * _MOBA_V_ROWS, tm), lambda b, j: (b, j, 0, 0)),
        tok(ATT_WIDTH), tok(ATT_WIDTH),
        pl.BlockSpec((1, H, nb, dh), lambda b, j: (b, 0, 0, 0)),
        tok(ATT_WIDTH), tok(2 * M_QK_WIDTH), tok(M_WIDTH), tok(M_WIDTH), tok(M_WIDTH), tok(D), tok(D),
        pl.BlockSpec((1, 2 * M_HEADS, tm), lambda b, j: (b, 0, j)),
        pl.BlockSpec((1, 1, M_WIDTH, tm), lambda b, j: (b, j, 0, 0)),
        tok(2 * M_HEADS),
        pl.BlockSpec((1, V7X_SUBLANES, W), lambda b, j: (b, 0, 0)),
    ]
    in_specs = [
        pl.BlockSpec((1, tm, D), lambda b, j: (b, j, 0)),
        pl.BlockSpec((1, 3, D), lambda b, j: (b, 0, 0)),
        pl.BlockSpec((1, D), const),
        pl.BlockSpec(w_main.shape, const, pipeline_mode=pl.Buffered(1)),
        pl.BlockSpec(w_ifT.shape, const),
        pl.BlockSpec((2 * M_HEADS, 1), const),
        pl.BlockSpec((1, 2 * M_HEADS), const),
        pl.BlockSpec((M_CONV, W), const),
        pl.BlockSpec((1, W), const),
    ]
    return pl.pallas_call(
        _inproj_kernel,
        grid=(B, nb),
        in_specs=in_specs,
        out_specs=out_specs,
        out_shape=out_shape,
        scratch_shapes=[pltpu.VMEM((V7X_SUBLANES, W), F32)],
        compiler_params=_params("arbitrary", "arbitrary"),
        name="prompt_inproj",
    )(x, mod3, norm_g.reshape(1, D), w_main, w_ifT, b_if.reshape(2 * M_HEADS, 1),
      b_if.reshape(1, 2 * M_HEADS), w_conv, b_conv.reshape(1, W))


_MOBA_HEADS_PER_STEP = 8
_MOBA_HEAD_GROUP = 8
_MOBA_V_PAD = 16
_MOBA_V_ROWS = ATT_HEAD_DIM + _MOBA_V_PAD


def _moba_phases(j):
    return j + 2 * ((j * j) // 4)


def _moba_kernel(pt_ref, qT_ref, kh_ref, vT_ref, kmean_ref, ck_ref, qb_ref, o_ref, sc_ref,
                 bias_ref, qs_ref, acc_ref, pbuf, qbuf, psem, qsem):
    b = pl.program_id(0)
    j = pl.program_id(2)
    nbq = pl.num_programs(2)
    nseq, n_pages = pt_ref.shape
    cpb = n_pages // _PAGES_PER_CHUNK
    bpc = _PAGES_PER_CHUNK // _PAGES_PER_BLOCK
    n_chunks = nseq * cpb
    ring = pbuf.shape[0]
    ahead = ring - 2

    def chunk_copy(g, i):
        page = pt_ref[g // cpb, (g % cpb) * _PAGES_PER_CHUNK + i]
        slot = g % ring
        return pltpu.make_async_copy(ck_ref.at[0, page], pbuf.at[slot, i], psem.at[slot])

    qslots = qbuf.shape[0]

    def q_copy(seq):
        s = seq % qslots
        return pltpu.make_async_copy(qb_ref.at[seq], qbuf.at[s], qsem.at[s])

    def start_chunk(g):
        @pl.when(g < n_chunks)
        def _():
            for i in range(_PAGES_PER_CHUNK):
                chunk_copy(g, i).start()

    step = b * pl.num_programs(1) + pl.program_id(1)

    @pl.when((step == 0) & (j == 0))
    def _():
        sc_ref[...] = jnp.zeros_like(sc_ref)
        q_copy(0).start()
        for g in range(ahead):
            start_chunk(g)

    lane_blk = lax.broadcasted_iota(jnp.int32, sc_ref.shape[1:], 1)

    def page_sync(g):
        live = g < n_chunks
        gc = jnp.minimum(g, n_chunks - 1)
        seq = gc // cpb
        ci = gc % cpb

        @pl.when(live & (ci == 0))
        def _():
            q_copy(seq).wait()

        @pl.when(live & (ci == 0) & (seq + 1 < nseq))
        def _():
            q_copy(seq + 1).start()

        start_chunk(g + ahead)

        @pl.when(live)
        def _():
            for i in range(_PAGES_PER_CHUNK):
                chunk_copy(gc, i).wait()

    def page_scores(g):
        gc = jnp.minimum(g, n_chunks - 1)
        seq = gc // cpb
        ci = gc % cpb
        slot = gc % ring
        qb = qbuf[seq % qslots]
        row = sc_ref[seq]
        for blk in range(bpc):
            p = pbuf[slot, blk * _PAGES_PER_BLOCK]
            for i in range(1, _PAGES_PER_BLOCK):
                p = p + pbuf[slot, blk * _PAGES_PER_BLOCK + i]
            part = jnp.sum(p * qb, axis=1)
            val = jnp.sum(part, axis=1, keepdims=True)
            row = jnp.where(lane_blk == ci * bpc + blk, val, row)
        sc_ref[seq] = row

    phase0 = step * _moba_phases(nbq) + _moba_phases(j)
    hp = kmean_ref.shape[1]
    nb = kmean_ref.shape[2]
    tq = qT_ref.shape[3]
    dh = ATT_HEAD_DIM
    vr = _MOBA_V_ROWS
    qTs = [qT_ref[0, 0, hh * dh:(hh + 1) * dh, :] for hh in range(hp)]
    for hh in range(hp):
        for slot in range(2):
            qs_ref[slot, hh, 0:dh, :] = qTs[hh]
            qs_ref[slot, hh, dh:2 * dh, :] = jnp.zeros((dh, tq), BF16)

    blk = lax.broadcasted_iota(jnp.int32, (nb, tq), 0)
    blk_f = blk.astype(F32)
    past = blk < j
    for hh in range(hp):
        sg = sum(jnp.dot(part, qTs[hh], preferred_element_type=F32)
                 for part in _split3(kmean_ref[0, hh]))
        work = jnp.where(past, sg, NEG_SCORE)
        bias = jnp.full((nb, tq), NEG_SCORE, F32)
        for _ in range(MOBA_TOPK):
            mx = jnp.max(work, axis=0, keepdims=True)
            first = jnp.min(jnp.where(work == mx, blk_f, float(nb)), axis=0, keepdims=True)
            chosen = blk_f == first
            bias = jnp.where(chosen & past, 0.0, bias)
            work = jnp.where(chosen, NEG_SCORE, work)
        bias_ref[hh] = bias

    kpos = lax.broadcasted_iota(jnp.int32, (tq, tq), 0)
    qpos = lax.broadcasted_iota(jnp.int32, (tq, tq), 1)
    causal = kpos <= qpos

    def score(n, slot, hh):
        return jnp.dot(kh_ref[0, n, hh], qs_ref[slot, hh], preferred_element_type=F32)

    def value(n, hh, p):
        return jnp.dot(vT_ref[0, n, hh * vr:(hh + 1) * vr, :], p, preferred_element_type=F32)

    def scores(n, slot=0):
        return [score(n, slot, hh) for hh in range(hp)]

    def values(n, ps):
        return [value(n, hh, ps[hh]) for hh in range(hp)]

    ss = [jnp.where(causal, s, NEG_SCORE) for s in scores(j)]
    ms = [jnp.max(s, axis=0, keepdims=True) for s in ss]
    pbs = [jnp.exp2((s - m).astype(BF16)) for s, m in zip(ss, ms)]
    for hh, x in enumerate(values(j, pbs)):
        acc_ref[hh] = x
    page_sync(phase0)
    page_scores(phase0)

    row0 = lax.broadcasted_iota(jnp.int32, (_MOBA_V_PAD, tq), 0) == 0

    def body(t, ms):
        n0 = 2 * t
        real1 = n0 + 1 < j
        n1 = jnp.minimum(n0 + 1, j - 1)
        g = phase0 + 1 + 2 * t
        page_sync(g)
        page_sync(g + 1)
        for hh in range(hp):
            mask0 = jnp.where(row0, bias_ref[hh, pl.ds(n0, 1), :], 0.0)
            qs_ref[0, hh, dh:dh + _MOBA_V_PAD, :] = mask0.astype(BF16)
            bias1 = jnp.where(real1, bias_ref[hh, pl.ds(n1, 1), :], NEG_SCORE)
            qs_ref[1, hh, dh:dh + _MOBA_V_PAD, :] = jnp.where(row0, bias1, 0.0).astype(BF16)
        page_scores(g)
        page_scores(g + 1)
        m_out = []
        for h0 in range(0, hp, _MOBA_HEAD_GROUP):
            hs = range(h0, min(h0 + _MOBA_HEAD_GROUP, hp))
            ss0 = [score(n0, 0, hh) for hh in hs]
            ss1 = [score(n1, 1, hh) for hh in hs]
            m0 = [ms[hh] for hh in hs]
            m1 = [jnp.maximum(m, jnp.max(s, axis=0, keepdims=True)) for m, s in zip(m0, ss0)]
            a0 = [jnp.exp2(m - mn) for m, mn in zip(m0, m1)]
            pv0 = [value(n0, hh, jnp.exp2((s - mn).astype(BF16)))
                   for hh, s, mn in zip(hs, ss0, m1)]
            m2 = [jnp.maximum(m, jnp.max(s, axis=0, keepdims=True)) for m, s in zip(m1, ss1)]
            a1 = [jnp.exp2(m - mn) for m, mn in zip(m1, m2)]
            pv1 = [value(n1, hh, jnp.exp2((s - mn).astype(BF16)))
                   for hh, s, mn in zip(hs, ss1, m2)]
            for i, hh in enumerate(hs):
                acc_ref[hh] = (a0[i] * acc_ref[hh] + pv0[i]) * a1[i] + pv1[i]
            m_out += m2
        return tuple(m_out)

    lax.fori_loop(0, (j + 1) // 2, body, tuple(ms))
    for hh in range(hp):
        acc = acc_ref[hh]
        o_ref[0, 0, hh * dh:(hh + 1) * dh, :] = (acc[0:dh] / acc[dh:dh + 1]).astype(o_ref.dtype)


def _moba_prompt(qT, kh, vT, kmean, page_table, cache_kT, q_lanes):
    B, nb, _, tq = qT.shape
    H, dh, hp = ATT_HEADS, ATT_HEAD_DIM, _MOBA_HEADS_PER_STEP
    nseq, n_pages = page_table.shape
    ps = cache_kT.shape[-1]
    assert n_pages % _PAGES_PER_CHUNK == 0 and n_pages // _PAGES_PER_BLOCK <= V7X_LANES
    n_chunks = nseq * (n_pages // _PAGES_PER_CHUNK)
    assert _PAGE_RING - 2 <= n_chunks <= B * (H // hp) * _moba_phases(nb), \
        "one page chunk per key-block phase"
    grid_spec = pltpu.PrefetchScalarGridSpec(
        num_scalar_prefetch=1,
        grid=(B, H // hp, nb),
        in_specs=[
            pl.BlockSpec((1, 1, hp * dh, tq), lambda b, g, j, pt: (b, j, g, 0)),
            pl.BlockSpec((1, nb, hp, tq, 2 * dh), lambda b, g, j, pt: (b, 0, g, 0, 0)),
            pl.BlockSpec((1, nb, hp * _MOBA_V_ROWS, tq), lambda b, g, j, pt: (b, 0, g, 0)),
            pl.BlockSpec((1, hp, nb, dh), lambda b, g, j, pt: (b, g, 0, 0)),
            pl.BlockSpec(memory_space=pl.ANY),
            pl.BlockSpec(memory_space=pl.ANY),
        ],
        out_specs=[
            pl.BlockSpec((1, 1, hp * dh, tq), lambda b, g, j, pt: (b, j, g, 0)),
            pl.BlockSpec((nseq, H, V7X_LANES), lambda b, g, j, pt: (0, 0, 0)),
        ],
        scratch_shapes=[
            pltpu.VMEM((hp, nb, tq), F32), pltpu.VMEM((2, hp, 2 * dh, tq), BF16),
            pltpu.VMEM((hp, _MOBA_V_ROWS, tq), F32),
            pltpu.VMEM((_PAGE_RING, _PAGES_PER_CHUNK, H, dh, ps), F32),
            pltpu.VMEM((3, H, dh, ps), F32),
            pltpu.SemaphoreType.DMA((_PAGE_RING,)), pltpu.SemaphoreType.DMA((3,)),
        ],
    )
    return pl.pallas_call(
        _moba_kernel,
        grid_spec=grid_spec,
        out_shape=[jax.ShapeDtypeStruct((B, nb, ATT_WIDTH, tq), BF16),
                   jax.ShapeDtypeStruct((nseq, H, V7X_LANES), F32)],
        compiler_params=_params("arbitrary", "arbitrary", "arbitrary"),
        name="prompt_moba",
    )(page_table, qT, kh, vT, kmean, cache_kT, q_lanes)


def _log_sigmoid(x):
    return jnp.minimum(x, 0.0) - jnp.log1p(jnp.exp(-jnp.abs(x)))


def _causal_conv(x, tail_ref, w_ref, b_ref):
    L = x.shape[0]
    prev = tail_ref[...]
    row = lax.broadcasted_iota(jnp.int32, (V7X_SUBLANES, x.shape[1]), 0)
    acc = x * w_ref[M_CONV - 1:M_CONV, :] + b_ref[...]
    for d in range(1, M_CONV):
        r = pltpu.roll(x, d, axis=0)
        head = jnp.where(row < d, pltpu.roll(prev, d, axis=0), r[:V7X_SUBLANES])
        shifted = jnp.concatenate([head, r[V7X_SUBLANES:]], axis=0)
        acc += shifted * w_ref[M_CONV - 1 - d:M_CONV - d, :]
    tail_ref[...] = x[L - V7X_SUBLANES:, :]
    return acc


def _mlstm_kernel(qk_ref, v_ref, vT_ref, g_ref, gc_ref, o_ref, ng_ref,
                  h_ref, C_out, n_out, m_out,
                  C_s, n_s, m_s):
    c = pl.program_id(1)
    L = qk_ref.shape[1]
    MH, dk, dv = M_HEADS, M_QK_DIM, M_V_DIM
    nt = (((1,), (1,)), ((), ()))

    @pl.when(c == 0)
    def _():
        C_s[...] = jnp.zeros_like(C_s)
        n_s[...] = jnp.zeros_like(n_s)
        m_s[...] = jnp.zeros_like(m_s)

    t_idx = lax.broadcasted_iota(jnp.int32, (L, L), 0)
    s_idx = lax.broadcasted_iota(jnp.int32, (L, L), 1)
    causal = s_idx <= t_idx
    tri = jnp.where(causal, 1.0, 0.0).astype(BF16)
    g_rows = g_ref[0]
    b_rows = sum(lax.dot_general(part, tri, nt, preferred_element_type=F32)
                 for part in _split3(_log_sigmoid(g_rows)))
    b_cols = sum(jnp.dot(tri, part, preferred_element_type=F32)
                 for part in _split3(_log_sigmoid(gc_ref[0])))

    for hd in range(MH):
        qb = qk_ref[0, :, hd * dk:(hd + 1) * dk]
        kb = qk_ref[0, :, M_QK_WIDTH + hd * dk:M_QK_WIDTH + (hd + 1) * dk]
        vsl = slice(hd * dv, (hd + 1) * dv)
        b_row = b_rows[MH + hd:MH + hd + 1]
        c_row = g_rows[hd:hd + 1] - b_row
        b_col = b_cols[:, MH + hd:MH + hd + 1]
        m_prev = m_s[hd:hd + 1, 0:1]
        M_col = jnp.maximum(jnp.max(jnp.where(causal, c_row, NEG_SCORE), axis=1, keepdims=True),
                            m_prev)
        M_last = jnp.maximum(jnp.max(c_row, axis=1, keepdims=True), m_prev)
        w = jnp.where(causal, jnp.exp(c_row - M_col), 0.0)
        a_col = jnp.exp(m_prev - M_col)

        s = lax.dot_general(qb, kb, nt, preferred_element_type=F32) * w
        inter = lax.dot_general(qb, C_s[hd].astype(BF16), nt, preferred_element_type=F32)
        num = a_col * inter + jnp.dot(s.astype(BF16), v_ref[0, :, vsl], preferred_element_type=F32)
        n_b = n_s[hd:hd + 1, :].astype(BF16).astype(F32)
        qn = jnp.sum(qb.astype(F32) * n_b, axis=1, keepdims=True)
        den = a_col * qn + jnp.sum(s, axis=1, keepdims=True)
        hh = num / jnp.maximum(jnp.abs(den), jnp.exp(-(b_col + M_col)))
        hh = hh * lax.rsqrt(jnp.mean(hh * hh, axis=-1, keepdims=True) + NORM_EPS)
        gate = jax.nn.sigmoid(o_ref[0, :, vsl].astype(F32))
        h_ref[0, :, vsl] = (hh * ng_ref[:, vsl] * gate).astype(h_ref.dtype)

        ws_row = jnp.exp(c_row - M_last)
        a_end = jnp.exp(m_prev - M_last)
        vwT = (vT_ref[0, 0, vsl, :].astype(F32) * ws_row).astype(BF16)
        C_s[hd] = a_end * C_s[hd] + jnp.dot(vwT, kb, preferred_element_type=F32)
        ws8 = jnp.broadcast_to(ws_row, (V7X_SUBLANES, L)).astype(BF16)
        n_s[hd:hd + 1, :] = (a_end * n_s[hd:hd + 1, :]
                             + jnp.dot(ws8, kb, preferred_element_type=F32)[0:1])
        m_s[hd:hd + 1, :] = jnp.broadcast_to(b_row[:, L - 1:L] + M_last, (1, m_s.shape[1]))

    @pl.when(c == pl.num_programs(1) - 1)
    def _():
        C_out[0] = C_s[...]
        n_out[0] = n_s[...]
        m_out[0] = m_s[...]


def _mlstm_prompt(qk, mv, mvT, gT, gc, mo, m_norm_g):
    B, S, W = qk.shape
    L = mvT.shape[3]
    nc = S // L
    MH, dk, dv = M_HEADS, M_QK_DIM, M_V_DIM
    const = lambda b, c: (0, 0)
    tok = lambda w: pl.BlockSpec((1, L, w), lambda b, c: (b, c, 0))
    in_specs = [
        tok(W), tok(M_WIDTH),
        pl.BlockSpec((1, 1, M_WIDTH, L), lambda b, c: (b, c, 0, 0)),
        pl.BlockSpec((1, 2 * MH, L), lambda b, c: (b, 0, c)),
        tok(2 * MH), tok(M_WIDTH),
        pl.BlockSpec((1, M_WIDTH), const),
    ]
    out_shape = [
        jax.ShapeDtypeStruct((B, S, M_WIDTH), BF16),
        jax.ShapeDtypeStruct((B, MH, dv, dk), F32),
        jax.ShapeDtypeStruct((B, V7X_SUBLANES, dk), F32),
        jax.ShapeDtypeStruct((B, V7X_SUBLANES, V7X_LANES), F32),
    ]
    out_specs = [
        tok(M_WIDTH),
        pl.BlockSpec((1, MH, dv, dk), lambda b, c: (b, 0, 0, 0)),
        pl.BlockSpec((1, V7X_SUBLANES, dk), lambda b, c: (b, 0, 0)),
        pl.BlockSpec((1, V7X_SUBLANES, V7X_LANES), lambda b, c: (b, 0, 0)),
    ]
    scratch = [
        pltpu.VMEM((MH, dv, dk), F32), pltpu.VMEM((V7X_SUBLANES, dk), F32),
        pltpu.VMEM((V7X_SUBLANES, V7X_LANES), F32),
    ]
    return pl.pallas_call(
        _mlstm_kernel,
        grid=(B, nc),
        in_specs=in_specs,
        out_specs=out_specs,
        out_shape=out_shape,
        scratch_shapes=scratch,
        compiler_params=_params("arbitrary", "arbitrary"),
        name="prompt_mlstm",
    )(qk, mv, mvT, gT, gc, mo, m_norm_g.reshape(1, M_WIDTH))


def _merge_kernel(yaT_ref, az_ref, hm_ref, mz_ref, ga_ref, gm_ref, x_ref, mod_ref,
                  watt_ref, wml_ref, wout_ref, fg_ref, y_ref, watt_s, wml_s, wout_s):
    @pl.when((pl.program_id(0) == 0) & (pl.program_id(1) == 0))
    def _():
        watt_s[...] = watt_ref[...].astype(BF16)
        wml_s[...] = wml_ref[...].astype(BF16)
        wout_s[...] = wout_ref[...].astype(BF16)

    ya = yaT_ref[0, 0].astype(F32).T * _silu(az_ref[0].astype(F32))
    ya = jnp.dot(ya.astype(BF16), watt_s[...], preferred_element_type=F32)
    ym = hm_ref[0].astype(F32) * _silu(mz_ref[0].astype(F32))
    ym = jnp.dot(ym.astype(BF16), wml_s[...], preferred_element_type=F32)
    u = (jax.nn.sigmoid(ga_ref[0].astype(F32)) * ya
         + jax.nn.sigmoid(gm_ref[0].astype(F32)) * ym)
    upd = jnp.dot(u.astype(BF16), wout_s[...], preferred_element_type=F32)
    y = x_ref[0] + mod_ref[0, 2:3, :] * upd
    y = y * lax.rsqrt(jnp.mean(y * y, axis=-1, keepdims=True) + NORM_EPS)
    y_ref[0] = y * fg_ref[...]


def _merge_prompt(yaT, az, hm, mz, ga, gm, x, mod3, w_att, w_mlstm, w_out, final_g):
    B, S, D = x.shape
    tm = yaT.shape[3]
    nb = S // tm
    const = lambda b, j: (0, 0)
    tok = lambda w: pl.BlockSpec((1, tm, w), lambda b, j: (b, j, 0))
    return pl.pallas_call(
        _merge_kernel,
        grid=(B, nb),
        in_specs=[
            pl.BlockSpec((1, 1, ATT_WIDTH, tm), lambda b, j: (b, j, 0, 0)),
            tok(ATT_WIDTH), tok(M_WIDTH), tok(M_WIDTH), tok(D), tok(D), tok(D),
            pl.BlockSpec((1, 3, D), lambda b, j: (b, 0, 0)),
            pl.BlockSpec(w_att.shape, const, pipeline_mode=pl.Buffered(1)),
            pl.BlockSpec(w_mlstm.shape, const, pipeline_mode=pl.Buffered(1)),
            pl.BlockSpec(w_out.shape, const, pipeline_mode=pl.Buffered(1)),
            pl.BlockSpec((1, D), const),
        ],
        out_specs=tok(D),
        out_shape=jax.ShapeDtypeStruct((B, S, D), F32),
        scratch_shapes=[pltpu.VMEM(w_att.shape, BF16), pltpu.VMEM(w_mlstm.shape, BF16),
                        pltpu.VMEM(w_out.shape, BF16)],
        compiler_params=_params("arbitrary", "arbitrary"),
        name="prompt_merge",
    )(yaT, az, hm, mz, ga, gm, x, mod3, w_att, w_mlstm, w_out, final_g.reshape(1, D))


def _prompt_layer(x, mod3, norm_g, w_main, w_ifT, b_if, w_conv, b_conv, m_norm_g,
                  w_att, w_mlstm, w_out, final_g, page_table, cache_kT, q_lanes):
    B, S, _ = x.shape
    (qT, kh, vT, kf, vf, kmean, az, qk, mv, mo, mz, ga, gm, gT, mvT, gc, ctail) = _inproj(
        x, mod3, norm_g, w_main, w_ifT, b_if, w_conv, b_conv)
    yaT, block_scores = _moba_prompt(qT, kh, vT, kmean, page_table, cache_kT, q_lanes)
    hm, C, n, m = _mlstm_prompt(qk, mv, mvT, gT, gc, mo, m_norm_g)
    y = _merge_prompt(yaT, az, hm, mz, ga, gm, x, mod3, w_att, w_mlstm, w_out, final_g)
    k_rows = kf.reshape(B, S, ATT_HEADS, ATT_HEAD_DIM)
    v_rows = vf.reshape(B, S, ATT_HEADS, ATT_HEAD_DIM)
    conv_new = ctail[:, V7X_SUBLANES - (M_CONV - 1):, :]
    return y, k_rows, v_rows, C, n[:, :M_HEADS, :], m[:, :M_HEADS, 0], conv_new, block_scores


def _inproj_sample_kernel(x_ref, mod_ref, ng_ref, w_ref, wif_ref, bif_ref, z_ref, g_ref):
    x = x_ref[...]
    d = x.shape[1]
    xn = x * lax.rsqrt(jnp.mean(x * x, axis=-1, keepdims=True) + NORM_EPS) * ng_ref[...]
    h = (xn * (1.0 + mod_ref[:, d:2 * d]) + mod_ref[:, 0:d]).astype(BF16)
    z_ref[...] = jnp.dot(h, w_ref[...], preferred_element_type=F32)
    g = lax.dot_general(h, wif_ref[...], (((1,), (1,)), ((), ())), preferred_element_type=F32)
    g_ref[...] = g + bif_ref[...]


def _inproj_sample(x, mod, norm_g, w_main, w_ifT, b_if):
    n, d = x.shape
    bn = 13 * V7X_LANES
    assert w_main.shape[1] % bn == 0
    const = lambda i: (0, 0)
    return pl.pallas_call(
        _inproj_sample_kernel,
        grid=(w_main.shape[1] // bn,),
        in_specs=[pl.BlockSpec((n, d), const),
                  pl.BlockSpec(mod.shape, const),
                  pl.BlockSpec((1, d), const),
                  pl.BlockSpec((d, bn), lambda i: (0, i)),
                  pl.BlockSpec(w_ifT.shape, const),
                  pl.BlockSpec((1, 2 * M_HEADS), const)],
        out_specs=[pl.BlockSpec((n, bn), lambda i: (0, i)),
                   pl.BlockSpec((n, 2 * M_HEADS), const)],
        out_shape=[jax.ShapeDtypeStruct((n, w_main.shape[1]), F32),
                   jax.ShapeDtypeStruct((n, 2 * M_HEADS), F32)],
        compiler_params=_params("arbitrary"),
        name="sample_inproj",
    )(x, mod, norm_g.reshape(1, d), w_main, w_ifT, b_if.reshape(1, 2 * M_HEADS))


_PAGES_PER_CHUNK = 8
_PAGE_RING = 7
_PAGES_PER_BLOCK = MOBA_BLOCK // PAGE_SIZE
_SAMPLE_SEQS_PER_STEP = 2


def _attend_sample_kernel(pt_ref, sel_ref, ck_ref, cv_ref, q_ref, ql_ref, kn_ref, vn_ref, o_ref,
                          kbuf, vbuf, sem):
    b = pl.program_id(0)
    nseq = pl.num_programs(0)
    H, dh = ATT_HEADS, ATT_HEAD_DIM
    npg = MOBA_TOPK * _PAGES_PER_BLOCK

    def copies(bb, slot):
        out = []
        for hh in range(H):
            for r in range(MOBA_TOPK):
                blk = sel_ref[(bb * H + hh) * MOBA_TOPK + r]
                for i in range(_PAGES_PER_BLOCK):
                    page = pt_ref[bb, blk * _PAGES_PER_BLOCK + i]
                    pg = r * _PAGES_PER_BLOCK + i
                    out.append(pltpu.make_async_copy(
                        ck_ref.at[0, page, hh], kbuf.at[slot, hh, pg], sem.at[0, slot]))
                    out.append(pltpu.make_async_copy(
                        cv_ref.at[0, page, hh], vbuf.at[slot, hh, pg], sem.at[1, slot]))
        return out

    @pl.when(b == 0)
    def _():
        for cp in copies(0, 0):
            cp.start()

    slot = b % 2

    @pl.when(b + 1 < nseq)
    def _():
        for cp in copies(b + 1, 1 - slot):
            cp.start()

    for cp in copies(b, slot):
        cp.wait()

    @pl.when(b == 0)
    def _():
        o_ref[...] = jnp.zeros_like(o_ref)

    scale = ATT_HEAD_DIM ** -0.5
    lane = lax.broadcasted_iota(jnp.int32, (dh, ql_ref.shape[3]), 1)
    eye = lax.broadcasted_iota(jnp.int32, lane.shape, 0) == lane
    for hh in range(H):
        qb = ql_ref[0, hh]
        s = jnp.concatenate([jnp.sum(kbuf[slot, hh, pg] * qb, axis=0, keepdims=True)
                             for pg in range(npg)], axis=1) * scale
        s_new = jnp.sum(q_ref[0, hh:hh + 1, :] * kn_ref[0, hh:hh + 1, :], axis=1,
                        keepdims=True) * scale
        m = jnp.maximum(jnp.max(s, axis=1, keepdims=True), s_new)
        p = jnp.exp(s - m)
        p_new = jnp.exp(s_new - m)
        l = jnp.sum(p, axis=1, keepdims=True) + p_new
        tile = vbuf[slot, hh, 0] * p[:, 0:PAGE_SIZE]
        for pg in range(1, npg):
            tile = tile + vbuf[slot, hh, pg] * p[:, pg * PAGE_SIZE:(pg + 1) * PAGE_SIZE]
        vn_b = jnp.broadcast_to(
            jnp.concatenate([vn_ref[0, hh:hh + 1, :], jnp.zeros((1, lane.shape[1] - dh), F32)],
                            axis=1), lane.shape)
        col = jnp.sum(tile + jnp.where(eye, vn_b * p_new, 0.0), axis=1, keepdims=True) / l
        sl = slice(hh * dh, (hh + 1) * dh)
        o_ref[sl, :] = jnp.where(lane == b, col, o_ref[sl, :])


def _attend_sample(cache_k, cache_v, page_table, sel_flat, q, q_lanes, k_new, v_new):
    nseq, n_pages = page_table.shape
    H, dh = ATT_HEADS, ATT_HEAD_DIM
    npg = MOBA_TOPK * _PAGES_PER_BLOCK
    ps = cache_k.shape[-1]
    assert nseq <= V7X_LANES and q_lanes.shape[3] == ps
    row = lambda: pl.BlockSpec((1, H, dh), lambda b, pt, sel: (b, 0, 0))
    grid_spec = pltpu.PrefetchScalarGridSpec(
        num_scalar_prefetch=2,
        grid=(nseq,),
        in_specs=[pl.BlockSpec(memory_space=pl.ANY),
                  pl.BlockSpec(memory_space=pl.ANY),
                  row(),
                  pl.BlockSpec((1, H, dh, ps), lambda b, pt, sel: (b, 0, 0, 0)),
                  row(), row()],
        out_specs=pl.BlockSpec((ATT_WIDTH, V7X_LANES), lambda b, pt, sel: (0, 0)),
        scratch_shapes=[pltpu.VMEM((2, H, npg, dh, ps), F32),
                        pltpu.VMEM((2, H, npg, dh, ps), F32),
                        pltpu.SemaphoreType.DMA((2, 2))],
    )
    return pl.pallas_call(
        _attend_sample_kernel,
        grid_spec=grid_spec,
        out_shape=jax.ShapeDtypeStruct((ATT_WIDTH, V7X_LANES), F32),
        compiler_params=_params("arbitrary"),
        name="sample_attend",
    )(page_table, sel_flat, cache_k, cache_v, q, q_lanes, k_new, v_new)


def _mlstm_sample_kernel(qk_ref, v_ref, o_ref, g_ref, conv_ref, C_ref, n_ref, wc_ref, bc_ref, ng_ref,
                         h_ref, C_out, n_out, m_out, conv_out):
    for sq in range(qk_ref.shape[0]):
        x = qk_ref[sq]
        conv = conv_ref[sq]
        acc = x * wc_ref[M_CONV - 1:M_CONV, :] + bc_ref[...]
        for jj in range(M_CONV - 1):
            acc += conv[jj:jj + 1, :] * wc_ref[jj:jj + 1, :]
        qk = _silu(acc)
        conv_out[sq] = jnp.concatenate([conv[1:], x], axis=0)
        g = g_ref[sq]
        MH = M_HEADS
        bb4 = _log_sigmoid(g[:, MH:2 * MH])
        m_prev4 = g[:, 2 * MH:3 * MH]
        cc4 = g[:, 0:MH] - bb4
        M4 = jnp.maximum(m_prev4, cc4)
        w4 = jnp.exp(cc4 - M4)
        a4 = jnp.exp(m_prev4 - M4)
        floor4 = jnp.exp(-(bb4 + M4))
        m_new4 = bb4 + M4
        nt = (((1,), (1,)), ((), ()))
        heads = range(MH)
        qbs = [qk[:, hh * M_QK_DIM:(hh + 1) * M_QK_DIM].astype(BF16) for hh in heads]
        ks = [qk[:, M_QK_WIDTH + hh * M_QK_DIM:M_QK_WIDTH + (hh + 1) * M_QK_DIM] * (M_QK_DIM ** -0.5)
              for hh in heads]
        vs = [v_ref[sq][:, hh * M_V_DIM:(hh + 1) * M_V_DIM] for hh in heads]
        Cs = [C_ref[sq, hh] for hh in heads]
        ns = [n_ref[sq, hh:hh + 1, :] for hh in heads]
        inters = [lax.dot_general(jnp.broadcast_to(qbs[hh], (V7X_SUBLANES, M_QK_DIM)),
                                  Cs[hh].astype(BF16), nt, preferred_element_type=F32)[0:1]
                  for hh in heads]
        wv_cols = []
        for hh in heads:
            wv = jnp.broadcast_to(w4[:, hh:hh + 1] * vs[hh], (V7X_SUBLANES, M_V_DIM))
            wv = jnp.concatenate([wv, jnp.zeros((V7X_LANES - V7X_SUBLANES, M_V_DIM), F32)], axis=0)
            wv_cols.append(wv.T[:, 0:1])
        for hh in heads:
            w, a = w4[:, hh:hh + 1], a4[:, hh:hh + 1]
            qf = qbs[hh].astype(F32)
            s = jnp.sum(qf * ks[hh].astype(BF16).astype(F32), axis=1, keepdims=True) * w
            vb = vs[hh].astype(BF16).astype(F32)
            num = a * inters[hh] + s.astype(BF16).astype(F32) * vb
            den = a * jnp.sum(qf * ns[hh].astype(BF16).astype(F32), axis=1, keepdims=True) + s
            hv = num / jnp.maximum(jnp.abs(den), floor4[:, hh:hh + 1])
            hv = hv * lax.rsqrt(jnp.mean(hv * hv, axis=-1, keepdims=True) + NORM_EPS)
            sl = slice(hh * M_V_DIM, (hh + 1) * M_V_DIM)
            h_ref[sq, :, sl] = hv * ng_ref[:, sl] * jax.nn.sigmoid(o_ref[sq][:, sl])
        for hh in heads:
            w, a = w4[:, hh:hh + 1], a4[:, hh:hh + 1]
            C_out[sq, hh] = a * Cs[hh] + wv_cols[hh] * ks[hh]
            n_out[sq, hh:hh + 1, :] = a * ns[hh] + w * ks[hh]
        m_out[sq] = jnp.concatenate(
            [m_new4, jnp.zeros((1, V7X_LANES - MH), F32)], axis=1)


def _mlstm_sample(z3, g16, state_conv, state_C, state_n, w_conv, b_conv, m_norm_g):
    nseq = z3.shape[0]
    MH, dk, dv = M_HEADS, M_QK_DIM, M_V_DIM
    W = 2 * M_QK_WIDTH
    sb = _SAMPLE_SEQS_PER_STEP
    assert nseq % sb == 0
    const = lambda b: (0, 0)
    return pl.pallas_call(
        _mlstm_sample_kernel,
        grid=(nseq // sb,),
        in_specs=[
            pl.BlockSpec((sb, 1, W), lambda b: (b, 0, _C_MQK // W)),
            pl.BlockSpec((sb, 1, M_WIDTH), lambda b: (b, 0, _C_MV // M_WIDTH)),
            pl.BlockSpec((sb, 1, M_WIDTH), lambda b: (b, 0, _C_MO // M_WIDTH)),
            pl.BlockSpec((sb, 1, 4 * MH), lambda b: (b, 0, 0)),
            pl.BlockSpec((sb, M_CONV - 1, W), lambda b: (b, 0, 0)),
            pl.BlockSpec((sb, MH, dv, dk), lambda b: (b, 0, 0, 0)),
            pl.BlockSpec((sb, MH, dk), lambda b: (b, 0, 0)),
            pl.BlockSpec((M_CONV, W), const),
            pl.BlockSpec((1, W), const),
            pl.BlockSpec((1, M_WIDTH), const),
        ],
        out_specs=[
            pl.BlockSpec((sb, 1, M_WIDTH), lambda b: (b, 0, 0)),
            pl.BlockSpec((sb, MH, dv, dk), lambda b: (b, 0, 0, 0)),
            pl.BlockSpec((sb, MH, dk), lambda b: (b, 0, 0)),
            pl.BlockSpec((sb, 1, V7X_LANES), lambda b: (b, 0, 0)),
            pl.BlockSpec((sb, M_CONV - 1, W), lambda b: (b, 0, 0)),
        ],
        out_shape=[
            jax.ShapeDtypeStruct((nseq, 1, M_WIDTH), F32),
            jax.ShapeDtypeStruct((nseq, MH, dv, dk), F32),
            jax.ShapeDtypeStruct((nseq, MH, dk), F32),
            jax.ShapeDtypeStruct((nseq, 1, V7X_LANES), F32),
            jax.ShapeDtypeStruct((nseq, M_CONV - 1, W), F32),
        ],
        compiler_params=_params("arbitrary"),
        name="sample_mlstm",
    )(z3, z3, z3, g16, state_conv, state_C, state_n, w_conv, b_conv.reshape(1, W),
      m_norm_g.reshape(1, M_WIDTH))


def _merge_sample_kernel(ya_ref, hm_ref, z_ref, x_ref, mod_ref, watt_ref, wml_ref, wout_ref, fg_ref,
                         y_ref):
    d = x_ref.shape[1]
    n_rows = x_ref.shape[0]
    ya = ya_ref[...].T[0:n_rows, :] * _silu(z_ref[:, _C_AZ:_C_MQK])
    ya = jnp.dot(ya.astype(BF16), watt_ref[...].astype(BF16), preferred_element_type=F32)
    ym = hm_ref[...] * _silu(z_ref[:, _C_MZ:_C_MIF])
    ym = jnp.dot(ym.astype(BF16), wml_ref[...].astype(BF16), preferred_element_type=F32)
    tail = z_ref[:, _C_MIF:]
    ga = tail[:, _C_GA - _C_MIF:_C_GM - _C_MIF]
    gm = tail[:, _C_GM - _C_MIF:_C_END - _C_MIF]
    u = jax.nn.sigmoid(ga) * ya + jax.nn.sigmoid(gm) * ym
    upd = jnp.dot(u.astype(BF16), wout_ref[...].astype(BF16), preferred_element_type=F32)
    y = x_ref[...] + mod_ref[:, 2 * d:3 * d] * upd
    y = y * lax.rsqrt(jnp.mean(y * y, axis=-1, keepdims=True) + NORM_EPS)
    y_ref[...] = y * fg_ref[...]


def _merge_sample(ya, hm, z, x, mod, w_att, w_mlstm, w_out, final_g):
    n, d = x.shape
    full = lambda a: pl.BlockSpec(a.shape, lambda i: (0,) * a.ndim)
    fg = final_g.reshape(1, d)
    args = (ya, hm, z, x, mod, w_att, w_mlstm, w_out, fg)
    return pl.pallas_call(
        _merge_sample_kernel,
        grid=(1,),
        in_specs=[full(a) for a in args],
        out_specs=pl.BlockSpec((n, d), lambda i: (0, 0)),
        out_shape=jax.ShapeDtypeStruct((n, d), F32),
        compiler_params=_params("arbitrary"),
        name="sample_merge",
    )(*args)


def _gate_scores_kernel(sc_ref, idx_ref, *, nblk):
    sc = sc_ref[...] / MOBA_BLOCK
    lane = lax.broadcasted_iota(jnp.int32, sc.shape, 1)
    cnt = jnp.zeros(sc.shape, jnp.int32)
    for m in range(nblk):
        col = sc[:, m:m + 1]
        beats = (col > sc) | ((col == sc) & (lane > m))
        cnt += beats.astype(jnp.int32)
    out = jnp.zeros(sc.shape, jnp.int32)
    for r in range(MOBA_TOPK):
        idx = jnp.sum(jnp.where((cnt == r) & (lane < nblk), lane, 0), axis=1, keepdims=True)
        out = jnp.where(lane == r, idx, out)
    idx_ref[...] = out


def _gate_scores(block_scores, nblk):
    nseq, H, lanes = block_scores.shape
    rows = nseq * H
    out = pl.pallas_call(
        functools.partial(_gate_scores_kernel, nblk=nblk),
        grid=(1,),
        in_specs=[pl.BlockSpec((rows, lanes), lambda i: (0, 0))],
        out_specs=pl.BlockSpec((rows, lanes), lambda i: (0, 0)),
        out_shape=jax.ShapeDtypeStruct((rows, lanes), jnp.int32),
        compiler_params=_params("arbitrary"),
        name="sample_gate",
    )(block_scores.reshape(rows, lanes))
    return out.reshape(nseq, H, lanes)


def _sample_inputs(x, mod, norm_g, w_main, w_ifT, b_if):
    nseq = x.shape[0]
    H, dh = ATT_HEADS, ATT_HEAD_DIM
    z, g = _inproj_sample(x, mod, norm_g, w_main, w_ifT, b_if)
    q = z[:, _C_AQ:_C_AK].reshape(nseq, H, dh)
    q_lanes = jnp.broadcast_to(q[..., None], (nseq, H, dh, PAGE_SIZE))
    return z, g, q, q_lanes


def _sample_layer(x, mod, z, g, q, q_lanes, block_scores, cache_kT, cache_vT, page_table,
                  state_C, state_n, state_m, state_conv,
                  w_conv, b_conv, m_norm_g, w_att, w_mlstm, w_out, final_g):
    nseq = x.shape[0]
    H, dh = ATT_HEADS, ATT_HEAD_DIM
    nblk = page_table.shape[1] // _PAGES_PER_BLOCK
    assert nblk >= MOBA_TOPK
    k_new = z[:, _C_AK:_C_AV].reshape(nseq, H, dh)
    v_new = z[:, _C_AV:_C_AZ].reshape(nseq, H, dh)
    sel = _gate_scores(block_scores, nblk)
    sel_flat = sel[:, :, :MOBA_TOPK].reshape(-1)
    yaT = _attend_sample(cache_kT, cache_vT, page_table, sel_flat, q, q_lanes, k_new, v_new)
    g16 = jnp.concatenate([g, state_m, jnp.zeros_like(state_m)], axis=1).reshape(nseq, 1, 4 * M_HEADS)
    hm, C, n, m, conv_new = _mlstm_sample(z.reshape(nseq, 1, -1), g16, state_conv, state_C, state_n,
                                          w_conv, b_conv, m_norm_g)
    y = _merge_sample(yaT, hm.reshape(nseq, M_WIDTH), z, x, mod,
                      w_att, w_mlstm, w_out, final_g)
    return (y.reshape(nseq, 1, -1), k_new.reshape(nseq, 1, H, dh), v_new.reshape(nseq, 1, H, dh),
            C, n, m[:, 0, :M_HEADS], conv_new)


def kernel(x_prompt, x_sample, cache_k, cache_v, state_C, state_n, state_m, state_conv, page_table,
           c_prompt, c_sample, w_ada, b_ada, norm_g, w_in, b_if, w_conv, b_conv, m_norm_g,
           w_att, w_mlstm, w_out, final_g):
    depth = w_in.shape[0]
    assert depth == 1, "single-layer step"
    B = x_prompt.shape[0]
    nseq = x_sample.shape[0]
    pad = (-(B + nseq)) % V7X_SUBLANES
    c_all = jnp.concatenate([c_prompt, c_sample, jnp.zeros((pad, c_prompt.shape[1]), F32)], axis=0)
    mod = _ada(c_all, w_ada[0], b_ada[0])
    mod_p = mod[:B].reshape(B, 3, D_MODEL)
    mod_s = mod[B:B + nseq]
    w_main, w_ifT = _pack_w_in(w_in[0])
    wa, wm, wo = w_att[0], w_mlstm[0], w_out[0]
    cache_kT = jnp.transpose(cache_k, (0, 1, 3, 4, 2))
    cache_vT = jnp.transpose(cache_v, (0, 1, 3, 4, 2))
    xs = x_sample[:, 0, :]
    z, g, q, q_lanes = _sample_inputs(xs, mod_s, norm_g[0], w_main, w_ifT, b_if[0])
    *outs_p, block_scores = _prompt_layer(
        x_prompt, mod_p, norm_g[0], w_main, w_ifT, b_if[0], w_conv[0], b_conv[0], m_norm_g[0],
        wa, wm, wo, final_g, page_table, cache_kT, q_lanes)
    outs_s = _sample_layer(xs, mod_s, z, g, q, q_lanes, block_scores, cache_kT, cache_vT, page_table,
                           state_C[0], state_n[0], state_m[0], state_conv[0],
                           w_conv[0], b_conv[0], m_norm_g[0], wa, wm, wo, final_g)
    y_p, *st_p = outs_p
    y_s, *st_s = outs_s
    return (y_p, y_s) + tuple(a[None] for a in st_p) + tuple(a[None] for a in st_s)
```

```python
import functools

import jax
import jax.numpy as jnp
from jax import lax
from jax.experimental import pallas as pl
from jax.experimental.pallas import tpu as pltpu

F32 = jnp.float32
BF16 = jnp.bfloat16

D_MODEL = 1024
ATT_HEADS = 8
ATT_HEAD_DIM = 64
ATT_WIDTH = ATT_HEADS * ATT_HEAD_DIM
MOBA_BLOCK = 256
MOBA_TOPK = 3
M_HEADS = 4
M_V_DIM = 256
M_QK_DIM = 128
M_QK_WIDTH = M_HEADS * M_QK_DIM
M_WIDTH = M_HEADS * M_V_DIM
M_CONV = 4
PAGE_SIZE = 128
NORM_EPS = 1e-6
NEG_SCORE = -1e30
_LOG2E = 1.4426950408889634

V7X_LANES = 128
V7X_SUBLANES = 8
V7X_VMEM_LIMIT_BYTES = 56 * 1024 * 1024

_C_AQ, _C_AK, _C_AV, _C_AZ = 0, 512, 1024, 1536
_C_MQK, _C_MV, _C_MO, _C_MZ = 2048, 3072, 4096, 5120
_C_MIF, _C_GA, _C_GM, _C_END = 6144, 6152, 7176, 8200


def _silu(x):
    return x * jax.nn.sigmoid(x)


def _split3(x):
    hi = x.astype(BF16)
    r1 = x - hi.astype(F32)
    mid = r1.astype(BF16)
    lo = (r1 - mid.astype(F32)).astype(BF16)
    return hi, mid, lo


def _pack_w_in(w_in):
    pad = (-w_in.shape[1]) % V7X_LANES
    w_main = jnp.concatenate([w_in.astype(BF16), jnp.zeros((w_in.shape[0], pad), BF16)], axis=1)
    w_ifT = w_main[:, _C_MIF:_C_GA].T
    return w_main, w_ifT


def _params(*sem):
    return pltpu.CompilerParams(dimension_semantics=sem, vmem_limit_bytes=V7X_VMEM_LIMIT_BYTES)


def _ada_kernel(c_ref, w_ref, b_ref, o_ref):
    a = _silu(c_ref[...]).astype(BF16)
    acc = jnp.dot(a, w_ref[...].astype(BF16), preferred_element_type=F32)
    o_ref[...] = acc + b_ref[...]


def _ada(c, w_ada, b_ada):
    n, d = c.shape
    d3 = w_ada.shape[1]
    bn = 1024
    return pl.pallas_call(
        _ada_kernel,
        grid=(d3 // bn,),
        in_specs=[pl.BlockSpec((n, d), lambda i: (0, 0)),
                  pl.BlockSpec((d, bn), lambda i: (0, i)),
                  pl.BlockSpec((1, bn), lambda i: (0, i))],
        out_specs=pl.BlockSpec((n, bn), lambda i: (0, i)),
        out_shape=jax.ShapeDtypeStruct((n, d3), F32),
        compiler_params=_params("arbitrary"),
        name="ada_mod",
    )(c, w_ada, b_ada.reshape(1, d3))


def _inproj_kernel(x_ref, mod_ref, ng_ref, w_ref, wif_ref, bif_ref, bifn_ref, wc_ref, bc_ref,
                   qT_ref, kh_ref, vT_ref, kf_ref, vf_ref, kmean_ref,
                   az_ref, qk_ref, mv_ref, mo_ref, mz_ref, ga_ref, gm_ref, gT_ref, mvT_ref, gc_ref,
                   ctail_ref, tail_s):
    j = pl.program_id(1)
    x = x_ref[0]
    xn = x * lax.rsqrt(jnp.mean(x * x, axis=-1, keepdims=True) + NORM_EPS) * ng_ref[...]
    shift = mod_ref[0, 0:1, :]
    scale = mod_ref[0, 1:2, :]
    h = (xn * (1.0 + scale) + shift).astype(BF16)

    def proj(a, b):
        return jnp.dot(h, w_ref[:, a:b], preferred_element_type=F32)

    @pl.when(j == 0)
    def _():
        tail_s[...] = jnp.zeros_like(tail_s)

    mqk = proj(_C_MQK, _C_MV)
    ctail_ref[0] = mqk[mqk.shape[0] - V7X_SUBLANES:, :]
    qk = _silu(_causal_conv(mqk, tail_s, wc_ref, bc_ref))
    qk_ref[0, :, 0:M_QK_WIDTH] = qk[:, 0:M_QK_WIDTH].astype(BF16)
    qk_ref[0, :, M_QK_WIDTH:] = (qk[:, M_QK_WIDTH:] * (M_QK_DIM ** -0.5)).astype(BF16)

    mo_ref[0] = proj(_C_MO, _C_MZ).astype(BF16)

    q = proj(_C_AQ, _C_AK) * (ATT_HEAD_DIM ** -0.5 * _LOG2E)
    qT_ref[0, 0] = q.T.astype(BF16)
    mz_ref[0] = proj(_C_MZ, _C_MIF).astype(BF16)

    dh = ATT_HEAD_DIM
    tm = x.shape[0]
    one_col = jnp.where(lax.broadcasted_iota(jnp.int32, (tm, dh), 1) == 0, 1.0, 0.0)
    one_row = jnp.where(lax.broadcasted_iota(jnp.int32, (_MOBA_V_PAD, tm), 0) == 0, 1.0, 0.0)
    k = proj(_C_AK, _C_AV)
    kf_ref[0] = k
    ksum = jnp.sum(k, axis=0, keepdims=True) / MOBA_BLOCK
    for hh in range(ATT_HEADS):
        sl = slice(hh * dh, (hh + 1) * dh)
        kh_ref[0, 0, hh] = jnp.concatenate([k[:, sl], one_col], axis=1).astype(BF16)
        kmean_ref[0, hh, pl.ds(j, 1), :] = ksum[:, sl]
    az_ref[0] = proj(_C_AZ, _C_MQK).astype(BF16)

    v = proj(_C_AV, _C_AZ)
    vf_ref[0] = v
    vT = v.T
    for hh in range(ATT_HEADS):
        r0 = hh * _MOBA_V_ROWS
        vT_ref[0, 0, r0:r0 + dh, :] = vT[hh * dh:(hh + 1) * dh].astype(BF16)
        vT_ref[0, 0, r0 + dh:r0 + _MOBA_V_ROWS, :] = one_row.astype(BF16)

    tail = proj(_C_MIF, w_ref.shape[1])
    gc_ref[0] = tail[:, 0:_C_GA - _C_MIF] + bifn_ref[...]
    ga_ref[0] = tail[:, _C_GA - _C_MIF:_C_GM - _C_MIF].astype(BF16)
    gm_ref[0] = tail[:, _C_GM - _C_MIF:_C_END - _C_MIF].astype(BF16)
    gT = lax.dot_general(wif_ref[...], h, (((1,), (1,)), ((), ())), preferred_element_type=F32)
    gT_ref[0] = gT + bif_ref[...]

    mv = proj(_C_MV, _C_MO)
    mv_ref[0] = mv.astype(BF16)
    mvT_ref[0, 0] = mv.T.astype(BF16)


def _inproj(x, mod3, norm_g, w_main, w_ifT, b_if, w_conv, b_conv):
    B, S, D = x.shape
    W = 2 * M_QK_WIDTH
    tm = MOBA_BLOCK
    nb = S // tm
    H, dh = ATT_HEADS, ATT_HEAD_DIM
    const = lambda b, j: (0, 0)
    tok = lambda w: pl.BlockSpec((1, tm, w), lambda b, j: (b, j, 0))
    blk = pl.BlockSpec((1, 1, ATT_WIDTH, tm), lambda b, j: (b, j, 0, 0))
    out_shape = [
        jax.ShapeDtypeStruct((B, nb, ATT_WIDTH, tm), BF16),
        jax.ShapeDtypeStruct((B, nb, H, tm, 2 * dh), BF16),
        jax.ShapeDtypeStruct((B, nb, H * _MOBA_V_ROWS, tm), BF16),
        jax.ShapeDtypeStruct((B, S, ATT_WIDTH), F32),
        jax.ShapeDtypeStruct((B, S, ATT_WIDTH), F32),
        jax.ShapeDtypeStruct((B, H, nb, dh), F32),
        jax.ShapeDtypeStruct((B, S, ATT_WIDTH), BF16),
        jax.ShapeDtypeStruct((B, S, W), BF16),
        jax.ShapeDtypeStruct((B, S, M_WIDTH), BF16),
        jax.ShapeDtypeStruct((B, S, M_WIDTH), BF16),
        jax.ShapeDtypeStruct((B, S, M_WIDTH), BF16),
        jax.ShapeDtypeStruct((B, S, D), BF16),
        jax.ShapeDtypeStruct((B, S, D), BF16),
        jax.ShapeDtypeStruct((B, 2 * M_HEADS, S), F32),
        jax.ShapeDtypeStruct((B, nb, M_WIDTH, tm), BF16),
        jax.ShapeDtypeStruct((B, S, 2 * M_HEADS), F32),
        jax.ShapeDtypeStruct((B, V7X_SUBLANES, W), F32),
    ]
    out_specs = [
        blk,
        pl.BlockSpec((1, 1, H, tm, 2 * dh), lambda b, j: (b, j, 0, 0, 0)),
        pl.BlockSpec((1, 1, H * _MOBA_V_ROWS, tm), lambda b, j: (b, j, 0, 0)),
        tok(ATT_WIDTH), tok(ATT_WIDTH),
        pl.BlockSpec((1, H, nb, dh), lambda b, j: (b, 0, 0, 0)),
        tok(ATT_WIDTH), tok(2 * M_QK_WIDTH), tok(M_WIDTH), tok(M_WIDTH), tok(M_WIDTH), tok(D), tok(D),
        pl.BlockSpec((1, 2 * M_HEADS, tm), lambda b, j: (b, 0, j)),
        pl.BlockSpec((1, 1, M_WIDTH, tm), lambda b, j: (b, j, 0, 0)),
        tok(2 * M_HEADS),
        pl.BlockSpec((1, V7X_SUBLANES, W), lambda b, j: (b, 0, 0)),
    ]
    in_specs = [
        pl.BlockSpec((1, tm, D), lambda b, j: (b, j, 0)),
        pl.BlockSpec((1, 3, D), lambda b, j: (b, 0, 0)),
        pl.BlockSpec((1, D), const),
        pl.BlockSpec(w_main.shape, const, pipeline_mode=pl.Buffered(1)),
        pl.BlockSpec(w_ifT.shape, const),
        pl.BlockSpec((2 * M_HEADS, 1), const),
        pl.BlockSpec((1, 2 * M_HEADS), const),
        pl.BlockSpec((M_CONV, W), const),
        pl.BlockSpec((1, W), const),
    ]
    return pl.pallas_call(
        _inproj_kernel,
        grid=(B, nb),
        in_specs=in_specs,
        out_specs=out_specs,
        out_shape=out_shape,
        scratch_shapes=[pltpu.VMEM((V7X_SUBLANES, W), F32)],
        compiler_params=_params("arbitrary", "arbitrary"),
        name="prompt_inproj",
    )(x, mod3, norm_g.reshape(1, D), w_main, w_ifT, b_if.reshape(2 * M_HEADS, 1),
      b_if.reshape(1, 2 * M_HEADS), w_conv, b_conv.reshape(1, W))


_MOBA_HEADS_PER_STEP = 8
_MOBA_HEAD_GROUP = 8
_MOBA_V_PAD = 16
_MOBA_V_ROWS = ATT_HEAD_DIM + _MOBA_V_PAD


def _moba_phases(j):
    return j + 2 * ((j * j) // 4)


def _moba_kernel(pt_ref, qT_ref, kh_ref, vT_ref, kmean_ref, ck_ref, qb_ref, o_ref, sc_ref,
                 bias_ref, qs_ref, acc_ref, pbuf, qbuf, psem, qsem):
    b = pl.program_id(0)
    j = pl.program_id(2)
    nbq = pl.num_programs(2)
    nseq, n_pages = pt_ref.shape
    cpb = n_pages // _PAGES_PER_CHUNK
    bpc = _PAGES_PER_CHUNK // _PAGES_PER_BLOCK
    n_chunks = nseq * cpb
    ring = pbuf.shape[0]
    ahead = ring - 2

    def chunk_copy(g, i):
        page = pt_ref[g // cpb, (g % cpb) * _PAGES_PER_CHUNK + i]
        slot = g % ring
        return pltpu.make_async_copy(ck_ref.at[0, page], pbuf.at[slot, i], psem.at[slot])

    qslots = qbuf.shape[0]

    def q_copy(seq):
        s = seq % qslots
        return pltpu.make_async_copy(qb_ref.at[seq], qbuf.at[s], qsem.at[s])

    def start_chunk(g):
        @pl.when(g < n_chunks)
        def _():
            for i in range(_PAGES_PER_CHUNK):
                chunk_copy(g, i).start()

    step = b * pl.num_programs(1) + pl.program_id(1)

    @pl.when((step == 0) & (j == 0))
    def _():
        sc_ref[...] = jnp.zeros_like(sc_ref)
        q_copy(0).start()
        for g in range(ahead):
            start_chunk(g)

    lane_blk = lax.broadcasted_iota(jnp.int32, sc_ref.shape[1:], 1)

    def page_sync(g):
        live = g < n_chunks
        gc = jnp.minimum(g, n_chunks - 1)
        seq = gc // cpb
        ci = gc % cpb

        @pl.when(live & (ci == 0))
        def _():
            q_copy(seq).wait()

        @pl.when(live & (ci == 0) & (seq + 1 < nseq))
        def _():
            q_copy(seq + 1).start()

        start_chunk(g + ahead)

        @pl.when(live)
        def _():
            for i in range(_PAGES_PER_CHUNK):
                chunk_copy(gc, i).wait()

    def page_scores(g):
        gc = jnp.minimum(g, n_chunks - 1)
        seq = gc // cpb
        ci = gc % cpb
        slot = gc % ring
        qb = qbuf[seq % qslots]
        row = sc_ref[seq]
        for blk in range(bpc):
            p = pbuf[slot, blk * _PAGES_PER_BLOCK]
            for i in range(1, _PAGES_PER_BLOCK):
                p = p + pbuf[slot, blk * _PAGES_PER_BLOCK + i]
            part = jnp.sum(p * qb, axis=1)
            val = jnp.sum(part, axis=1, keepdims=True)
            row = jnp.where(lane_blk == ci * bpc + blk, val, row)
        sc_ref[seq] = row

    phase0 = step * _moba_phases(nbq) + _moba_phases(j)
    hp = kmean_ref.shape[1]
    nb = kmean_ref.shape[2]
    tq = qT_ref.shape[3]
    dh = ATT_HEAD_DIM
    vr = _MOBA_V_ROWS
    qTs = [qT_ref[0, 0, hh * dh:(hh + 1) * dh, :] for hh in range(hp)]
    for hh in range(hp):
        for slot in range(2):
            qs_ref[slot, hh, 0:dh, :] = qTs[hh]
            qs_ref[slot, hh, dh:2 * dh, :] = jnp.zeros((dh, tq), BF16)

    blk = lax.broadcasted_iota(jnp.int32, (nb, tq), 0)
    blk_f = blk.astype(F32)
    past = blk < j
    for hh in range(hp):
        sg = sum(jnp.dot(part, qTs[hh], preferred_element_type=F32)
                 for part in _split3(kmean_ref[0, hh]))
        work = jnp.where(past, sg, NEG_SCORE)
        bias = jnp.full((nb, tq), NEG_SCORE, F32)
        for _ in range(MOBA_TOPK):
            mx = jnp.max(work, axis=0, keepdims=True)
            first = jnp.min(jnp.where(work == mx, blk_f, float(nb)), axis=0, keepdims=True)
            chosen = blk_f == first
            bias = jnp.where(chosen & past, 0.0, bias)
            work = jnp.where(chosen, NEG_SCORE, work)
        bias_ref[hh] = bias

    kpos = lax.broadcasted_iota(jnp.int32, (tq, tq), 0)
    qpos = lax.broadcasted_iota(jnp.int32, (tq, tq), 1)
    causal = kpos <= qpos

    def score(n, slot, hh):
        return jnp.dot(kh_ref[0, n, hh], qs_ref[slot, hh], preferred_element_type=F32)

    def value(n, hh, p):
        return jnp.dot(vT_ref[0, n, hh * vr:(hh + 1) * vr, :], p, preferred_element_type=F32)

    def scores(n, slot=0):
        return [score(n, slot, hh) for hh in range(hp)]

    def values(n, ps):
        return [value(n, hh, ps[hh]) for hh in range(hp)]

    ss = [jnp.where(causal, s, NEG_SCORE) for s in scores(j)]
    ms = [jnp.max(s, axis=0, keepdims=True) for s in ss]
    pbs = [jnp.exp2((s - m).astype(BF16)) for s, m in zip(ss, ms)]
    for hh, x in enumerate(values(j, pbs)):
        acc_ref[hh] = x
    page_sync(phase0)
    page_scores(phase0)

    row0 = lax.broadcasted_iota(jnp.int32, (_MOBA_V_PAD, tq), 0) == 0

    def body(t, ms):
        n0 = 2 * t
        real1 = n0 + 1 < j
        n1 = jnp.minimum(n0 + 1, j - 1)
        g = phase0 + 1 + 2 * t
        page_sync(g)
        page_sync(g + 1)
        for hh in range(hp):
            mask0 = jnp.where(row0, bias_ref[hh, pl.ds(n0, 1), :], 0.0)
            qs_ref[0, hh, dh:dh + _MOBA_V_PAD, :] = mask0.astype(BF16)
            bias1 = jnp.where(real1, bias_ref[hh, pl.ds(n1, 1), :], NEG_SCORE)
            qs_ref[1, hh, dh:dh + _MOBA_V_PAD, :] = jnp.where(row0, bias1, 0.0).astype(BF16)
        page_scores(g)
        page_scores(g + 1)
        m_out = []
        for h0 in range(0, hp, _MOBA_HEAD_GROUP):
            hs = range(h0, min(h0 + _MOBA_HEAD_GROUP, hp))
            ss0 = [score(n0, 0, hh) for hh in hs]
            ss1 = [score(n1, 1, hh) for hh in hs]
            m0 = [ms[hh] for hh in hs]
            m1 = [jnp.maximum(m, jnp.max(s, axis=0, keepdims=True)) for m, s in zip(m0, ss0)]
            a0 = [jnp.exp2(m - mn) for m, mn in zip(m0, m1)]
            pv0 = [value(n0, hh, jnp.exp2((s - mn).astype(BF16)))
                   for hh, s, mn in zip(hs, ss0, m1)]
            m2 = [jnp.maximum(m, jnp.max(s, axis=0, keepdims=True)) for m, s in zip(m1, ss1)]
            a1 = [jnp.exp2(m - mn) for m, mn in zip(m1, m2)]
            pv1 = [value(n1, hh, jnp.exp2((s - mn).astype(BF16)))
                   for hh, s, mn in zip(hs, ss1, m2)]
            for i, hh in enumerate(hs):
                acc_ref[hh] = (a0[i] * acc_ref[hh] + pv0[i]) * a1[i] + pv1[i]
            m_out += m2
        return tuple(m_out)

    lax.fori_loop(0, (j + 1) // 2, body, tuple(ms))
    for hh in range(hp):
        acc = acc_ref[hh]
        o_ref[0, 0, hh * dh:(hh + 1) * dh, :] = (acc[0:dh] / acc[dh:dh + 1]).astype(o_ref.dtype)


def _moba_prompt(qT, kh, vT, kmean, page_table, cache_kT, q_lanes):
    B, nb, _, tq = qT.shape
    H, dh, hp = ATT_HEADS, ATT_HEAD_DIM, _MOBA_HEADS_PER_STEP
    nseq, n_pages = page_table.shape
    ps = cache_kT.shape[-1]
    assert n_pages % _PAGES_PER_CHUNK == 0 and n_pages // _PAGES_PER_BLOCK <= V7X_LANES
    n_chunks = nseq * (n_pages // _PAGES_PER_CHUNK)
    assert _PAGE_RING - 2 <= n_chunks <= B * (H // hp) * _moba_phases(nb), \
        "one page chunk per key-block phase"
    grid_spec = pltpu.PrefetchScalarGridSpec(
        num_scalar_prefetch=1,
        grid=(B, H // hp, nb),
        in_specs=[
            pl.BlockSpec((1, 1, hp * dh, tq), lambda b, g, j, pt: (b, j, g, 0)),
            pl.BlockSpec((1, nb, hp, tq, 2 * dh), lambda b, g, j, pt: (b, 0, g, 0, 0)),
            pl.BlockSpec((1, nb, hp * _MOBA_V_ROWS, tq), lambda b, g, j, pt: (b, 0, g, 0)),
            pl.BlockSpec((1, hp, nb, dh), lambda b, g, j, pt: (b, g, 0, 0)),
            pl.BlockSpec(memory_space=pl.ANY),
            pl.BlockSpec(memory_space=pl.ANY),
        ],
        out_specs=[
            pl.BlockSpec((1, 1, hp * dh, tq), lambda b, g, j, pt: (b, j, g, 0)),
            pl.BlockSpec((nseq, H, V7X_LANES), lambda b, g, j, pt: (0, 0, 0)),
        ],
        scratch_shapes=[
            pltpu.VMEM((hp, nb, tq), F32), pltpu.VMEM((2, hp, 2 * dh, tq), BF16),
            pltpu.VMEM((hp, _MOBA_V_ROWS, tq), F32),
            pltpu.VMEM((_PAGE_RING, _PAGES_PER_CHUNK, H, dh, ps), F32),
            pltpu.VMEM((3, H, dh, ps), F32),
            pltpu.SemaphoreType.DMA((_PAGE_RING,)), pltpu.SemaphoreType.DMA((3,)),
        ],
    )
    return pl.pallas_call(
        _moba_kernel,
        grid_spec=grid_spec,
        out_shape=[jax.ShapeDtypeStruct((B, nb, ATT_WIDTH, tq), BF16),
                   jax.ShapeDtypeStruct((nseq, H, V7X_LANES), F32)],
        compiler_params=_params("arbitrary", "arbitrary", "arbitrary"),
        name="prompt_moba",
    )(page_table, qT, kh, vT, kmean, cache_kT, q_lanes)


def _log_sigmoid(x):
    return jnp.minimum(x, 0.0) - jnp.log1p(jnp.exp(-jnp.abs(x)))


def _causal_conv(x, tail_ref, w_ref, b_ref):
    L = x.shape[0]
    prev = tail_ref[...]
    row = lax.broadcasted_iota(jnp.int32, (V7X_SUBLANES, x.shape[1]), 0)
    acc = x * w_ref[M_CONV - 1:M_CONV, :] + b_ref[...]
    for d in range(1, M_CONV):
        r = pltpu.roll(x, d, axis=0)
        head = jnp.where(row < d, pltpu.roll(prev, d, axis=0), r[:V7X_SUBLANES])
        shifted = jnp.concatenate([head, r[V7X_SUBLANES:]], axis=0)
        acc += shifted * w_ref[M_CONV - 1 - d:M_CONV - d, :]
    tail_ref[...] = x[L - V7X_SUBLANES:, :]
    return acc


def _mlstm_kernel(qk_ref, v_ref, vT_ref, g_ref, gc_ref, o_ref, ng_ref,
                  h_ref, C_out, n_out, m_out,
                  C_s, n_s, m_s):
    c = pl.program_id(1)
    L = qk_ref.shape[1]
    MH, dk, dv = M_HEADS, M_QK_DIM, M_V_DIM
    nt = (((1,), (1,)), ((), ()))

    @pl.when(c == 0)
    def _():
        C_s[...] = jnp.zeros_like(C_s)
        n_s[...] = jnp.zeros_like(n_s)
        m_s[...] = jnp.zeros_like(m_s)

    t_idx = lax.broadcasted_iota(jnp.int32, (L, L), 0)
    s_idx = lax.broadcasted_iota(jnp.int32, (L, L), 1)
    causal = s_idx <= t_idx
    tri = jnp.where(causal, 1.0, 0.0).astype(BF16)
    g_rows = g_ref[0]
    b_rows = sum(lax.dot_general(part, tri, nt, preferred_element_type=F32)
                 for part in _split3(_log_sigmoid(g_rows)))
    b_cols = sum(jnp.dot(tri, part, preferred_element_type=F32)
                 for part in _split3(_log_sigmoid(gc_ref[0])))

    for hd in range(MH):
        qb = qk_ref[0, :, hd * dk:(hd + 1) * dk]
        kb = qk_ref[0, :, M_QK_WIDTH + hd * dk:M_QK_WIDTH + (hd + 1) * dk]
        vsl = slice(hd * dv, (hd + 1) * dv)
        b_row = b_rows[MH + hd:MH + hd + 1]
        c_row = g_rows[hd:hd + 1] - b_row
        b_col = b_cols[:, MH + hd:MH + hd + 1]
        m_prev = m_s[hd:hd + 1, 0:1]
        M_col = jnp.maximum(jnp.max(jnp.where(causal, c_row, NEG_SCORE), axis=1, keepdims=True),
                            m_prev)
        M_last = jnp.maximum(jnp.max(c_row, axis=1, keepdims=True), m_prev)
        w = jnp.where(causal, jnp.exp(c_row - M_col), 0.0)
        a_col = jnp.exp(m_prev - M_col)

        s = lax.dot_general(qb, kb, nt, preferred_element_type=F32) * w
        inter = lax.dot_general(qb, C_s[hd].astype(BF16), nt, preferred_element_type=F32)
        num = a_col * inter + jnp.dot(s.astype(BF16), v_ref[0, :, vsl], preferred_element_type=F32)
        n_b = n_s[hd:hd + 1, :].astype(BF16).astype(F32)
        qn = jnp.sum(qb.astype(F32) * n_b, axis=1, keepdims=True)
        den = a_col * qn + jnp.sum(s, axis=1, keepdims=True)
        hh = num / jnp.maximum(jnp.abs(den), jnp.exp(-(b_col + M_col)))
        hh = hh * lax.rsqrt(jnp.mean(hh * hh, axis=-1, keepdims=True) + NORM_EPS)
        gate = jax.nn.sigmoid(o_ref[0, :, vsl].astype(F32))
        h_ref[0, :, vsl] = (hh * ng_ref[:, vsl] * gate).astype(h_ref.dtype)

        ws_row = jnp.exp(c_row - M_last)
        a_end = jnp.exp(m_prev - M_last)
        vwT = (vT_ref[0, 0, vsl, :].astype(F32) * ws_row).astype(BF16)
        C_s[hd] = a_end * C_s[hd] + jnp.dot(vwT, kb, preferred_element_type=F32)
        ws8 = jnp.broadcast_to(ws_row, (V7X_SUBLANES, L)).astype(BF16)
        n_s[hd:hd + 1, :] = (a_end * n_s[hd:hd + 1, :]
                             + jnp.dot(ws8, kb, preferred_element_type=F32)[0:1])
        m_s[hd:hd + 1, :] = jnp.broadcast_to(b_row[:, L - 1:L] + M_last, (1, m_s.shape[1]))

    @pl.when(c == pl.num_programs(1) - 1)
    def _():
        C_out[0] = C_s[...]
        n_out[0] = n_s[...]
        m_out[0] = m_s[...]


def _mlstm_prompt(qk, mv, mvT, gT, gc, mo, m_norm_g):
    B, S, W = qk.shape
    L = mvT.shape[3]
    nc = S // L
    MH, dk, dv = M_HEADS, M_QK_DIM, M_V_DIM
    const = lambda b, c: (0, 0)
    tok = lambda w: pl.BlockSpec((1, L, w), lambda b, c: (b, c, 0))
    in_specs = [
        tok(W), tok(M_WIDTH),
        pl.BlockSpec((1, 1, M_WIDTH, L), lambda b, c: (b, c, 0, 0)),
        pl.BlockSpec((1, 2 * MH, L), lambda b, c: (b, 0, c)),
        tok(2 * MH), tok(M_WIDTH),
        pl.BlockSpec((1, M_WIDTH), const),
    ]
    out_shape = [
        jax.ShapeDtypeStruct((B, S, M_WIDTH), BF16),
        jax.ShapeDtypeStruct((B, MH, dv, dk), F32),
        jax.ShapeDtypeStruct((B, V7X_SUBLANES, dk), F32),
        jax.ShapeDtypeStruct((B, V7X_SUBLANES, V7X_LANES), F32),
    ]
    out_specs = [
        tok(M_WIDTH),
        pl.BlockSpec((1, MH, dv, dk), lambda b, c: (b, 0, 0, 0)),
        pl.BlockSpec((1, V7X_SUBLANES, dk), lambda b, c: (b, 0, 0)),
        pl.BlockSpec((1, V7X_SUBLANES, V7X_LANES), lambda b, c: (b, 0, 0)),
    ]
    scratch = [
        pltpu.VMEM((MH, dv, dk), F32), pltpu.VMEM((V7X_SUBLANES, dk), F32),
        pltpu.VMEM((V7X_SUBLANES, V7X_LANES), F32),
    ]
    return pl.pallas_call(
        _mlstm_kernel,
        grid=(B, nc),
        in_specs=in_specs,
        out_specs=out_specs,
        out_shape=out_shape,
        scratch_shapes=scratch,
        compiler_params=_params("arbitrary", "arbitrary"),
        name="prompt_mlstm",
    )(qk, mv, mvT, gT, gc, mo, m_norm_g.reshape(1, M_WIDTH))


def _merge_kernel(yaT_ref, az_ref, hm_ref, mz_ref, ga_ref, gm_ref, x_ref, mod_ref,
                  watt_ref, wml_ref, wout_ref, fg_ref, y_ref, watt_s, wml_s, wout_s):
    @pl.when((pl.program_id(0) == 0) & (pl.program_id(1) == 0))
    def _():
        watt_s[...] = watt_ref[...].astype(BF16)
        wml_s[...] = wml_ref[...].astype(BF16)
        wout_s[...] = wout_ref[...].astype(BF16)

    ya = yaT_ref[0, 0].astype(F32).T * _silu(az_ref[0].astype(F32))
    ya = jnp.dot(ya.astype(BF16), watt_s[...], preferred_element_type=F32)
    ym = hm_ref[0].astype(F32) * _silu(mz_ref[0].astype(F32))
    ym = jnp.dot(ym.astype(BF16), wml_s[...], preferred_element_type=F32)
    u = (jax.nn.sigmoid(ga_ref[0].astype(F32)) * ya
         + jax.nn.sigmoid(gm_ref[0].astype(F32)) * ym)
    upd = jnp.dot(u.astype(BF16), wout_s[...], preferred_element_type=F32)
    y = x_ref[0] + mod_ref[0, 2:3, :] * upd
    y = y * lax.rsqrt(jnp.mean(y * y, axis=-1, keepdims=True) + NORM_EPS)
    y_ref[0] = y * fg_ref[...]


def _merge_prompt(yaT, az, hm, mz, ga, gm, x, mod3, w_att, w_mlstm, w_out, final_g):
    B, S, D = x.shape
    tm = yaT.shape[3]
    nb = S // tm
    const = lambda b, j: (0, 0)
    tok = lambda w: pl.BlockSpec((1, tm, w), lambda b, j: (b, j, 0))
    return pl.pallas_call(
        _merge_kernel,
        grid=(B, nb),
        in_specs=[
            pl.BlockSpec((1, 1, ATT_WIDTH, tm), lambda b, j: (b, j, 0, 0)),
            tok(ATT_WIDTH), tok(M_WIDTH), tok(M_WIDTH), tok(D), tok(D), tok(D),
            pl.BlockSpec((1, 3, D), lambda b, j: (b, 0, 0)),
            pl.BlockSpec(w_att.shape, const, pipeline_mode=pl.Buffered(1)),
            pl.BlockSpec(w_mlstm.shape, const, pipeline_mode=pl.Buffered(1)),
            pl.BlockSpec(w_out.shape, const, pipeline_mode=pl.Buffered(1)),
            pl.BlockSpec((1, D), const),
        ],
        out_specs=tok(D),
        out_shape=jax.ShapeDtypeStruct((B, S, D), F32),
        scratch_shapes=[pltpu.VMEM(w_att.shape, BF16), pltpu.VMEM(w_mlstm.shape, BF16),
                        pltpu.VMEM(w_out.shape, BF16)],
        compiler_params=_params("arbitrary", "arbitrary"),
        name="prompt_merge",
    )(yaT, az, hm, mz, ga, gm, x, mod3, w_att, w_mlstm, w_out, final_g.reshape(1, D))


def _prompt_layer(x, mod3, norm_g, w_main, w_ifT, b_if, w_conv, b_conv, m_norm_g,
                  w_att, w_mlstm, w_out, final_g, page_table, cache_kT, q_lanes):
    B, S, _ = x.shape
    (qT, kh, vT, kf, vf, kmean, az, qk, mv, mo, mz, ga, gm, gT, mvT, gc, ctail) = _inproj(
        x, mod3, norm_g, w_main, w_ifT, b_if, w_conv, b_conv)
    yaT, block_scores = _moba_prompt(qT, kh, vT, kmean, page_table, cache_kT, q_lanes)
    hm, C, n, m = _mlstm_prompt(qk, mv, mvT, gT, gc, mo, m_norm_g)
    y = _merge_prompt(yaT, az, hm, mz, ga, gm, x, mod3, w_att, w_mlstm, w_out, final_g)
    k_rows = kf.reshape(B, S, ATT_HEADS, ATT_HEAD_DIM)
    v_rows = vf.reshape(B, S, ATT_HEADS, ATT_HEAD_DIM)
    conv_new = ctail[:, V7X_SUBLANES - (M_CONV - 1):, :]
    return y, k_rows, v_rows, C, n[:, :M_HEADS, :], m[:, :M_HEADS, 0], conv_new, block_scores


def _inproj_sample_kernel(x_ref, mod_ref, ng_ref, w_ref, wif_ref, bif_ref, z_ref, g_ref):
    x = x_ref[...]
    d = x.shape[1]
    xn = x * lax.rsqrt(jnp.mean(x * x, axis=-1, keepdims=True) + NORM_EPS) * ng_ref[...]
    h = (xn * (1.0 + mod_ref[:, d:2 * d]) + mod_ref[:, 0:d]).astype(BF16)
    z_ref[...] = jnp.dot(h, w_ref[...], preferred_element_type=F32)
    g = lax.dot_general(h, wif_ref[...], (((1,), (1,)), ((), ())), preferred_element_type=F32)
    g_ref[...] = g + bif_ref[...]


def _inproj_sample(x, mod, norm_g, w_main, w_ifT, b_if):
    n, d = x.shape
    bn = 13 * V7X_LANES
    assert w_main.shape[1] % bn == 0
    const = lambda i: (0, 0)
    return pl.pallas_call(
        _inproj_sample_kernel,
        grid=(w_main.shape[1] // bn,),
        in_specs=[pl.BlockSpec((n, d), const),
                  pl.BlockSpec(mod.shape, const),
                  pl.BlockSpec((1, d), const),
                  pl.BlockSpec((d, bn), lambda i: (0, i)),
                  pl.BlockSpec(w_ifT.shape, const),
                  pl.BlockSpec((1, 2 * M_HEADS), const)],
        out_specs=[pl.BlockSpec((n, bn), lambda i: (0, i)),
                   pl.BlockSpec((n, 2 * M_HEADS), const)],
        out_shape=[jax.ShapeDtypeStruct((n, w_main.shape[1]), F32),
                   jax.ShapeDtypeStruct((n, 2 * M_HEADS), F32)],
        compiler_params=_params("arbitrary"),
        name="sample_inproj",
    )(x, mod, norm_g.reshape(1, d), w_main, w_ifT, b_if.reshape(1, 2 * M_HEADS))


_PAGES_PER_CHUNK = 8
_PAGE_RING = 7
_PAGES_PER_BLOCK = MOBA_BLOCK // PAGE_SIZE
_SAMPLE_SEQS_PER_STEP = 1


def _attend_sample_kernel(pt_ref, sel_ref, ck_ref, cv_ref, q_ref, ql_ref, kn_ref, vn_ref, o_ref,
                          kbuf, vbuf, sem):
    b = pl.program_id(0)
    nseq = pl.num_programs(0)
    H, dh = ATT_HEADS, ATT_HEAD_DIM
    npg = MOBA_TOPK * _PAGES_PER_BLOCK

    def copies(bb, slot):
        out = []
        for hh in range(H):
            for r in range(MOBA_TOPK):
                blk = sel_ref[(bb * H + hh) * MOBA_TOPK + r]
                for i in range(_PAGES_PER_BLOCK):
                    page = pt_ref[bb, blk * _PAGES_PER_BLOCK + i]
                    pg = r * _PAGES_PER_BLOCK + i
                    out.append(pltpu.make_async_copy(
                        ck_ref.at[0, page, hh], kbuf.at[slot, hh, pg], sem.at[0, slot]))
                    out.append(pltpu.make_async_copy(
                        cv_ref.at[0, page, hh], vbuf.at[slot, hh, pg], sem.at[1, slot]))
        return out

    @pl.when(b == 0)
    def _():
        for cp in copies(0, 0):
            cp.start()

    slot = b % 2

    @pl.when(b + 1 < nseq)
    def _():
        for cp in copies(b + 1, 1 - slot):
            cp.start()

    for cp in copies(b, slot):
        cp.wait()

    @pl.when(b == 0)
    def _():
        o_ref[...] = jnp.zeros_like(o_ref)

    scale = ATT_HEAD_DIM ** -0.5
    lane = lax.broadcasted_iota(jnp.int32, (dh, ql_ref.shape[3]), 1)
    eye = lax.broadcasted_iota(jnp.int32, lane.shape, 0) == lane
    for hh in range(H):
        qb = ql_ref[0, hh]
        s = jnp.concatenate([jnp.sum(kbuf[slot, hh, pg] * qb, axis=0, keepdims=True)
                             for pg in range(npg)], axis=1) * scale
        s_new = jnp.sum(q_ref[0, hh:hh + 1, :] * kn_ref[0, hh:hh + 1, :], axis=1,
                        keepdims=True) * scale
        m = jnp.maximum(jnp.max(s, axis=1, keepdims=True), s_new)
        p = jnp.exp(s - m)
        p_new = jnp.exp(s_new - m)
        l = jnp.sum(p, axis=1, keepdims=True) + p_new
        tile = vbuf[slot, hh, 0] * p[:, 0:PAGE_SIZE]
        for pg in range(1, npg):
            tile = tile + vbuf[slot, hh, pg] * p[:, pg * PAGE_SIZE:(pg + 1) * PAGE_SIZE]
        vn_b = jnp.broadcast_to(
            jnp.concatenate([vn_ref[0, hh:hh + 1, :], jnp.zeros((1, lane.shape[1] - dh), F32)],
                            axis=1), lane.shape)
        col = jnp.sum(tile + jnp.where(eye, vn_b * p_new, 0.0), axis=1, keepdims=True) / l
        sl = slice(hh * dh, (hh + 1) * dh)
        o_ref[sl, :] = jnp.where(lane == b, col, o_ref[sl, :])


def _attend_sample(cache_k, cache_v, page_table, sel_flat, q, q_lanes, k_new, v_new):
    nseq, n_pages = page_table.shape
    H, dh = ATT_HEADS, ATT_HEAD_DIM
    npg = MOBA_TOPK * _PAGES_PER_BLOCK
    ps = cache_k.shape[-1]
    assert nseq <= V7X_LANES and q_lanes.shape[3] == ps
    row = lambda: pl.BlockSpec((1, H, dh), lambda b, pt, sel: (b, 0, 0))
    grid_spec = pltpu.PrefetchScalarGridSpec(
        num_scalar_prefetch=2,
        grid=(nseq,),
        in_specs=[pl.BlockSpec(memory_space=pl.ANY),
                  pl.BlockSpec(memory_space=pl.ANY),
                  row(),
                  pl.BlockSpec((1, H, dh, ps), lambda b, pt, sel: (b, 0, 0, 0)),
                  row(), row()],
        out_specs=pl.BlockSpec((ATT_WIDTH, V7X_LANES), lambda b, pt, sel: (0, 0)),
        scratch_shapes=[pltpu.VMEM((2, H, npg, dh, ps), F32),
                        pltpu.VMEM((2, H, npg, dh, ps), F32),
                        pltpu.SemaphoreType.DMA((2, 2))],
    )
    return pl.pallas_call(
        _attend_sample_kernel,
        grid_spec=grid_spec,
        out_shape=jax.ShapeDtypeStruct((ATT_WIDTH, V7X_LANES), F32),
        compiler_params=_params("arbitrary"),
        name="sample_attend",
    )(page_table, sel_flat, cache_k, cache_v, q, q_lanes, k_new, v_new)


def _mlstm_sample_kernel(qk_ref, v_ref, o_ref, g_ref, conv_ref, C_ref, n_ref, wc_ref, bc_ref, ng_ref,
                         h_ref, C_out, n_out, m_out, conv_out):
    for sq in range(qk_ref.shape[0]):
        x = qk_ref[sq]
        conv = conv_ref[sq]
        acc = x * wc_ref[M_CONV - 1:M_CONV, :] + bc_ref[...]
        for jj in range(M_CONV - 1):
            acc += conv[jj:jj + 1, :] * wc_ref[jj:jj + 1, :]
        qk = _silu(acc)
        conv_out[sq] = jnp.concatenate([conv[1:], x], axis=0)
        g = g_ref[sq]
        MH = M_HEADS
        bb4 = _log_sigmoid(g[:, MH:2 * MH])
        m_prev4 = g[:, 2 * MH:3 * MH]
        cc4 = g[:, 0:MH] - bb4
        M4 = jnp.maximum(m_prev4, cc4)
        w4 = jnp.exp(cc4 - M4)
        a4 = jnp.exp(m_prev4 - M4)
        floor4 = jnp.exp(-(bb4 + M4))
        m_new4 = bb4 + M4
        nt = (((1,), (1,)), ((), ()))
        heads = range(MH)
        qbs = [qk[:, hh * M_QK_DIM:(hh + 1) * M_QK_DIM].astype(BF16) for hh in heads]
        ks = [qk[:, M_QK_WIDTH + hh * M_QK_DIM:M_QK_WIDTH + (hh + 1) * M_QK_DIM] * (M_QK_DIM ** -0.5)
              for hh in heads]
        vs = [v_ref[sq][:, hh * M_V_DIM:(hh + 1) * M_V_DIM] for hh in heads]
        Cs = [C_ref[sq, hh] for hh in heads]
        ns = [n_ref[sq, hh:hh + 1, :] for hh in heads]
        inters = [lax.dot_general(jnp.broadcast_to(qbs[hh], (V7X_SUBLANES, M_QK_DIM)),
                                  Cs[hh].astype(BF16), nt, preferred_element_type=F32)[0:1]
                  for hh in heads]
        wv_cols = []
        for hh in heads:
            wv = jnp.broadcast_to(w4[:, hh:hh + 1] * vs[hh], (V7X_SUBLANES, M_V_DIM))
            wv = jnp.concatenate([wv, jnp.zeros((V7X_LANES - V7X_SUBLANES, M_V_DIM), F32)], axis=0)
            wv_cols.append(wv.T[:, 0:1])
        for hh in heads:
            w, a = w4[:, hh:hh + 1], a4[:, hh:hh + 1]
            qf = qbs[hh].astype(F32)
            s = jnp.sum(qf * ks[hh].astype(BF16).astype(F32), axis=1, keepdims=True) * w
            vb = vs[hh].astype(BF16).astype(F32)
            num = a * inters[hh] + s.astype(BF16).astype(F32) * vb
            den = a * jnp.sum(qf * ns[hh].astype(BF16).astype(F32), axis=1, keepdims=True) + s
            hv = num / jnp.maximum(jnp.abs(den), floor4[:, hh:hh + 1])
            hv = hv * lax.rsqrt(jnp.mean(hv * hv, axis=-1, keepdims=True) + NORM_EPS)
            sl = slice(hh * M_V_DIM, (hh + 1) * M_V_DIM)
            h_ref[sq, :, sl] = hv * ng_ref[:, sl] * jax.nn.sigmoid(o_ref[sq][:, sl])
        for hh in heads:
            w, a = w4[:, hh:hh + 1], a4[:, hh:hh + 1]
            C_out[sq, hh] = a * Cs[hh] + wv_cols[hh] * ks[hh]
            n_out[sq, hh:hh + 1, :] = a * ns[hh] + w * ks[hh]
        m_out[sq] = jnp.concatenate(
            [m_new4, jnp.zeros((1, V7X_LANES - MH), F32)], axis=1)


def _mlstm_sample(z3, g16, state_conv, state_C, state_n, w_conv, b_conv, m_norm_g):
    nseq = z3.shape[0]
    MH, dk, dv = M_HEADS, M_QK_DIM, M_V_DIM
    W = 2 * M_QK_WIDTH
    sb = _SAMPLE_SEQS_PER_STEP
    assert nseq % sb == 0
    const = lambda b: (0, 0)
    return pl.pallas_call(
        _mlstm_sample_kernel,
        grid=(nseq // sb,),
        in_specs=[
            pl.BlockSpec((sb, 1, W), lambda b: (b, 0, _C_MQK // W)),
            pl.BlockSpec((sb, 1, M_WIDTH), lambda b: (b, 0, _C_MV // M_WIDTH)),
            pl.BlockSpec((sb, 1, M_WIDTH), lambda b: (b, 0, _C_MO // M_WIDTH)),
            pl.BlockSpec((sb, 1, 4 * MH), lambda b: (b, 0, 0)),
            pl.BlockSpec((sb, M_CONV - 1, W), lambda b: (b, 0, 0)),
            pl.BlockSpec((sb, MH, dv, dk), lambda b: (b, 0, 0, 0)),
            pl.BlockSpec((sb, MH, dk), lambda b: (b, 0, 0)),
            pl.BlockSpec((M_CONV, W), const),
            pl.BlockSpec((1, W), const),
            pl.BlockSpec((1, M_WIDTH), const),
        ],
        out_specs=[
            pl.BlockSpec((sb, 1, M_WIDTH), lambda b: (b, 0, 0)),
            pl.BlockSpec((sb, MH, dv, dk), lambda b: (b, 0, 0, 0)),
            pl.BlockSpec((sb, MH, dk), lambda b: (b, 0, 0)),
            pl.BlockSpec((sb, 1, V7X_LANES), lambda b: (b, 0, 0)),
            pl.BlockSpec((sb, M_CONV - 1, W), lambda b: (b, 0, 0)),
        ],
        out_shape=[
            jax.ShapeDtypeStruct((nseq, 1, M_WIDTH), F32),
            jax.ShapeDtypeStruct((nseq, MH, dv, dk), F32),
            jax.ShapeDtypeStruct((nseq, MH, dk), F32),
            jax.ShapeDtypeStruct((nseq, 1, V7X_LANES), F32),
            jax.ShapeDtypeStruct((nseq, M_CONV - 1, W), F32),
        ],
        compiler_params=_params("arbitrary"),
        name="sample_mlstm",
    )(z3, z3, z3, g16, state_conv, state_C, state_n, w_conv, b_conv.reshape(1, W),
      m_norm_g.reshape(1, M_WIDTH))


def _merge_sample_kernel(ya_ref, hm_ref, z_ref, x_ref, mod_ref, watt_ref, wml_ref, wout_ref, fg_ref,
                         y_ref):
    d = x_ref.shape[1]
    n_rows = x_ref.shape[0]
    ya = ya_ref[...].T[0:n_rows, :] * _silu(z_ref[:, _C_AZ:_C_MQK])
    ya = jnp.dot(ya.astype(BF16), watt_ref[...].astype(BF16), preferred_element_type=F32)
    ym = hm_ref[...] * _silu(z_ref[:, _C_MZ:_C_MIF])
    ym = jnp.dot(ym.astype(BF16), wml_ref[...].astype(BF16), preferred_element_type=F32)
    tail = z_ref[:, _C_MIF:]
    ga = tail[:, _C_GA - _C_MIF:_C_GM - _C_MIF]
    gm = tail[:, _C_GM - _C_MIF:_C_END - _C_MIF]
    u = jax.nn.sigmoid(ga) * ya + jax.nn.sigmoid(gm) * ym
    upd = jnp.dot(u.astype(BF16), wout_ref[...].astype(BF16), preferred_element_type=F32)
    y = x_ref[...] + mod_ref[:, 2 * d:3 * d] * upd
    y = y * lax.rsqrt(jnp.mean(y * y, axis=-1, keepdims=True) + NORM_EPS)
    y_ref[...] = y * fg_ref[...]


def _merge_sample(ya, hm, z, x, mod, w_att, w_mlstm, w_out, final_g):
    n, d = x.shape
    full = lambda a: pl.BlockSpec(a.shape, lambda i: (0,) * a.ndim)
    fg = final_g.reshape(1, d)
    args = (ya, hm, z, x, mod, w_att, w_mlstm, w_out, fg)
    return pl.pallas_call(
        _merge_sample_kernel,
        grid=(1,),
        in_specs=[full(a) for a in args],
        out_specs=pl.BlockSpec((n, d), lambda i: (0, 0)),
        out_shape=jax.ShapeDtypeStruct((n, d), F32),
        compiler_params=_params("arbitrary"),
        name="sample_merge",
    )(*args)


def _gate_scores_kernel(sc_ref, idx_ref, *, nblk):
    sc = sc_ref[...] / MOBA_BLOCK
    lane = lax.broadcasted_iota(jnp.int32, sc.shape, 1)
    cnt = jnp.zeros(sc.shape, jnp.int32)
    for m in range(nblk):
        col = sc[:, m:m + 1]
        beats = (col > sc) | ((col == sc) & (lane > m))
        cnt += beats.astype(jnp.int32)
    out = jnp.zeros(sc.shape, jnp.int32)
    for r in range(MOBA_TOPK):
        idx = jnp.sum(jnp.where((cnt == r) & (lane < nblk), lane, 0), axis=1, keepdims=True)
        out = jnp.where(lane == r, idx, out)
    idx_ref[...] = out


def _gate_scores(block_scores, nblk):
    nseq, H, lanes = block_scores.shape
    rows = nseq * H
    out = pl.pallas_call(
        functools.partial(_gate_scores_kernel, nblk=nblk),
        grid=(1,),
        in_specs=[pl.BlockSpec((rows, lanes), lambda i: (0, 0))],
        out_specs=pl.BlockSpec((rows, lanes), lambda i: (0, 0)),
        out_shape=jax.ShapeDtypeStruct((rows, lanes), jnp.int32),
        compiler_params=_params("arbitrary"),
        name="sample_gate",
    )(block_scores.reshape(rows, lanes))
    return out.reshape(nseq, H, lanes)


def _sample_inputs(x, mod, norm_g, w_main, w_ifT, b_if):
    nseq = x.shape[0]
    H, dh = ATT_HEADS, ATT_HEAD_DIM
    z, g = _inproj_sample(x, mod, norm_g, w_main, w_ifT, b_if)
    q = z[:, _C_AQ:_C_AK].reshape(nseq, H, dh)
    q_lanes = jnp.broadcast_to(q[..., None], (nseq, H, dh, PAGE_SIZE))
    return z, g, q, q_lanes


def _sample_layer(x, mod, z, g, q, q_lanes, block_scores, cache_kT, cache_vT, page_table,
                  state_C, state_n, state_m, state_conv,
                  w_conv, b_conv, m_norm_g, w_att, w_mlstm, w_out, final_g):
    nseq = x.shape[0]
    H, dh = ATT_HEADS, ATT_HEAD_DIM
    nblk = page_table.shape[1] // _PAGES_PER_BLOCK
    assert nblk >= MOBA_TOPK
    k_new = z[:, _C_AK:_C_AV].reshape(nseq, H, dh)
    v_new = z[:, _C_AV:_C_AZ].reshape(nseq, H, dh)
    sel = _gate_scores(block_scores, nblk)
    sel_flat = sel[:, :, :MOBA_TOPK].reshape(-1)
    yaT = _attend_sample(cache_kT, cache_vT, page_table, sel_flat, q, q_lanes, k_new, v_new)
    g16 = jnp.concatenate([g, state_m, jnp.zeros_like(state_m)], axis=1).reshape(nseq, 1, 4 * M_HEADS)
    hm, C, n, m, conv_new = _mlstm_sample(z.reshape(nseq, 1, -1), g16, state_conv, state_C, state_n,
                                          w_conv, b_conv, m_norm_g)
    y = _merge_sample(yaT, hm.reshape(nseq, M_WIDTH), z, x, mod,
                      w_att, w_mlstm, w_out, final_g)
    return (y.reshape(nseq, 1, -1), k_new.reshape(nseq, 1, H, dh), v_new.reshape(nseq, 1, H, dh),
            C, n, m[:, 0, :M_HEADS], conv_new)


def kernel(x_prompt, x_sample, cache_k, cache_v, state_C, state_n, state_m, state_conv, page_table,
           c_prompt, c_sample, w_ada, b_ada, norm_g, w_in, b_if, w_conv, b_conv, m_norm_g,
           w_att, w_mlstm, w_out, final_g):
    depth = w_in.shape[0]
    assert depth == 1, "single-layer step"
    B = x_prompt.shape[0]
    nseq = x_sample.shape[0]
    pad = (-(B + nseq)) % V7X_SUBLANES
    c_all = jnp.concatenate([c_prompt, c_sample, jnp.zeros((pad, c_prompt.shape[1]), F32)], axis=0)
    mod = _ada(c_all, w_ada[0], b_ada[0])
    mod_p = mod[:B].reshape(B, 3, D_MODEL)
    mod_s = mod[B:B + nseq]
    w_main, w_ifT = _pack_w_in(w_in[0])
    wa, wm, wo = w_att[0], w_mlstm[0], w_out[0]
    cache_kT = jnp.transpose(cache_k, (0, 1, 3, 4, 2))
    cache_vT = jnp.transpose(cache_v, (0, 1, 3, 4, 2))
    xs = x_sample[:, 0, :]
    z, g, q, q_lanes = _sample_inputs(xs, mod_s, norm_g[0], w_main, w_ifT, b_if[0])
    *outs_p, block_scores = _prompt_layer(
        x_prompt, mod_p, norm_g[0], w_main, w_ifT, b_if[0], w_conv[0], b_conv[0], m_norm_g[0],
        wa, wm, wo, final_g, page_table, cache_kT, q_lanes)
    outs_s = _sample_layer(xs, mod_s, z, g, q, q_lanes, block_scores, cache_kT, cache_vT, page_table,
                           state_C[0], state_n[0], state_m[0], state_conv[0],
                           w_conv[0], b_conv[0], m_norm_g[0], wa, wm, wo, final_g)
    y_p, *st_p = outs_p
    y_s, *st_s = outs_s
    return (y_p, y_s) + tuple(a[None] for a in st_p) + tuple(a[None] for a in st_s)
```
